```python
import jax, jax.numpy as jnp
from jax import lax
import numpy as np

D_MODEL = 1024
BATCH = 8
SEQ = 2048
DEPTH = 1

PLE_DIM = 256
EPS = 1e-6
A_HEADS = 4
A_DQK = 128
A_DV = 256
A_QK_WIDTH = A_HEADS * A_DQK
A_WIDTH = A_HEADS * A_DV
A_CHUNK = 64
CONV_K = 4
B_HEADS = 16
B_DH = 64
B_WIDTH = B_HEADS * B_DH
Q_BLOCK = 128

IN_SPLITS = (A_QK_WIDTH, A_QK_WIDTH, A_WIDTH, A_HEADS, A_HEADS, A_WIDTH, A_WIDTH,
             B_WIDTH, B_WIDTH, B_WIDTH, B_HEADS, B_WIDTH, D_MODEL, D_MODEL)
N_IN = sum(IN_SPLITS)

kernel_name = "hybrid_mlstm_fox_gated_parallel"


def rms_norm(x, g):
    xf = x.astype(jnp.float32)
    y = xf * lax.rsqrt(jnp.mean(xf * xf, axis=-1, keepdims=True) + EPS)
    return (y * g.astype(jnp.float32)).astype(x.dtype)


def split_proj(proj):
    idx = np.cumsum(np.array(IN_SPLITS))[:-1].tolist()
    return jnp.split(proj, idx, axis=-1)


def heads(t, n_heads):
    b, s, c = t.shape
    return t.reshape(b, s, n_heads, c // n_heads).transpose(0, 2, 1, 3)


def merge_heads(t):
    b, h, s, d = t.shape
    return t.transpose(0, 2, 1, 3).reshape(b, s, h * d)


def causal_conv(x, w, b):
    s = x.shape[1]
    xp = jnp.pad(x, ((0, 0), (CONV_K - 1, 0), (0, 0)))
    y = xp[:, 0:s, :] * w[0]
    for kk in range(1, CONV_K):
        y = y + xp[:, kk:kk + s, :] * w[kk]
    return y + b


def mlstm_chunkwise(q, k, v, log_i, log_f):
    bsz, nh, s, dk = q.shape
    dv = v.shape[-1]
    nc = s // A_CHUNK

    def to_chunks(t):
        t = t.reshape((bsz, nh, nc, A_CHUNK) + t.shape[3:])
        return jnp.moveaxis(t, 2, 0)

    xs = (to_chunks(q), to_chunks(k), to_chunks(v), to_chunks(log_i), to_chunks(log_f))
    causal = jnp.tril(jnp.ones((A_CHUNK, A_CHUNK), dtype=bool))

    def step(carry, chunk):
        c_st, n_st, m_st = carry
        qj, kj, vj, ij, fj = chunk
        b = jnp.cumsum(fj, axis=-1)
        g = b[..., -1]
        dmat = b[..., :, None] - b[..., None, :] + ij[..., None, :]
        dmat = jnp.where(causal, dmat, -jnp.inf)
        inter = b + m_st[..., None]
        m_row = jnp.maximum(inter, jnp.max(dmat, axis=-1))
        w_intra = jnp.exp(dmat - m_row[..., None])
        w_inter = jnp.exp(inter - m_row)
        scores = jnp.einsum("bhld,bhsd->bhls", qj, kj) * w_intra
        num = (jnp.einsum("bhls,bhsv->bhlv", scores, vj)
               + w_inter[..., None] * jnp.einsum("bhld,bhdv->bhlv", qj, c_st))
        den = jnp.sum(scores, axis=-1) + w_inter * jnp.einsum("bhld,bhd->bhl", qj, n_st)
        h = num / jnp.maximum(jnp.abs(den), jnp.exp(-m_row))[..., None]
        to_end = g[..., None] - b + ij
        m_new = jnp.maximum(g + m_st, jnp.max(to_end, axis=-1))
        w_k = jnp.exp(to_end - m_new[..., None])
        decay = jnp.exp(g + m_st - m_new)
        c_new = decay[..., None, None] * c_st + jnp.einsum("bhs,bhsd,bhsv->bhdv", w_k, kj, vj)
        n_new = decay[..., None] * n_st + jnp.einsum("bhs,bhsd->bhd", w_k, kj)
        return (c_new, n_new, m_new), h

    init = (jnp.zeros((bsz, nh, dk, dv), jnp.float32),
            jnp.zeros((bsz, nh, dk), jnp.float32),
            jnp.zeros((bsz, nh), jnp.float32))
    _, hs = lax.scan(step, init, xs)
    return jnp.moveaxis(hs, 0, 2).reshape(bsz, nh, s, dv)


def forgetting_attention(q, k, v, log_f):
    s = q.shape[2]
    cum = jnp.cumsum(log_f, axis=-1)
    scale = B_DH ** -0.5
    outs = []
    for blk in range(s // Q_BLOCK):
        q0, q1 = blk * Q_BLOCK, (blk + 1) * Q_BLOCK
        qb = q[:, :, q0:q1]
        kb = k[:, :, :q1]
        vb = v[:, :, :q1]
        logits = (jnp.einsum("bhqd,bhkd->bhqk", qb, kb) * scale
                  + cum[:, :, q0:q1, None] - cum[:, :, None, :q1])
        mask = (q0 + jnp.arange(Q_BLOCK))[:, None] >= jnp.arange(q1)[None, :]
        logits = jnp.where(mask, logits, -jnp.inf)
        probs = jax.nn.softmax(logits, axis=-1)
        outs.append(jnp.einsum("bhqk,bhkd->bhqd", probs, vb))
    return jnp.concatenate(outs, axis=2)


def setup_inputs(seed: int = 0) -> dict:
    key = jax.random.key(seed)
    ks = jax.random.split(key, 17)
    f32 = jnp.float32

    def nrm(k, shape, scale):
        return jax.random.normal(k, shape, f32) * scale

    return {
        "x": nrm(ks[0], (BATCH, SEQ, D_MODEL), 1.0),
        "p": nrm(ks[1], (DEPTH, BATCH, SEQ, PLE_DIM), 1.0),
        "attn_norm_g": 1.0 + nrm(ks[2], (DEPTH, D_MODEL), 0.02),
        "w_in": nrm(ks[3], (DEPTH, D_MODEL, N_IN), D_MODEL ** -0.5),
        "conv_w": nrm(ks[4], (DEPTH, CONV_K, 2 * A_QK_WIDTH), 0.5),
        "conv_b": nrm(ks[5], (DEPTH, 2 * A_QK_WIDTH), 0.02),
        "a_bias_i": nrm(ks[6], (DEPTH, A_HEADS), 0.1),
        "a_bias_f": jnp.linspace(3.0, 6.0, A_HEADS, dtype=f32)[None, :] + nrm(ks[7], (DEPTH, A_HEADS), 0.1),
        "a_head_norm_g": 1.0 + nrm(ks[8], (DEPTH, A_WIDTH), 0.02),
        "b_bias_f": jnp.linspace(1.0, 4.0, B_HEADS, dtype=f32)[None, :] + nrm(ks[9], (DEPTH, B_HEADS), 0.1),
        "w_branch_a": nrm(ks[10], (DEPTH, A_WIDTH, D_MODEL), A_WIDTH ** -0.5),
        "w_branch_b": nrm(ks[11], (DEPTH, B_WIDTH, D_MODEL), B_WIDTH ** -0.5),
        "w_out": nrm(ks[12], (DEPTH, D_MODEL, D_MODEL), D_MODEL ** -0.5),
        "ple_norm_g": 1.0 + nrm(ks[13], (DEPTH, D_MODEL), 0.02),
        "w_ple_gate": nrm(ks[14], (DEPTH, D_MODEL, D_MODEL), D_MODEL ** -0.5),
        "w_ple_proj": nrm(ks[15], (DEPTH, PLE_DIM, D_MODEL), PLE_DIM ** -0.5),
        "final_norm_g": 1.0 + nrm(ks[16], (D_MODEL,), 0.02),
    }


def reference(x, p, attn_norm_g, w_in, conv_w, conv_b, a_bias_i, a_bias_f, a_head_norm_g,
              b_bias_f, w_branch_a, w_branch_b, w_out, ple_norm_g, w_ple_gate, w_ple_proj,
              final_norm_g):
    f32 = jnp.float32
    bsz, s, _ = x.shape
    for i in range(DEPTH):
        h = rms_norm(x, attn_norm_g[i])
        proj = h @ w_in[i]
        (a_q, a_k, a_v, a_i, a_f, a_o, a_z,
         b_q, b_k, b_v, b_f, b_z, g_a, g_b) = split_proj(proj)

        qk = jax.nn.silu(causal_conv(jnp.concatenate([a_q, a_k], axis=-1), conv_w[i], conv_b[i]))
        a_q, a_k = jnp.split(qk, 2, axis=-1)
        q_a = heads(a_q, A_HEADS).astype(f32)
        k_a = heads(a_k, A_HEADS).astype(f32) * (A_DQK ** -0.5)
        v_a = heads(a_v, A_HEADS).astype(f32)
        log_i = (a_i.astype(f32) + a_bias_i[i].astype(f32)).transpose(0, 2, 1)
        log_fa = jax.nn.log_sigmoid(a_f.astype(f32) + a_bias_f[i].astype(f32)).transpose(0, 2, 1)
        ha = mlstm_chunkwise(q_a, k_a, v_a, log_i, log_fa)
        ha = ha * lax.rsqrt(jnp.mean(ha * ha, axis=-1, keepdims=True) + EPS)
        ha = merge_heads(ha) * a_head_norm_g[i].astype(f32)
        ha = (jax.nn.sigmoid(a_o.astype(f32)) * ha).astype(x.dtype) * jax.nn.silu(a_z)
        y_a = ha @ w_branch_a[i]

        q_b = heads(b_q, B_HEADS).astype(f32)
        k_b = heads(b_k, B_HEADS).astype(f32)
        v_b = heads(b_v, B_HEADS).astype(f32)
        log_fb = jax.nn.log_sigmoid(b_f.astype(f32) + b_bias_f[i].astype(f32)).transpose(0, 2, 1)
        hb = forgetting_attention(q_b, k_b, v_b, log_fb)
        hb = merge_heads(hb).astype(x.dtype) * jax.nn.silu(b_z)
        y_b = hb @ w_branch_b[i]

        merged = jax.nn.sigmoid(g_a) * y_a + jax.nn.sigmoid(g_b) * y_b
        x = x + merged @ w_out[i]

        gate = jax.nn.sigmoid(rms_norm(x, ple_norm_g[i]) @ w_ple_gate[i])
        x = x + gate * (p[i] @ w_ple_proj[i])
    return rms_norm(x, final_norm_g)
```

```python
import functools

import jax
import jax.numpy as jnp
from jax import lax
from jax.experimental import pallas as pl
from jax.experimental.pallas import tpu as pltpu

F32 = jnp.float32
BF16 = jnp.bfloat16

EPS = 1e-6
A_HEADS = 4
A_DQK = 128
A_DV = 256
CONV_K = 4
B_HEADS = 16
B_DH = 64
LANES = 128
SUBLANES = 8
NEG_BIG = -1e30

IN_TM = 1024
IN_TN = 1024
CUM_BLK = 256
A_CHUNK = 256
FOX_T = 256
MERGE_TM = 512
VMEM_LIMIT = 48 * 1024 * 1024


def _sigmoid(x):
    return 1.0 / (1.0 + jnp.exp(-x))


def _silu(x):
    return x * _sigmoid(x)


def _rms_norm(x, g):
    ms = jnp.mean(x * x, axis=-1, keepdims=True)
    return (x * lax.rsqrt(ms + EPS)) * g


def _in_proj_kernel(x_ref, g_ref, w_ref, wg_ref, proj_ref, gates_ref, h_scr):
    @pl.when(pl.program_id(1) == 0)
    def _():
        h = _rms_norm(x_ref[...], g_ref[...]).astype(BF16)
        h_scr[...] = h
        gates_ref[...] = jnp.dot(h, wg_ref[...], preferred_element_type=F32)

    proj_ref[...] = jnp.dot(h_scr[...], w_ref[...], preferred_element_type=F32).astype(BF16)


def _in_proj(x2, g, w_main, w_gate):
    m, d = x2.shape
    n = w_main.shape[1]
    return pl.pallas_call(
        _in_proj_kernel,
        grid=(m // IN_TM, n // IN_TN),
        in_specs=[
            pl.BlockSpec((IN_TM, d), lambda i, j: (i, 0)),
            pl.BlockSpec((1, d), lambda i, j: (0, 0)),
            pl.BlockSpec((d, IN_TN), lambda i, j: (0, j)),
            pl.BlockSpec((d, LANES), lambda i, j: (0, 0)),
        ],
        out_specs=[
            pl.BlockSpec((IN_TM, IN_TN), lambda i, j: (i, j)),
            pl.BlockSpec((IN_TM, LANES), lambda i, j: (i, 0)),
        ],
        out_shape=[
            jax.ShapeDtypeStruct((m, n), BF16),
            jax.ShapeDtypeStruct((m, LANES), F32),
        ],
        scratch_shapes=[pltpu.VMEM((IN_TM, d), BF16)],
        compiler_params=pltpu.CompilerParams(
            dimension_semantics=("arbitrary", "arbitrary"),
            vmem_limit_bytes=VMEM_LIMIT),
        name="in_proj",
    )(x2, g, w_main, w_gate)


def _gates_kernel(g_ref, bias_ref, col_ref, row_ref):
    x = g_ref[...] + bias_ref[...]
    s = x.shape[0]
    ls = jnp.minimum(x, 0.0) - jnp.log1p(jnp.exp(-jnp.abs(x)))
    r = lax.broadcasted_iota(jnp.int32, (CUM_BLK, CUM_BLK), 0)
    c = lax.broadcasted_iota(jnp.int32, (CUM_BLK, CUM_BLK), 1)
    tri = jnp.where(r >= c, 1.0, 0.0).astype(BF16)
    carry = jnp.zeros((1, LANES), F32)
    blocks = []
    for blk in range(s // CUM_BLK):
        xb = ls[blk * CUM_BLK:(blk + 1) * CUM_BLK]
        x1 = xb.astype(BF16)
        r1 = xb - x1.astype(F32)
        x2 = r1.astype(BF16)
        x3 = (r1 - x2.astype(F32)).astype(BF16)
        cs = (jnp.dot(tri, x3, preferred_element_type=F32)
              + jnp.dot(tri, x2, preferred_element_type=F32)
              + jnp.dot(tri, x1, preferred_element_type=F32)) + carry
        carry = cs[CUM_BLK - 1:CUM_BLK, :]
        blocks.append(cs)
    cum = jnp.concatenate(blocks, axis=0)
    lane = lax.broadcasted_iota(jnp.int32, x.shape, 1)
    res = jnp.where(lane < A_HEADS, x, cum)
    col_ref[...] = res
    row_ref[...] = res.T


def _gates(gates3, bias):
    b, s, _ = gates3.shape
    return pl.pallas_call(
        _gates_kernel,
        grid=(b,),
        in_specs=[
            pl.BlockSpec((None, s, LANES), lambda i: (i, 0, 0)),
            pl.BlockSpec((1, LANES), lambda i: (0, 0)),
        ],
        out_specs=[
            pl.BlockSpec((None, s, LANES), lambda i: (i, 0, 0)),
            pl.BlockSpec((None, LANES, s), lambda i: (i, 0, 0)),
        ],
        out_shape=[
            jax.ShapeDtypeStruct((b, s, LANES), F32),
            jax.ShapeDtypeStruct((b, LANES, s), F32),
        ],
        compiler_params=pltpu.CompilerParams(
            dimension_semantics=("arbitrary",), vmem_limit_bytes=VMEM_LIMIT),
        name="gates",
    )(gates3, bias)


def _mlstm_kernel(qk_ref, v_ref, o_ref, z_ref, gcol_ref, grow_ref, cw_ref, cb_ref, hg_ref,
                  out_ref, xpad_scr, c_scr, n_scr, m_scr, fprev_scr):
    t = pl.program_id(1)
    L = A_CHUNK
    qkw = A_HEADS * A_DQK

    @pl.when(t == 0)
    def _():
        xpad_scr[0:SUBLANES, :] = jnp.zeros((SUBLANES, 2 * qkw), F32)
        c_scr[...] = jnp.zeros_like(c_scr)
        n_scr[...] = jnp.zeros_like(n_scr)
        m_scr[...] = jnp.zeros_like(m_scr)
        fprev_scr[...] = jnp.zeros_like(fprev_scr)

    xpad_scr[SUBLANES:SUBLANES + L, :] = qk_ref[...].astype(F32)
    y = cb_ref[...] + xpad_scr[SUBLANES:SUBLANES + L, :] * cw_ref[CONV_K - 1:CONV_K, :]
    for d in range(1, CONV_K):
        y = y + xpad_scr[SUBLANES - d:SUBLANES - d + L, :] * cw_ref[CONV_K - 1 - d:CONV_K - d, :]
    xpad_scr[0:SUBLANES, :] = xpad_scr[L:L + SUBLANES, :]
    qk = _silu(y)

    row = lax.broadcasted_iota(jnp.int32, (L, L), 0)
    col = lax.broadcasted_iota(jnp.int32, (L, L), 1)
    causal = row >= col

    for h in range(A_HEADS):
        q = qk[:, h * A_DQK:(h + 1) * A_DQK]
        k = qk[:, qkw + h * A_DQK:qkw + (h + 1) * A_DQK] * (A_DQK ** -0.5)
        qb = q.astype(BF16)
        kb = k.astype(BF16)
        v = v_ref[:, h * A_DV:(h + 1) * A_DV]

        li_c = gcol_ref[:, h:h + 1]
        f_c = gcol_ref[:, A_HEADS + h:A_HEADS + h + 1]
        li_r = grow_ref[h:h + 1, :]
        f_r = grow_ref[A_HEADS + h:A_HEADS + h + 1, :]
        f_prev = fprev_scr[0:1, A_HEADS + h:A_HEADS + h + 1]
        f_end = gcol_ref[L - 1:L, A_HEADS + h:A_HEADS + h + 1]
        m_st = m_scr[h, 0:1, 0:1]
        c_st = c_scr[h]
        n_st = n_scr[h]

        dmat = jnp.where(causal, (f_c - f_r) + li_r, NEG_BIG)
        inter = (f_c - f_prev) + m_st
        m_row = jnp.maximum(inter, jnp.max(dmat, axis=-1, keepdims=True))
        w_intra = jnp.exp(dmat - m_row)
        w_inter = jnp.exp(inter - m_row)
        s = lax.dot_general(qb, kb, (((1,), (1,)), ((), ())), preferred_element_type=F32)
        scores = s * w_intra
        num = (jnp.dot(scores.astype(BF16), v, preferred_element_type=F32)
               + w_inter * jnp.dot(qb, c_st.astype(BF16), preferred_element_type=F32))
        den = (jnp.sum(scores, axis=-1, keepdims=True)
               + w_inter * jnp.sum(q * n_st, axis=-1, keepdims=True))
        hh = num * (1.0 / jnp.maximum(jnp.abs(den), jnp.exp(-m_row)))

        g_tot = f_end - f_prev
        to_end = (f_end - f_c) + li_c
        m_new = jnp.maximum(g_tot + m_st, jnp.max(to_end, axis=0, keepdims=True))
        w_k = jnp.exp(to_end - m_new)
        decay = jnp.exp(g_tot + m_st - m_new)
        kw = k * w_k
        c_scr[h] = decay * c_st + jnp.dot(kw.T.astype(BF16), v, preferred_element_type=F32)
        n_scr[h] = decay * n_st + jnp.sum(kw, axis=0, keepdims=True)
        m_scr[h] = jnp.broadcast_to(m_new, (SUBLANES, LANES))

        hn = hh * lax.rsqrt(jnp.mean(hh * hh, axis=-1, keepdims=True) + EPS)
        hn = hn * hg_ref[:, h * A_DV:(h + 1) * A_DV]
        og = _sigmoid(o_ref[:, h * A_DV:(h + 1) * A_DV].astype(F32))
        zz = _silu(z_ref[:, h * A_DV:(h + 1) * A_DV].astype(F32))
        out_ref[:, h * A_DV:(h + 1) * A_DV] = ((og * hn) * zz).astype(BF16)

    fprev_scr[...] = gcol_ref[L - 1:L, :]


def _mlstm(proj, gcol, grow, conv_w, conv_b, head_g, bsz, seq):
    m = proj.shape[0]
    width = A_HEADS * A_DV
    nt = seq // A_CHUNK
    row_blk = lambda c: pl.BlockSpec((A_CHUNK, width), lambda b, t: (b * nt + t, c))
    return pl.pallas_call(
        _mlstm_kernel,
        grid=(bsz, nt),
        in_specs=[
            row_blk(0), row_blk(1), row_blk(2), row_blk(3),
            pl.BlockSpec((None, A_CHUNK, LANES), lambda b, t: (b, t, 0)),
            pl.BlockSpec((None, SUBLANES, A_CHUNK), lambda b, t: (b, 0, t)),
            pl.BlockSpec((CONV_K, width), lambda b, t: (0, 0)),
            pl.BlockSpec((1, width), lambda b, t: (0, 0)),
            pl.BlockSpec((1, width), lambda b, t: (0, 0)),
        ],
        out_specs=pl.BlockSpec((A_CHUNK, width), lambda b, t: (b * nt + t, 0)),
        out_shape=jax.ShapeDtypeStruct((m, width), BF16),
        scratch_shapes=[
            pltpu.VMEM((A_CHUNK + SUBLANES, width), F32),
            pltpu.VMEM((A_HEADS, A_DQK, A_DV), F32),
            pltpu.VMEM((A_HEADS, 1, A_DQK), F32),
            pltpu.VMEM((A_HEADS, SUBLANES, LANES), F32),
            pltpu.VMEM((1, LANES), F32),
        ],
        compiler_params=pltpu.CompilerParams(
            dimension_semantics=("arbitrary", "arbitrary"), vmem_limit_bytes=VMEM_LIMIT),
        name="mlstm",
    )(proj, proj, proj, proj, gcol, grow, conv_w, conv_b, head_g)


def _fox_kernel(q_ref, k_ref, v_ref, z_ref, f_ref, out_ref, m_scr, l_scr, acc_scr):
    qi = pl.program_id(2)
    T = FOX_T
    lane = lax.broadcasted_iota(jnp.int32, (T, LANES), 1)
    lo = lane < B_DH
    q = q_ref[...] * (B_DH ** -0.5)
    zero = jnp.zeros_like(q)
    qh = (jnp.where(lo, q, zero), jnp.where(lo, zero, q))

    m_scr[...] = jnp.full(m_scr.shape, NEG_BIG, F32)
    l_scr[...] = jnp.zeros_like(l_scr)
    acc_scr[...] = jnp.zeros_like(acc_scr)

    def step(kj, masked):
        k0 = pl.multiple_of(kj * T, T)
        kblk = k_ref[pl.ds(k0, T), :]
        vblk = v_ref[pl.ds(k0, T), :]
        for hh in range(2):
            bias = -f_ref[hh:hh + 1, pl.ds(k0, T)]
            s = lax.dot_general(qh[hh], kblk, (((1,), (1,)), ((), ())),
                                preferred_element_type=F32) + bias
            if masked:
                r = lax.broadcasted_iota(jnp.int32, (T, T), 0)
                c = lax.broadcasted_iota(jnp.int32, (T, T), 1)
                s = jnp.where(r >= c, s, NEG_BIG)
            m_old = m_scr[hh]
            m_new = jnp.maximum(m_old, jnp.max(s, axis=-1, keepdims=True))
            alpha = jnp.exp(m_old - m_new)
            p = jnp.exp(s - m_new)
            l_scr[hh] = alpha * l_scr[hh] + jnp.sum(p, axis=-1, keepdims=True)
            acc_scr[hh] = alpha * acc_scr[hh] + jnp.dot(p.astype(BF16), vblk,
                                                        preferred_element_type=F32)
            m_scr[hh] = m_new

    def body(kj, carry):
        step(kj, False)
        return carry

    lax.fori_loop(0, qi, body, 0)
    step(qi, True)

    o0 = acc_scr[0] * (1.0 / l_scr[0])
    o1 = acc_scr[1] * (1.0 / l_scr[1])
    hb = jnp.where(lo, o0, o1)
    out_ref[...] = (hb * _silu(z_ref[...].astype(F32))).astype(BF16)


def _fox(proj, f_rows, bsz, seq):
    m = proj.shape[0]
    nq = seq // FOX_T
    npair = B_HEADS // 2
    col0 = (A_HEADS * A_DV * 4) // LANES
    step = (B_HEADS * B_DH) // LANES
    return pl.pallas_call(
        _fox_kernel,
        grid=(bsz, npair, nq),
        in_specs=[
            pl.BlockSpec((FOX_T, LANES), lambda b, hp, qi: (b * nq + qi, col0 + hp)),
            pl.BlockSpec((seq, LANES), lambda b, hp, qi: (b, col0 + step + hp)),
            pl.BlockSpec((seq, LANES), lambda b, hp, qi: (b, col0 + 2 * step + hp)),
            pl.BlockSpec((FOX_T, LANES), lambda b, hp, qi: (b * nq + qi, col0 + 3 * step + hp)),
            pl.BlockSpec((None, None, 2, seq), lambda b, hp, qi: (b, hp, 0, 0)),
        ],
        out_specs=pl.BlockSpec((FOX_T, LANES), lambda b, hp, qi: (b * nq + qi, hp)),
        out_shape=jax.ShapeDtypeStruct((m, B_HEADS * B_DH), BF16),
        scratch_shapes=[
            pltpu.VMEM((2, FOX_T, 1), F32),
            pltpu.VMEM((2, FOX_T, 1), F32),
            pltpu.VMEM((2, FOX_T, LANES), F32),
        ],
        compiler_params=pltpu.CompilerParams(
            dimension_semantics=("arbitrary", "arbitrary", "arbitrary"),
            vmem_limit_bytes=VMEM_LIMIT),
        name="fox",
    )(proj, proj, proj, proj, f_rows)


def _merge_kernel(ha_ref, hb_ref, ga_ref, gb_ref, x_ref, p_ref, wa_ref, wb_ref, wo_ref, wg_ref,
                  wp_ref, png_ref, fng_ref, out_ref):
    ya = jnp.dot(ha_ref[...], wa_ref[...], preferred_element_type=F32)
    yb = jnp.dot(hb_ref[...], wb_ref[...], preferred_element_type=F32)
    merged = (_sigmoid(ga_ref[...].astype(F32)) * ya + _sigmoid(gb_ref[...].astype(F32)) * yb)
    x1 = x_ref[...] + jnp.dot(merged.astype(BF16), wo_ref[...], preferred_element_type=F32)
    r = _rms_norm(x1, png_ref[...]).astype(BF16)
    gate = _sigmoid(jnp.dot(r, wg_ref[...], preferred_element_type=F32))
    pp = jnp.dot(p_ref[...].astype(BF16), wp_ref[...], preferred_element_type=F32)
    x2 = x1 + gate * pp
    out_ref[...] = _rms_norm(x2, fng_ref[...])


def _merge(ha, hb, proj, x2, p2, wa, wb, wo, wg, wp, png, fng):
    m, d = x2.shape
    pd = p2.shape[1]
    tm = MERGE_TM
    full = lambda r, c: pl.BlockSpec((r, c), lambda i: (0, 0))
    return pl.pallas_call(
        _merge_kernel,
        grid=(m // tm,),
        in_specs=[
            pl.BlockSpec((tm, d), lambda i: (i, 0)),
            pl.BlockSpec((tm, d), lambda i: (i, 0)),
            pl.BlockSpec((tm, d), lambda i: (i, 8)),
            pl.BlockSpec((tm, d), lambda i: (i, 9)),
            pl.BlockSpec((tm, d), lambda i: (i, 0)),
            pl.BlockSpec((tm, pd), lambda i: (i, 0)),
            full(d, d), full(d, d), full(d, d), full(d, d), full(pd, d),
            full(1, d), full(1, d),
        ],
        out_specs=pl.BlockSpec((tm, d), lambda i: (i, 0)),
        out_shape=jax.ShapeDtypeStruct((m, d), F32),
        compiler_params=pltpu.CompilerParams(
            dimension_semantics=("arbitrary",), vmem_limit_bytes=VMEM_LIMIT),
        name="merge",
    )(ha, hb, proj, proj, x2, p2, wa, wb, wo, wg, wp, png, fng)


def _split_w_in(w):
    qkw = A_HEADS * A_DQK
    aw = A_HEADS * A_DV
    bw = B_HEADS * B_DH
    d = w.shape[0]
    o_ai = 2 * qkw + aw
    o_ao = o_ai + 2 * A_HEADS
    o_bf = o_ao + 2 * aw + 3 * bw
    o_bz = o_bf + B_HEADS
    w_main = jnp.concatenate([w[:, :o_ai], w[:, o_ao:o_bf], w[:, o_bz:]], axis=1)
    n_gate = 2 * A_HEADS + B_HEADS
    w_gate = jnp.concatenate(
        [w[:, o_ai:o_ao], w[:, o_bf:o_bz], jnp.zeros((d, LANES - n_gate), w.dtype)], axis=1)
    return w_main.astype(BF16), w_gate.astype(BF16)


def _layer(x, p_i, attn_norm_g, w_in, conv_w, conv_b, a_bias_i, a_bias_f, a_head_norm_g, b_bias_f,
           w_branch_a, w_branch_b, w_out, ple_norm_g, w_ple_gate, w_ple_proj, out_norm_g):
    bsz, seq, d = x.shape
    m = bsz * seq
    x2 = x.reshape(m, d)
    w_main, w_gate = _split_w_in(w_in)
    proj, gates = _in_proj(x2, attn_norm_g.reshape(1, d), w_main, w_gate)

    n_gate = 2 * A_HEADS + B_HEADS
    bias = jnp.concatenate([a_bias_i, a_bias_f, b_bias_f, jnp.zeros((LANES - n_gate,), F32)])
    gcol, grow = _gates(gates.reshape(bsz, seq, LANES), bias.reshape(1, LANES))

    ha = _mlstm(proj, gcol, grow, conv_w, conv_b.reshape(1, -1), a_head_norm_g.reshape(1, -1),
                bsz, seq)
    f_rows = grow[:, 2 * A_HEADS:n_gate, :].reshape(bsz, B_HEADS // 2, 2, seq)
    hb = _fox(proj, f_rows, bsz, seq)

    out = _merge(ha, hb, proj, x2, p_i.reshape(m, -1),
                 w_branch_a.astype(BF16), w_branch_b.astype(BF16), w_out.astype(BF16),
                 w_ple_gate.astype(BF16), w_ple_proj.astype(BF16),
                 ple_norm_g.reshape(1, d), out_norm_g.reshape(1, d))
    return out.reshape(bsz, seq, d)


def kernel(x, p, attn_norm_g, w_in, conv_w, conv_b, a_bias_i, a_bias_f, a_head_norm_g, b_bias_f,
           w_branch_a, w_branch_b, w_out, ple_norm_g, w_ple_gate, w_ple_proj, final_norm_g):
    depth = w_in.shape[0]
    assert depth == 1, "the final norm is fused into the single layer's merge kernel"
    return _layer(x, p[0], attn_norm_g[0], w_in[0], conv_w[0], conv_b[0], a_bias_i[0], a_bias_f[0],
                  a_head_norm_g[0], b_bias_f[0], w_branch_a[0], w_branch_b[0], w_out[0],
                  ple_norm_g[0], w_ple_gate[0], w_ple_proj[0], final_norm_g)
```

```python
import math

import jax
import jax.numpy as jnp
from jax import lax
from jax.experimental import pallas as pl
from jax.experimental.pallas import tpu as pltpu

F32 = jnp.float32
BF16 = jnp.bfloat16

EPS = 1e-6
A_HEADS = 4
A_DQK = 128
A_DV = 256
CONV_K = 4
B_HEADS = 16
B_DH = 64
LANES = 128
SUBLANES = 8
BF16_ROWS = 16
NEG_BIG = -1e30
LOG2E = math.log2(math.e)

IN_TM = 1024
IN_TN = 1024
CUM_BLK = 256
A_CHUNK = 256
FOX_T = 256
FOX_G = 2
MERGE_TM = 512
VMEM_LIMIT = 48 * 1024 * 1024

N_GATE = 2 * A_HEADS + B_HEADS
B_LANE0 = 2 * A_HEADS
PIECE_OFFS = (B_LANE0, B_LANE0 + B_HEADS, B_LANE0 + 2 * B_HEADS)


def _sigmoid(x):
    return 1.0 / (1.0 + jnp.exp(-x))


def _silu(x):
    return x * _sigmoid(x)


def _rms_norm(x, g):
    ms = jnp.mean(x * x, axis=-1, keepdims=True)
    return (x * lax.rsqrt(ms + EPS)) * g


def _split3(x):
    x1 = x.astype(BF16)
    r1 = x - x1.astype(F32)
    x2 = r1.astype(BF16)
    x3 = (r1 - x2.astype(F32)).astype(BF16)
    return x1, x2, x3


def _in_proj_kernel(x_ref, g_ref, w_ref, wg_ref, proj_ref, gates_ref, h_scr):
    @pl.when(pl.program_id(1) == 0)
    def _():
        h = _rms_norm(x_ref[...], g_ref[...]).astype(BF16)
        h_scr[...] = h
        gates_ref[...] = jnp.dot(h, wg_ref[...], preferred_element_type=F32)

    proj_ref[...] = jnp.dot(h_scr[...], w_ref[...], preferred_element_type=F32).astype(BF16)


def _in_proj(x2, g, w_main, w_gate):
    m, d = x2.shape
    n = w_main.shape[1]
    return pl.pallas_call(
        _in_proj_kernel,
        grid=(m // IN_TM, n // IN_TN),
        in_specs=[
            pl.BlockSpec((IN_TM, d), lambda i, j: (i, 0)),
            pl.BlockSpec((1, d), lambda i, j: (0, 0)),
            pl.BlockSpec((d, IN_TN), lambda i, j: (0, j)),
            pl.BlockSpec((d, LANES), lambda i, j: (0, 0)),
        ],
        out_specs=[
            pl.BlockSpec((IN_TM, IN_TN), lambda i, j: (i, j)),
            pl.BlockSpec((IN_TM, LANES), lambda i, j: (i, 0)),
        ],
        out_shape=[
            jax.ShapeDtypeStruct((m, n), BF16),
            jax.ShapeDtypeStruct((m, LANES), F32),
        ],
        scratch_shapes=[pltpu.VMEM((IN_TM, d), BF16)],
        compiler_params=pltpu.CompilerParams(
            dimension_semantics=("arbitrary", "arbitrary"),
            vmem_limit_bytes=VMEM_LIMIT),
        name="in_proj",
    )(x2, g, w_main, w_gate)


def _gates_kernel(g_ref, bias_ref, col_ref, row_ref, pc_ref):
    x = g_ref[...] + bias_ref[...]
    s = x.shape[0]
    ls = jnp.minimum(x, 0.0) - jnp.log1p(jnp.exp(-jnp.abs(x)))
    r = lax.broadcasted_iota(jnp.int32, (CUM_BLK, CUM_BLK), 0)
    c = lax.broadcasted_iota(jnp.int32, (CUM_BLK, CUM_BLK), 1)
    tri = jnp.where(r >= c, 1.0, 0.0).astype(BF16)
    carry = jnp.zeros((1, LANES), F32)
    blocks = []
    for blk in range(s // CUM_BLK):
        x1, x2, x3 = _split3(ls[blk * CUM_BLK:(blk + 1) * CUM_BLK])
        cs = (jnp.dot(tri, x3, preferred_element_type=F32)
              + jnp.dot(tri, x2, preferred_element_type=F32)
              + jnp.dot(tri, x1, preferred_element_type=F32)) + carry
        carry = cs[CUM_BLK - 1:CUM_BLK, :]
        blocks.append(cs)
    cum = jnp.concatenate(blocks, axis=0)
    lane = lax.broadcasted_iota(jnp.int32, x.shape, 1)
    res = jnp.where(lane < A_HEADS, x, cum)
    col_ref[...] = res
    row_ref[...] = res.T[0:SUBLANES, :]

    in_b = (lane >= B_LANE0) & (lane < B_LANE0 + B_HEADS)
    p1, p2, p3 = _split3(jnp.where(in_b, cum * (-LOG2E), 0.0))
    pieces = (p1.astype(F32)
              + pltpu.roll(p2.astype(F32), PIECE_OFFS[1] - B_LANE0, axis=1)
              + pltpu.roll(p3.astype(F32), PIECE_OFFS[2] - B_LANE0, axis=1))
    pc_ref[...] = pieces.astype(BF16)


def _gates(gates3, bias):
    b, s, _ = gates3.shape
    return pl.pallas_call(
        _gates_kernel,
        grid=(b,),
        in_specs=[
            pl.BlockSpec((None, s, LANES), lambda i: (i, 0, 0)),
            pl.BlockSpec((1, LANES), lambda i: (0, 0)),
        ],
        out_specs=[
            pl.BlockSpec((None, s, LANES), lambda i: (i, 0, 0)),
            pl.BlockSpec((None, SUBLANES, s), lambda i: (i, 0, 0)),
            pl.BlockSpec((None, s, LANES), lambda i: (i, 0, 0)),
        ],
        out_shape=[
            jax.ShapeDtypeStruct((b, s, LANES), F32),
            jax.ShapeDtypeStruct((b, SUBLANES, s), F32),
            jax.ShapeDtypeStruct((b, s, LANES), BF16),
        ],
        compiler_params=pltpu.CompilerParams(
            dimension_semantics=("arbitrary",), vmem_limit_bytes=VMEM_LIMIT),
        name="gates",
    )(gates3, bias)


def _mlstm_kernel(qk_ref, v_ref, o_ref, z_ref, gcol_ref, grow_ref, cw_ref, cb_ref, hg_ref,
                  out_ref, xpad_scr, c_scr, n_scr, m_scr, fprev_scr):
    t = pl.program_id(1)
    L = A_CHUNK
    qkw = A_HEADS * A_DQK

    @pl.when(t == 0)
    def _():
        xpad_scr[0:SUBLANES, :] = jnp.zeros((SUBLANES, 2 * qkw), F32)
        c_scr[...] = jnp.zeros_like(c_scr)
        n_scr[...] = jnp.zeros_like(n_scr)
        m_scr[...] = jnp.zeros_like(m_scr)
        fprev_scr[...] = jnp.zeros_like(fprev_scr)

    xpad_scr[SUBLANES:SUBLANES + L, :] = qk_ref[...].astype(F32)
    y = cb_ref[...] + xpad_scr[SUBLANES:SUBLANES + L, :] * cw_ref[CONV_K - 1:CONV_K, :]
    for d in range(1, CONV_K):
        y = y + xpad_scr[SUBLANES - d:SUBLANES - d + L, :] * cw_ref[CONV_K - 1 - d:CONV_K - d, :]
    xpad_scr[0:SUBLANES, :] = xpad_scr[L:L + SUBLANES, :]
    qk = _silu(y)

    row = lax.broadcasted_iota(jnp.int32, (L, L), 0)
    col = lax.broadcasted_iota(jnp.int32, (L, L), 1)
    causal = row >= col

    for h in range(A_HEADS):
        q = qk[:, h * A_DQK:(h + 1) * A_DQK]
        k = qk[:, qkw + h * A_DQK:qkw + (h + 1) * A_DQK] * (A_DQK ** -0.5)
        qb = q.astype(BF16)
        kb = k.astype(BF16)
        v = v_ref[:, h * A_DV:(h + 1) * A_DV]

        li_c = gcol_ref[:, h:h + 1]
        f_c = gcol_ref[:, A_HEADS + h:A_HEADS + h + 1]
        li_r = grow_ref[h:h + 1, :]
        f_r = grow_ref[A_HEADS + h:A_HEADS + h + 1, :]
        f_prev = fprev_scr[0:1, A_HEADS + h:A_HEADS + h + 1]
        f_end = gcol_ref[L - 1:L, A_HEADS + h:A_HEADS + h + 1]
        m_st = m_scr[h, 0:1, 0:1]
        c_st = c_scr[h]
        n_st = n_scr[h]

        dmat = jnp.where(causal, (f_c - f_r) + li_r, NEG_BIG)
        inter = (f_c - f_prev) + m_st
        m_row = jnp.maximum(inter, jnp.max(dmat, axis=-1, keepdims=True))
        w_intra = jnp.exp(dmat - m_row)
        w_inter = jnp.exp(inter - m_row)
        s = lax.dot_general(qb, kb, (((1,), (1,)), ((), ())), preferred_element_type=F32)
        scores = s * w_intra
        num = (jnp.dot(scores.astype(BF16), v, preferred_element_type=F32)
               + w_inter * jnp.dot(qb, c_st.astype(BF16), preferred_element_type=F32))
        den = (jnp.sum(scores, axis=-1, keepdims=True)
               + w_inter * jnp.sum(q * n_st, axis=-1, keepdims=True))
        hh = num * (1.0 / jnp.maximum(jnp.abs(den), jnp.exp(-m_row)))

        g_tot = f_end - f_prev
        to_end = (f_end - f_c) + li_c
        m_new = jnp.maximum(g_tot + m_st, jnp.max(to_end, axis=0, keepdims=True))
        w_k = jnp.exp(to_end - m_new)
        decay = jnp.exp(g_tot + m_st - m_new)
        kw = k * w_k
        c_scr[h] = decay * c_st + jnp.dot(kw.T.astype(BF16), v, preferred_element_type=F32)
        n_scr[h] = decay * n_st + jnp.sum(kw, axis=0, keepdims=True)
        m_scr[h] = jnp.broadcast_to(m_new, (SUBLANES, LANES))

        hn = hh * lax.rsqrt(jnp.mean(hh * hh, axis=-1, keepdims=True) + EPS)
        hn = hn * hg_ref[:, h * A_DV:(h + 1) * A_DV]
        og = _sigmoid(o_ref[:, h * A_DV:(h + 1) * A_DV].astype(F32))
        zz = _silu(z_ref[:, h * A_DV:(h + 1) * A_DV].astype(F32))
        out_ref[:, h * A_DV:(h + 1) * A_DV] = ((og * hn) * zz).astype(BF16)

    fprev_scr[...] = gcol_ref[L - 1:L, :]


def _mlstm(proj, gcol, grow, conv_w, conv_b, head_g, bsz, seq):
    m = proj.shape[0]
    width = A_HEADS * A_DV
    nt = seq // A_CHUNK
    row_blk = lambda c: pl.BlockSpec((A_CHUNK, width), lambda b, t: (b * nt + t, c))
    return pl.pallas_call(
        _mlstm_kernel,
        grid=(bsz, nt),
        in_specs=[
            row_blk(0), row_blk(1), row_blk(2), row_blk(3),
            pl.BlockSpec((None, A_CHUNK, LANES), lambda b, t: (b, t, 0)),
            pl.BlockSpec((None, SUBLANES, A_CHUNK), lambda b, t: (b, 0, t)),
            pl.BlockSpec((CONV_K, width), lambda b, t: (0, 0)),
            pl.BlockSpec((1, width), lambda b, t: (0, 0)),
            pl.BlockSpec((1, width), lambda b, t: (0, 0)),
        ],
        out_specs=pl.BlockSpec((A_CHUNK, width), lambda b, t: (b * nt + t, 0)),
        out_shape=jax.ShapeDtypeStruct((m, width), BF16),
        scratch_shapes=[
            pltpu.VMEM((A_CHUNK + SUBLANES, width), F32),
            pltpu.VMEM((A_HEADS, A_DQK, A_DV), F32),
            pltpu.VMEM((A_HEADS, 1, A_DQK), F32),
            pltpu.VMEM((A_HEADS, SUBLANES, LANES), F32),
            pltpu.VMEM((1, LANES), F32),
        ],
        compiler_params=pltpu.CompilerParams(
            dimension_semantics=("arbitrary", "arbitrary"), vmem_limit_bytes=VMEM_LIMIT),
        name="mlstm",
    )(proj, proj, proj, proj, gcol, grow, conv_w, conv_b, head_g)


FOX_ACC_ROWS = B_DH + BF16_ROWS


def _fox_kernel(q_ref, k_ref, v_ref, z_ref, pc_ref, out_ref, vt_scr, acc_scr):
    hg = pl.program_id(1)
    qi = pl.program_id(2)
    T = FOX_T
    seq = k_ref.shape[0]
    nh = 2 * FOX_G

    @pl.when(qi == 0)
    def _():
        for g in range(FOX_G):
            vt = v_ref[:, g * LANES:(g + 1) * LANES].astype(F32).T
            for hh in range(2):
                vt_scr[2 * g + hh, 0:B_DH, :] = vt[hh * B_DH:(hh + 1) * B_DH, :].astype(BF16)
                vt_scr[2 * g + hh, B_DH:FOX_ACC_ROWS, :] = jnp.ones((BF16_ROWS, seq), BF16)

    row = lax.broadcasted_iota(jnp.int32, (LANES, T), 0)
    rhs = []
    for g in range(FOX_G):
        qt = (q_ref[:, g * LANES:(g + 1) * LANES].astype(F32) * (B_DH ** -0.5 * LOG2E)).T
        for hh in range(2):
            head = hg * nh + 2 * g + hh
            qm = jnp.where((row >= hh * B_DH) & (row < (hh + 1) * B_DH), qt, 0.0)
            sel = jnp.where((row == PIECE_OFFS[0] + head) | (row == PIECE_OFFS[1] + head)
                            | (row == PIECE_OFFS[2] + head), 1.0, 0.0)
            rhs.append(jnp.concatenate([qm, sel], axis=0).astype(BF16))

    acc_scr[...] = jnp.zeros_like(acc_scr)

    def step(kj, ms, masked):
        k0 = pl.multiple_of(kj * T, T)
        pcs = pc_ref[pl.ds(k0, T), :]
        ss = []
        for g in range(FOX_G):
            lhs = jnp.concatenate([k_ref[pl.ds(k0, T), g * LANES:(g + 1) * LANES], pcs], axis=1)
            for hh in range(2):
                ss.append(jnp.dot(lhs, rhs[2 * g + hh], preferred_element_type=F32))
        out = []
        for h in range(nh):
            s = ss[h]
            if masked:
                r = lax.broadcasted_iota(jnp.int32, (T, T), 0)
                c = lax.broadcasted_iota(jnp.int32, (T, T), 1)
                s = jnp.where(c >= r, s, NEG_BIG)
            m_new = jnp.maximum(ms[h], jnp.max(s, axis=0, keepdims=True))
            alpha = jnp.exp2(ms[h] - m_new)
            p = jnp.exp2(s - m_new).astype(BF16)
            pv = jnp.dot(vt_scr[h, :, pl.ds(k0, T)], p, preferred_element_type=F32)
            acc_scr[h] = alpha * acc_scr[h] + pv
            out.append(m_new)
        return tuple(out)

    ms = tuple(jnp.full((1, T), NEG_BIG, F32) for _ in range(nh))
    ms = lax.fori_loop(0, qi, lambda kj, c: step(kj, c, False), ms)
    step(qi, ms, True)

    for g in range(FOX_G):
        parts = []
        for hh in range(2):
            a = acc_scr[2 * g + hh]
            parts.append(a[0:B_DH, :] * (1.0 / a[B_DH:B_DH + 1, :]))
        o = jnp.concatenate(parts, axis=0).T
        zz = _silu(z_ref[:, g * LANES:(g + 1) * LANES].astype(F32))
        out_ref[:, g * LANES:(g + 1) * LANES] = (o * zz).astype(BF16)


def _fox(proj, pieces, bsz, seq):
    m = proj.shape[0]
    nq = seq // FOX_T
    w = FOX_G * LANES
    ngrp = (B_HEADS * B_DH) // w
    col0 = (A_HEADS * A_DV * 4) // w
    sec = (B_HEADS * B_DH) // w
    return pl.pallas_call(
        _fox_kernel,
        grid=(bsz, ngrp, nq),
        in_specs=[
            pl.BlockSpec((FOX_T, w), lambda b, hg, qi: (b * nq + qi, col0 + hg)),
            pl.BlockSpec((seq, w), lambda b, hg, qi: (b, col0 + sec + hg)),
            pl.BlockSpec((seq, w), lambda b, hg, qi: (b, col0 + 2 * sec + hg)),
            pl.BlockSpec((FOX_T, w), lambda b, hg, qi: (b * nq + qi, col0 + 3 * sec + hg)),
            pl.BlockSpec((None, seq, LANES), lambda b, hg, qi: (b, 0, 0)),
        ],
        out_specs=pl.BlockSpec((FOX_T, w), lambda b, hg, qi: (b * nq + qi, hg)),
        out_shape=jax.ShapeDtypeStruct((m, B_HEADS * B_DH), BF16),
        scratch_shapes=[
            pltpu.VMEM((2 * FOX_G, FOX_ACC_ROWS, seq), BF16),
            pltpu.VMEM((2 * FOX_G, FOX_ACC_ROWS, FOX_T), F32),
        ],
        compiler_params=pltpu.CompilerParams(
            dimension_semantics=("arbitrary", "arbitrary", "arbitrary"),
            vmem_limit_bytes=VMEM_LIMIT),
        name="fox",
    )(proj, proj, proj, proj, pieces)


def _merge_kernel(ha_ref, hb_ref, ga_ref, gb_ref, x_ref, p_ref, wa_ref, wb_ref, wo_ref, wg_ref,
                  wp_ref, png_ref, fng_ref, out_ref):
    ya = jnp.dot(ha_ref[...], wa_ref[...], preferred_element_type=F32)
    yb = jnp.dot(hb_ref[...], wb_ref[...], preferred_element_type=F32)
    merged = (_sigmoid(ga_ref[...].astype(F32)) * ya + _sigmoid(gb_ref[...].astype(F32)) * yb)
    x1 = x_ref[...] + jnp.dot(merged.astype(BF16), wo_ref[...], preferred_element_type=F32)
    r = _rms_norm(x1, png_ref[...]).astype(BF16)
    gate = _sigmoid(jnp.dot(r, wg_ref[...], preferred_element_type=F32))
    pp = jnp.dot(p_ref[...].astype(BF16), wp_ref[...], preferred_element_type=F32)
    x2 = x1 + gate * pp
    out_ref[...] = _rms_norm(x2, fng_ref[...])


def _merge(ha, hb, proj, x2, p2, wa, wb, wo, wg, wp, png, fng):
    m, d = x2.shape
    pd = p2.shape[1]
    tm = MERGE_TM
    full = lambda r, c: pl.BlockSpec((r, c), lambda i: (0, 0))
    return pl.pallas_call(
        _merge_kernel,
        grid=(m // tm,),
        in_specs=[
            pl.BlockSpec((tm, d), lambda i: (i, 0)),
            pl.BlockSpec((tm, d), lambda i: (i, 0)),
            pl.BlockSpec((tm, d), lambda i: (i, 8)),
            pl.BlockSpec((tm, d), lambda i: (i, 9)),
            pl.BlockSpec((tm, d), lambda i: (i, 0)),
            pl.BlockSpec((tm, pd), lambda i: (i, 0)),
            full(d, d), full(d, d), full(d, d), full(d, d), full(pd, d),
            full(1, d), full(1, d),
        ],
        out_specs=pl.BlockSpec((tm, d), lambda i: (i, 0)),
        out_shape=jax.ShapeDtypeStruct((m, d), F32),
        compiler_params=pltpu.CompilerParams(
            dimension_semantics=("arbitrary",), vmem_limit_bytes=VMEM_LIMIT),
        name="merge",
    )(ha, hb, proj, proj, x2, p2, wa, wb, wo, wg, wp, png, fng)


def _split_w_in(w):
    qkw = A_HEADS * A_DQK
    aw = A_HEADS * A_DV
    bw = B_HEADS * B_DH
    d = w.shape[0]
    o_ai = 2 * qkw + aw
    o_ao = o_ai + 2 * A_HEADS
    o_bf = o_ao + 2 * aw + 3 * bw
    o_bz = o_bf + B_HEADS
    w_main = jnp.concatenate([w[:, :o_ai], w[:, o_ao:o_bf], w[:, o_bz:]], axis=1)
    w_gate = jnp.concatenate(
        [w[:, o_ai:o_ao], w[:, o_bf:o_bz], jnp.zeros((d, LANES - N_GATE), w.dtype)], axis=1)
    return w_main.astype(BF16), w_gate.astype(BF16)


def _layer(x, p_i, attn_norm_g, w_in, conv_w, conv_b, a_bias_i, a_bias_f, a_head_norm_g, b_bias_f,
           w_branch_a, w_branch_b, w_out, ple_norm_g, w_ple_gate, w_ple_proj, out_norm_g):
    bsz, seq, d = x.shape
    m = bsz * seq
    x2 = x.reshape(m, d)
    w_main, w_gate = _split_w_in(w_in)
    proj, gates = _in_proj(x2, attn_norm_g.reshape(1, d), w_main, w_gate)

    bias = jnp.concatenate([a_bias_i, a_bias_f, b_bias_f, jnp.zeros((LANES - N_GATE,), F32)])
    gcol, grow, pieces = _gates(gates.reshape(bsz, seq, LANES), bias.reshape(1, LANES))

    ha = _mlstm(proj, gcol, grow, conv_w, conv_b.reshape(1, -1), a_head_norm_g.reshape(1, -1),
                bsz, seq)
    hb = _fox(proj, pieces, bsz, seq)

    out = _merge(ha, hb, proj, x2, p_i.reshape(m, -1),
                 w_branch_a.astype(BF16), w_branch_b.astype(BF16), w_out.astype(BF16),
                 w_ple_gate.astype(BF16), w_ple_proj.astype(BF16),
                 ple_norm_g.reshape(1, d), out_norm_g.reshape(1, d))
    return out.reshape(bsz, seq, d)


def kernel(x, p, attn_norm_g, w_in, conv_w, conv_b, a_bias_i, a_bias_f, a_head_norm_g, b_bias_f,
           w_branch_a, w_branch_b, w_out, ple_norm_g, w_ple_gate, w_ple_proj, final_norm_g):
    depth = w_in.shape[0]
    assert depth == 1, "the final norm is fused into the single layer's merge kernel"
    return _layer(x, p[0], attn_norm_g[0], w_in[0], conv_w[0], conv_b[0], a_bias_i[0], a_bias_f[0],
                  a_head_norm_g[0], b_bias_f[0], w_branch_a[0], w_branch_b[0], w_out[0],
                  ple_norm_g[0], w_ple_gate[0], w_ple_proj[0], final_norm_g)
```

```python
import math

import jax
import jax.numpy as jnp
from jax import lax
from jax.experimental import pallas as pl
from jax.experimental.pallas import tpu as pltpu

F32 = jnp.float32
BF16 = jnp.bfloat16

EPS = 1e-6
A_HEADS = 4
A_DQK = 128
A_DV = 256
CONV_K = 4
B_HEADS = 16
B_DH = 64
LANES = 128
SUBLANES = 8
BF16_ROWS = 16
NEG_BIG = -1e30
LOG2E = math.log2(math.e)

IN_TM = 2048
IN_TN = 512
CUM_BLK = 256
A_CHUNK = 256
FOX_T = 256
FOX_G = 4
MERGE_TM = 512
VMEM_LIMIT = 48 * 1024 * 1024

N_GATE = 2 * A_HEADS + B_HEADS
B_LANE0 = 2 * A_HEADS
PIECE_OFFS = (B_LANE0, B_LANE0 + B_HEADS, B_LANE0 + 2 * B_HEADS)


def _sigmoid(x):
    return 1.0 / (1.0 + jnp.exp(-x))


def _silu(x):
    return x * _sigmoid(x)


def _rms_norm(x, g):
    ms = jnp.mean(x * x, axis=-1, keepdims=True)
    return (x * lax.rsqrt(ms + EPS)) * g


def _split3(x):
    x1 = x.astype(BF16)
    r1 = x - x1.astype(F32)
    x2 = r1.astype(BF16)
    x3 = (r1 - x2.astype(F32)).astype(BF16)
    return x1, x2, x3


def _in_proj_kernel(x_ref, g_ref, w_ref, wg_ref, proj_ref, gates_ref, h_scr):
    @pl.when(pl.program_id(1) == 0)
    def _():
        h = _rms_norm(x_ref[...], g_ref[...]).astype(BF16)
        h_scr[...] = h
        gates_ref[...] = jnp.dot(h, wg_ref[...], preferred_element_type=F32)

    proj_ref[...] = jnp.dot(h_scr[...], w_ref[...], preferred_element_type=F32).astype(BF16)


def _in_proj(x2, g, w_main, w_gate):
    m, d = x2.shape
    n = w_main.shape[1]
    return pl.pallas_call(
        _in_proj_kernel,
        grid=(m // IN_TM, n // IN_TN),
        in_specs=[
            pl.BlockSpec((IN_TM, d), lambda i, j: (i, 0)),
            pl.BlockSpec((1, d), lambda i, j: (0, 0)),
            pl.BlockSpec((d, IN_TN), lambda i, j: (0, j)),
            pl.BlockSpec((d, LANES), lambda i, j: (0, 0)),
        ],
        out_specs=[
            pl.BlockSpec((IN_TM, IN_TN), lambda i, j: (i, j)),
            pl.BlockSpec((IN_TM, LANES), lambda i, j: (i, 0)),
        ],
        out_shape=[
            jax.ShapeDtypeStruct((m, n), BF16),
            jax.ShapeDtypeStruct((m, LANES), F32),
        ],
        scratch_shapes=[pltpu.VMEM((IN_TM, d), BF16)],
        compiler_params=pltpu.CompilerParams(
            dimension_semantics=("arbitrary", "arbitrary"),
            vmem_limit_bytes=VMEM_LIMIT),
        name="in_proj",
    )(x2, g, w_main, w_gate)


def _gates_kernel(g_ref, bias_ref, col_ref, row_ref, pc_ref):
    x = g_ref[...] + bias_ref[...]
    s = x.shape[0]
    ls = jnp.minimum(x, 0.0) - jnp.log1p(jnp.exp(-jnp.abs(x)))
    r = lax.broadcasted_iota(jnp.int32, (CUM_BLK, CUM_BLK), 0)
    c = lax.broadcasted_iota(jnp.int32, (CUM_BLK, CUM_BLK), 1)
    tri = jnp.where(r >= c, 1.0, 0.0).astype(BF16)
    carry = jnp.zeros((1, LANES), F32)
    blocks = []
    for blk in range(s // CUM_BLK):
        x1, x2, x3 = _split3(ls[blk * CUM_BLK:(blk + 1) * CUM_BLK])
        cs = (jnp.dot(tri, x3, preferred_element_type=F32)
              + jnp.dot(tri, x2, preferred_element_type=F32)
              + jnp.dot(tri, x1, preferred_element_type=F32)) + carry
        carry = cs[CUM_BLK - 1:CUM_BLK, :]
        blocks.append(cs)
    cum = jnp.concatenate(blocks, axis=0)
    lane = lax.broadcasted_iota(jnp.int32, x.shape, 1)
    res = jnp.where(lane < A_HEADS, x, cum)
    col_ref[...] = res
    row_ref[...] = res.T[0:SUBLANES, :]

    in_b = (lane >= B_LANE0) & (lane < B_LANE0 + B_HEADS)
    p1, p2, p3 = _split3(jnp.where(in_b, cum * (-LOG2E), 0.0))
    pieces = (p1.astype(F32)
              + pltpu.roll(p2.astype(F32), PIECE_OFFS[1] - B_LANE0, axis=1)
              + pltpu.roll(p3.astype(F32), PIECE_OFFS[2] - B_LANE0, axis=1))
    pc_ref[...] = pieces.astype(BF16)


def _gates(gates3, bias):
    b, s, _ = gates3.shape
    return pl.pallas_call(
        _gates_kernel,
        grid=(b,),
        in_specs=[
            pl.BlockSpec((None, s, LANES), lambda i: (i, 0, 0)),
            pl.BlockSpec((1, LANES), lambda i: (0, 0)),
        ],
        out_specs=[
            pl.BlockSpec((None, s, LANES), lambda i: (i, 0, 0)),
            pl.BlockSpec((None, SUBLANES, s), lambda i: (i, 0, 0)),
            pl.BlockSpec((None, s, LANES), lambda i: (i, 0, 0)),
        ],
        out_shape=[
            jax.ShapeDtypeStruct((b, s, LANES), F32),
            jax.ShapeDtypeStruct((b, SUBLANES, s), F32),
            jax.ShapeDtypeStruct((b, s, LANES), BF16),
        ],
        compiler_params=pltpu.CompilerParams(
            dimension_semantics=("arbitrary",), vmem_limit_bytes=VMEM_LIMIT),
        name="gates",
    )(gates3, bias)


def _mlstm_kernel(qk_ref, v_ref, o_ref, z_ref, gcol_ref, grow_ref, cw_ref, cb_ref, hg_ref,
                  out_ref, xpad_scr, c_scr, n_scr, m_scr, fprev_scr):
    t = pl.program_id(1)
    L = A_CHUNK
    qkw = A_HEADS * A_DQK

    @pl.when(t == 0)
    def _():
        xpad_scr[0:SUBLANES, :] = jnp.zeros((SUBLANES, 2 * qkw), F32)
        c_scr[...] = jnp.zeros_like(c_scr)
        n_scr[...] = jnp.zeros_like(n_scr)
        m_scr[...] = jnp.zeros_like(m_scr)
        fprev_scr[...] = jnp.zeros_like(fprev_scr)

    xpad_scr[SUBLANES:SUBLANES + L, :] = qk_ref[...].astype(F32)
    y = cb_ref[...] + xpad_scr[SUBLANES:SUBLANES + L, :] * cw_ref[CONV_K - 1:CONV_K, :]
    for d in range(1, CONV_K):
        y = y + xpad_scr[SUBLANES - d:SUBLANES - d + L, :] * cw_ref[CONV_K - 1 - d:CONV_K - d, :]
    xpad_scr[0:SUBLANES, :] = xpad_scr[L:L + SUBLANES, :]
    qk = _silu(y)

    row = lax.broadcasted_iota(jnp.int32, (L, L), 0)
    col = lax.broadcasted_iota(jnp.int32, (L, L), 1)
    causal = row >= col

    for h in range(A_HEADS):
        q = qk[:, h * A_DQK:(h + 1) * A_DQK]
        k = qk[:, qkw + h * A_DQK:qkw + (h + 1) * A_DQK] * (A_DQK ** -0.5)
        qb = q.astype(BF16)
        kb = k.astype(BF16)
        v = v_ref[:, h * A_DV:(h + 1) * A_DV]

        li_c = gcol_ref[:, h:h + 1]
        f_c = gcol_ref[:, A_HEADS + h:A_HEADS + h + 1]
        li_r = grow_ref[h:h + 1, :]
        f_r = grow_ref[A_HEADS + h:A_HEADS + h + 1, :]
        f_prev = fprev_scr[0:1, A_HEADS + h:A_HEADS + h + 1]
        f_end = gcol_ref[L - 1:L, A_HEADS + h:A_HEADS + h + 1]
        m_st = m_scr[h, 0:1, 0:1]
        c_st = c_scr[h]
        n_st = n_scr[h]

        dmat = jnp.where(causal, (f_c - f_r) + li_r, NEG_BIG)
        inter = (f_c - f_prev) + m_st
        m_row = jnp.maximum(inter, jnp.max(dmat, axis=-1, keepdims=True))
        w_intra = jnp.exp(dmat - m_row)
        w_inter = jnp.exp(inter - m_row)
        s = lax.dot_general(qb, kb, (((1,), (1,)), ((), ())), preferred_element_type=F32)
        scores = s * w_intra
        num = (jnp.dot(scores.astype(BF16), v, preferred_element_type=F32)
               + w_inter * jnp.dot(qb, c_st.astype(BF16), preferred_element_type=F32))
        den = (jnp.sum(scores, axis=-1, keepdims=True)
               + w_inter * jnp.sum(q * n_st, axis=-1, keepdims=True))
        hh = num * (1.0 / jnp.maximum(jnp.abs(den), jnp.exp(-m_row)))

        g_tot = f_end - f_prev
        to_end = (f_end - f_c) + li_c
        m_new = jnp.maximum(g_tot + m_st, jnp.max(to_end, axis=0, keepdims=True))
        w_k = jnp.exp(to_end - m_new)
        decay = jnp.exp(g_tot + m_st - m_new)
        kw = k * w_k
        c_scr[h] = decay * c_st + jnp.dot(kw.T.astype(BF16), v, preferred_element_type=F32)
        n_scr[h] = decay * n_st + jnp.sum(kw, axis=0, keepdims=True)
        m_scr[h] = jnp.broadcast_to(m_new, (SUBLANES, LANES))

        hn = hh * lax.rsqrt(jnp.mean(hh * hh, axis=-1, keepdims=True) + EPS)
        hn = hn * hg_ref[:, h * A_DV:(h + 1) * A_DV]
        og = _sigmoid(o_ref[:, h * A_DV:(h + 1) * A_DV].astype(F32))
        zz = _silu(z_ref[:, h * A_DV:(h + 1) * A_DV].astype(F32))
        out_ref[:, h * A_DV:(h + 1) * A_DV] = ((og * hn) * zz).astype(BF16)

    fprev_scr[...] = gcol_ref[L - 1:L, :]


def _mlstm(proj, gcol, grow, conv_w, conv_b, head_g, bsz, seq):
    m = proj.shape[0]
    width = A_HEADS * A_DV
    nt = seq // A_CHUNK
    row_blk = lambda c: pl.BlockSpec((A_CHUNK, width), lambda b, t: (b * nt + t, c))
    return pl.pallas_call(
        _mlstm_kernel,
        grid=(bsz, nt),
        in_specs=[
            row_blk(0), row_blk(1), row_blk(2), row_blk(3),
            pl.BlockSpec((None, A_CHUNK, LANES), lambda b, t: (b, t, 0)),
            pl.BlockSpec((None, SUBLANES, A_CHUNK), lambda b, t: (b, 0, t)),
            pl.BlockSpec((CONV_K, width), lambda b, t: (0, 0)),
            pl.BlockSpec((1, width), lambda b, t: (0, 0)),
            pl.BlockSpec((1, width), lambda b, t: (0, 0)),
        ],
        out_specs=pl.BlockSpec((A_CHUNK, width), lambda b, t: (b * nt + t, 0)),
        out_shape=jax.ShapeDtypeStruct((m, width), BF16),
        scratch_shapes=[
            pltpu.VMEM((A_CHUNK + SUBLANES, width), F32),
            pltpu.VMEM((A_HEADS, A_DQK, A_DV), F32),
            pltpu.VMEM((A_HEADS, 1, A_DQK), F32),
            pltpu.VMEM((A_HEADS, SUBLANES, LANES), F32),
            pltpu.VMEM((1, LANES), F32),
        ],
        compiler_params=pltpu.CompilerParams(
            dimension_semantics=("arbitrary", "arbitrary"), vmem_limit_bytes=VMEM_LIMIT),
        name="mlstm",
    )(proj, proj, proj, proj, gcol, grow, conv_w, conv_b, head_g)


FOX_ACC_ROWS = B_DH + BF16_ROWS


def _fox_kernel(q_ref, k_ref, v_ref, z_ref, pc_ref, out_ref, vt_scr, acc_scr):
    hg = pl.program_id(1)
    qi = pl.program_id(2)
    T = FOX_T
    seq = k_ref.shape[0]
    nh = 2 * FOX_G

    @pl.when(qi == 0)
    def _():
        for g in range(FOX_G):
            vt = v_ref[:, g * LANES:(g + 1) * LANES].astype(F32).T
            for hh in range(2):
                vt_scr[2 * g + hh, 0:B_DH, :] = vt[hh * B_DH:(hh + 1) * B_DH, :].astype(BF16)
                vt_scr[2 * g + hh, B_DH:FOX_ACC_ROWS, :] = jnp.ones((BF16_ROWS, seq), BF16)

    row = lax.broadcasted_iota(jnp.int32, (LANES, T), 0)
    rhs = []
    for g in range(FOX_G):
        qt = (q_ref[:, g * LANES:(g + 1) * LANES].astype(F32) * (B_DH ** -0.5 * LOG2E)).T
        for hh in range(2):
            head = hg * nh + 2 * g + hh
            qm = jnp.where((row >= hh * B_DH) & (row < (hh + 1) * B_DH), qt, 0.0)
            sel = jnp.where((row == PIECE_OFFS[0] + head) | (row == PIECE_OFFS[1] + head)
                            | (row == PIECE_OFFS[2] + head), 1.0, 0.0)
            rhs.append(jnp.concatenate([qm, sel], axis=0).astype(BF16))

    acc_scr[...] = jnp.zeros_like(acc_scr)

    def step(kj, ms, masked):
        k0 = pl.multiple_of(kj * T, T)
        pcs = pc_ref[pl.ds(k0, T), :]
        ss = []
        for g in range(FOX_G):
            lhs = jnp.concatenate([k_ref[pl.ds(k0, T), g * LANES:(g + 1) * LANES], pcs], axis=1)
            for hh in range(2):
                ss.append(jnp.dot(lhs, rhs[2 * g + hh], preferred_element_type=F32))
        out = []
        for h in range(nh):
            s = ss[h]
            if masked:
                r = lax.broadcasted_iota(jnp.int32, (T, T), 0)
                c = lax.broadcasted_iota(jnp.int32, (T, T), 1)
                s = jnp.where(c >= r, s, NEG_BIG)
            m_new = jnp.maximum(ms[h], jnp.max(s, axis=0, keepdims=True))
            alpha = jnp.exp2(ms[h] - m_new)
            p = jnp.exp2(s - m_new).astype(BF16)
            pv = jnp.dot(vt_scr[h, :, pl.ds(k0, T)], p, preferred_element_type=F32)
            acc_scr[h] = alpha * acc_scr[h] + pv
            out.append(m_new)
        return tuple(out)

    ms = tuple(jnp.full((1, T), NEG_BIG, F32) for _ in range(nh))
    ms = lax.fori_loop(0, qi, lambda kj, c: step(kj, c, False), ms)
    step(qi, ms, True)

    for g in range(FOX_G):
        parts = []
        for hh in range(2):
            a = acc_scr[2 * g + hh]
            parts.append(a[0:B_DH, :] * (1.0 / a[B_DH:B_DH + 1, :]))
        o = jnp.concatenate(parts, axis=0).T
        zz = _silu(z_ref[:, g * LANES:(g + 1) * LANES].astype(F32))
        out_ref[:, g * LANES:(g + 1) * LANES] = (o * zz).astype(BF16)


def _fox(proj, pieces, bsz, seq):
    m = proj.shape[0]
    nq = seq // FOX_T
    w = FOX_G * LANES
    ngrp = (B_HEADS * B_DH) // w
    col0 = (A_HEADS * A_DV * 4) // w
    sec = (B_HEADS * B_DH) // w
    return pl.pallas_call(
        _fox_kernel,
        grid=(bsz, ngrp, nq),
        in_specs=[
            pl.BlockSpec((FOX_T, w), lambda b, hg, qi: (b * nq + qi, col0 + hg)),
            pl.BlockSpec((seq, w), lambda b, hg, qi: (b, col0 + sec + hg)),
            pl.BlockSpec((seq, w), lambda b, hg, qi: (b, col0 + 2 * sec + hg)),
            pl.BlockSpec((FOX_T, w), lambda b, hg, qi: (b * nq + qi, col0 + 3 * sec + hg)),
            pl.BlockSpec((None, seq, LANES), lambda b, hg, qi: (b, 0, 0)),
        ],
        out_specs=pl.BlockSpec((FOX_T, w), lambda b, hg, qi: (b * nq + qi, hg)),
        out_shape=jax.ShapeDtypeStruct((m, B_HEADS * B_DH), BF16),
        scratch_shapes=[
            pltpu.VMEM((2 * FOX_G, FOX_ACC_ROWS, seq), BF16),
            pltpu.VMEM((2 * FOX_G, FOX_ACC_ROWS, FOX_T), F32),
        ],
        compiler_params=pltpu.CompilerParams(
            dimension_semantics=("arbitrary", "arbitrary", "arbitrary"),
            vmem_limit_bytes=VMEM_LIMIT),
        name="fox",
    )(proj, proj, proj, proj, pieces)


def _merge_kernel(ha_ref, hb_ref, ga_ref, gb_ref, x_ref, p_ref, wa_ref, wb_ref, wo_ref, wg_ref,
                  wp_ref, png_ref, fng_ref, out_ref):
    ya = jnp.dot(ha_ref[...], wa_ref[...], preferred_element_type=F32)
    yb = jnp.dot(hb_ref[...], wb_ref[...], preferred_element_type=F32)
    merged = (_sigmoid(ga_ref[...].astype(F32)) * ya + _sigmoid(gb_ref[...].astype(F32)) * yb)
    x1 = x_ref[...] + jnp.dot(merged.astype(BF16), wo_ref[...], preferred_element_type=F32)
    r = _rms_norm(x1, png_ref[...]).astype(BF16)
    gate = _sigmoid(jnp.dot(r, wg_ref[...], preferred_element_type=F32))
    pp = jnp.dot(p_ref[...].astype(BF16), wp_ref[...], preferred_element_type=F32)
    x2 = x1 + gate * pp
    out_ref[...] = _rms_norm(x2, fng_ref[...])


def _merge(ha, hb, proj, x2, p2, wa, wb, wo, wg, wp, png, fng):
    m, d = x2.shape
    pd = p2.shape[1]
    tm = MERGE_TM
    full = lambda r, c: pl.BlockSpec((r, c), lambda i: (0, 0))
    return pl.pallas_call(
        _merge_kernel,
        grid=(m // tm,),
        in_specs=[
            pl.BlockSpec((tm, d), lambda i: (i, 0)),
            pl.BlockSpec((tm, d), lambda i: (i, 0)),
            pl.BlockSpec((tm, d), lambda i: (i, 8)),
            pl.BlockSpec((tm, d), lambda i: (i, 9)),
            pl.BlockSpec((tm, d), lambda i: (i, 0)),
            pl.BlockSpec((tm, pd), lambda i: (i, 0)),
            full(d, d), full(d, d), full(d, d), full(d, d), full(pd, d),
            full(1, d), full(1, d),
        ],
        out_specs=pl.BlockSpec((tm, d), lambda i: (i, 0)),
        out_shape=jax.ShapeDtypeStruct((m, d), F32),
        compiler_params=pltpu.CompilerParams(
            dimension_semantics=("arbitrary",), vmem_limit_bytes=VMEM_LIMIT),
        name="merge",
    )(ha, hb, proj, proj, x2, p2, wa, wb, wo, wg, wp, png, fng)


def _split_w_in(w):
    qkw = A_HEADS * A_DQK
    aw = A_HEADS * A_DV
    bw = B_HEADS * B_DH
    d = w.shape[0]
    o_ai = 2 * qkw + aw
    o_ao = o_ai + 2 * A_HEADS
    o_bf = o_ao + 2 * aw + 3 * bw
    o_bz = o_bf + B_HEADS
    cast = lambda a: a.astype(BF16)
    w_main = jnp.concatenate([cast(w[:, :o_ai]), cast(w[:, o_ao:o_bf]), cast(w[:, o_bz:])], axis=1)
    w_gate = jnp.concatenate(
        [cast(w[:, o_ai:o_ao]), cast(w[:, o_bf:o_bz]), jnp.zeros((d, LANES - N_GATE), BF16)], axis=1)
    return w_main, w_gate


def _layer(x, p_i, attn_norm_g, w_in, conv_w, conv_b, a_bias_i, a_bias_f, a_head_norm_g, b_bias_f,
           w_branch_a, w_branch_b, w_out, ple_norm_g, w_ple_gate, w_ple_proj, out_norm_g):
    bsz, seq, d = x.shape
    m = bsz * seq
    x2 = x.reshape(m, d)
    w_main, w_gate = _split_w_in(w_in)
    proj, gates = _in_proj(x2, attn_norm_g.reshape(1, d), w_main, w_gate)

    bias = jnp.concatenate([a_bias_i, a_bias_f, b_bias_f, jnp.zeros((LANES - N_GATE,), F32)])
    gcol, grow, pieces = _gates(gates.reshape(bsz, seq, LANES), bias.reshape(1, LANES))

    ha = _mlstm(proj, gcol, grow, conv_w, conv_b.reshape(1, -1), a_head_norm_g.reshape(1, -1),
                bsz, seq)
    hb = _fox(proj, pieces, bsz, seq)

    out = _merge(ha, hb, proj, x2, p_i.reshape(m, -1),
                 w_branch_a.astype(BF16), w_branch_b.astype(BF16), w_out.astype(BF16),
                 w_ple_gate.astype(BF16), w_ple_proj.astype(BF16),
                 ple_norm_g.reshape(1, d), out_norm_g.reshape(1, d))
    return out.reshape(bsz, seq, d)


def kernel(x, p, attn_norm_g, w_in, conv_w, conv_b, a_bias_i, a_bias_f, a_head_norm_g, b_bias_f,
           w_branch_a, w_branch_b, w_out, ple_norm_g, w_ple_gate, w_ple_proj, final_norm_g):
    depth = w_in.shape[0]
    assert depth == 1, "the final norm is fused into the single layer's merge kernel"
    return _layer(x, p[0], attn_norm_g[0], w_in[0], conv_w[0], conv_b[0], a_bias_i[0], a_bias_f[0],
                  a_head_norm_g[0], b_bias_f[0], w_branch_a[0], w_branch_b[0], w_out[0],
                  ple_norm_g[0], w_ple_gate[0], w_ple_proj[0], final_norm_g)
```

```python
import math

import jax
import jax.numpy as jnp
from jax import lax
from jax.experimental import pallas as pl
from jax.experimental.pallas import tpu as pltpu

F32 = jnp.float32
BF16 = jnp.bfloat16

EPS = 1e-6
A_HEADS = 4
A_DQK = 128
A_DV = 256
CONV_K = 4
B_HEADS = 16
B_DH = 64
LANES = 128
SUBLANES = 8
BF16_ROWS = 16
NEG_BIG = -1e30
LOG2E = math.log2(math.e)

IN_TM = 2048
IN_TN = 512
CUM_BLK = 256
A_CHUNK = 256
FOX_TK = 256
FOX_TQ = 2 * FOX_TK
FOX_G = 4
MERGE_TM = 512
VMEM_LIMIT = 48 * 1024 * 1024

N_GATE = 2 * A_HEADS + B_HEADS
B_LANE0 = 2 * A_HEADS
PIECE_OFFS = (B_LANE0, B_LANE0 + B_HEADS, B_LANE0 + 2 * B_HEADS)


def _sigmoid(x):
    return 1.0 / (1.0 + jnp.exp(-x))


def _silu(x):
    return x * _sigmoid(x)


def _rms_norm(x, g):
    ms = jnp.mean(x * x, axis=-1, keepdims=True)
    return (x * lax.rsqrt(ms + EPS)) * g


def _split3(x):
    x1 = x.astype(BF16)
    r1 = x - x1.astype(F32)
    x2 = r1.astype(BF16)
    x3 = (r1 - x2.astype(F32)).astype(BF16)
    return x1, x2, x3


def _in_proj_kernel(x_ref, g_ref, w_ref, wg_ref, proj_ref, gates_ref, h_scr):
    @pl.when(pl.program_id(1) == 0)
    def _():
        h = _rms_norm(x_ref[...], g_ref[...]).astype(BF16)
        h_scr[...] = h
        gates_ref[...] = jnp.dot(h, wg_ref[...], preferred_element_type=F32)

    proj_ref[...] = jnp.dot(h_scr[...], w_ref[...], preferred_element_type=F32).astype(BF16)


def _in_proj(x2, g, w_main, w_gate):
    m, d = x2.shape
    n = w_main.shape[1]
    return pl.pallas_call(
        _in_proj_kernel,
        grid=(m // IN_TM, n // IN_TN),
        in_specs=[
            pl.BlockSpec((IN_TM, d), lambda i, j: (i, 0)),
            pl.BlockSpec((1, d), lambda i, j: (0, 0)),
            pl.BlockSpec((d, IN_TN), lambda i, j: (0, j)),
            pl.BlockSpec((d, LANES), lambda i, j: (0, 0)),
        ],
        out_specs=[
            pl.BlockSpec((IN_TM, IN_TN), lambda i, j: (i, j)),
            pl.BlockSpec((IN_TM, LANES), lambda i, j: (i, 0)),
        ],
        out_shape=[
            jax.ShapeDtypeStruct((m, n), BF16),
            jax.ShapeDtypeStruct((m, LANES), F32),
        ],
        scratch_shapes=[pltpu.VMEM((IN_TM, d), BF16)],
        compiler_params=pltpu.CompilerParams(
            dimension_semantics=("arbitrary", "arbitrary"),
            vmem_limit_bytes=VMEM_LIMIT),
        name="in_proj",
    )(x2, g, w_main, w_gate)


def _gates_kernel(g_ref, bias_ref, col_ref, row_ref, pc_ref):
    x = g_ref[...] + bias_ref[...]
    s = x.shape[0]
    ls = jnp.minimum(x, 0.0) - jnp.log1p(jnp.exp(-jnp.abs(x)))
    r = lax.broadcasted_iota(jnp.int32, (CUM_BLK, CUM_BLK), 0)
    c = lax.broadcasted_iota(jnp.int32, (CUM_BLK, CUM_BLK), 1)
    tri = jnp.where(r >= c, 1.0, 0.0).astype(BF16)
    carry = jnp.zeros((1, LANES), F32)
    blocks = []
    for blk in range(s // CUM_BLK):
        x1, x2, x3 = _split3(ls[blk * CUM_BLK:(blk + 1) * CUM_BLK])
        cs = (jnp.dot(tri, x3, preferred_element_type=F32)
              + jnp.dot(tri, x2, preferred_element_type=F32)
              + jnp.dot(tri, x1, preferred_element_type=F32)) + carry
        carry = cs[CUM_BLK - 1:CUM_BLK, :]
        blocks.append(cs)
    cum = jnp.concatenate(blocks, axis=0)
    lane = lax.broadcasted_iota(jnp.int32, x.shape, 1)
    res = jnp.where(lane < A_HEADS, x, cum)
    col_ref[...] = res
    row_ref[...] = res.T[0:SUBLANES, :]

    in_b = (lane >= B_LANE0) & (lane < B_LANE0 + B_HEADS)
    p1, p2, p3 = _split3(jnp.where(in_b, cum * (-LOG2E), 0.0))
    pieces = (p1.astype(F32)
              + pltpu.roll(p2.astype(F32), PIECE_OFFS[1] - B_LANE0, axis=1)
              + pltpu.roll(p3.astype(F32), PIECE_OFFS[2] - B_LANE0, axis=1))
    pc_ref[...] = pieces.astype(BF16)


def _gates(gates3, bias):
    b, s, _ = gates3.shape
    return pl.pallas_call(
        _gates_kernel,
        grid=(b,),
        in_specs=[
            pl.BlockSpec((None, s, LANES), lambda i: (i, 0, 0)),
            pl.BlockSpec((1, LANES), lambda i: (0, 0)),
        ],
        out_specs=[
            pl.BlockSpec((None, s, LANES), lambda i: (i, 0, 0)),
            pl.BlockSpec((None, SUBLANES, s), lambda i: (i, 0, 0)),
            pl.BlockSpec((None, s, LANES), lambda i: (i, 0, 0)),
        ],
        out_shape=[
            jax.ShapeDtypeStruct((b, s, LANES), F32),
            jax.ShapeDtypeStruct((b, SUBLANES, s), F32),
            jax.ShapeDtypeStruct((b, s, LANES), BF16),
        ],
        compiler_params=pltpu.CompilerParams(
            dimension_semantics=("arbitrary",), vmem_limit_bytes=VMEM_LIMIT),
        name="gates",
    )(gates3, bias)


def _mlstm_kernel(qk_ref, v_ref, o_ref, z_ref, gcol_ref, grow_ref, cw_ref, cb_ref, hg_ref,
                  out_ref, xpad_scr, c_scr, n_scr, m_scr, fprev_scr):
    t = pl.program_id(1)
    L = A_CHUNK
    qkw = A_HEADS * A_DQK

    @pl.when(t == 0)
    def _():
        xpad_scr[0:SUBLANES, :] = jnp.zeros((SUBLANES, 2 * qkw), F32)
        c_scr[...] = jnp.zeros_like(c_scr)
        n_scr[...] = jnp.zeros_like(n_scr)
        m_scr[...] = jnp.zeros_like(m_scr)
        fprev_scr[...] = jnp.zeros_like(fprev_scr)

    xpad_scr[SUBLANES:SUBLANES + L, :] = qk_ref[...].astype(F32)
    y = cb_ref[...] + xpad_scr[SUBLANES:SUBLANES + L, :] * cw_ref[CONV_K - 1:CONV_K, :]
    for d in range(1, CONV_K):
        y = y + xpad_scr[SUBLANES - d:SUBLANES - d + L, :] * cw_ref[CONV_K - 1 - d:CONV_K - d, :]
    xpad_scr[0:SUBLANES, :] = xpad_scr[L:L + SUBLANES, :]
    qk = _silu(y)

    row = lax.broadcasted_iota(jnp.int32, (L, L), 0)
    col = lax.broadcasted_iota(jnp.int32, (L, L), 1)
    causal = row >= col

    for h in range(A_HEADS):
        q = qk[:, h * A_DQK:(h + 1) * A_DQK]
        k = qk[:, qkw + h * A_DQK:qkw + (h + 1) * A_DQK] * (A_DQK ** -0.5)
        qb = q.astype(BF16)
        kb = k.astype(BF16)
        v = v_ref[:, h * A_DV:(h + 1) * A_DV]

        li_c = gcol_ref[:, h:h + 1]
        f_c = gcol_ref[:, A_HEADS + h:A_HEADS + h + 1]
        li_r = grow_ref[h:h + 1, :]
        f_r = grow_ref[A_HEADS + h:A_HEADS + h + 1, :]
        f_prev = fprev_scr[0:1, A_HEADS + h:A_HEADS + h + 1]
        f_end = gcol_ref[L - 1:L, A_HEADS + h:A_HEADS + h + 1]
        m_st = m_scr[h, 0:1, 0:1]
        c_st = c_scr[h]
        n_st = n_scr[h]

        dmat = jnp.where(causal, (f_c - f_r) + li_r, NEG_BIG)
        inter = (f_c - f_prev) + m_st
        m_row = jnp.maximum(inter, jnp.max(dmat, axis=-1, keepdims=True))
        w_intra = jnp.exp(dmat - m_row)
        w_inter = jnp.exp(inter - m_row)
        s = lax.dot_general(qb, kb, (((1,), (1,)), ((), ())), preferred_element_type=F32)
        scores = s * w_intra
        num = (jnp.dot(scores.astype(BF16), v, preferred_element_type=F32)
               + w_inter * jnp.dot(qb, c_st.astype(BF16), preferred_element_type=F32))
        den = (jnp.sum(scores, axis=-1, keepdims=True)
               + w_inter * jnp.sum(q * n_st, axis=-1, keepdims=True))
        hh = num * (1.0 / jnp.maximum(jnp.abs(den), jnp.exp(-m_row)))

        g_tot = f_end - f_prev
        to_end = (f_end - f_c) + li_c
        m_new = jnp.maximum(g_tot + m_st, jnp.max(to_end, axis=0, keepdims=True))
        w_k = jnp.exp(to_end - m_new)
        decay = jnp.exp(g_tot + m_st - m_new)
        kw = k * w_k
        c_scr[h] = decay * c_st + jnp.dot(kw.T.astype(BF16), v, preferred_element_type=F32)
        n_scr[h] = decay * n_st + jnp.sum(kw, axis=0, keepdims=True)
        m_scr[h] = jnp.broadcast_to(m_new, (SUBLANES, LANES))

        hn = hh * lax.rsqrt(jnp.mean(hh * hh, axis=-1, keepdims=True) + EPS)
        hn = hn * hg_ref[:, h * A_DV:(h + 1) * A_DV]
        og = _sigmoid(o_ref[:, h * A_DV:(h + 1) * A_DV].astype(F32))
        zz = _silu(z_ref[:, h * A_DV:(h + 1) * A_DV].astype(F32))
        out_ref[:, h * A_DV:(h + 1) * A_DV] = ((og * hn) * zz).astype(BF16)

    fprev_scr[...] = gcol_ref[L - 1:L, :]


def _mlstm(proj, gcol, grow, conv_w, conv_b, head_g, bsz, seq):
    m = proj.shape[0]
    width = A_HEADS * A_DV
    nt = seq // A_CHUNK
    row_blk = lambda c: pl.BlockSpec((A_CHUNK, width), lambda b, t: (b * nt + t, c))
    return pl.pallas_call(
        _mlstm_kernel,
        grid=(bsz, nt),
        in_specs=[
            row_blk(0), row_blk(1), row_blk(2), row_blk(3),
            pl.BlockSpec((None, A_CHUNK, LANES), lambda b, t: (b, t, 0)),
            pl.BlockSpec((None, SUBLANES, A_CHUNK), lambda b, t: (b, 0, t)),
            pl.BlockSpec((CONV_K, width), lambda b, t: (0, 0)),
            pl.BlockSpec((1, width), lambda b, t: (0, 0)),
            pl.BlockSpec((1, width), lambda b, t: (0, 0)),
        ],
        out_specs=pl.BlockSpec((A_CHUNK, width), lambda b, t: (b * nt + t, 0)),
        out_shape=jax.ShapeDtypeStruct((m, width), BF16),
        scratch_shapes=[
            pltpu.VMEM((A_CHUNK + SUBLANES, width), F32),
            pltpu.VMEM((A_HEADS, A_DQK, A_DV), F32),
            pltpu.VMEM((A_HEADS, 1, A_DQK), F32),
            pltpu.VMEM((A_HEADS, SUBLANES, LANES), F32),
            pltpu.VMEM((1, LANES), F32),
        ],
        compiler_params=pltpu.CompilerParams(
            dimension_semantics=("arbitrary", "arbitrary"), vmem_limit_bytes=VMEM_LIMIT),
        name="mlstm",
    )(proj, proj, proj, proj, gcol, grow, conv_w, conv_b, head_g)


FOX_ACC_ROWS = B_DH + BF16_ROWS


def _fox_kernel(q_ref, k_ref, v_ref, z_ref, pc_ref, out_ref, vt_scr, acc_scr, rhs_scr, m_scr,
                s2_scr, cm2_scr):
    s_scr = (s2_scr.at[0], s2_scr.at[1])
    cm_scr = (cm2_scr.at[0], cm2_scr.at[1])
    hg = pl.program_id(1)
    qi = pl.program_id(2)
    TQ, TK = FOX_TQ, FOX_TK
    seq = k_ref.shape[0]
    nh = 2 * FOX_G

    @pl.when(qi == 0)
    def _():
        for g in range(FOX_G):
            vt = v_ref[:, g * LANES:(g + 1) * LANES].astype(F32).T
            for hh in range(2):
                vt_scr[2 * g + hh, 0:B_DH, :] = vt[hh * B_DH:(hh + 1) * B_DH, :].astype(BF16)
                vt_scr[2 * g + hh, B_DH:FOX_ACC_ROWS, :] = jnp.ones((BF16_ROWS, seq), BF16)

    row = lax.broadcasted_iota(jnp.int32, (LANES, TQ), 0)
    for g in range(FOX_G):
        qt = (q_ref[:, g * LANES:(g + 1) * LANES].astype(F32) * (B_DH ** -0.5 * LOG2E)).T
        for hh in range(2):
            h = 2 * g + hh
            head = hg * nh + h
            qm = jnp.where((row >= hh * B_DH) & (row < (hh + 1) * B_DH), qt, 0.0)
            sel = jnp.where((row == PIECE_OFFS[0] + head) | (row == PIECE_OFFS[1] + head)
                            | (row == PIECE_OFFS[2] + head), 1.0, 0.0)
            rhs_scr[h, 0:LANES, :] = qm.astype(BF16)
            rhs_scr[h, LANES:2 * LANES, :] = sel.astype(BF16)

    acc_scr[...] = jnp.zeros_like(acc_scr)
    m_scr[...] = jnp.full(m_scr.shape, NEG_BIG, F32)

    def key_block(kj):
        k0 = pl.multiple_of(kj * TK, TK)
        pcs = pc_ref[pl.ds(k0, TK), :]
        return [jnp.concatenate([k_ref[pl.ds(k0, TK), g * LANES:(g + 1) * LANES], pcs], axis=1)
                for g in range(FOX_G)]

    def scores_head(h, lhs, slot):
        s = jnp.dot(lhs[h // 2], rhs_scr[h], preferred_element_type=F32)
        s_scr[slot][h] = s
        cm_scr[slot][h] = jnp.broadcast_to(jnp.max(s, axis=0, keepdims=True), (SUBLANES, TQ))

    def scores(kj, slot):
        lhs = key_block(kj)
        for h in range(nh):
            scores_head(h, lhs, slot)

    def softmax_pv(h, kj, s, cmax, lo):
        k0 = pl.multiple_of(kj * TK, TK)
        m_old = m_scr[h, 0:1, lo:TQ]
        m_new = jnp.maximum(m_old, cmax)
        alpha = jnp.exp2(m_old - m_new)
        p = jnp.exp2(s - m_new).astype(BF16)
        pv = jnp.dot(vt_scr[h, :, pl.ds(k0, TK)], p, preferred_element_type=F32)
        acc_scr[h, :, lo:TQ] = alpha * acc_scr[h, :, lo:TQ] + pv
        m_scr[h, :, lo:TQ] = jnp.broadcast_to(m_new, (SUBLANES, TQ - lo))

    def overlapped(kj_next, slot_next, kj, slot):
        lhs = key_block(kj_next)
        for h in range(nh):
            scores_head(h, lhs, slot_next)
            softmax_pv(h, kj, s_scr[slot][h], cm_scr[slot][h, 0:1, :], 0)

    def pair(i, carry):
        overlapped(2 * i + 1, 1, 2 * i, 0)
        overlapped(2 * i + 2, 0, 2 * i + 1, 1)
        return carry

    scores(0, 0)
    lax.fori_loop(0, qi, pair, 0)

    half = TQ - TK
    lhs_b = key_block(2 * qi + 1)
    sb = [jnp.dot(lhs_b[h // 2], rhs_scr[h, :, half:TQ], preferred_element_type=F32)
          for h in range(nh)]
    r = lax.broadcasted_iota(jnp.int32, (TK, TQ), 0)
    c = lax.broadcasted_iota(jnp.int32, (TK, TQ), 1)
    rb = lax.broadcasted_iota(jnp.int32, (TK, TK), 0)
    cb = lax.broadcasted_iota(jnp.int32, (TK, TK), 1)
    for h in range(nh):
        s = jnp.where(c >= r, s_scr[0][h], NEG_BIG)
        softmax_pv(h, 2 * qi, s, jnp.max(s, axis=0, keepdims=True), 0)
    for h in range(nh):
        s = jnp.where(cb >= rb, sb[h], NEG_BIG)
        softmax_pv(h, 2 * qi + 1, s, jnp.max(s, axis=0, keepdims=True), half)

    for g in range(FOX_G):
        parts = []
        for hh in range(2):
            a = acc_scr[2 * g + hh]
            parts.append(a[0:B_DH, :] * (1.0 / a[B_DH:B_DH + 1, :]))
        o = jnp.concatenate(parts, axis=0).T
        zz = _silu(z_ref[:, g * LANES:(g + 1) * LANES].astype(F32))
        out_ref[:, g * LANES:(g + 1) * LANES] = (o * zz).astype(BF16)


def _fox(proj, pieces, bsz, seq):
    m = proj.shape[0]
    nq = seq // FOX_TQ
    nh = 2 * FOX_G
    w = FOX_G * LANES
    ngrp = (B_HEADS * B_DH) // w
    col0 = (A_HEADS * A_DV * 4) // w
    sec = (B_HEADS * B_DH) // w
    return pl.pallas_call(
        _fox_kernel,
        grid=(bsz, ngrp, nq),
        in_specs=[
            pl.BlockSpec((FOX_TQ, w), lambda b, hg, qi: (b * nq + qi, col0 + hg)),
            pl.BlockSpec((seq, w), lambda b, hg, qi: (b, col0 + sec + hg)),
            pl.BlockSpec((seq, w), lambda b, hg, qi: (b, col0 + 2 * sec + hg)),
            pl.BlockSpec((FOX_TQ, w), lambda b, hg, qi: (b * nq + qi, col0 + 3 * sec + hg)),
            pl.BlockSpec((None, seq, LANES), lambda b, hg, qi: (b, 0, 0)),
        ],
        out_specs=pl.BlockSpec((FOX_TQ, w), lambda b, hg, qi: (b * nq + qi, hg)),
        out_shape=jax.ShapeDtypeStruct((m, B_HEADS * B_DH), BF16),
        scratch_shapes=[
            pltpu.VMEM((nh, FOX_ACC_ROWS, seq), BF16),
            pltpu.VMEM((nh, FOX_ACC_ROWS, FOX_TQ), F32),
            pltpu.VMEM((nh, 2 * LANES, FOX_TQ), BF16),
            pltpu.VMEM((nh, SUBLANES, FOX_TQ), F32),
            pltpu.VMEM((2, nh, FOX_TK, FOX_TQ), F32),
            pltpu.VMEM((2, nh, SUBLANES, FOX_TQ), F32),
        ],
        compiler_params=pltpu.CompilerParams(
            dimension_semantics=("arbitrary", "arbitrary", "arbitrary"),
            vmem_limit_bytes=VMEM_LIMIT),
        name="fox",
    )(proj, proj, proj, proj, pieces)


def _merge_kernel(ha_ref, hb_ref, ga_ref, gb_ref, x_ref, p_ref, wa_ref, wb_ref, wo_ref, wg_ref,
                  wp_ref, png_ref, fng_ref, out_ref):
    ya = jnp.dot(ha_ref[...], wa_ref[...], preferred_element_type=F32)
    yb = jnp.dot(hb_ref[...], wb_ref[...], preferred_element_type=F32)
    merged = (_sigmoid(ga_ref[...].astype(F32)) * ya + _sigmoid(gb_ref[...].astype(F32)) * yb)
    x1 = x_ref[...] + jnp.dot(merged.astype(BF16), wo_ref[...], preferred_element_type=F32)
    r = _rms_norm(x1, png_ref[...]).astype(BF16)
    gate = _sigmoid(jnp.dot(r, wg_ref[...], preferred_element_type=F32))
    pp = jnp.dot(p_ref[...].astype(BF16), wp_ref[...], preferred_element_type=F32)
    x2 = x1 + gate * pp
    out_ref[...] = _rms_norm(x2, fng_ref[...])


def _merge(ha, hb, proj, x2, p2, wa, wb, wo, wg, wp, png, fng):
    m, d = x2.shape
    pd = p2.shape[1]
    tm = MERGE_TM
    full = lambda r, c: pl.BlockSpec((r, c), lambda i: (0, 0))
    return pl.pallas_call(
        _merge_kernel,
        grid=(m // tm,),
        in_specs=[
            pl.BlockSpec((tm, d), lambda i: (i, 0)),
            pl.BlockSpec((tm, d), lambda i: (i, 0)),
            pl.BlockSpec((tm, d), lambda i: (i, 8)),
            pl.BlockSpec((tm, d), lambda i: (i, 9)),
            pl.BlockSpec((tm, d), lambda i: (i, 0)),
            pl.BlockSpec((tm, pd), lambda i: (i, 0)),
            full(d, d), full(d, d), full(d, d), full(d, d), full(pd, d),
            full(1, d), full(1, d),
        ],
        out_specs=pl.BlockSpec((tm, d), lambda i: (i, 0)),
        out_shape=jax.ShapeDtypeStruct((m, d), F32),
        compiler_params=pltpu.CompilerParams(
            dimension_semantics=("arbitrary",), vmem_limit_bytes=VMEM_LIMIT),
        name="merge",
    )(ha, hb, proj, proj, x2, p2, wa, wb, wo, wg, wp, png, fng)


def _split_w_in(w):
    qkw = A_HEADS * A_DQK
    aw = A_HEADS * A_DV
    bw = B_HEADS * B_DH
    d = w.shape[0]
    o_ai = 2 * qkw + aw
    o_ao = o_ai + 2 * A_HEADS
    o_bf = o_ao + 2 * aw + 3 * bw
    o_bz = o_bf + B_HEADS
    cast = lambda a: a.astype(BF16)
    w_main = jnp.concatenate([cast(w[:, :o_ai]), cast(w[:, o_ao:o_bf]), cast(w[:, o_bz:])], axis=1)
    w_gate = jnp.concatenate(
        [cast(w[:, o_ai:o_ao]), cast(w[:, o_bf:o_bz]), jnp.zeros((d, LANES - N_GATE), BF16)], axis=1)
    return w_main, w_gate


def _layer(x, p_i, attn_norm_g, w_in, conv_w, conv_b, a_bias_i, a_bias_f, a_head_norm_g, b_bias_f,
           w_branch_a, w_branch_b, w_out, ple_norm_g, w_ple_gate, w_ple_proj, out_norm_g):
    bsz, seq, d = x.shape
    m = bsz * seq
    x2 = x.reshape(m, d)
    w_main, w_gate = _split_w_in(w_in)
    proj, gates = _in_proj(x2, attn_norm_g.reshape(1, d), w_main, w_gate)

    bias = jnp.concatenate([a_bias_i, a_bias_f, b_bias_f, jnp.zeros((LANES - N_GATE,), F32)])
    gcol, grow, pieces = _gates(gates.reshape(bsz, seq, LANES), bias.reshape(1, LANES))

    ha = _mlstm(proj, gcol, grow, conv_w, conv_b.reshape(1, -1), a_head_norm_g.reshape(1, -1),
                bsz, seq)
    hb = _fox(proj, pieces, bsz, seq)

    out = _merge(ha, hb, proj, x2, p_i.reshape(m, -1),
                 w_branch_a.astype(BF16), w_branch_b.astype(BF16), w_out.astype(BF16),
                 w_ple_gate.astype(BF16), w_ple_proj.astype(BF16),
                 ple_norm_g.reshape(1, d), out_norm_g.reshape(1, d))
    return out.reshape(bsz, seq, d)


def kernel(x, p, attn_norm_g, w_in, conv_w, conv_b, a_bias_i, a_bias_f, a_head_norm_g, b_bias_f,
           w_branch_a, w_branch_b, w_out, ple_norm_g, w_ple_gate, w_ple_proj, final_norm_g):
    depth = w_in.shape[0]
    assert depth == 1, "the final norm is fused into the single layer's merge kernel"
    return _layer(x, p[0], attn_norm_g[0], w_in[0], conv_w[0], conv_b[0], a_bias_i[0], a_bias_f[0],
                  a_head_norm_g[0], b_bias_f[0], w_branch_a[0], w_branch_b[0], w_out[0],
                  ple_norm_g[0], w_ple_gate[0], w_ple_proj[0], final_norm_g)
```

```python
import math

import jax
import jax.numpy as jnp
from jax import lax
from jax.experimental import pallas as pl
from jax.experimental.pallas import tpu as pltpu

F32 = jnp.float32
BF16 = jnp.bfloat16

EPS = 1e-6
A_HEADS = 4
A_DQK = 128
A_DV = 256
CONV_K = 4
B_HEADS = 16
B_DH = 64
LANES = 128
SUBLANES = 8
BF16_ROWS = 16
NEG_BIG = -1e30
LOG2E = math.log2(math.e)

IN_TM = 2048
IN_TN = 512
IN_CHUNKS = 8
CUM_BLK = 256
A_CHUNK = 256
FOX_TK = 256
FOX_TQ = 2 * FOX_TK
FOX_G = 4
MERGE_TM = 512
VMEM_LIMIT = 48 * 1024 * 1024

SECTION_W = A_HEADS * A_DV
SEC_QK, SEC_AV, SEC_AO, SEC_AZ, SEC_BQ, SEC_BK, SEC_BV, SEC_BZ, SEC_GA, SEC_GB = range(10)

N_GATE = 2 * A_HEADS + B_HEADS
B_LANE0 = 2 * A_HEADS
PIECE_OFFS = (B_LANE0, B_LANE0 + B_HEADS, B_LANE0 + 2 * B_HEADS)


def _sigmoid(x):
    return 1.0 / (1.0 + jnp.exp(-x))


def _silu(x):
    return x * _sigmoid(x)


def _rms_norm(x, g):
    ms = jnp.mean(x * x, axis=-1, keepdims=True)
    return (x * lax.rsqrt(ms + EPS)) * g


def _split3(x):
    x1 = x.astype(BF16)
    r1 = x - x1.astype(F32)
    x2 = r1.astype(BF16)
    x3 = (r1 - x2.astype(F32)).astype(BF16)
    return x1, x2, x3


def _in_proj_kernel(x_ref, g_ref, w_ref, wg_ref, proj_ref, gates_ref, h_scr, raw_scr):
    j = pl.program_id(1)
    sec = j // (SECTION_W // IN_TN)

    @pl.when(j == 0)
    def _():
        h = _rms_norm(x_ref[...], g_ref[...]).astype(BF16)
        h_scr[...] = h
        gates_ref[...] = jnp.dot(h, wg_ref[...], preferred_element_type=F32)

    is_sigmoid = (sec == SEC_AO) | (sec == SEC_GA) | (sec == SEC_GB)
    is_silu = (sec == SEC_AZ) | (sec == SEC_BZ)

    @pl.when(is_sigmoid | is_silu)
    def _():
        ch = IN_TM // IN_CHUNKS
        for c in range(IN_CHUNKS + 1):
            if c < IN_CHUNKS:
                raw_scr[c * ch:(c + 1) * ch, :] = jnp.dot(
                    h_scr[c * ch:(c + 1) * ch, :], w_ref[...], preferred_element_type=F32)
            if c > 0:
                y = raw_scr[(c - 1) * ch:c * ch, :]
                act = _sigmoid(y) * jnp.where(is_silu, y, 1.0)
                proj_ref[(c - 1) * ch:c * ch, :] = act.astype(BF16)

    @pl.when(jnp.logical_not(is_sigmoid | is_silu))
    def _():
        proj_ref[...] = jnp.dot(h_scr[...], w_ref[...], preferred_element_type=F32).astype(BF16)


def _in_proj(x2, g, w_main, w_gate):
    m, d = x2.shape
    n = w_main.shape[1]
    return pl.pallas_call(
        _in_proj_kernel,
        grid=(m // IN_TM, n // IN_TN),
        in_specs=[
            pl.BlockSpec((IN_TM, d), lambda i, j: (i, 0)),
            pl.BlockSpec((1, d), lambda i, j: (0, 0)),
            pl.BlockSpec((d, IN_TN), lambda i, j: (0, j)),
            pl.BlockSpec((d, LANES), lambda i, j: (0, 0)),
        ],
        out_specs=[
            pl.BlockSpec((IN_TM, IN_TN), lambda i, j: (i, j)),
            pl.BlockSpec((IN_TM, LANES), lambda i, j: (i, 0)),
        ],
        out_shape=[
            jax.ShapeDtypeStruct((m, n), BF16),
            jax.ShapeDtypeStruct((m, LANES), F32),
        ],
        scratch_shapes=[pltpu.VMEM((IN_TM, d), BF16),
                        pltpu.VMEM((IN_TM, IN_TN), F32)],
        compiler_params=pltpu.CompilerParams(
            dimension_semantics=("arbitrary", "arbitrary"),
            vmem_limit_bytes=VMEM_LIMIT),
        name="in_proj",
    )(x2, g, w_main, w_gate)


def _gates_kernel(g_ref, bias_ref, col_ref, row_ref, pc_ref):
    x = g_ref[...] + bias_ref[...]
    s = x.shape[0]
    ls = jnp.minimum(x, 0.0) - jnp.log1p(jnp.exp(-jnp.abs(x)))
    r = lax.broadcasted_iota(jnp.int32, (CUM_BLK, CUM_BLK), 0)
    c = lax.broadcasted_iota(jnp.int32, (CUM_BLK, CUM_BLK), 1)
    tri = jnp.where(r >= c, 1.0, 0.0).astype(BF16)
    carry = jnp.zeros((1, LANES), F32)
    blocks = []
    for blk in range(s // CUM_BLK):
        x1, x2, x3 = _split3(ls[blk * CUM_BLK:(blk + 1) * CUM_BLK])
        cs = (jnp.dot(tri, x3, preferred_element_type=F32)
              + jnp.dot(tri, x2, preferred_element_type=F32)
              + jnp.dot(tri, x1, preferred_element_type=F32)) + carry
        carry = cs[CUM_BLK - 1:CUM_BLK, :]
        blocks.append(cs)
    cum = jnp.concatenate(blocks, axis=0)
    lane = lax.broadcasted_iota(jnp.int32, x.shape, 1)
    res = jnp.where(lane < A_HEADS, x, cum)
    col_ref[...] = res
    row_ref[...] = res.T[0:SUBLANES, :]

    in_b = (lane >= B_LANE0) & (lane < B_LANE0 + B_HEADS)
    p1, p2, p3 = _split3(jnp.where(in_b, cum * (-LOG2E), 0.0))
    pieces = (p1.astype(F32)
              + pltpu.roll(p2.astype(F32), PIECE_OFFS[1] - B_LANE0, axis=1)
              + pltpu.roll(p3.astype(F32), PIECE_OFFS[2] - B_LANE0, axis=1))
    pc_ref[...] = pieces.astype(BF16)


def _gates(gates3, bias):
    b, s, _ = gates3.shape
    return pl.pallas_call(
        _gates_kernel,
        grid=(b,),
        in_specs=[
            pl.BlockSpec((None, s, LANES), lambda i: (i, 0, 0)),
            pl.BlockSpec((1, LANES), lambda i: (0, 0)),
        ],
        out_specs=[
            pl.BlockSpec((None, s, LANES), lambda i: (i, 0, 0)),
            pl.BlockSpec((None, SUBLANES, s), lambda i: (i, 0, 0)),
            pl.BlockSpec((None, s, LANES), lambda i: (i, 0, 0)),
        ],
        out_shape=[
            jax.ShapeDtypeStruct((b, s, LANES), F32),
            jax.ShapeDtypeStruct((b, SUBLANES, s), F32),
            jax.ShapeDtypeStruct((b, s, LANES), BF16),
        ],
        compiler_params=pltpu.CompilerParams(
            dimension_semantics=("arbitrary",), vmem_limit_bytes=VMEM_LIMIT),
        name="gates",
    )(gates3, bias)


def _mlstm_kernel(qk_ref, v_ref, o_ref, z_ref, gcol_ref, grow_ref, cw_ref, cb_ref, hg_ref,
                  out_ref, xpad_scr, c_scr, n_scr, m_scr, fprev_scr):
    t = pl.program_id(1)
    L = A_CHUNK
    qkw = A_HEADS * A_DQK

    @pl.when(t == 0)
    def _():
        xpad_scr[0:SUBLANES, :] = jnp.zeros((SUBLANES, 2 * qkw), F32)
        c_scr[...] = jnp.zeros_like(c_scr)
        n_scr[...] = jnp.zeros_like(n_scr)
        m_scr[...] = jnp.zeros_like(m_scr)
        fprev_scr[...] = jnp.zeros_like(fprev_scr)

    xpad_scr[SUBLANES:SUBLANES + L, :] = qk_ref[...].astype(F32)
    y = cb_ref[...] + xpad_scr[SUBLANES:SUBLANES + L, :] * cw_ref[CONV_K - 1:CONV_K, :]
    for d in range(1, CONV_K):
        y = y + xpad_scr[SUBLANES - d:SUBLANES - d + L, :] * cw_ref[CONV_K - 1 - d:CONV_K - d, :]
    xpad_scr[0:SUBLANES, :] = xpad_scr[L:L + SUBLANES, :]
    qk = _silu(y)

    row = lax.broadcasted_iota(jnp.int32, (L, L), 0)
    col = lax.broadcasted_iota(jnp.int32, (L, L), 1)
    causal = row >= col

    for h in range(A_HEADS):
        q = qk[:, h * A_DQK:(h + 1) * A_DQK]
        k = qk[:, qkw + h * A_DQK:qkw + (h + 1) * A_DQK] * (A_DQK ** -0.5)
        qb = q.astype(BF16)
        kb = k.astype(BF16)
        v = v_ref[:, h * A_DV:(h + 1) * A_DV]

        li_c = gcol_ref[:, h:h + 1]
        f_c = gcol_ref[:, A_HEADS + h:A_HEADS + h + 1]
        li_r = grow_ref[h:h + 1, :]
        f_r = grow_ref[A_HEADS + h:A_HEADS + h + 1, :]
        f_prev = fprev_scr[0:1, A_HEADS + h:A_HEADS + h + 1]
        f_end = gcol_ref[L - 1:L, A_HEADS + h:A_HEADS + h + 1]
        m_st = m_scr[h, 0:1, 0:1]
        c_st = c_scr[h]
        n_st = n_scr[h]

        dmat = jnp.where(causal, (f_c - f_r) + li_r, NEG_BIG)
        inter = (f_c - f_prev) + m_st
        m_row = jnp.maximum(inter, jnp.max(dmat, axis=-1, keepdims=True))
        w_intra = jnp.exp(dmat - m_row)
        w_inter = jnp.exp(inter - m_row)
        s = lax.dot_general(qb, kb, (((1,), (1,)), ((), ())), preferred_element_type=F32)
        scores = s * w_intra
        num = (jnp.dot(scores.astype(BF16), v, preferred_element_type=F32)
               + w_inter * jnp.dot(qb, c_st.astype(BF16), preferred_element_type=F32))
        den = (jnp.sum(scores, axis=-1, keepdims=True)
               + w_inter * jnp.sum(q * n_st, axis=-1, keepdims=True))
        hh = num * (1.0 / jnp.maximum(jnp.abs(den), jnp.exp(-m_row)))

        g_tot = f_end - f_prev
        to_end = (f_end - f_c) + li_c
        m_new = jnp.maximum(g_tot + m_st, jnp.max(to_end, axis=0, keepdims=True))
        w_k = jnp.exp(to_end - m_new)
        decay = jnp.exp(g_tot + m_st - m_new)
        kw = k * w_k
        c_scr[h] = decay * c_st + jnp.dot(kw.T.astype(BF16), v, preferred_element_type=F32)
        n_scr[h] = decay * n_st + jnp.sum(kw, axis=0, keepdims=True)
        m_scr[h] = jnp.broadcast_to(m_new, (SUBLANES, LANES))

        hn = hh * lax.rsqrt(jnp.mean(hh * hh, axis=-1, keepdims=True) + EPS)
        hn = hn * hg_ref[:, h * A_DV:(h + 1) * A_DV]
        og = o_ref[:, h * A_DV:(h + 1) * A_DV].astype(F32)
        zz = z_ref[:, h * A_DV:(h + 1) * A_DV].astype(F32)
        out_ref[:, h * A_DV:(h + 1) * A_DV] = ((og * hn) * zz).astype(BF16)

    fprev_scr[...] = gcol_ref[L - 1:L, :]


def _mlstm(proj, gcol, grow, conv_w, conv_b, head_g, bsz, seq):
    m = proj.shape[0]
    width = SECTION_W
    nt = seq // A_CHUNK
    row_blk = lambda c: pl.BlockSpec((A_CHUNK, width), lambda b, t: (b * nt + t, c))
    return pl.pallas_call(
        _mlstm_kernel,
        grid=(bsz, nt),
        in_specs=[
            row_blk(SEC_QK), row_blk(SEC_AV), row_blk(SEC_AO), row_blk(SEC_AZ),
            pl.BlockSpec((None, A_CHUNK, LANES), lambda b, t: (b, t, 0)),
            pl.BlockSpec((None, SUBLANES, A_CHUNK), lambda b, t: (b, 0, t)),
            pl.BlockSpec((CONV_K, width), lambda b, t: (0, 0)),
            pl.BlockSpec((1, width), lambda b, t: (0, 0)),
            pl.BlockSpec((1, width), lambda b, t: (0, 0)),
        ],
        out_specs=pl.BlockSpec((A_CHUNK, width), lambda b, t: (b * nt + t, 0)),
        out_shape=jax.ShapeDtypeStruct((m, width), BF16),
        scratch_shapes=[
            pltpu.VMEM((A_CHUNK + SUBLANES, width), F32),
            pltpu.VMEM((A_HEADS, A_DQK, A_DV), F32),
            pltpu.VMEM((A_HEADS, 1, A_DQK), F32),
            pltpu.VMEM((A_HEADS, SUBLANES, LANES), F32),
            pltpu.VMEM((1, LANES), F32),
        ],
        compiler_params=pltpu.CompilerParams(
            dimension_semantics=("arbitrary", "arbitrary"), vmem_limit_bytes=VMEM_LIMIT),
        name="mlstm",
    )(proj, proj, proj, proj, gcol, grow, conv_w, conv_b, head_g)


FOX_ACC_ROWS = B_DH + BF16_ROWS


def _fox_kernel(q_ref, k_ref, v_ref, z_ref, pc_ref, out_ref, vt_scr, acc_scr, rhs_scr, m_scr,
                s2_scr, cm2_scr):
    s_scr = (s2_scr.at[0], s2_scr.at[1])
    cm_scr = (cm2_scr.at[0], cm2_scr.at[1])
    hg = pl.program_id(1)
    qi = pl.program_id(2)
    TQ, TK = FOX_TQ, FOX_TK
    seq = k_ref.shape[0]
    nh = 2 * FOX_G

    @pl.when(qi == 0)
    def _():
        for g in range(FOX_G):
            vt = v_ref[:, g * LANES:(g + 1) * LANES].astype(F32).T
            for hh in range(2):
                vt_scr[2 * g + hh, 0:B_DH, :] = vt[hh * B_DH:(hh + 1) * B_DH, :].astype(BF16)
                vt_scr[2 * g + hh, B_DH:FOX_ACC_ROWS, :] = jnp.ones((BF16_ROWS, seq), BF16)

    row = lax.broadcasted_iota(jnp.int32, (LANES, TQ), 0)
    for g in range(FOX_G):
        qt = (q_ref[:, g * LANES:(g + 1) * LANES].astype(F32) * (B_DH ** -0.5 * LOG2E)).T
        for hh in range(2):
            h = 2 * g + hh
            head = hg * nh + h
            qm = jnp.where((row >= hh * B_DH) & (row < (hh + 1) * B_DH), qt, 0.0)
            sel = jnp.where((row == PIECE_OFFS[0] + head) | (row == PIECE_OFFS[1] + head)
                            | (row == PIECE_OFFS[2] + head), 1.0, 0.0)
            rhs_scr[h, 0:LANES, :] = qm.astype(BF16)
            rhs_scr[h, LANES:2 * LANES, :] = sel.astype(BF16)

    acc_scr[...] = jnp.zeros_like(acc_scr)
    m_scr[...] = jnp.full(m_scr.shape, NEG_BIG, F32)

    def key_block(kj):
        k0 = pl.multiple_of(kj * TK, TK)
        pcs = pc_ref[pl.ds(k0, TK), :]
        return [jnp.concatenate([k_ref[pl.ds(k0, TK), g * LANES:(g + 1) * LANES], pcs], axis=1)
                for g in range(FOX_G)]

    def scores_head(h, lhs, slot):
        s = jnp.dot(lhs[h // 2], rhs_scr[h], preferred_element_type=F32)
        s_scr[slot][h] = s
        cm_scr[slot][h] = jnp.broadcast_to(jnp.max(s, axis=0, keepdims=True), (SUBLANES, TQ))

    def scores(kj, slot):
        lhs = key_block(kj)
        for h in range(nh):
            scores_head(h, lhs, slot)

    def softmax_pv(h, kj, s, cmax, lo):
        k0 = pl.multiple_of(kj * TK, TK)
        m_old = m_scr[h, 0:1, lo:TQ]
        m_new = jnp.maximum(m_old, cmax)
        alpha = jnp.exp2(m_old - m_new)
        p = jnp.exp2(s - m_new).astype(BF16)
        pv = jnp.dot(vt_scr[h, :, pl.ds(k0, TK)], p, preferred_element_type=F32)
        acc_scr[h, :, lo:TQ] = alpha * acc_scr[h, :, lo:TQ] + pv
        m_scr[h, :, lo:TQ] = jnp.broadcast_to(m_new, (SUBLANES, TQ - lo))

    def overlapped(kj_next, slot_next, kj, slot):
        lhs = key_block(kj_next)
        for h in range(nh):
            scores_head(h, lhs, slot_next)
            softmax_pv(h, kj, s_scr[slot][h], cm_scr[slot][h, 0:1, :], 0)

    def pair(i, carry):
        overlapped(2 * i + 1, 1, 2 * i, 0)
        overlapped(2 * i + 2, 0, 2 * i + 1, 1)
        return carry

    scores(0, 0)
    lax.fori_loop(0, qi, pair, 0)

    half = TQ - TK
    lhs_b = key_block(2 * qi + 1)
    sb = [jnp.dot(lhs_b[h // 2], rhs_scr[h, :, half:TQ], preferred_element_type=F32)
          for h in range(nh)]
    r = lax.broadcasted_iota(jnp.int32, (TK, TQ), 0)
    c = lax.broadcasted_iota(jnp.int32, (TK, TQ), 1)
    rb = lax.broadcasted_iota(jnp.int32, (TK, TK), 0)
    cb = lax.broadcasted_iota(jnp.int32, (TK, TK), 1)
    for h in range(nh):
        s = jnp.where(c >= r, s_scr[0][h], NEG_BIG)
        softmax_pv(h, 2 * qi, s, jnp.max(s, axis=0, keepdims=True), 0)
    for h in range(nh):
        s = jnp.where(cb >= rb, sb[h], NEG_BIG)
        softmax_pv(h, 2 * qi + 1, s, jnp.max(s, axis=0, keepdims=True), half)

    for g in range(FOX_G):
        parts = []
        for hh in range(2):
            a = acc_scr[2 * g + hh]
            parts.append(a[0:B_DH, :] * (1.0 / a[B_DH:B_DH + 1, :]))
        o = jnp.concatenate(parts, axis=0).T
        zz = z_ref[:, g * LANES:(g + 1) * LANES].astype(F32)
        out_ref[:, g * LANES:(g + 1) * LANES] = (o * zz).astype(BF16)


def _fox(proj, pieces, bsz, seq):
    m = proj.shape[0]
    nq = seq // FOX_TQ
    nh = 2 * FOX_G
    w = FOX_G * LANES
    ngrp = (B_HEADS * B_DH) // w
    sec = SECTION_W // w
    col0 = SEC_BQ * sec
    assert (SEC_BK, SEC_BV, SEC_BZ) == (SEC_BQ + 1, SEC_BQ + 2, SEC_BQ + 3)
    return pl.pallas_call(
        _fox_kernel,
        grid=(bsz, ngrp, nq),
        in_specs=[
            pl.BlockSpec((FOX_TQ, w), lambda b, hg, qi: (b * nq + qi, col0 + hg)),
            pl.BlockSpec((seq, w), lambda b, hg, qi: (b, col0 + sec + hg)),
            pl.BlockSpec((seq, w), lambda b, hg, qi: (b, col0 + 2 * sec + hg)),
            pl.BlockSpec((FOX_TQ, w), lambda b, hg, qi: (b * nq + qi, col0 + 3 * sec + hg)),
            pl.BlockSpec((None, seq, LANES), lambda b, hg, qi: (b, 0, 0)),
        ],
        out_specs=pl.BlockSpec((FOX_TQ, w), lambda b, hg, qi: (b * nq + qi, hg)),
        out_shape=jax.ShapeDtypeStruct((m, B_HEADS * B_DH), BF16),
        scratch_shapes=[
            pltpu.VMEM((nh, FOX_ACC_ROWS, seq), BF16),
            pltpu.VMEM((nh, FOX_ACC_ROWS, FOX_TQ), F32),
            pltpu.VMEM((nh, 2 * LANES, FOX_TQ), BF16),
            pltpu.VMEM((nh, SUBLANES, FOX_TQ), F32),
            pltpu.VMEM((2, nh, FOX_TK, FOX_TQ), F32),
            pltpu.VMEM((2, nh, SUBLANES, FOX_TQ), F32),
        ],
        compiler_params=pltpu.CompilerParams(
            dimension_semantics=("arbitrary", "arbitrary", "arbitrary"),
            vmem_limit_bytes=VMEM_LIMIT),
        name="fox",
    )(proj, proj, proj, proj, pieces)


def _merge_kernel(ha_ref, hb_ref, ga_ref, gb_ref, x_ref, p_ref, wa_ref, wb_ref, wo_ref, wg_ref,
                  wp_ref, png_ref, fng_ref, out_ref):
    ya = jnp.dot(ha_ref[...], wa_ref[...], preferred_element_type=F32)
    yb = jnp.dot(hb_ref[...], wb_ref[...], preferred_element_type=F32)
    merged = ga_ref[...].astype(F32) * ya + gb_ref[...].astype(F32) * yb
    x1 = x_ref[...] + jnp.dot(merged.astype(BF16), wo_ref[...], preferred_element_type=F32)
    r = _rms_norm(x1, png_ref[...]).astype(BF16)
    gate = _sigmoid(jnp.dot(r, wg_ref[...], preferred_element_type=F32))
    pp = jnp.dot(p_ref[...].astype(BF16), wp_ref[...], preferred_element_type=F32)
    x2 = x1 + gate * pp
    out_ref[...] = _rms_norm(x2, fng_ref[...])


def _merge(ha, hb, proj, x2, p2, wa, wb, wo, wg, wp, png, fng):
    m, d = x2.shape
    pd = p2.shape[1]
    tm = MERGE_TM
    full = lambda r, c: pl.BlockSpec((r, c), lambda i: (0, 0))
    return pl.pallas_call(
        _merge_kernel,
        grid=(m // tm,),
        in_specs=[
            pl.BlockSpec((tm, d), lambda i: (i, 0)),
            pl.BlockSpec((tm, d), lambda i: (i, 0)),
            pl.BlockSpec((tm, d), lambda i: (i, SEC_GA)),
            pl.BlockSpec((tm, d), lambda i: (i, SEC_GB)),
            pl.BlockSpec((tm, d), lambda i: (i, 0)),
            pl.BlockSpec((tm, pd), lambda i: (i, 0)),
            full(d, d), full(d, d), full(d, d), full(d, d), full(pd, d),
            full(1, d), full(1, d),
        ],
        out_specs=pl.BlockSpec((tm, d), lambda i: (i, 0)),
        out_shape=jax.ShapeDtypeStruct((m, d), F32),
        compiler_params=pltpu.CompilerParams(
            dimension_semantics=("arbitrary",), vmem_limit_bytes=VMEM_LIMIT),
        name="merge",
    )(ha, hb, proj, proj, x2, p2, wa, wb, wo, wg, wp, png, fng)


def _split_w_in(w):
    qkw = A_HEADS * A_DQK
    aw = A_HEADS * A_DV
    bw = B_HEADS * B_DH
    d = w.shape[0]
    o_ai = 2 * qkw + aw
    o_ao = o_ai + 2 * A_HEADS
    o_bf = o_ao + 2 * aw + 3 * bw
    o_bz = o_bf + B_HEADS
    cast = lambda a: a.astype(BF16)
    w_main = jnp.concatenate([cast(w[:, :o_ai]), cast(w[:, o_ao:o_bf]), cast(w[:, o_bz:])], axis=1)
    w_gate = jnp.concatenate(
        [cast(w[:, o_ai:o_ao]), cast(w[:, o_bf:o_bz]), jnp.zeros((d, LANES - N_GATE), BF16)], axis=1)
    return w_main, w_gate


def _layer(x, p_i, attn_norm_g, w_in, conv_w, conv_b, a_bias_i, a_bias_f, a_head_norm_g, b_bias_f,
           w_branch_a, w_branch_b, w_out, ple_norm_g, w_ple_gate, w_ple_proj, out_norm_g):
    bsz, seq, d = x.shape
    m = bsz * seq
    x2 = x.reshape(m, d)
    w_main, w_gate = _split_w_in(w_in)
    proj, gates = _in_proj(x2, attn_norm_g.reshape(1, d), w_main, w_gate)

    bias = jnp.concatenate([a_bias_i, a_bias_f, b_bias_f, jnp.zeros((LANES - N_GATE,), F32)])
    gcol, grow, pieces = _gates(gates.reshape(bsz, seq, LANES), bias.reshape(1, LANES))

    ha = _mlstm(proj, gcol, grow, conv_w, conv_b.reshape(1, -1), a_head_norm_g.reshape(1, -1),
                bsz, seq)
    hb = _fox(proj, pieces, bsz, seq)

    out = _merge(ha, hb, proj, x2, p_i.reshape(m, -1),
                 w_branch_a.astype(BF16), w_branch_b.astype(BF16), w_out.astype(BF16),
                 w_ple_gate.astype(BF16), w_ple_proj.astype(BF16),
                 ple_norm_g.reshape(1, d), out_norm_g.reshape(1, d))
    return out.reshape(bsz, seq, d)


def kernel(x, p, attn_norm_g, w_in, conv_w, conv_b, a_bias_i, a_bias_f, a_head_norm_g, b_bias_f,
           w_branch_a, w_branch_b, w_out, ple_norm_g, w_ple_gate, w_ple_proj, final_norm_g):
    depth = w_in.shape[0]
    assert depth == 1, "the final norm is fused into the single layer's merge kernel"
    return _layer(x, p[0], attn_norm_g[0], w_in[0], conv_w[0], conv_b[0], a_bias_i[0], a_bias_f[0],
                  a_head_norm_g[0], b_bias_f[0], w_branch_a[0], w_branch_b[0], w_out[0],
                  ple_norm_g[0], w_ple_gate[0], w_ple_proj[0], final_norm_g)
```

```python
import functools
import math

import jax
import jax.numpy as jnp
from jax import lax
from jax.experimental import pallas as pl
from jax.experimental.pallas import tpu as pltpu

F32 = jnp.float32
BF16 = jnp.bfloat16

EPS = 1e-6
A_HEADS = 4
A_DQK = 128
A_DV = 256
CONV_K = 4
B_HEADS = 16
B_DH = 64
LANES = 128
SUBLANES = 8
BF16_ROWS = 16
NEG_BIG = -1e30
LOG2E = math.log2(math.e)

IN_TM = 2048
IN_TN = 512
W_PREP_TN = 512
CUM_BLK = 256
A_CHUNK = 256
FOX_TK = 256
FOX_TQ = 2 * FOX_TK
FOX_G = 4
MERGE_TM = 512
VMEM_LIMIT = 48 * 1024 * 1024

SECTION_W = A_HEADS * A_DV
SEC_QK, SEC_AV, SEC_AO, SEC_AZ, SEC_BQ, SEC_BK, SEC_BV, SEC_BZ, SEC_GA, SEC_GB = range(10)

N_GATE = 2 * A_HEADS + B_HEADS
B_LANE0 = 2 * A_HEADS
PIECE_OFFS = (B_LANE0, B_LANE0 + B_HEADS, B_LANE0 + 2 * B_HEADS)


def _sigmoid(x):
    return 1.0 / (1.0 + jnp.exp(-x))


def _silu(x):
    return x * _sigmoid(x)


def _rms_norm(x, g):
    ms = jnp.mean(x * x, axis=-1, keepdims=True)
    return (x * lax.rsqrt(ms + EPS)) * g


def _split3(x):
    x1 = x.astype(BF16)
    r1 = x - x1.astype(F32)
    x2 = r1.astype(BF16)
    x3 = (r1 - x2.astype(F32)).astype(BF16)
    return x1, x2, x3


def _in_proj_kernel(x_ref, g_ref, w_ref, wg_ref, proj_ref, gates_ref, h_scr):
    @pl.when(pl.program_id(1) == 0)
    def _():
        h = _rms_norm(x_ref[...], g_ref[...]).astype(BF16)
        h_scr[...] = h
        gates_ref[...] = jnp.dot(h, wg_ref[...], preferred_element_type=F32)

    proj_ref[...] = jnp.dot(h_scr[...], w_ref[...], preferred_element_type=F32).astype(BF16)


def _in_proj(x2, g, w_main, w_gate):
    m, d = x2.shape
    n = w_main.shape[1]
    return pl.pallas_call(
        _in_proj_kernel,
        grid=(m // IN_TM, n // IN_TN),
        in_specs=[
            pl.BlockSpec((IN_TM, d), lambda i, j: (i, 0)),
            pl.BlockSpec((1, d), lambda i, j: (0, 0)),
            pl.BlockSpec((d, IN_TN), lambda i, j: (0, j)),
            pl.BlockSpec((d, LANES), lambda i, j: (0, 0)),
        ],
        out_specs=[
            pl.BlockSpec((IN_TM, IN_TN), lambda i, j: (i, j)),
            pl.BlockSpec((IN_TM, LANES), lambda i, j: (i, 0)),
        ],
        out_shape=[
            jax.ShapeDtypeStruct((m, n), BF16),
            jax.ShapeDtypeStruct((m, LANES), F32),
        ],
        scratch_shapes=[pltpu.VMEM((IN_TM, d), BF16)],
        compiler_params=pltpu.CompilerParams(
            dimension_semantics=("arbitrary", "arbitrary"),
            vmem_limit_bytes=VMEM_LIMIT),
        name="in_proj",
    )(x2, g, w_main, w_gate)


def _gates_kernel(g_ref, bias_ref, col_ref, row_ref, pc_ref):
    x = g_ref[...] + bias_ref[...]
    s = x.shape[0]
    ls = jnp.minimum(x, 0.0) - jnp.log1p(jnp.exp(-jnp.abs(x)))
    r = lax.broadcasted_iota(jnp.int32, (CUM_BLK, CUM_BLK), 0)
    c = lax.broadcasted_iota(jnp.int32, (CUM_BLK, CUM_BLK), 1)
    tri = jnp.where(r >= c, 1.0, 0.0).astype(BF16)
    carry = jnp.zeros((1, LANES), F32)
    blocks = []
    for blk in range(s // CUM_BLK):
        x1, x2, x3 = _split3(ls[blk * CUM_BLK:(blk + 1) * CUM_BLK])
        cs = (jnp.dot(tri, x3, preferred_element_type=F32)
              + jnp.dot(tri, x2, preferred_element_type=F32)
              + jnp.dot(tri, x1, preferred_element_type=F32)) + carry
        carry = cs[CUM_BLK - 1:CUM_BLK, :]
        blocks.append(cs)
    cum = jnp.concatenate(blocks, axis=0)
    lane = lax.broadcasted_iota(jnp.int32, x.shape, 1)
    res = jnp.where(lane < A_HEADS, x, cum)
    col_ref[...] = res
    row_ref[...] = res.T[0:SUBLANES, :]

    in_b = (lane >= B_LANE0) & (lane < B_LANE0 + B_HEADS)
    p1, p2, p3 = _split3(jnp.where(in_b, cum * (-LOG2E), 0.0))
    pieces = (p1.astype(F32)
              + pltpu.roll(p2.astype(F32), PIECE_OFFS[1] - B_LANE0, axis=1)
              + pltpu.roll(p3.astype(F32), PIECE_OFFS[2] - B_LANE0, axis=1))
    pc_ref[...] = pieces.astype(BF16)


def _gates(gates3, bias):
    b, s, _ = gates3.shape
    return pl.pallas_call(
        _gates_kernel,
        grid=(b,),
        in_specs=[
            pl.BlockSpec((None, s, LANES), lambda i: (i, 0, 0)),
            pl.BlockSpec((1, LANES), lambda i: (0, 0)),
        ],
        out_specs=[
            pl.BlockSpec((None, s, LANES), lambda i: (i, 0, 0)),
            pl.BlockSpec((None, SUBLANES, s), lambda i: (i, 0, 0)),
            pl.BlockSpec((None, s, LANES), lambda i: (i, 0, 0)),
        ],
        out_shape=[
            jax.ShapeDtypeStruct((b, s, LANES), F32),
            jax.ShapeDtypeStruct((b, SUBLANES, s), F32),
            jax.ShapeDtypeStruct((b, s, LANES), BF16),
        ],
        compiler_params=pltpu.CompilerParams(
            dimension_semantics=("arbitrary",), vmem_limit_bytes=VMEM_LIMIT),
        name="gates",
    )(gates3, bias)


def _mlstm_kernel(qk_ref, v_ref, o_ref, z_ref, gcol_ref, grow_ref, cw_ref, cb_ref, hg_ref,
                  out_ref, xpad_scr, c_scr, n_scr, m_scr, fprev_scr):
    t = pl.program_id(1)
    L = A_CHUNK
    qkw = A_HEADS * A_DQK

    @pl.when(t == 0)
    def _():
        xpad_scr[0:SUBLANES, :] = jnp.zeros((SUBLANES, 2 * qkw), F32)
        c_scr[...] = jnp.zeros_like(c_scr)
        n_scr[...] = jnp.zeros_like(n_scr)
        m_scr[...] = jnp.zeros_like(m_scr)
        fprev_scr[...] = jnp.zeros_like(fprev_scr)

    xpad_scr[SUBLANES:SUBLANES + L, :] = qk_ref[...].astype(F32)
    y = cb_ref[...] + xpad_scr[SUBLANES:SUBLANES + L, :] * cw_ref[CONV_K - 1:CONV_K, :]
    for d in range(1, CONV_K):
        y = y + xpad_scr[SUBLANES - d:SUBLANES - d + L, :] * cw_ref[CONV_K - 1 - d:CONV_K - d, :]
    xpad_scr[0:SUBLANES, :] = xpad_scr[L:L + SUBLANES, :]
    qk = _silu(y)

    row = lax.broadcasted_iota(jnp.int32, (L, L), 0)
    col = lax.broadcasted_iota(jnp.int32, (L, L), 1)
    causal = row >= col

    for h in range(A_HEADS):
        q = qk[:, h * A_DQK:(h + 1) * A_DQK]
        k = qk[:, qkw + h * A_DQK:qkw + (h + 1) * A_DQK] * (A_DQK ** -0.5)
        qb = q.astype(BF16)
        kb = k.astype(BF16)
        v = v_ref[:, h * A_DV:(h + 1) * A_DV]

        li_c = gcol_ref[:, h:h + 1]
        f_c = gcol_ref[:, A_HEADS + h:A_HEADS + h + 1]
        li_r = grow_ref[h:h + 1, :]
        f_r = grow_ref[A_HEADS + h:A_HEADS + h + 1, :]
        f_prev = fprev_scr[0:1, A_HEADS + h:A_HEADS + h + 1]
        f_end = gcol_ref[L - 1:L, A_HEADS + h:A_HEADS + h + 1]
        m_st = m_scr[h, 0:1, 0:1]
        c_st = c_scr[h]
        n_st = n_scr[h]

        dmat = jnp.where(causal, (f_c - f_r) + li_r, NEG_BIG)
        inter = (f_c - f_prev) + m_st
        m_row = jnp.maximum(inter, jnp.max(dmat, axis=-1, keepdims=True))
        w_intra = jnp.exp(dmat - m_row)
        w_inter = jnp.exp(inter - m_row)
        s = lax.dot_general(qb, kb, (((1,), (1,)), ((), ())), preferred_element_type=F32)
        scores = s * w_intra
        num = (jnp.dot(scores.astype(BF16), v, preferred_element_type=F32)
               + w_inter * jnp.dot(qb, c_st.astype(BF16), preferred_element_type=F32))
        den = (jnp.sum(scores, axis=-1, keepdims=True)
               + w_inter * jnp.sum(q * n_st, axis=-1, keepdims=True))
        hh = num * (1.0 / jnp.maximum(jnp.abs(den), jnp.exp(-m_row)))

        g_tot = f_end - f_prev
        to_end = (f_end - f_c) + li_c
        m_new = jnp.maximum(g_tot + m_st, jnp.max(to_end, axis=0, keepdims=True))
        w_k = jnp.exp(to_end - m_new)
        decay = jnp.exp(g_tot + m_st - m_new)
        kw = k * w_k
        c_scr[h] = decay * c_st + jnp.dot(kw.T.astype(BF16), v, preferred_element_type=F32)
        n_scr[h] = decay * n_st + jnp.sum(kw, axis=0, keepdims=True)
        m_scr[h] = jnp.broadcast_to(m_new, (SUBLANES, LANES))

        hn = hh * lax.rsqrt(jnp.mean(hh * hh, axis=-1, keepdims=True) + EPS)
        hn = hn * hg_ref[:, h * A_DV:(h + 1) * A_DV]
        og = _sigmoid(o_ref[:, h * A_DV:(h + 1) * A_DV].astype(F32))
        zz = _silu(z_ref[:, h * A_DV:(h + 1) * A_DV].astype(F32))
        out_ref[:, h * A_DV:(h + 1) * A_DV] = ((og * hn) * zz).astype(BF16)

    fprev_scr[...] = gcol_ref[L - 1:L, :]


def _mlstm(proj, gcol, grow, conv_w, conv_b, head_g, bsz, seq):
    m = proj.shape[0]
    width = SECTION_W
    nt = seq // A_CHUNK
    row_blk = lambda c: pl.BlockSpec((A_CHUNK, width), lambda b, t: (b * nt + t, c))
    return pl.pallas_call(
        _mlstm_kernel,
        grid=(bsz, nt),
        in_specs=[
            row_blk(SEC_QK), row_blk(SEC_AV), row_blk(SEC_AO), row_blk(SEC_AZ),
            pl.BlockSpec((None, A_CHUNK, LANES), lambda b, t: (b, t, 0)),
            pl.BlockSpec((None, SUBLANES, A_CHUNK), lambda b, t: (b, 0, t)),
            pl.BlockSpec((CONV_K, width), lambda b, t: (0, 0)),
            pl.BlockSpec((1, width), lambda b, t: (0, 0)),
            pl.BlockSpec((1, width), lambda b, t: (0, 0)),
        ],
        out_specs=pl.BlockSpec((A_CHUNK, width), lambda b, t: (b * nt + t, 0)),
        out_shape=jax.ShapeDtypeStruct((m, width), BF16),
        scratch_shapes=[
            pltpu.VMEM((A_CHUNK + SUBLANES, width), F32),
            pltpu.VMEM((A_HEADS, A_DQK, A_DV), F32),
            pltpu.VMEM((A_HEADS, 1, A_DQK), F32),
            pltpu.VMEM((A_HEADS, SUBLANES, LANES), F32),
            pltpu.VMEM((1, LANES), F32),
        ],
        compiler_params=pltpu.CompilerParams(
            dimension_semantics=("arbitrary", "arbitrary"), vmem_limit_bytes=VMEM_LIMIT),
        name="mlstm",
    )(proj, proj, proj, proj, gcol, grow, conv_w, conv_b, head_g)


FOX_ACC_ROWS = B_DH + BF16_ROWS


def _fox_kernel(q_ref, k_ref, v_ref, z_ref, pc_ref, out_ref, vt_scr, acc_scr, rhs_scr, m_scr,
                s2_scr, cm2_scr):
    s_scr = (s2_scr.at[0], s2_scr.at[1])
    cm_scr = (cm2_scr.at[0], cm2_scr.at[1])
    hg = pl.program_id(1)
    qi = pl.program_id(2)
    TQ, TK = FOX_TQ, FOX_TK
    seq = k_ref.shape[0]
    nh = 2 * FOX_G

    @pl.when(qi == 0)
    def _():
        for g in range(FOX_G):
            vt = v_ref[:, g * LANES:(g + 1) * LANES].astype(F32).T
            for hh in range(2):
                vt_scr[2 * g + hh, 0:B_DH, :] = vt[hh * B_DH:(hh + 1) * B_DH, :].astype(BF16)
                vt_scr[2 * g + hh, B_DH:FOX_ACC_ROWS, :] = jnp.ones((BF16_ROWS, seq), BF16)

    row = lax.broadcasted_iota(jnp.int32, (LANES, TQ), 0)
    for g in range(FOX_G):
        qt = (q_ref[:, g * LANES:(g + 1) * LANES].astype(F32) * (B_DH ** -0.5 * LOG2E)).T
        for hh in range(2):
            h = 2 * g + hh
            head = hg * nh + h
            qm = jnp.where((row >= hh * B_DH) & (row < (hh + 1) * B_DH), qt, 0.0)
            sel = jnp.where((row == PIECE_OFFS[0] + head) | (row == PIECE_OFFS[1] + head)
                            | (row == PIECE_OFFS[2] + head), 1.0, 0.0)
            rhs_scr[h, 0:LANES, :] = qm.astype(BF16)
            rhs_scr[h, LANES:2 * LANES, :] = sel.astype(BF16)

    acc_scr[...] = jnp.zeros_like(acc_scr)
    m_scr[...] = jnp.full(m_scr.shape, NEG_BIG, F32)

    def key_block(kj):
        k0 = pl.multiple_of(kj * TK, TK)
        pcs = pc_ref[pl.ds(k0, TK), :]
        return [jnp.concatenate([k_ref[pl.ds(k0, TK), g * LANES:(g + 1) * LANES], pcs], axis=1)
                for g in range(FOX_G)]

    def scores_head(h, lhs, slot):
        s = jnp.dot(lhs[h // 2], rhs_scr[h], preferred_element_type=F32)
        s_scr[slot][h] = s
        cm_scr[slot][h] = jnp.broadcast_to(jnp.max(s, axis=0, keepdims=True), (SUBLANES, TQ))

    def scores(kj, slot):
        lhs = key_block(kj)
        for h in range(nh):
            scores_head(h, lhs, slot)

    def softmax_pv(h, kj, s, cmax, lo):
        k0 = pl.multiple_of(kj * TK, TK)
        m_old = m_scr[h, 0:1, lo:TQ]
        m_new = jnp.maximum(m_old, cmax)
        alpha = jnp.exp2(m_old - m_new)
        p = jnp.exp2(s - m_new).astype(BF16)
        pv = jnp.dot(vt_scr[h, :, pl.ds(k0, TK)], p, preferred_element_type=F32)
        acc_scr[h, :, lo:TQ] = alpha * acc_scr[h, :, lo:TQ] + pv
        m_scr[h, :, lo:TQ] = jnp.broadcast_to(m_new, (SUBLANES, TQ - lo))

    def overlapped(kj_next, slot_next, kj, slot):
        lhs = key_block(kj_next)
        for h in range(nh):
            scores_head(h, lhs, slot_next)
            softmax_pv(h, kj, s_scr[slot][h], cm_scr[slot][h, 0:1, :], 0)

    def pair(i, carry):
        overlapped(2 * i + 1, 1, 2 * i, 0)
        overlapped(2 * i + 2, 0, 2 * i + 1, 1)
        return carry

    scores(0, 0)
    lax.fori_loop(0, qi, pair, 0)

    half = TQ - TK
    lhs_b = key_block(2 * qi + 1)
    sb = [jnp.dot(lhs_b[h // 2], rhs_scr[h, :, half:TQ], preferred_element_type=F32)
          for h in range(nh)]
    r = lax.broadcasted_iota(jnp.int32, (TK, TQ), 0)
    c = lax.broadcasted_iota(jnp.int32, (TK, TQ), 1)
    rb = lax.broadcasted_iota(jnp.int32, (TK, TK), 0)
    cb = lax.broadcasted_iota(jnp.int32, (TK, TK), 1)
    for h in range(nh):
        s = jnp.where(c >= r, s_scr[0][h], NEG_BIG)
        softmax_pv(h, 2 * qi, s, jnp.max(s, axis=0, keepdims=True), 0)
    for h in range(nh):
        s = jnp.where(cb >= rb, sb[h], NEG_BIG)
        softmax_pv(h, 2 * qi + 1, s, jnp.max(s, axis=0, keepdims=True), half)

    for g in range(FOX_G):
        parts = []
        for hh in range(2):
            a = acc_scr[2 * g + hh]
            parts.append(a[0:B_DH, :] * (1.0 / a[B_DH:B_DH + 1, :]))
        o = jnp.concatenate(parts, axis=0).T
        zz = _silu(z_ref[:, g * LANES:(g + 1) * LANES].astype(F32))
        out_ref[:, g * LANES:(g + 1) * LANES] = (o * zz).astype(BF16)


def _fox(proj, pieces, bsz, seq):
    m = proj.shape[0]
    nq = seq // FOX_TQ
    nh = 2 * FOX_G
    w = FOX_G * LANES
    ngrp = (B_HEADS * B_DH) // w
    sec = SECTION_W // w
    col0 = SEC_BQ * sec
    assert (SEC_BK, SEC_BV, SEC_BZ) == (SEC_BQ + 1, SEC_BQ + 2, SEC_BQ + 3)
    return pl.pallas_call(
        _fox_kernel,
        grid=(bsz, ngrp, nq),
        in_specs=[
            pl.BlockSpec((FOX_TQ, w), lambda b, hg, qi: (b * nq + qi, col0 + hg)),
            pl.BlockSpec((seq, w), lambda b, hg, qi: (b, col0 + sec + hg)),
            pl.BlockSpec((seq, w), lambda b, hg, qi: (b, col0 + 2 * sec + hg)),
            pl.BlockSpec((FOX_TQ, w), lambda b, hg, qi: (b * nq + qi, col0 + 3 * sec + hg)),
            pl.BlockSpec((None, seq, LANES), lambda b, hg, qi: (b, 0, 0)),
        ],
        out_specs=pl.BlockSpec((FOX_TQ, w), lambda b, hg, qi: (b * nq + qi, hg)),
        out_shape=jax.ShapeDtypeStruct((m, B_HEADS * B_DH), BF16),
        scratch_shapes=[
            pltpu.VMEM((nh, FOX_ACC_ROWS, seq), BF16),
            pltpu.VMEM((nh, FOX_ACC_ROWS, FOX_TQ), F32),
            pltpu.VMEM((nh, 2 * LANES, FOX_TQ), BF16),
            pltpu.VMEM((nh, SUBLANES, FOX_TQ), F32),
            pltpu.VMEM((2, nh, FOX_TK, FOX_TQ), F32),
            pltpu.VMEM((2, nh, SUBLANES, FOX_TQ), F32),
        ],
        compiler_params=pltpu.CompilerParams(
            dimension_semantics=("arbitrary", "arbitrary", "arbitrary"),
            vmem_limit_bytes=VMEM_LIMIT),
        name="fox",
    )(proj, proj, proj, proj, pieces)


def _merge_kernel(ha_ref, hb_ref, ga_ref, gb_ref, x_ref, p_ref, wa_ref, wb_ref, wo_ref, wg_ref,
                  wp_ref, png_ref, fng_ref, out_ref):
    ya = jnp.dot(ha_ref[...], wa_ref[...], preferred_element_type=F32)
    yb = jnp.dot(hb_ref[...], wb_ref[...], preferred_element_type=F32)
    merged = (_sigmoid(ga_ref[...].astype(F32)) * ya + _sigmoid(gb_ref[...].astype(F32)) * yb)
    x1 = x_ref[...] + jnp.dot(merged.astype(BF16), wo_ref[...], preferred_element_type=F32)
    r = _rms_norm(x1, png_ref[...]).astype(BF16)
    gate = _sigmoid(jnp.dot(r, wg_ref[...], preferred_element_type=F32))
    pp = jnp.dot(p_ref[...].astype(BF16), wp_ref[...], preferred_element_type=F32)
    x2 = x1 + gate * pp
    out_ref[...] = _rms_norm(x2, fng_ref[...])


def _merge(ha, hb, proj, x2, p2, wa, wb, wo, wg, wp, png, fng):
    m, d = x2.shape
    pd = p2.shape[1]
    tm = MERGE_TM
    full = lambda r, c: pl.BlockSpec((r, c), lambda i: (0, 0))
    return pl.pallas_call(
        _merge_kernel,
        grid=(m // tm,),
        in_specs=[
            pl.BlockSpec((tm, d), lambda i: (i, 0)),
            pl.BlockSpec((tm, d), lambda i: (i, 0)),
            pl.BlockSpec((tm, d), lambda i: (i, SEC_GA)),
            pl.BlockSpec((tm, d), lambda i: (i, SEC_GB)),
            pl.BlockSpec((tm, d), lambda i: (i, 0)),
            pl.BlockSpec((tm, pd), lambda i: (i, 0)),
            full(d, d), full(d, d), full(d, d), full(d, d), full(pd, d),
            full(1, d), full(1, d),
        ],
        out_specs=pl.BlockSpec((tm, d), lambda i: (i, 0)),
        out_shape=jax.ShapeDtypeStruct((m, d), F32),
        compiler_params=pltpu.CompilerParams(
            dimension_semantics=("arbitrary",), vmem_limit_bytes=VMEM_LIMIT),
        name="merge",
    )(ha, hb, proj, proj, x2, p2, wa, wb, wo, wg, wp, png, fng)


def _w_prep_kernel(seg_tiles, shifts, a_ref, b_ref, out_ref):
    j = pl.program_id(0)
    lo = 0
    for n_tiles, shift in zip(seg_tiles, shifts):
        @pl.when((j >= lo) & (j < lo + n_tiles))
        def _(shift=shift):
            if shift == 0:
                out_ref[...] = a_ref[...].astype(BF16)
            else:
                wide = jnp.concatenate([a_ref[...], b_ref[...]], axis=1)
                out_ref[...] = wide[:, shift:shift + W_PREP_TN].astype(BF16)
        lo += n_tiles


def _split_w_in(w):
    qkw = A_HEADS * A_DQK
    aw = A_HEADS * A_DV
    bw = B_HEADS * B_DH
    d = w.shape[0]
    o_ai = 2 * qkw + aw
    o_ao = o_ai + 2 * A_HEADS
    o_bf = o_ao + 2 * aw + 3 * bw
    o_bz = o_bf + B_HEADS
    n_main = w.shape[1] - N_GATE
    seg_cols = (o_ai, o_bf - o_ao, w.shape[1] - o_bz)
    shifts = (0, o_ao - o_ai, o_ao - o_ai + o_bz - o_bf)
    assert all(c % W_PREP_TN == 0 for c in seg_cols) and max(shifts) <= LANES
    seg_tiles = tuple(c // W_PREP_TN for c in seg_cols)
    per = W_PREP_TN // LANES
    w_main = pl.pallas_call(
        functools.partial(_w_prep_kernel, seg_tiles, shifts),
        grid=(n_main // W_PREP_TN,),
        in_specs=[
            pl.BlockSpec((d, W_PREP_TN), lambda j: (0, j)),
            pl.BlockSpec((d, LANES), lambda j: (0, per * (j + 1))),
        ],
        out_specs=pl.BlockSpec((d, W_PREP_TN), lambda j: (0, j)),
        out_shape=jax.ShapeDtypeStruct((d, n_main), BF16),
        compiler_params=pltpu.CompilerParams(
            dimension_semantics=("arbitrary",), vmem_limit_bytes=VMEM_LIMIT),
        name="w_prep",
    )(w, w)
    cast = lambda a: a.astype(BF16)
    w_gate = jnp.concatenate(
        [cast(w[:, o_ai:o_ao]), cast(w[:, o_bf:o_bz]), jnp.zeros((d, LANES - N_GATE), BF16)], axis=1)
    return w_main, w_gate


def _layer(x, p_i, attn_norm_g, w_in, conv_w, conv_b, a_bias_i, a_bias_f, a_head_norm_g, b_bias_f,
           w_branch_a, w_branch_b, w_out, ple_norm_g, w_ple_gate, w_ple_proj, out_norm_g):
    bsz, seq, d = x.shape
    m = bsz * seq
    x2 = x.reshape(m, d)
    w_main, w_gate = _split_w_in(w_in)
    proj, gates = _in_proj(x2, attn_norm_g.reshape(1, d), w_main, w_gate)

    bias = jnp.concatenate([a_bias_i, a_bias_f, b_bias_f, jnp.zeros((LANES - N_GATE,), F32)])
    gcol, grow, pieces = _gates(gates.reshape(bsz, seq, LANES), bias.reshape(1, LANES))

    ha = _mlstm(proj, gcol, grow, conv_w, conv_b.reshape(1, -1), a_head_norm_g.reshape(1, -1),
                bsz, seq)
    hb = _fox(proj, pieces, bsz, seq)

    out = _merge(ha, hb, proj, x2, p_i.reshape(m, -1),
                 w_branch_a.astype(BF16), w_branch_b.astype(BF16), w_out.astype(BF16),
                 w_ple_gate.astype(BF16), w_ple_proj.astype(BF16),
                 ple_norm_g.reshape(1, d), out_norm_g.reshape(1, d))
    return out.reshape(bsz, seq, d)


def kernel(x, p, attn_norm_g, w_in, conv_w, conv_b, a_bias_i, a_bias_f, a_head_norm_g, b_bias_f,
           w_branch_a, w_branch_b, w_out, ple_norm_g, w_ple_gate, w_ple_proj, final_norm_g):
    depth = w_in.shape[0]
    assert depth == 1, "the final norm is fused into the single layer's merge kernel"
    return _layer(x, p[0], attn_norm_g[0], w_in[0], conv_w[0], conv_b[0], a_bias_i[0], a_bias_f[0],
                  a_head_norm_g[0], b_bias_f[0], w_branch_a[0], w_branch_b[0], w_out[0],
                  ple_norm_g[0], w_ple_gate[0], w_ple_proj[0], final_norm_g)
```

```python
import math

import jax
import jax.numpy as jnp
from jax import lax
from jax.experimental import pallas as pl
from jax.experimental.pallas import tpu as pltpu

F32 = jnp.float32
BF16 = jnp.bfloat16

EPS = 1e-6
A_HEADS = 4
A_DQK = 128
A_DV = 256
CONV_K = 4
B_HEADS = 16
B_DH = 64
LANES = 128
SUBLANES = 8
BF16_ROWS = 16
NEG_BIG = -1e30
LOG2E = math.log2(math.e)

IN_TM = 2048
IN_TN = 512
CUM_BLK = 256
A_CHUNK = 256
FOX_TK = 256
FOX_TQ = 2 * FOX_TK
FOX_G = 4
MERGE_TM = 512
VMEM_LIMIT = 48 * 1024 * 1024

SECTION_W = A_HEADS * A_DV
SEC_QK, SEC_AV, SEC_AO, SEC_AZ, SEC_BQ, SEC_BK, SEC_BV, SEC_BZ, SEC_GA, SEC_GB = range(10)

N_GATE = 2 * A_HEADS + B_HEADS
B_LANE0 = 2 * A_HEADS
PIECE_OFFS = (B_LANE0, B_LANE0 + B_HEADS, B_LANE0 + 2 * B_HEADS)


def _sigmoid(x):
    return 1.0 / (1.0 + jnp.exp(-x))


def _silu(x):
    return x * _sigmoid(x)


def _rms_norm(x, g):
    ms = jnp.mean(x * x, axis=-1, keepdims=True)
    return (x * lax.rsqrt(ms + EPS)) * g


def _split3(x):
    x1 = x.astype(BF16)
    r1 = x - x1.astype(F32)
    x2 = r1.astype(BF16)
    x3 = (r1 - x2.astype(F32)).astype(BF16)
    return x1, x2, x3


_NT = (((1,), (1,)), ((), ()))


def _in_proj_kernel(x_ref, g_ref, wt_ref, wgt_ref, proj_ref, gates_ref, h_scr):
    @pl.when(pl.program_id(1) == 0)
    def _():
        h = _rms_norm(x_ref[...], g_ref[...]).astype(BF16)
        h_scr[...] = h
        gates_ref[...] = lax.dot_general(h, wgt_ref[...].astype(BF16), _NT,
                                         preferred_element_type=F32)

    proj_ref[...] = lax.dot_general(h_scr[...], wt_ref[...].astype(BF16), _NT,
                                    preferred_element_type=F32).astype(BF16)


def _in_proj(x2, g, w_t, w_gate_t, seg_tiles, shifts):
    m, d = x2.shape
    n = sum(seg_tiles) * IN_TN

    def w_rows(i, j):
        shift = shifts[0] // SUBLANES
        lo = 0
        for n_tiles, s in zip(seg_tiles[:-1], shifts[1:]):
            lo += n_tiles
            shift = jnp.where(j >= lo, s // SUBLANES, shift)
        return ((j * (IN_TN // SUBLANES) + shift) * SUBLANES, 0)

    return pl.pallas_call(
        _in_proj_kernel,
        grid=(m // IN_TM, n // IN_TN),
        in_specs=[
            pl.BlockSpec((IN_TM, d), lambda i, j: (i, 0)),
            pl.BlockSpec((1, d), lambda i, j: (0, 0)),
            pl.BlockSpec((pl.Element(IN_TN), pl.Element(d)), w_rows),
            pl.BlockSpec((LANES, d), lambda i, j: (0, 0)),
        ],
        out_specs=[
            pl.BlockSpec((IN_TM, IN_TN), lambda i, j: (i, j)),
            pl.BlockSpec((IN_TM, LANES), lambda i, j: (i, 0)),
        ],
        out_shape=[
            jax.ShapeDtypeStruct((m, n), BF16),
            jax.ShapeDtypeStruct((m, LANES), F32),
        ],
        scratch_shapes=[pltpu.VMEM((IN_TM, d), BF16)],
        compiler_params=pltpu.CompilerParams(
            dimension_semantics=("arbitrary", "arbitrary"),
            vmem_limit_bytes=VMEM_LIMIT),
        name="in_proj",
    )(x2, g, w_t, w_gate_t)


def _gates_kernel(g_ref, bias_ref, col_ref, row_ref, pc_ref):
    x = g_ref[...] + bias_ref[...]
    s = x.shape[0]
    ls = jnp.minimum(x, 0.0) - jnp.log1p(jnp.exp(-jnp.abs(x)))
    r = lax.broadcasted_iota(jnp.int32, (CUM_BLK, CUM_BLK), 0)
    c = lax.broadcasted_iota(jnp.int32, (CUM_BLK, CUM_BLK), 1)
    tri = jnp.where(r >= c, 1.0, 0.0).astype(BF16)
    carry = jnp.zeros((1, LANES), F32)
    blocks = []
    for blk in range(s // CUM_BLK):
        x1, x2, x3 = _split3(ls[blk * CUM_BLK:(blk + 1) * CUM_BLK])
        cs = (jnp.dot(tri, x3, preferred_element_type=F32)
              + jnp.dot(tri, x2, preferred_element_type=F32)
              + jnp.dot(tri, x1, preferred_element_type=F32)) + carry
        carry = cs[CUM_BLK - 1:CUM_BLK, :]
        blocks.append(cs)
    cum = jnp.concatenate(blocks, axis=0)
    lane = lax.broadcasted_iota(jnp.int32, x.shape, 1)
    res = jnp.where(lane < A_HEADS, x, cum)
    col_ref[...] = res
    row_ref[...] = res.T[0:SUBLANES, :]

    in_b = (lane >= B_LANE0) & (lane < B_LANE0 + B_HEADS)
    p1, p2, p3 = _split3(jnp.where(in_b, cum * (-LOG2E), 0.0))
    pieces = (p1.astype(F32)
              + pltpu.roll(p2.astype(F32), PIECE_OFFS[1] - B_LANE0, axis=1)
              + pltpu.roll(p3.astype(F32), PIECE_OFFS[2] - B_LANE0, axis=1))
    pc_ref[...] = pieces.astype(BF16)


def _gates(gates3, bias):
    b, s, _ = gates3.shape
    return pl.pallas_call(
        _gates_kernel,
        grid=(b,),
        in_specs=[
            pl.BlockSpec((None, s, LANES), lambda i: (i, 0, 0)),
            pl.BlockSpec((1, LANES), lambda i: (0, 0)),
        ],
        out_specs=[
            pl.BlockSpec((None, s, LANES), lambda i: (i, 0, 0)),
            pl.BlockSpec((None, SUBLANES, s), lambda i: (i, 0, 0)),
            pl.BlockSpec((None, s, LANES), lambda i: (i, 0, 0)),
        ],
        out_shape=[
            jax.ShapeDtypeStruct((b, s, LANES), F32),
            jax.ShapeDtypeStruct((b, SUBLANES, s), F32),
            jax.ShapeDtypeStruct((b, s, LANES), BF16),
        ],
        compiler_params=pltpu.CompilerParams(
            dimension_semantics=("arbitrary",), vmem_limit_bytes=VMEM_LIMIT),
        name="gates",
    )(gates3, bias)


def _mlstm_kernel(qk_ref, v_ref, o_ref, z_ref, gcol_ref, grow_ref, cw_ref, cb_ref, hg_ref,
                  out_ref, xpad_scr, c_scr, n_scr, m_scr, fprev_scr):
    t = pl.program_id(1)
    L = A_CHUNK
    qkw = A_HEADS * A_DQK

    @pl.when(t == 0)
    def _():
        xpad_scr[0:SUBLANES, :] = jnp.zeros((SUBLANES, 2 * qkw), F32)
        c_scr[...] = jnp.zeros_like(c_scr)
        n_scr[...] = jnp.zeros_like(n_scr)
        m_scr[...] = jnp.zeros_like(m_scr)
        fprev_scr[...] = jnp.zeros_like(fprev_scr)

    xpad_scr[SUBLANES:SUBLANES + L, :] = qk_ref[...].astype(F32)
    y = cb_ref[...] + xpad_scr[SUBLANES:SUBLANES + L, :] * cw_ref[CONV_K - 1:CONV_K, :]
    for d in range(1, CONV_K):
        y = y + xpad_scr[SUBLANES - d:SUBLANES - d + L, :] * cw_ref[CONV_K - 1 - d:CONV_K - d, :]
    xpad_scr[0:SUBLANES, :] = xpad_scr[L:L + SUBLANES, :]
    qk = _silu(y)

    row = lax.broadcasted_iota(jnp.int32, (L, L), 0)
    col = lax.broadcasted_iota(jnp.int32, (L, L), 1)
    causal = row >= col

    for h in range(A_HEADS):
        q = qk[:, h * A_DQK:(h + 1) * A_DQK]
        k = qk[:, qkw + h * A_DQK:qkw + (h + 1) * A_DQK] * (A_DQK ** -0.5)
        qb = q.astype(BF16)
        kb = k.astype(BF16)
        v = v_ref[:, h * A_DV:(h + 1) * A_DV]

        li_c = gcol_ref[:, h:h + 1]
        f_c = gcol_ref[:, A_HEADS + h:A_HEADS + h + 1]
        li_r = grow_ref[h:h + 1, :]
        f_r = grow_ref[A_HEADS + h:A_HEADS + h + 1, :]
        f_prev = fprev_scr[0:1, A_HEADS + h:A_HEADS + h + 1]
        f_end = gcol_ref[L - 1:L, A_HEADS + h:A_HEADS + h + 1]
        m_st = m_scr[h, 0:1, 0:1]
        c_st = c_scr[h]
        n_st = n_scr[h]

        dmat = jnp.where(causal, (f_c - f_r) + li_r, NEG_BIG)
        inter = (f_c - f_prev) + m_st
        m_row = jnp.maximum(inter, jnp.max(dmat, axis=-1, keepdims=True))
        w_intra = jnp.exp(dmat - m_row)
        w_inter = jnp.exp(inter - m_row)
        s = lax.dot_general(qb, kb, (((1,), (1,)), ((), ())), preferred_element_type=F32)
        scores = s * w_intra
        num = (jnp.dot(scores.astype(BF16), v, preferred_element_type=F32)
               + w_inter * jnp.dot(qb, c_st.astype(BF16), preferred_element_type=F32))
        den = (jnp.sum(scores, axis=-1, keepdims=True)
               + w_inter * jnp.sum(q * n_st, axis=-1, keepdims=True))
        hh = num * (1.0 / jnp.maximum(jnp.abs(den), jnp.exp(-m_row)))

        g_tot = f_end - f_prev
        to_end = (f_end - f_c) + li_c
        m_new = jnp.maximum(g_tot + m_st, jnp.max(to_end, axis=0, keepdims=True))
        w_k = jnp.exp(to_end - m_new)
        decay = jnp.exp(g_tot + m_st - m_new)
        kw = k * w_k
        c_scr[h] = decay * c_st + jnp.dot(kw.T.astype(BF16), v, preferred_element_type=F32)
        n_scr[h] = decay * n_st + jnp.sum(kw, axis=0, keepdims=True)
        m_scr[h] = jnp.broadcast_to(m_new, (SUBLANES, LANES))

        hn = hh * lax.rsqrt(jnp.mean(hh * hh, axis=-1, keepdims=True) + EPS)
        hn = hn * hg_ref[:, h * A_DV:(h + 1) * A_DV]
        og = _sigmoid(o_ref[:, h * A_DV:(h + 1) * A_DV].astype(F32))
        zz = _silu(z_ref[:, h * A_DV:(h + 1) * A_DV].astype(F32))
        out_ref[:, h * A_DV:(h + 1) * A_DV] = ((og * hn) * zz).astype(BF16)

    fprev_scr[...] = gcol_ref[L - 1:L, :]


def _mlstm(proj, gcol, grow, conv_w, conv_b, head_g, bsz, seq):
    m = proj.shape[0]
    width = SECTION_W
    nt = seq // A_CHUNK
    row_blk = lambda c: pl.BlockSpec((A_CHUNK, width), lambda b, t: (b * nt + t, c))
    return pl.pallas_call(
        _mlstm_kernel,
        grid=(bsz, nt),
        in_specs=[
            row_blk(SEC_QK), row_blk(SEC_AV), row_blk(SEC_AO), row_blk(SEC_AZ),
            pl.BlockSpec((None, A_CHUNK, LANES), lambda b, t: (b, t, 0)),
            pl.BlockSpec((None, SUBLANES, A_CHUNK), lambda b, t: (b, 0, t)),
            pl.BlockSpec((CONV_K, width), lambda b, t: (0, 0)),
            pl.BlockSpec((1, width), lambda b, t: (0, 0)),
            pl.BlockSpec((1, width), lambda b, t: (0, 0)),
        ],
        out_specs=pl.BlockSpec((A_CHUNK, width), lambda b, t: (b * nt + t, 0)),
        out_shape=jax.ShapeDtypeStruct((m, width), BF16),
        scratch_shapes=[
            pltpu.VMEM((A_CHUNK + SUBLANES, width), F32),
            pltpu.VMEM((A_HEADS, A_DQK, A_DV), F32),
            pltpu.VMEM((A_HEADS, 1, A_DQK), F32),
            pltpu.VMEM((A_HEADS, SUBLANES, LANES), F32),
            pltpu.VMEM((1, LANES), F32),
        ],
        compiler_params=pltpu.CompilerParams(
            dimension_semantics=("arbitrary", "arbitrary"), vmem_limit_bytes=VMEM_LIMIT),
        name="mlstm",
    )(proj, proj, proj, proj, gcol, grow, conv_w, conv_b, head_g)


FOX_ACC_ROWS = B_DH + BF16_ROWS


def _fox_kernel(q_ref, k_ref, v_ref, z_ref, pc_ref, out_ref, vt_scr, acc_scr, rhs_scr, m_scr,
                s2_scr, cm2_scr):
    s_scr = (s2_scr.at[0], s2_scr.at[1])
    cm_scr = (cm2_scr.at[0], cm2_scr.at[1])
    hg = pl.program_id(1)
    qi = pl.program_id(2)
    TQ, TK = FOX_TQ, FOX_TK
    seq = k_ref.shape[0]
    nh = 2 * FOX_G

    @pl.when(qi == 0)
    def _():
        for g in range(FOX_G):
            vt = v_ref[:, g * LANES:(g + 1) * LANES].astype(F32).T
            for hh in range(2):
                vt_scr[2 * g + hh, 0:B_DH, :] = vt[hh * B_DH:(hh + 1) * B_DH, :].astype(BF16)
                vt_scr[2 * g + hh, B_DH:FOX_ACC_ROWS, :] = jnp.ones((BF16_ROWS, seq), BF16)

    row = lax.broadcasted_iota(jnp.int32, (LANES, TQ), 0)
    for g in range(FOX_G):
        qt = (q_ref[:, g * LANES:(g + 1) * LANES].astype(F32) * (B_DH ** -0.5 * LOG2E)).T
        for hh in range(2):
            h = 2 * g + hh
            head = hg * nh + h
            qm = jnp.where((row >= hh * B_DH) & (row < (hh + 1) * B_DH), qt, 0.0)
            sel = jnp.where((row == PIECE_OFFS[0] + head) | (row == PIECE_OFFS[1] + head)
                            | (row == PIECE_OFFS[2] + head), 1.0, 0.0)
            rhs_scr[h, 0:LANES, :] = qm.astype(BF16)
            rhs_scr[h, LANES:2 * LANES, :] = sel.astype(BF16)

    acc_scr[...] = jnp.zeros_like(acc_scr)
    m_scr[...] = jnp.full(m_scr.shape, NEG_BIG, F32)

    def key_block(kj):
        k0 = pl.multiple_of(kj * TK, TK)
        pcs = pc_ref[pl.ds(k0, TK), :]
        return [jnp.concatenate([k_ref[pl.ds(k0, TK), g * LANES:(g + 1) * LANES], pcs], axis=1)
                for g in range(FOX_G)]

    def scores_head(h, lhs, slot):
        s = jnp.dot(lhs[h // 2], rhs_scr[h], preferred_element_type=F32)
        s_scr[slot][h] = s
        cm_scr[slot][h] = jnp.broadcast_to(jnp.max(s, axis=0, keepdims=True), (SUBLANES, TQ))

    def scores(kj, slot):
        lhs = key_block(kj)
        for h in range(nh):
            scores_head(h, lhs, slot)

    def softmax_pv(h, kj, s, cmax, lo):
        k0 = pl.multiple_of(kj * TK, TK)
        m_old = m_scr[h, 0:1, lo:TQ]
        m_new = jnp.maximum(m_old, cmax)
        alpha = jnp.exp2(m_old - m_new)
        p = jnp.exp2(s - m_new).astype(BF16)
        pv = jnp.dot(vt_scr[h, :, pl.ds(k0, TK)], p, preferred_element_type=F32)
        acc_scr[h, :, lo:TQ] = alpha * acc_scr[h, :, lo:TQ] + pv
        m_scr[h, :, lo:TQ] = jnp.broadcast_to(m_new, (SUBLANES, TQ - lo))

    def overlapped(kj_next, slot_next, kj, slot):
        lhs = key_block(kj_next)
        for h in range(nh):
            scores_head(h, lhs, slot_next)
            softmax_pv(h, kj, s_scr[slot][h], cm_scr[slot][h, 0:1, :], 0)

    def pair(i, carry):
        overlapped(2 * i + 1, 1, 2 * i, 0)
        overlapped(2 * i + 2, 0, 2 * i + 1, 1)
        return carry

    scores(0, 0)
    lax.fori_loop(0, qi, pair, 0)

    half = TQ - TK
    lhs_b = key_block(2 * qi + 1)
    sb = [jnp.dot(lhs_b[h // 2], rhs_scr[h, :, half:TQ], preferred_element_type=F32)
          for h in range(nh)]
    r = lax.broadcasted_iota(jnp.int32, (TK, TQ), 0)
    c = lax.broadcasted_iota(jnp.int32, (TK, TQ), 1)
    rb = lax.broadcasted_iota(jnp.int32, (TK, TK), 0)
    cb = lax.broadcasted_iota(jnp.int32, (TK, TK), 1)
    for h in range(nh):
        s = jnp.where(c >= r, s_scr[0][h], NEG_BIG)
        softmax_pv(h, 2 * qi, s, jnp.max(s, axis=0, keepdims=True), 0)
    for h in range(nh):
        s = jnp.where(cb >= rb, sb[h], NEG_BIG)
        softmax_pv(h, 2 * qi + 1, s, jnp.max(s, axis=0, keepdims=True), half)

    for g in range(FOX_G):
        parts = []
        for hh in range(2):
            a = acc_scr[2 * g + hh]
            parts.append(a[0:B_DH, :] * (1.0 / a[B_DH:B_DH + 1, :]))
        o = jnp.concatenate(parts, axis=0).T
        zz = _silu(z_ref[:, g * LANES:(g + 1) * LANES].astype(F32))
        out_ref[:, g * LANES:(g + 1) * LANES] = (o * zz).astype(BF16)


def _fox(proj, pieces, bsz, seq):
    m = proj.shape[0]
    nq = seq // FOX_TQ
    nh = 2 * FOX_G
    w = FOX_G * LANES
    ngrp = (B_HEADS * B_DH) // w
    sec = SECTION_W // w
    col0 = SEC_BQ * sec
    assert (SEC_BK, SEC_BV, SEC_BZ) == (SEC_BQ + 1, SEC_BQ + 2, SEC_BQ + 3)
    return pl.pallas_call(
        _fox_kernel,
        grid=(bsz, ngrp, nq),
        in_specs=[
            pl.BlockSpec((FOX_TQ, w), lambda b, hg, qi: (b * nq + qi, col0 + hg)),
            pl.BlockSpec((seq, w), lambda b, hg, qi: (b, col0 + sec + hg)),
            pl.BlockSpec((seq, w), lambda b, hg, qi: (b, col0 + 2 * sec + hg)),
            pl.BlockSpec((FOX_TQ, w), lambda b, hg, qi: (b * nq + qi, col0 + 3 * sec + hg)),
            pl.BlockSpec((None, seq, LANES), lambda b, hg, qi: (b, 0, 0)),
        ],
        out_specs=pl.BlockSpec((FOX_TQ, w), lambda b, hg, qi: (b * nq + qi, hg)),
        out_shape=jax.ShapeDtypeStruct((m, B_HEADS * B_DH), BF16),
        scratch_shapes=[
            pltpu.VMEM((nh, FOX_ACC_ROWS, seq), BF16),
            pltpu.VMEM((nh, FOX_ACC_ROWS, FOX_TQ), F32),
            pltpu.VMEM((nh, 2 * LANES, FOX_TQ), BF16),
            pltpu.VMEM((nh, SUBLANES, FOX_TQ), F32),
            pltpu.VMEM((2, nh, FOX_TK, FOX_TQ), F32),
            pltpu.VMEM((2, nh, SUBLANES, FOX_TQ), F32),
        ],
        compiler_params=pltpu.CompilerParams(
            dimension_semantics=("arbitrary", "arbitrary", "arbitrary"),
            vmem_limit_bytes=VMEM_LIMIT),
        name="fox",
    )(proj, proj, proj, proj, pieces)


def _merge_kernel(ha_ref, hb_ref, ga_ref, gb_ref, x_ref, p_ref, wa_ref, wb_ref, wo_ref, wg_ref,
                  wp_ref, png_ref, fng_ref, out_ref):
    ya = jnp.dot(ha_ref[...], wa_ref[...], preferred_element_type=F32)
    yb = jnp.dot(hb_ref[...], wb_ref[...], preferred_element_type=F32)
    merged = (_sigmoid(ga_ref[...].astype(F32)) * ya + _sigmoid(gb_ref[...].astype(F32)) * yb)
    x1 = x_ref[...] + jnp.dot(merged.astype(BF16), wo_ref[...], preferred_element_type=F32)
    r = _rms_norm(x1, png_ref[...]).astype(BF16)
    gate = _sigmoid(jnp.dot(r, wg_ref[...], preferred_element_type=F32))
    pp = jnp.dot(p_ref[...].astype(BF16), wp_ref[...], preferred_element_type=F32)
    x2 = x1 + gate * pp
    out_ref[...] = _rms_norm(x2, fng_ref[...])


def _merge(ha, hb, proj, x2, p2, wa, wb, wo, wg, wp, png, fng):
    m, d = x2.shape
    pd = p2.shape[1]
    tm = MERGE_TM
    full = lambda r, c: pl.BlockSpec((r, c), lambda i: (0, 0))
    return pl.pallas_call(
        _merge_kernel,
        grid=(m // tm,),
        in_specs=[
            pl.BlockSpec((tm, d), lambda i: (i, 0)),
            pl.BlockSpec((tm, d), lambda i: (i, 0)),
            pl.BlockSpec((tm, d), lambda i: (i, SEC_GA)),
            pl.BlockSpec((tm, d), lambda i: (i, SEC_GB)),
            pl.BlockSpec((tm, d), lambda i: (i, 0)),
            pl.BlockSpec((tm, pd), lambda i: (i, 0)),
            full(d, d), full(d, d), full(d, d), full(d, d), full(pd, d),
            full(1, d), full(1, d),
        ],
        out_specs=pl.BlockSpec((tm, d), lambda i: (i, 0)),
        out_shape=jax.ShapeDtypeStruct((m, d), F32),
        compiler_params=pltpu.CompilerParams(
            dimension_semantics=("arbitrary",), vmem_limit_bytes=VMEM_LIMIT),
        name="merge",
    )(ha, hb, proj, proj, x2, p2, wa, wb, wo, wg, wp, png, fng)


def _split_w_in(w):
    qkw = A_HEADS * A_DQK
    aw = A_HEADS * A_DV
    bw = B_HEADS * B_DH
    d = w.shape[0]
    o_ai = 2 * qkw + aw
    o_ao = o_ai + 2 * A_HEADS
    o_bf = o_ao + 2 * aw + 3 * bw
    o_bz = o_bf + B_HEADS
    seg_cols = (o_ai, o_bf - o_ao, w.shape[1] - o_bz)
    shifts = (0, o_ao - o_ai, o_ao - o_ai + o_bz - o_bf)
    assert all(c % IN_TN == 0 for c in seg_cols) and all(s % SUBLANES == 0 for s in shifts)
    seg_tiles = tuple(c // IN_TN for c in seg_cols)
    w_t = w.T
    w_gate_t = jnp.concatenate(
        [w_t[o_ai:o_ao], w_t[o_bf:o_bz], jnp.zeros((LANES - N_GATE, d), w.dtype)], axis=0)
    return w_t, w_gate_t, seg_tiles, shifts


def _layer(x, p_i, attn_norm_g, w_in, conv_w, conv_b, a_bias_i, a_bias_f, a_head_norm_g, b_bias_f,
           w_branch_a, w_branch_b, w_out, ple_norm_g, w_ple_gate, w_ple_proj, out_norm_g):
    bsz, seq, d = x.shape
    m = bsz * seq
    x2 = x.reshape(m, d)
    w_t, w_gate_t, seg_tiles, shifts = _split_w_in(w_in)
    proj, gates = _in_proj(x2, attn_norm_g.reshape(1, d), w_t, w_gate_t, seg_tiles, shifts)

    bias = jnp.concatenate([a_bias_i, a_bias_f, b_bias_f, jnp.zeros((LANES - N_GATE,), F32)])
    gcol, grow, pieces = _gates(gates.reshape(bsz, seq, LANES), bias.reshape(1, LANES))

    ha = _mlstm(proj, gcol, grow, conv_w, conv_b.reshape(1, -1), a_head_norm_g.reshape(1, -1),
                bsz, seq)
    hb = _fox(proj, pieces, bsz, seq)

    out = _merge(ha, hb, proj, x2, p_i.reshape(m, -1),
                 w_branch_a.astype(BF16), w_branch_b.astype(BF16), w_out.astype(BF16),
                 w_ple_gate.astype(BF16), w_ple_proj.astype(BF16),
                 ple_norm_g.reshape(1, d), out_norm_g.reshape(1, d))
    return out.reshape(bsz, seq, d)


def kernel(x, p, attn_norm_g, w_in, conv_w, conv_b, a_bias_i, a_bias_f, a_head_norm_g, b_bias_f,
           w_branch_a, w_branch_b, w_out, ple_norm_g, w_ple_gate, w_ple_proj, final_norm_g):
    depth = w_in.shape[0]
    assert depth == 1, "the final norm is fused into the single layer's merge kernel"
    return _layer(x, p[0], attn_norm_g[0], w_in[0], conv_w[0], conv_b[0], a_bias_i[0], a_bias_f[0],
                  a_head_norm_g[0], b_bias_f[0], w_branch_a[0], w_branch_b[0], w_out[0],
                  ple_norm_g[0], w_ple_gate[0], w_ple_proj[0], final_norm_g)
```

```python
import math

import jax
import jax.numpy as jnp
from jax import lax
from jax.experimental import pallas as pl
from jax.experimental.pallas import tpu as pltpu

F32 = jnp.float32
BF16 = jnp.bfloat16

EPS = 1e-6
A_HEADS = 4
A_DQK = 128
A_DV = 256
CONV_K = 4
B_HEADS = 16
B_DH = 64
LANES = 128
SUBLANES = 8
BF16_ROWS = 16
NEG_BIG = -1e30
LOG2E = math.log2(math.e)

IN_TM = 2048
IN_TN = 1024
CUM_BLK = 256
A_CHUNK = 256
FOX_TK = 256
FOX_TQ = 2 * FOX_TK
FOX_G = 4
MERGE_TM = 512
VMEM_LIMIT = 56 * 1024 * 1024

SECTION_W = A_HEADS * A_DV
SEC_QK, SEC_AV, SEC_AO, SEC_AZ, SEC_BQ, SEC_BK, SEC_BV, SEC_BZ, SEC_GA, SEC_GB = range(10)

N_GATE = 2 * A_HEADS + B_HEADS
B_LANE0 = 2 * A_HEADS
PIECE_OFFS = (B_LANE0, B_LANE0 + B_HEADS, B_LANE0 + 2 * B_HEADS)


def _sigmoid(x):
    return 1.0 / (1.0 + jnp.exp(-x))


def _silu(x):
    return x * _sigmoid(x)


def _rms_norm(x, g):
    ms = jnp.mean(x * x, axis=-1, keepdims=True)
    return (x * lax.rsqrt(ms + EPS)) * g


def _split3(x):
    x1 = x.astype(BF16)
    r1 = x - x1.astype(F32)
    x2 = r1.astype(BF16)
    x3 = (r1 - x2.astype(F32)).astype(BF16)
    return x1, x2, x3


_NT = (((1,), (1,)), ((), ()))


def _in_proj_kernel(x_ref, g_ref, wt_ref, wgt_ref, proj_ref, gates_ref, h_scr):
    @pl.when(pl.program_id(1) == 0)
    def _():
        h = _rms_norm(x_ref[...], g_ref[...]).astype(BF16)
        h_scr[...] = h
        gates_ref[...] = lax.dot_general(h, wgt_ref[...].astype(BF16), _NT,
                                         preferred_element_type=F32)

    proj_ref[...] = lax.dot_general(h_scr[...], wt_ref[...].astype(BF16), _NT,
                                    preferred_element_type=F32).astype(BF16)


def _in_proj(x2, g, w_t, w_gate_t, seg_tiles, shifts):
    m, d = x2.shape
    n = sum(seg_tiles) * IN_TN

    def w_rows(i, j):
        shift = shifts[0] // SUBLANES
        lo = 0
        for n_tiles, s in zip(seg_tiles[:-1], shifts[1:]):
            lo += n_tiles
            shift = jnp.where(j >= lo, s // SUBLANES, shift)
        return ((j * (IN_TN // SUBLANES) + shift) * SUBLANES, 0)

    return pl.pallas_call(
        _in_proj_kernel,
        grid=(m // IN_TM, n // IN_TN),
        in_specs=[
            pl.BlockSpec((IN_TM, d), lambda i, j: (i, 0)),
            pl.BlockSpec((1, d), lambda i, j: (0, 0)),
            pl.BlockSpec((pl.Element(IN_TN), pl.Element(d)), w_rows),
            pl.BlockSpec((LANES, d), lambda i, j: (0, 0)),
        ],
        out_specs=[
            pl.BlockSpec((IN_TM, IN_TN), lambda i, j: (i, j)),
            pl.BlockSpec((IN_TM, LANES), lambda i, j: (i, 0)),
        ],
        out_shape=[
            jax.ShapeDtypeStruct((m, n), BF16),
            jax.ShapeDtypeStruct((m, LANES), F32),
        ],
        scratch_shapes=[pltpu.VMEM((IN_TM, d), BF16)],
        compiler_params=pltpu.CompilerParams(
            dimension_semantics=("arbitrary", "arbitrary"),
            vmem_limit_bytes=VMEM_LIMIT),
        name="in_proj",
    )(x2, g, w_t, w_gate_t)


def _gates_kernel(g_ref, bias_ref, col_ref, row_ref, pc_ref):
    x = g_ref[...] + bias_ref[...]
    s = x.shape[0]
    ls = jnp.minimum(x, 0.0) - jnp.log1p(jnp.exp(-jnp.abs(x)))
    r = lax.broadcasted_iota(jnp.int32, (CUM_BLK, CUM_BLK), 0)
    c = lax.broadcasted_iota(jnp.int32, (CUM_BLK, CUM_BLK), 1)
    tri = jnp.where(r >= c, 1.0, 0.0).astype(BF16)
    carry = jnp.zeros((1, LANES), F32)
    blocks = []
    for blk in range(s // CUM_BLK):
        x1, x2, x3 = _split3(ls[blk * CUM_BLK:(blk + 1) * CUM_BLK])
        cs = (jnp.dot(tri, x3, preferred_element_type=F32)
              + jnp.dot(tri, x2, preferred_element_type=F32)
              + jnp.dot(tri, x1, preferred_element_type=F32)) + carry
        carry = cs[CUM_BLK - 1:CUM_BLK, :]
        blocks.append(cs)
    cum = jnp.concatenate(blocks, axis=0)
    lane = lax.broadcasted_iota(jnp.int32, x.shape, 1)
    res = jnp.where(lane < A_HEADS, x, cum)
    col_ref[...] = res
    row_ref[...] = res.T[0:SUBLANES, :]

    in_b = (lane >= B_LANE0) & (lane < B_LANE0 + B_HEADS)
    p1, p2, p3 = _split3(jnp.where(in_b, cum * (-LOG2E), 0.0))
    pieces = (p1.astype(F32)
              + pltpu.roll(p2.astype(F32), PIECE_OFFS[1] - B_LANE0, axis=1)
              + pltpu.roll(p3.astype(F32), PIECE_OFFS[2] - B_LANE0, axis=1))
    pc_ref[...] = pieces.astype(BF16)


def _gates(gates3, bias):
    b, s, _ = gates3.shape
    return pl.pallas_call(
        _gates_kernel,
        grid=(b,),
        in_specs=[
            pl.BlockSpec((None, s, LANES), lambda i: (i, 0, 0)),
            pl.BlockSpec((1, LANES), lambda i: (0, 0)),
        ],
        out_specs=[
            pl.BlockSpec((None, s, LANES), lambda i: (i, 0, 0)),
            pl.BlockSpec((None, SUBLANES, s), lambda i: (i, 0, 0)),
            pl.BlockSpec((None, s, LANES), lambda i: (i, 0, 0)),
        ],
        out_shape=[
            jax.ShapeDtypeStruct((b, s, LANES), F32),
            jax.ShapeDtypeStruct((b, SUBLANES, s), F32),
            jax.ShapeDtypeStruct((b, s, LANES), BF16),
        ],
        compiler_params=pltpu.CompilerParams(
            dimension_semantics=("arbitrary",), vmem_limit_bytes=VMEM_LIMIT),
        name="gates",
    )(gates3, bias)


def _mlstm_kernel(qk_ref, v_ref, o_ref, z_ref, gcol_ref, grow_ref, cw_ref, cb_ref, hg_ref,
                  out_ref, xpad_scr, c_scr, n_scr, m_scr, fprev_scr):
    t = pl.program_id(1)
    L = A_CHUNK
    qkw = A_HEADS * A_DQK

    @pl.when(t == 0)
    def _():
        xpad_scr[0:SUBLANES, :] = jnp.zeros((SUBLANES, 2 * qkw), F32)
        c_scr[...] = jnp.zeros_like(c_scr)
        n_scr[...] = jnp.zeros_like(n_scr)
        m_scr[...] = jnp.zeros_like(m_scr)
        fprev_scr[...] = jnp.zeros_like(fprev_scr)

    xpad_scr[SUBLANES:SUBLANES + L, :] = qk_ref[...].astype(F32)
    y = cb_ref[...] + xpad_scr[SUBLANES:SUBLANES + L, :] * cw_ref[CONV_K - 1:CONV_K, :]
    for d in range(1, CONV_K):
        y = y + xpad_scr[SUBLANES - d:SUBLANES - d + L, :] * cw_ref[CONV_K - 1 - d:CONV_K - d, :]
    xpad_scr[0:SUBLANES, :] = xpad_scr[L:L + SUBLANES, :]
    qk = _silu(y)

    row = lax.broadcasted_iota(jnp.int32, (L, L), 0)
    col = lax.broadcasted_iota(jnp.int32, (L, L), 1)
    causal = row >= col

    for h in range(A_HEADS):
        q = qk[:, h * A_DQK:(h + 1) * A_DQK]
        k = qk[:, qkw + h * A_DQK:qkw + (h + 1) * A_DQK] * (A_DQK ** -0.5)
        qb = q.astype(BF16)
        kb = k.astype(BF16)
        v = v_ref[:, h * A_DV:(h + 1) * A_DV]

        li_c = gcol_ref[:, h:h + 1]
        f_c = gcol_ref[:, A_HEADS + h:A_HEADS + h + 1]
        li_r = grow_ref[h:h + 1, :]
        f_r = grow_ref[A_HEADS + h:A_HEADS + h + 1, :]
        f_prev = fprev_scr[0:1, A_HEADS + h:A_HEADS + h + 1]
        f_end = gcol_ref[L - 1:L, A_HEADS + h:A_HEADS + h + 1]
        m_st = m_scr[h, 0:1, 0:1]
        c_st = c_scr[h]
        n_st = n_scr[h]

        dmat = jnp.where(causal, (f_c - f_r) + li_r, NEG_BIG)
        inter = (f_c - f_prev) + m_st
        m_row = jnp.maximum(inter, jnp.max(dmat, axis=-1, keepdims=True))
        w_intra = jnp.exp(dmat - m_row)
        w_inter = jnp.exp(inter - m_row)
        s = lax.dot_general(qb, kb, (((1,), (1,)), ((), ())), preferred_element_type=F32)
        scores = s * w_intra
        num = (jnp.dot(scores.astype(BF16), v, preferred_element_type=F32)
               + w_inter * jnp.dot(qb, c_st.astype(BF16), preferred_element_type=F32))
        den = (jnp.sum(scores, axis=-1, keepdims=True)
               + w_inter * jnp.sum(q * n_st, axis=-1, keepdims=True))
        hh = num * (1.0 / jnp.maximum(jnp.abs(den), jnp.exp(-m_row)))

        g_tot = f_end - f_prev
        to_end = (f_end - f_c) + li_c
        m_new = jnp.maximum(g_tot + m_st, jnp.max(to_end, axis=0, keepdims=True))
        w_k = jnp.exp(to_end - m_new)
        decay = jnp.exp(g_tot + m_st - m_new)
        kw = k * w_k
        c_scr[h] = decay * c_st + jnp.dot(kw.T.astype(BF16), v, preferred_element_type=F32)
        n_scr[h] = decay * n_st + jnp.sum(kw, axis=0, keepdims=True)
        m_scr[h] = jnp.broadcast_to(m_new, (SUBLANES, LANES))

        hn = hh * lax.rsqrt(jnp.mean(hh * hh, axis=-1, keepdims=True) + EPS)
        hn = hn * hg_ref[:, h * A_DV:(h + 1) * A_DV]
        og = _sigmoid(o_ref[:, h * A_DV:(h + 1) * A_DV].astype(F32))
        zz = _silu(z_ref[:, h * A_DV:(h + 1) * A_DV].astype(F32))
        out_ref[:, h * A_DV:(h + 1) * A_DV] = ((og * hn) * zz).astype(BF16)

    fprev_scr[...] = gcol_ref[L - 1:L, :]


def _mlstm(proj, gcol, grow, conv_w, conv_b, head_g, bsz, seq):
    m = proj.shape[0]
    width = SECTION_W
    nt = seq // A_CHUNK
    row_blk = lambda c: pl.BlockSpec((A_CHUNK, width), lambda b, t: (b * nt + t, c))
    return pl.pallas_call(
        _mlstm_kernel,
        grid=(bsz, nt),
        in_specs=[
            row_blk(SEC_QK), row_blk(SEC_AV), row_blk(SEC_AO), row_blk(SEC_AZ),
            pl.BlockSpec((None, A_CHUNK, LANES), lambda b, t: (b, t, 0)),
            pl.BlockSpec((None, SUBLANES, A_CHUNK), lambda b, t: (b, 0, t)),
            pl.BlockSpec((CONV_K, width), lambda b, t: (0, 0)),
            pl.BlockSpec((1, width), lambda b, t: (0, 0)),
            pl.BlockSpec((1, width), lambda b, t: (0, 0)),
        ],
        out_specs=pl.BlockSpec((A_CHUNK, width), lambda b, t: (b * nt + t, 0)),
        out_shape=jax.ShapeDtypeStruct((m, width), BF16),
        scratch_shapes=[
            pltpu.VMEM((A_CHUNK + SUBLANES, width), F32),
            pltpu.VMEM((A_HEADS, A_DQK, A_DV), F32),
            pltpu.VMEM((A_HEADS, 1, A_DQK), F32),
            pltpu.VMEM((A_HEADS, SUBLANES, LANES), F32),
            pltpu.VMEM((1, LANES), F32),
        ],
        compiler_params=pltpu.CompilerParams(
            dimension_semantics=("arbitrary", "arbitrary"), vmem_limit_bytes=VMEM_LIMIT),
        name="mlstm",
    )(proj, proj, proj, proj, gcol, grow, conv_w, conv_b, head_g)


FOX_ACC_ROWS = B_DH + BF16_ROWS


def _fox_kernel(q_ref, k_ref, v_ref, z_ref, pc_ref, out_ref, vt_scr, acc_scr, rhs_scr, m_scr,
                s2_scr, cm2_scr):
    s_scr = (s2_scr.at[0], s2_scr.at[1])
    cm_scr = (cm2_scr.at[0], cm2_scr.at[1])
    hg = pl.program_id(1)
    qi = pl.program_id(2)
    TQ, TK = FOX_TQ, FOX_TK
    seq = k_ref.shape[0]
    nh = 2 * FOX_G

    @pl.when(qi == 0)
    def _():
        for g in range(FOX_G):
            vt = v_ref[:, g * LANES:(g + 1) * LANES].astype(F32).T
            for hh in range(2):
                vt_scr[2 * g + hh, 0:B_DH, :] = vt[hh * B_DH:(hh + 1) * B_DH, :].astype(BF16)
                vt_scr[2 * g + hh, B_DH:FOX_ACC_ROWS, :] = jnp.ones((BF16_ROWS, seq), BF16)

    row = lax.broadcasted_iota(jnp.int32, (LANES, TQ), 0)
    for g in range(FOX_G):
        qt = (q_ref[:, g * LANES:(g + 1) * LANES].astype(F32) * (B_DH ** -0.5 * LOG2E)).T
        for hh in range(2):
            h = 2 * g + hh
            head = hg * nh + h
            qm = jnp.where((row >= hh * B_DH) & (row < (hh + 1) * B_DH), qt, 0.0)
            sel = jnp.where((row == PIECE_OFFS[0] + head) | (row == PIECE_OFFS[1] + head)
                            | (row == PIECE_OFFS[2] + head), 1.0, 0.0)
            rhs_scr[h, 0:LANES, :] = qm.astype(BF16)
            rhs_scr[h, LANES:2 * LANES, :] = sel.astype(BF16)

    acc_scr[...] = jnp.zeros_like(acc_scr)
    m_scr[...] = jnp.full(m_scr.shape, NEG_BIG, F32)

    def key_block(kj):
        k0 = pl.multiple_of(kj * TK, TK)
        pcs = pc_ref[pl.ds(k0, TK), :]
        return [jnp.concatenate([k_ref[pl.ds(k0, TK), g * LANES:(g + 1) * LANES], pcs], axis=1)
                for g in range(FOX_G)]

    def scores_head(h, lhs, slot):
        s = jnp.dot(lhs[h // 2], rhs_scr[h], preferred_element_type=F32)
        s_scr[slot][h] = s
        cm_scr[slot][h] = jnp.broadcast_to(jnp.max(s, axis=0, keepdims=True), (SUBLANES, TQ))

    def scores(kj, slot):
        lhs = key_block(kj)
        for h in range(nh):
            scores_head(h, lhs, slot)

    def softmax_pv(h, kj, s, cmax, lo):
        k0 = pl.multiple_of(kj * TK, TK)
        m_old = m_scr[h, 0:1, lo:TQ]
        m_new = jnp.maximum(m_old, cmax)
        alpha = jnp.exp2(m_old - m_new)
        p = jnp.exp2(s - m_new).astype(BF16)
        pv = jnp.dot(vt_scr[h, :, pl.ds(k0, TK)], p, preferred_element_type=F32)
        acc_scr[h, :, lo:TQ] = alpha * acc_scr[h, :, lo:TQ] + pv
        m_scr[h, :, lo:TQ] = jnp.broadcast_to(m_new, (SUBLANES, TQ - lo))

    def overlapped(kj_next, slot_next, kj, slot):
        lhs = key_block(kj_next)
        for h in range(nh):
            scores_head(h, lhs, slot_next)
            softmax_pv(h, kj, s_scr[slot][h], cm_scr[slot][h, 0:1, :], 0)

    def pair(i, carry):
        overlapped(2 * i + 1, 1, 2 * i, 0)
        overlapped(2 * i + 2, 0, 2 * i + 1, 1)
        return carry

    scores(0, 0)
    lax.fori_loop(0, qi, pair, 0)

    half = TQ - TK
    lhs_b = key_block(2 * qi + 1)
    sb = [jnp.dot(lhs_b[h // 2], rhs_scr[h, :, half:TQ], preferred_element_type=F32)
          for h in range(nh)]
    r = lax.broadcasted_iota(jnp.int32, (TK, TQ), 0)
    c = lax.broadcasted_iota(jnp.int32, (TK, TQ), 1)
    rb = lax.broadcasted_iota(jnp.int32, (TK, TK), 0)
    cb = lax.broadcasted_iota(jnp.int32, (TK, TK), 1)
    for h in range(nh):
        s = jnp.where(c >= r, s_scr[0][h], NEG_BIG)
        softmax_pv(h, 2 * qi, s, jnp.max(s, axis=0, keepdims=True), 0)
    for h in range(nh):
        s = jnp.where(cb >= rb, sb[h], NEG_BIG)
        softmax_pv(h, 2 * qi + 1, s, jnp.max(s, axis=0, keepdims=True), half)

    for g in range(FOX_G):
        parts = []
        for hh in range(2):
            a = acc_scr[2 * g + hh]
            parts.append(a[0:B_DH, :] * (1.0 / a[B_DH:B_DH + 1, :]))
        o = jnp.concatenate(parts, axis=0).T
        zz = _silu(z_ref[:, g * LANES:(g + 1) * LANES].astype(F32))
        out_ref[:, g * LANES:(g + 1) * LANES] = (o * zz).astype(BF16)


def _fox(proj, pieces, bsz, seq):
    m = proj.shape[0]
    nq = seq // FOX_TQ
    nh = 2 * FOX_G
    w = FOX_G * LANES
    ngrp = (B_HEADS * B_DH) // w
    sec = SECTION_W // w
    col0 = SEC_BQ * sec
    assert (SEC_BK, SEC_BV, SEC_BZ) == (SEC_BQ + 1, SEC_BQ + 2, SEC_BQ + 3)
    return pl.pallas_call(
        _fox_kernel,
        grid=(bsz, ngrp, nq),
        in_specs=[
            pl.BlockSpec((FOX_TQ, w), lambda b, hg, qi: (b * nq + qi, col0 + hg)),
            pl.BlockSpec((seq, w), lambda b, hg, qi: (b, col0 + sec + hg)),
            pl.BlockSpec((seq, w), lambda b, hg, qi: (b, col0 + 2 * sec + hg)),
            pl.BlockSpec((FOX_TQ, w), lambda b, hg, qi: (b * nq + qi, col0 + 3 * sec + hg)),
            pl.BlockSpec((None, seq, LANES), lambda b, hg, qi: (b, 0, 0)),
        ],
        out_specs=pl.BlockSpec((FOX_TQ, w), lambda b, hg, qi: (b * nq + qi, hg)),
        out_shape=jax.ShapeDtypeStruct((m, B_HEADS * B_DH), BF16),
        scratch_shapes=[
            pltpu.VMEM((nh, FOX_ACC_ROWS, seq), BF16),
            pltpu.VMEM((nh, FOX_ACC_ROWS, FOX_TQ), F32),
            pltpu.VMEM((nh, 2 * LANES, FOX_TQ), BF16),
            pltpu.VMEM((nh, SUBLANES, FOX_TQ), F32),
            pltpu.VMEM((2, nh, FOX_TK, FOX_TQ), F32),
            pltpu.VMEM((2, nh, SUBLANES, FOX_TQ), F32),
        ],
        compiler_params=pltpu.CompilerParams(
            dimension_semantics=("arbitrary", "arbitrary", "arbitrary"),
            vmem_limit_bytes=VMEM_LIMIT),
        name="fox",
    )(proj, proj, proj, proj, pieces)


def _merge_kernel(ha_ref, hb_ref, ga_ref, gb_ref, x_ref, p_ref, wa_ref, wb_ref, wo_ref, wg_ref,
                  wp_ref, png_ref, fng_ref, out_ref):
    ya = jnp.dot(ha_ref[...], wa_ref[...], preferred_element_type=F32)
    yb = jnp.dot(hb_ref[...], wb_ref[...], preferred_element_type=F32)
    merged = (_sigmoid(ga_ref[...].astype(F32)) * ya + _sigmoid(gb_ref[...].astype(F32)) * yb)
    x1 = x_ref[...] + jnp.dot(merged.astype(BF16), wo_ref[...], preferred_element_type=F32)
    r = _rms_norm(x1, png_ref[...]).astype(BF16)
    gate = _sigmoid(jnp.dot(r, wg_ref[...], preferred_element_type=F32))
    pp = jnp.dot(p_ref[...].astype(BF16), wp_ref[...], preferred_element_type=F32)
    x2 = x1 + gate * pp
    out_ref[...] = _rms_norm(x2, fng_ref[...])


def _merge(ha, hb, proj, x2, p2, wa, wb, wo, wg, wp, png, fng):
    m, d = x2.shape
    pd = p2.shape[1]
    tm = MERGE_TM
    full = lambda r, c: pl.BlockSpec((r, c), lambda i: (0, 0))
    return pl.pallas_call(
        _merge_kernel,
        grid=(m // tm,),
        in_specs=[
            pl.BlockSpec((tm, d), lambda i: (i, 0)),
            pl.BlockSpec((tm, d), lambda i: (i, 0)),
            pl.BlockSpec((tm, d), lambda i: (i, SEC_GA)),
            pl.BlockSpec((tm, d), lambda i: (i, SEC_GB)),
            pl.BlockSpec((tm, d), lambda i: (i, 0)),
            pl.BlockSpec((tm, pd), lambda i: (i, 0)),
            full(d, d), full(d, d), full(d, d), full(d, d), full(pd, d),
            full(1, d), full(1, d),
        ],
        out_specs=pl.BlockSpec((tm, d), lambda i: (i, 0)),
        out_shape=jax.ShapeDtypeStruct((m, d), F32),
        compiler_params=pltpu.CompilerParams(
            dimension_semantics=("arbitrary",), vmem_limit_bytes=VMEM_LIMIT),
        name="merge",
    )(ha, hb, proj, proj, x2, p2, wa, wb, wo, wg, wp, png, fng)


def _split_w_in(w):
    qkw = A_HEADS * A_DQK
    aw = A_HEADS * A_DV
    bw = B_HEADS * B_DH
    d = w.shape[0]
    o_ai = 2 * qkw + aw
    o_ao = o_ai + 2 * A_HEADS
    o_bf = o_ao + 2 * aw + 3 * bw
    o_bz = o_bf + B_HEADS
    seg_cols = (o_ai, o_bf - o_ao, w.shape[1] - o_bz)
    shifts = (0, o_ao - o_ai, o_ao - o_ai + o_bz - o_bf)
    assert all(c % IN_TN == 0 for c in seg_cols) and all(s % SUBLANES == 0 for s in shifts)
    seg_tiles = tuple(c // IN_TN for c in seg_cols)
    w_t = w.T
    w_gate_t = jnp.concatenate(
        [w_t[o_ai:o_ao], w_t[o_bf:o_bz], jnp.zeros((LANES - N_GATE, d), w.dtype)], axis=0)
    return w_t, w_gate_t, seg_tiles, shifts


def _layer(x, p_i, attn_norm_g, w_in, conv_w, conv_b, a_bias_i, a_bias_f, a_head_norm_g, b_bias_f,
           w_branch_a, w_branch_b, w_out, ple_norm_g, w_ple_gate, w_ple_proj, out_norm_g):
    bsz, seq, d = x.shape
    m = bsz * seq
    x2 = x.reshape(m, d)
    w_t, w_gate_t, seg_tiles, shifts = _split_w_in(w_in)
    proj, gates = _in_proj(x2, attn_norm_g.reshape(1, d), w_t, w_gate_t, seg_tiles, shifts)

    bias = jnp.concatenate([a_bias_i, a_bias_f, b_bias_f, jnp.zeros((LANES - N_GATE,), F32)])
    gcol, grow, pieces = _gates(gates.reshape(bsz, seq, LANES), bias.reshape(1, LANES))

    ha = _mlstm(proj, gcol, grow, conv_w, conv_b.reshape(1, -1), a_head_norm_g.reshape(1, -1),
                bsz, seq)
    hb = _fox(proj, pieces, bsz, seq)

    out = _merge(ha, hb, proj, x2, p_i.reshape(m, -1),
                 w_branch_a.astype(BF16), w_branch_b.astype(BF16), w_out.astype(BF16),
                 w_ple_gate.astype(BF16), w_ple_proj.astype(BF16),
                 ple_norm_g.reshape(1, d), out_norm_g.reshape(1, d))
    return out.reshape(bsz, seq, d)


def kernel(x, p, attn_norm_g, w_in, conv_w, conv_b, a_bias_i, a_bias_f, a_head_norm_g, b_bias_f,
           w_branch_a, w_branch_b, w_out, ple_norm_g, w_ple_gate, w_ple_proj, final_norm_g):
    depth = w_in.shape[0]
    assert depth == 1, "the final norm is fused into the single layer's merge kernel"
    return _layer(x, p[0], attn_norm_g[0], w_in[0], conv_w[0], conv_b[0], a_bias_i[0], a_bias_f[0],
                  a_head_norm_g[0], b_bias_f[0], w_branch_a[0], w_branch_b[0], w_out[0],
                  ple_norm_g[0], w_ple_gate[0], w_ple_proj[0], final_norm_g)
```

```python
import functools
import math

import jax
import jax.numpy as jnp
from jax import lax
from jax.experimental import pallas as pl
from jax.experimental.pallas import tpu as pltpu

F32 = jnp.float32
BF16 = jnp.bfloat16

EPS = 1e-6
A_HEADS = 4
A_DQK = 128
A_DV = 256
CONV_K = 4
B_HEADS = 16
B_DH = 64
LANES = 128
SUBLANES = 8
BF16_ROWS = 16
NEG_BIG = -1e30
LOG2E = math.log2(math.e)

IN_TM = 2048
IN_TN = 1024
CUM_BLK = 256
A_CHUNK = 256
FOX_TK = 256
FOX_TQ = 2 * FOX_TK
FOX_G = 4
MERGE_TM = 512
VMEM_LIMIT = 56 * 1024 * 1024

SECTION_W = A_HEADS * A_DV
SEC_QK, SEC_AV, SEC_AO, SEC_AZ, SEC_BQ, SEC_BK, SEC_BV, SEC_BZ, SEC_GA, SEC_GB = range(10)

N_GATE = 2 * A_HEADS + B_HEADS
B_LANE0 = 2 * A_HEADS
PIECE_OFFS = (B_LANE0, B_LANE0 + B_HEADS, B_LANE0 + 2 * B_HEADS)


def _sigmoid(x):
    return 1.0 / (1.0 + jnp.exp(-x))


def _silu(x):
    return x * _sigmoid(x)


def _rms_norm(x, g):
    ms = jnp.mean(x * x, axis=-1, keepdims=True)
    return (x * lax.rsqrt(ms + EPS)) * g


def _split3(x):
    x1 = x.astype(BF16)
    r1 = x - x1.astype(F32)
    x2 = r1.astype(BF16)
    x3 = (r1 - x2.astype(F32)).astype(BF16)
    return x1, x2, x3


_NT = (((1,), (1,)), ((), ()))


def _in_proj_kernel(x_ref, g_ref, wt_ref, wgt_ref, proj_ref, gates_ref, h_scr):
    @pl.when(pl.program_id(1) == 0)
    def _():
        h = _rms_norm(x_ref[...], g_ref[...]).astype(BF16)
        h_scr[...] = h
        gates_ref[...] = lax.dot_general(h, wgt_ref[...].astype(BF16), _NT,
                                         preferred_element_type=F32)

    proj_ref[...] = lax.dot_general(h_scr[...], wt_ref[...].astype(BF16), _NT,
                                    preferred_element_type=F32).astype(BF16)


def _in_proj(x2, g, w_t, w_gate_t, seg_tiles, shifts):
    m, d = x2.shape
    n = sum(seg_tiles) * IN_TN

    def w_rows(i, j):
        shift = shifts[0] // SUBLANES
        lo = 0
        for n_tiles, s in zip(seg_tiles[:-1], shifts[1:]):
            lo += n_tiles
            shift = jnp.where(j >= lo, s // SUBLANES, shift)
        return ((j * (IN_TN // SUBLANES) + shift) * SUBLANES, 0)

    return pl.pallas_call(
        _in_proj_kernel,
        grid=(m // IN_TM, n // IN_TN),
        in_specs=[
            pl.BlockSpec((IN_TM, d), lambda i, j: (i, 0)),
            pl.BlockSpec((1, d), lambda i, j: (0, 0)),
            pl.BlockSpec((pl.Element(IN_TN), pl.Element(d)), w_rows),
            pl.BlockSpec((LANES, d), lambda i, j: (0, 0)),
        ],
        out_specs=[
            pl.BlockSpec((IN_TM, IN_TN), lambda i, j: (i, j)),
            pl.BlockSpec((IN_TM, LANES), lambda i, j: (i, 0)),
        ],
        out_shape=[
            jax.ShapeDtypeStruct((m, n), BF16),
            jax.ShapeDtypeStruct((m, LANES), F32),
        ],
        scratch_shapes=[pltpu.VMEM((IN_TM, d), BF16)],
        compiler_params=pltpu.CompilerParams(
            dimension_semantics=("arbitrary", "arbitrary"),
            vmem_limit_bytes=VMEM_LIMIT),
        name="in_proj",
    )(x2, g, w_t, w_gate_t)


def _gates_kernel(g_ref, bias_ref, col_ref, row_ref, pc_ref):
    x = g_ref[...] + bias_ref[...]
    s = x.shape[0]
    ls = jnp.minimum(x, 0.0) - jnp.log1p(jnp.exp(-jnp.abs(x)))
    r = lax.broadcasted_iota(jnp.int32, (CUM_BLK, CUM_BLK), 0)
    c = lax.broadcasted_iota(jnp.int32, (CUM_BLK, CUM_BLK), 1)
    tri = jnp.where(r >= c, 1.0, 0.0).astype(BF16)
    carry = jnp.zeros((1, LANES), F32)
    blocks = []
    for blk in range(s // CUM_BLK):
        x1, x2, x3 = _split3(ls[blk * CUM_BLK:(blk + 1) * CUM_BLK])
        cs = (jnp.dot(tri, x3, preferred_element_type=F32)
              + jnp.dot(tri, x2, preferred_element_type=F32)
              + jnp.dot(tri, x1, preferred_element_type=F32)) + carry
        carry = cs[CUM_BLK - 1:CUM_BLK, :]
        blocks.append(cs)
    cum = jnp.concatenate(blocks, axis=0)
    lane = lax.broadcasted_iota(jnp.int32, x.shape, 1)
    res = jnp.where(lane < A_HEADS, x, cum)
    col_ref[...] = res
    row_ref[...] = res.T[0:SUBLANES, :]

    in_b = (lane >= B_LANE0) & (lane < B_LANE0 + B_HEADS)
    p1, p2, p3 = _split3(jnp.where(in_b, cum * (-LOG2E), 0.0))
    pieces = (p1.astype(F32)
              + pltpu.roll(p2.astype(F32), PIECE_OFFS[1] - B_LANE0, axis=1)
              + pltpu.roll(p3.astype(F32), PIECE_OFFS[2] - B_LANE0, axis=1))
    pc_ref[...] = pieces.astype(BF16)


def _gates(gates3, bias):
    b, s, _ = gates3.shape
    return pl.pallas_call(
        _gates_kernel,
        grid=(b,),
        in_specs=[
            pl.BlockSpec((None, s, LANES), lambda i: (i, 0, 0)),
            pl.BlockSpec((1, LANES), lambda i: (0, 0)),
        ],
        out_specs=[
            pl.BlockSpec((None, s, LANES), lambda i: (i, 0, 0)),
            pl.BlockSpec((None, SUBLANES, s), lambda i: (i, 0, 0)),
            pl.BlockSpec((None, s, LANES), lambda i: (i, 0, 0)),
        ],
        out_shape=[
            jax.ShapeDtypeStruct((b, s, LANES), F32),
            jax.ShapeDtypeStruct((b, SUBLANES, s), F32),
            jax.ShapeDtypeStruct((b, s, LANES), BF16),
        ],
        compiler_params=pltpu.CompilerParams(
            dimension_semantics=("arbitrary",), vmem_limit_bytes=VMEM_LIMIT),
        name="gates",
    )(gates3, bias)


def _mlstm_reset(xpad_scr, c_scr, n_scr, m_scr, fprev_scr):
    xpad_scr[0:SUBLANES, :] = jnp.zeros((SUBLANES, xpad_scr.shape[1]), F32)
    c_scr[...] = jnp.zeros_like(c_scr)
    n_scr[...] = jnp.zeros_like(n_scr)
    m_scr[...] = jnp.zeros_like(m_scr)
    fprev_scr[...] = jnp.zeros_like(fprev_scr)


def _mlstm_chunk(qk_ref, v_ref, o_ref, z_ref, gcol_ref, grow_ref, cw_ref, cb_ref, hg_ref,
                 out_ref, xpad_scr, c_scr, n_scr, m_scr, fprev_scr):
    L = A_CHUNK
    qkw = A_HEADS * A_DQK

    xpad_scr[SUBLANES:SUBLANES + L, :] = qk_ref[...].astype(F32)
    y = cb_ref[...] + xpad_scr[SUBLANES:SUBLANES + L, :] * cw_ref[CONV_K - 1:CONV_K, :]
    for d in range(1, CONV_K):
        y = y + xpad_scr[SUBLANES - d:SUBLANES - d + L, :] * cw_ref[CONV_K - 1 - d:CONV_K - d, :]
    xpad_scr[0:SUBLANES, :] = xpad_scr[L:L + SUBLANES, :]
    qk = _silu(y)

    row = lax.broadcasted_iota(jnp.int32, (L, L), 0)
    col = lax.broadcasted_iota(jnp.int32, (L, L), 1)
    causal = row >= col

    for h in range(A_HEADS):
        q = qk[:, h * A_DQK:(h + 1) * A_DQK]
        k = qk[:, qkw + h * A_DQK:qkw + (h + 1) * A_DQK] * (A_DQK ** -0.5)
        qb = q.astype(BF16)
        kb = k.astype(BF16)
        v = v_ref[:, h * A_DV:(h + 1) * A_DV]

        li_c = gcol_ref[:, h:h + 1]
        f_c = gcol_ref[:, A_HEADS + h:A_HEADS + h + 1]
        li_r = grow_ref[h:h + 1, :]
        f_r = grow_ref[A_HEADS + h:A_HEADS + h + 1, :]
        f_prev = fprev_scr[0:1, A_HEADS + h:A_HEADS + h + 1]
        f_end = gcol_ref[L - 1:L, A_HEADS + h:A_HEADS + h + 1]
        m_st = m_scr[h, 0:1, 0:1]
        c_st = c_scr[h]
        n_st = n_scr[h]

        dmat = jnp.where(causal, (f_c - f_r) + li_r, NEG_BIG)
        inter = (f_c - f_prev) + m_st
        m_row = jnp.maximum(inter, jnp.max(dmat, axis=-1, keepdims=True))
        w_intra = jnp.exp(dmat - m_row)
        w_inter = jnp.exp(inter - m_row)
        s = lax.dot_general(qb, kb, (((1,), (1,)), ((), ())), preferred_element_type=F32)
        scores = s * w_intra
        num = (jnp.dot(scores.astype(BF16), v, preferred_element_type=F32)
               + w_inter * jnp.dot(qb, c_st.astype(BF16), preferred_element_type=F32))
        den = (jnp.sum(scores, axis=-1, keepdims=True)
               + w_inter * jnp.sum(q * n_st, axis=-1, keepdims=True))
        hh = num * (1.0 / jnp.maximum(jnp.abs(den), jnp.exp(-m_row)))

        g_tot = f_end - f_prev
        to_end = (f_end - f_c) + li_c
        m_new = jnp.maximum(g_tot + m_st, jnp.max(to_end, axis=0, keepdims=True))
        w_k = jnp.exp(to_end - m_new)
        decay = jnp.exp(g_tot + m_st - m_new)
        kw = k * w_k
        c_scr[h] = decay * c_st + jnp.dot(kw.T.astype(BF16), v, preferred_element_type=F32)
        n_scr[h] = decay * n_st + jnp.sum(kw, axis=0, keepdims=True)
        m_scr[h] = jnp.broadcast_to(m_new, (SUBLANES, LANES))

        hn = hh * lax.rsqrt(jnp.mean(hh * hh, axis=-1, keepdims=True) + EPS)
        hn = hn * hg_ref[:, h * A_DV:(h + 1) * A_DV]
        og = _sigmoid(o_ref[:, h * A_DV:(h + 1) * A_DV].astype(F32))
        zz = _silu(z_ref[:, h * A_DV:(h + 1) * A_DV].astype(F32))
        out_ref[:, h * A_DV:(h + 1) * A_DV] = ((og * hn) * zz).astype(BF16)

    fprev_scr[...] = gcol_ref[L - 1:L, :]


FOX_ACC_ROWS = B_DH + BF16_ROWS


def _fox_kernel(q_ref, k_ref, v_ref, z_ref, pc_ref, out_ref, vt_scr, acc_scr, rhs_scr, m_scr,
                s2_scr, cm2_scr):
    s_scr = (s2_scr.at[0], s2_scr.at[1])
    cm_scr = (cm2_scr.at[0], cm2_scr.at[1])
    hg = pl.program_id(1)
    qi = pl.program_id(2)
    TQ, TK = FOX_TQ, FOX_TK
    seq = k_ref.shape[0]
    nh = 2 * FOX_G

    @pl.when(qi == 0)
    def _():
        for g in range(FOX_G):
            vt = v_ref[:, g * LANES:(g + 1) * LANES].astype(F32).T
            for hh in range(2):
                vt_scr[2 * g + hh, 0:B_DH, :] = vt[hh * B_DH:(hh + 1) * B_DH, :].astype(BF16)
                vt_scr[2 * g + hh, B_DH:FOX_ACC_ROWS, :] = jnp.ones((BF16_ROWS, seq), BF16)

    row = lax.broadcasted_iota(jnp.int32, (LANES, TQ), 0)
    for g in range(FOX_G):
        qt = (q_ref[:, g * LANES:(g + 1) * LANES].astype(F32) * (B_DH ** -0.5 * LOG2E)).T
        for hh in range(2):
            h = 2 * g + hh
            head = hg * nh + h
            qm = jnp.where((row >= hh * B_DH) & (row < (hh + 1) * B_DH), qt, 0.0)
            sel = jnp.where((row == PIECE_OFFS[0] + head) | (row == PIECE_OFFS[1] + head)
                            | (row == PIECE_OFFS[2] + head), 1.0, 0.0)
            rhs_scr[h, 0:LANES, :] = qm.astype(BF16)
            rhs_scr[h, LANES:2 * LANES, :] = sel.astype(BF16)

    acc_scr[...] = jnp.zeros_like(acc_scr)
    m_scr[...] = jnp.full(m_scr.shape, NEG_BIG, F32)

    def key_block(kj):
        k0 = pl.multiple_of(kj * TK, TK)
        pcs = pc_ref[pl.ds(k0, TK), :]
        return [jnp.concatenate([k_ref[pl.ds(k0, TK), g * LANES:(g + 1) * LANES], pcs], axis=1)
                for g in range(FOX_G)]

    def scores_head(h, lhs, slot):
        s = jnp.dot(lhs[h // 2], rhs_scr[h], preferred_element_type=F32)
        s_scr[slot][h] = s
        cm_scr[slot][h] = jnp.broadcast_to(jnp.max(s, axis=0, keepdims=True), (SUBLANES, TQ))

    def scores(kj, slot):
        lhs = key_block(kj)
        for h in range(nh):
            scores_head(h, lhs, slot)

    def softmax_pv(h, kj, s, cmax, lo):
        k0 = pl.multiple_of(kj * TK, TK)
        m_old = m_scr[h, 0:1, lo:TQ]
        m_new = jnp.maximum(m_old, cmax)
        alpha = jnp.exp2(m_old - m_new)
        p = jnp.exp2(s - m_new).astype(BF16)
        pv = jnp.dot(vt_scr[h, :, pl.ds(k0, TK)], p, preferred_element_type=F32)
        acc_scr[h, :, lo:TQ] = alpha * acc_scr[h, :, lo:TQ] + pv
        m_scr[h, :, lo:TQ] = jnp.broadcast_to(m_new, (SUBLANES, TQ - lo))

    def overlapped(kj_next, slot_next, kj, slot):
        lhs = key_block(kj_next)
        for h in range(nh):
            scores_head(h, lhs, slot_next)
            softmax_pv(h, kj, s_scr[slot][h], cm_scr[slot][h, 0:1, :], 0)

    def pair(i, carry):
        overlapped(2 * i + 1, 1, 2 * i, 0)
        overlapped(2 * i + 2, 0, 2 * i + 1, 1)
        return carry

    scores(0, 0)
    lax.fori_loop(0, qi, pair, 0)

    half = TQ - TK
    lhs_b = key_block(2 * qi + 1)
    sb = [jnp.dot(lhs_b[h // 2], rhs_scr[h, :, half:TQ], preferred_element_type=F32)
          for h in range(nh)]
    r = lax.broadcasted_iota(jnp.int32, (TK, TQ), 0)
    c = lax.broadcasted_iota(jnp.int32, (TK, TQ), 1)
    rb = lax.broadcasted_iota(jnp.int32, (TK, TK), 0)
    cb = lax.broadcasted_iota(jnp.int32, (TK, TK), 1)
    for h in range(nh):
        s = jnp.where(c >= r, s_scr[0][h], NEG_BIG)
        softmax_pv(h, 2 * qi, s, jnp.max(s, axis=0, keepdims=True), 0)
    for h in range(nh):
        s = jnp.where(cb >= rb, sb[h], NEG_BIG)
        softmax_pv(h, 2 * qi + 1, s, jnp.max(s, axis=0, keepdims=True), half)

    for g in range(FOX_G):
        parts = []
        for hh in range(2):
            a = acc_scr[2 * g + hh]
            parts.append(a[0:B_DH, :] * (1.0 / a[B_DH:B_DH + 1, :]))
        o = jnp.concatenate(parts, axis=0).T
        zz = _silu(z_ref[:, g * LANES:(g + 1) * LANES].astype(F32))
        out_ref[:, g * LANES:(g + 1) * LANES] = (o * zz).astype(BF16)


def _fox(proj, pieces, bsz, seq):
    m = proj.shape[0]
    nq = seq // FOX_TQ
    nh = 2 * FOX_G
    w = FOX_G * LANES
    ngrp = (B_HEADS * B_DH) // w
    sec = SECTION_W // w
    col0 = SEC_BQ * sec
    assert (SEC_BK, SEC_BV, SEC_BZ) == (SEC_BQ + 1, SEC_BQ + 2, SEC_BQ + 3)
    return pl.pallas_call(
        _fox_kernel,
        grid=(bsz, ngrp, nq),
        in_specs=[
            pl.BlockSpec((FOX_TQ, w), lambda b, hg, qi: (b * nq + qi, col0 + hg)),
            pl.BlockSpec((seq, w), lambda b, hg, qi: (b, col0 + sec + hg)),
            pl.BlockSpec((seq, w), lambda b, hg, qi: (b, col0 + 2 * sec + hg)),
            pl.BlockSpec((FOX_TQ, w), lambda b, hg, qi: (b * nq + qi, col0 + 3 * sec + hg)),
            pl.BlockSpec((None, seq, LANES), lambda b, hg, qi: (b, 0, 0)),
        ],
        out_specs=pl.BlockSpec((FOX_TQ, w), lambda b, hg, qi: (b * nq + qi, hg)),
        out_shape=jax.ShapeDtypeStruct((m, B_HEADS * B_DH), BF16),
        scratch_shapes=[
            pltpu.VMEM((nh, FOX_ACC_ROWS, seq), BF16),
            pltpu.VMEM((nh, FOX_ACC_ROWS, FOX_TQ), F32),
            pltpu.VMEM((nh, 2 * LANES, FOX_TQ), BF16),
            pltpu.VMEM((nh, SUBLANES, FOX_TQ), F32),
            pltpu.VMEM((2, nh, FOX_TK, FOX_TQ), F32),
            pltpu.VMEM((2, nh, SUBLANES, FOX_TQ), F32),
        ],
        compiler_params=pltpu.CompilerParams(
            dimension_semantics=("arbitrary", "arbitrary", "arbitrary"),
            vmem_limit_bytes=VMEM_LIMIT),
        name="fox",
    )(proj, proj, proj, proj, pieces)


def _merge_rows(ha, hb_ref, ga_ref, gb_ref, x_ref, p_ref, wa_ref, wb_ref, wo_ref, wg_ref,
                wp_ref, png_ref, fng_ref, out_ref):
    ya = jnp.dot(ha, wa_ref[...], preferred_element_type=F32)
    yb = jnp.dot(hb_ref[...], wb_ref[...], preferred_element_type=F32)
    merged = (_sigmoid(ga_ref[...].astype(F32)) * ya + _sigmoid(gb_ref[...].astype(F32)) * yb)
    x1 = x_ref[...] + jnp.dot(merged.astype(BF16), wo_ref[...], preferred_element_type=F32)
    r = _rms_norm(x1, png_ref[...]).astype(BF16)
    gate = _sigmoid(jnp.dot(r, wg_ref[...], preferred_element_type=F32))
    pp = jnp.dot(p_ref[...].astype(BF16), wp_ref[...], preferred_element_type=F32)
    x2 = x1 + gate * pp
    out_ref[...] = _rms_norm(x2, fng_ref[...])


def _mlstm_merge_kernel(nt, n_chunks,
                        qk_ref, v_ref, o_ref, z_ref, gcol_ref, grow_ref, cw_ref, cb_ref, hg_ref,
                        hb_ref, ga_ref, gb_ref, x_ref, p_ref, wa_ref, wb_ref, wo_ref, wg_ref,
                        wp_ref, png_ref, fng_ref, out_ref,
                        ha_scr, xpad_scr, c_scr, n_scr, m_scr, fprev_scr):
    s = pl.program_id(0)
    chunk = jnp.minimum(s, n_chunks - 1)
    state = (xpad_scr, c_scr, n_scr, m_scr, fprev_scr)

    @pl.when(s == 0)
    def _():
        ha_scr[...] = jnp.zeros_like(ha_scr)

    @pl.when(chunk % nt == 0)
    def _():
        _mlstm_reset(*state)

    _merge_rows(ha_scr[...], hb_ref, ga_ref, gb_ref, x_ref, p_ref, wa_ref, wb_ref, wo_ref,
                wg_ref, wp_ref, png_ref, fng_ref, out_ref)
    _mlstm_chunk(qk_ref, v_ref, o_ref, z_ref, gcol_ref, grow_ref, cw_ref, cb_ref, hg_ref,
                 ha_scr, *state)


def _mlstm_merge(proj, gcol, grow, conv_w, conv_b, head_g, hb, x2, p2, wa, wb, wo, wg, wp,
                 png, fng, bsz, seq):
    m, d = x2.shape
    pd = p2.shape[1]
    width = SECTION_W
    nt = seq // A_CHUNK
    n_chunks = bsz * nt
    cur = lambda s: jnp.minimum(s, n_chunks - 1)
    prev = lambda s: jnp.maximum(s - 1, 0)
    a_blk = lambda sec: pl.BlockSpec((A_CHUNK, width), lambda s: (cur(s), sec))
    m_blk = lambda w, sec: pl.BlockSpec((A_CHUNK, w), lambda s: (prev(s), sec))
    full = lambda r, c: pl.BlockSpec((r, c), lambda s: (0, 0))
    return pl.pallas_call(
        functools.partial(_mlstm_merge_kernel, nt, n_chunks),
        grid=(n_chunks + 1,),
        in_specs=[
            a_blk(SEC_QK), a_blk(SEC_AV), a_blk(SEC_AO), a_blk(SEC_AZ),
            pl.BlockSpec((None, A_CHUNK, LANES), lambda s: (cur(s) // nt, cur(s) % nt, 0)),
            pl.BlockSpec((None, SUBLANES, A_CHUNK), lambda s: (cur(s) // nt, 0, cur(s) % nt)),
            full(CONV_K, width), full(1, width), full(1, width),
            m_blk(d, 0), m_blk(d, SEC_GA), m_blk(d, SEC_GB), m_blk(d, 0), m_blk(pd, 0),
            full(d, d), full(d, d), full(d, d), full(d, d), full(pd, d),
            full(1, d), full(1, d),
        ],
        out_specs=pl.BlockSpec((A_CHUNK, d), lambda s: (prev(s), 0)),
        out_shape=jax.ShapeDtypeStruct((m, d), F32),
        scratch_shapes=[
            pltpu.VMEM((A_CHUNK, width), BF16),
            pltpu.VMEM((A_CHUNK + SUBLANES, width), F32),
            pltpu.VMEM((A_HEADS, A_DQK, A_DV), F32),
            pltpu.VMEM((A_HEADS, 1, A_DQK), F32),
            pltpu.VMEM((A_HEADS, SUBLANES, LANES), F32),
            pltpu.VMEM((1, LANES), F32),
        ],
        compiler_params=pltpu.CompilerParams(
            dimension_semantics=("arbitrary",), vmem_limit_bytes=VMEM_LIMIT),
        name="mlstm_merge",
    )(proj, proj, proj, proj, gcol, grow, conv_w, conv_b, head_g,
      hb, proj, proj, x2, p2, wa, wb, wo, wg, wp, png, fng)


def _split_w_in(w):
    qkw = A_HEADS * A_DQK
    aw = A_HEADS * A_DV
    bw = B_HEADS * B_DH
    d = w.shape[0]
    o_ai = 2 * qkw + aw
    o_ao = o_ai + 2 * A_HEADS
    o_bf = o_ao + 2 * aw + 3 * bw
    o_bz = o_bf + B_HEADS
    seg_cols = (o_ai, o_bf - o_ao, w.shape[1] - o_bz)
    shifts = (0, o_ao - o_ai, o_ao - o_ai + o_bz - o_bf)
    assert all(c % IN_TN == 0 for c in seg_cols) and all(s % SUBLANES == 0 for s in shifts)
    seg_tiles = tuple(c // IN_TN for c in seg_cols)
    w_t = w.T
    w_gate_t = jnp.concatenate(
        [w_t[o_ai:o_ao], w_t[o_bf:o_bz], jnp.zeros((LANES - N_GATE, d), w.dtype)], axis=0)
    return w_t, w_gate_t, seg_tiles, shifts


def _layer(x, p_i, attn_norm_g, w_in, conv_w, conv_b, a_bias_i, a_bias_f, a_head_norm_g, b_bias_f,
           w_branch_a, w_branch_b, w_out, ple_norm_g, w_ple_gate, w_ple_proj, out_norm_g):
    bsz, seq, d = x.shape
    m = bsz * seq
    x2 = x.reshape(m, d)
    w_t, w_gate_t, seg_tiles, shifts = _split_w_in(w_in)
    proj, gates = _in_proj(x2, attn_norm_g.reshape(1, d), w_t, w_gate_t, seg_tiles, shifts)

    bias = jnp.concatenate([a_bias_i, a_bias_f, b_bias_f, jnp.zeros((LANES - N_GATE,), F32)])
    gcol, grow, pieces = _gates(gates.reshape(bsz, seq, LANES), bias.reshape(1, LANES))

    hb = _fox(proj, pieces, bsz, seq)
    out = _mlstm_merge(proj, gcol, grow, conv_w, conv_b.reshape(1, -1),
                       a_head_norm_g.reshape(1, -1), hb, x2, p_i.reshape(m, -1),
                       w_branch_a.astype(BF16), w_branch_b.astype(BF16), w_out.astype(BF16),
                       w_ple_gate.astype(BF16), w_ple_proj.astype(BF16),
                       ple_norm_g.reshape(1, d), out_norm_g.reshape(1, d), bsz, seq)
    return out.reshape(bsz, seq, d)


def kernel(x, p, attn_norm_g, w_in, conv_w, conv_b, a_bias_i, a_bias_f, a_head_norm_g, b_bias_f,
           w_branch_a, w_branch_b, w_out, ple_norm_g, w_ple_gate, w_ple_proj, final_norm_g):
    depth = w_in.shape[0]
    assert depth == 1, "the final norm is fused into the single layer's merge kernel"
    return _layer(x, p[0], attn_norm_g[0], w_in[0], conv_w[0], conv_b[0], a_bias_i[0], a_bias_f[0],
                  a_head_norm_g[0], b_bias_f[0], w_branch_a[0], w_branch_b[0], w_out[0],
                  ple_norm_g[0], w_ple_gate[0], w_ple_proj[0], final_norm_g)
```

```python
import functools
import math

import jax
import jax.numpy as jnp
from jax import lax
from jax.experimental import pallas as pl
from jax.experimental.pallas import tpu as pltpu

F32 = jnp.float32
BF16 = jnp.bfloat16

EPS = 1e-6
A_HEADS = 4
A_DQK = 128
A_DV = 256
CONV_K = 4
B_HEADS = 16
B_DH = 64
LANES = 128
SUBLANES = 8
BF16_ROWS = 16
NEG_BIG = -1e30
LOG2E = math.log2(math.e)

IN_TM = 2048
IN_TN = 1024
CUM_BLK = 256
A_CHUNK = 256
FOX_TK = 256
FOX_TQ = 2 * FOX_TK
FOX_G = 4
MERGE_TM = 512
VMEM_LIMIT = 56 * 1024 * 1024

SECTION_W = A_HEADS * A_DV
SEC_QK, SEC_AV, SEC_AO, SEC_AZ, SEC_BQ, SEC_BK, SEC_BV, SEC_BZ, SEC_GA, SEC_GB = range(10)

N_GATE = 2 * A_HEADS + B_HEADS
B_LANE0 = 2 * A_HEADS
PIECE_OFFS = (B_LANE0, B_LANE0 + B_HEADS, B_LANE0 + 2 * B_HEADS)


def _sigmoid(x):
    return 1.0 / (1.0 + jnp.exp(-x))


def _silu(x):
    return x * _sigmoid(x)


def _rms_norm(x, g):
    ms = jnp.mean(x * x, axis=-1, keepdims=True)
    return (x * lax.rsqrt(ms + EPS)) * g


def _split3(x):
    x1 = x.astype(BF16)
    r1 = x - x1.astype(F32)
    x2 = r1.astype(BF16)
    x3 = (r1 - x2.astype(F32)).astype(BF16)
    return x1, x2, x3


_NT = (((1,), (1,)), ((), ()))


def _in_proj_kernel(x_ref, g_ref, wt_ref, wgt_ref, proj_ref, gates_ref, h_scr):
    @pl.when(pl.program_id(1) == 0)
    def _():
        h = _rms_norm(x_ref[...], g_ref[...]).astype(BF16)
        h_scr[...] = h
        gates_ref[...] = lax.dot_general(h, wgt_ref[...].astype(BF16), _NT,
                                         preferred_element_type=F32)

    proj_ref[...] = lax.dot_general(h_scr[...], wt_ref[...].astype(BF16), _NT,
                                    preferred_element_type=F32).astype(BF16)


def _in_proj(x2, g, w_t, w_gate_t, seg_tiles, shifts):
    m, d = x2.shape
    n = sum(seg_tiles) * IN_TN

    def w_rows(i, j):
        shift = shifts[0] // SUBLANES
        lo = 0
        for n_tiles, s in zip(seg_tiles[:-1], shifts[1:]):
            lo += n_tiles
            shift = jnp.where(j >= lo, s // SUBLANES, shift)
        return ((j * (IN_TN // SUBLANES) + shift) * SUBLANES, 0)

    return pl.pallas_call(
        _in_proj_kernel,
        grid=(m // IN_TM, n // IN_TN),
        in_specs=[
            pl.BlockSpec((IN_TM, d), lambda i, j: (i, 0)),
            pl.BlockSpec((1, d), lambda i, j: (0, 0)),
            pl.BlockSpec((pl.Element(IN_TN), pl.Element(d)), w_rows),
            pl.BlockSpec((LANES, d), lambda i, j: (0, 0)),
        ],
        out_specs=[
            pl.BlockSpec((IN_TM, IN_TN), lambda i, j: (i, j)),
            pl.BlockSpec((IN_TM, LANES), lambda i, j: (i, 0)),
        ],
        out_shape=[
            jax.ShapeDtypeStruct((m, n), BF16),
            jax.ShapeDtypeStruct((m, LANES), F32),
        ],
        scratch_shapes=[pltpu.VMEM((IN_TM, d), BF16)],
        compiler_params=pltpu.CompilerParams(
            dimension_semantics=("arbitrary", "arbitrary"),
            vmem_limit_bytes=VMEM_LIMIT),
        name="in_proj",
    )(x2, g, w_t, w_gate_t)


def _gates_kernel(g_ref, bias_ref, col_ref, row_ref, pc_ref):
    x = g_ref[...] + bias_ref[...]
    s = x.shape[0]
    ls = jnp.minimum(x, 0.0) - jnp.log1p(jnp.exp(-jnp.abs(x)))
    r = lax.broadcasted_iota(jnp.int32, (CUM_BLK, CUM_BLK), 0)
    c = lax.broadcasted_iota(jnp.int32, (CUM_BLK, CUM_BLK), 1)
    tri = jnp.where(r >= c, 1.0, 0.0).astype(BF16)
    carry = jnp.zeros((1, LANES), F32)
    blocks = []
    for blk in range(s // CUM_BLK):
        x1, x2, x3 = _split3(ls[blk * CUM_BLK:(blk + 1) * CUM_BLK])
        cs = (jnp.dot(tri, x3, preferred_element_type=F32)
              + jnp.dot(tri, x2, preferred_element_type=F32)
              + jnp.dot(tri, x1, preferred_element_type=F32)) + carry
        carry = cs[CUM_BLK - 1:CUM_BLK, :]
        blocks.append(cs)
    cum = jnp.concatenate(blocks, axis=0)
    lane = lax.broadcasted_iota(jnp.int32, x.shape, 1)
    res = jnp.where(lane < A_HEADS, x, cum)
    col_ref[...] = res
    row_ref[...] = res.T[0:SUBLANES, :]

    in_b = (lane >= B_LANE0) & (lane < B_LANE0 + B_HEADS)
    p1, p2, p3 = _split3(jnp.where(in_b, cum * (-LOG2E), 0.0))
    pieces = (p1.astype(F32)
              + pltpu.roll(p2.astype(F32), PIECE_OFFS[1] - B_LANE0, axis=1)
              + pltpu.roll(p3.astype(F32), PIECE_OFFS[2] - B_LANE0, axis=1))
    pc_ref[...] = pieces.astype(BF16)


def _gates(gates3, bias):
    b, s, _ = gates3.shape
    return pl.pallas_call(
        _gates_kernel,
        grid=(b,),
        in_specs=[
            pl.BlockSpec((None, s, LANES), lambda i: (i, 0, 0)),
            pl.BlockSpec((1, LANES), lambda i: (0, 0)),
        ],
        out_specs=[
            pl.BlockSpec((None, s, LANES), lambda i: (i, 0, 0)),
            pl.BlockSpec((None, SUBLANES, s), lambda i: (i, 0, 0)),
            pl.BlockSpec((None, s, LANES), lambda i: (i, 0, 0)),
        ],
        out_shape=[
            jax.ShapeDtypeStruct((b, s, LANES), F32),
            jax.ShapeDtypeStruct((b, SUBLANES, s), F32),
            jax.ShapeDtypeStruct((b, s, LANES), BF16),
        ],
        compiler_params=pltpu.CompilerParams(
            dimension_semantics=("arbitrary",), vmem_limit_bytes=VMEM_LIMIT),
        name="gates",
    )(gates3, bias)


def _mlstm_reset(xpad_scr, c_scr, n_scr, m_scr, fprev_scr):
    xpad_scr[0:SUBLANES, :] = jnp.zeros((SUBLANES, xpad_scr.shape[1]), F32)
    c_scr[...] = jnp.zeros_like(c_scr)
    n_scr[...] = jnp.zeros_like(n_scr)
    m_scr[...] = jnp.zeros_like(m_scr)
    fprev_scr[...] = jnp.zeros_like(fprev_scr)


def _mlstm_chunk(qk_ref, v_ref, o_ref, z_ref, gcol_ref, grow_ref, cw_ref, cb_ref, hg_ref,
                 out_ref, xpad_scr, c_scr, n_scr, m_scr, fprev_scr, after_head=lambda: None):
    L = A_CHUNK
    qkw = A_HEADS * A_DQK

    xpad_scr[SUBLANES:SUBLANES + L, :] = qk_ref[...].astype(F32)
    y = cb_ref[...] + xpad_scr[SUBLANES:SUBLANES + L, :] * cw_ref[CONV_K - 1:CONV_K, :]
    for d in range(1, CONV_K):
        y = y + xpad_scr[SUBLANES - d:SUBLANES - d + L, :] * cw_ref[CONV_K - 1 - d:CONV_K - d, :]
    xpad_scr[0:SUBLANES, :] = xpad_scr[L:L + SUBLANES, :]
    qk = _silu(y)

    row = lax.broadcasted_iota(jnp.int32, (L, L), 0)
    col = lax.broadcasted_iota(jnp.int32, (L, L), 1)
    causal = row >= col

    for h in range(A_HEADS):
        q = qk[:, h * A_DQK:(h + 1) * A_DQK]
        k = qk[:, qkw + h * A_DQK:qkw + (h + 1) * A_DQK] * (A_DQK ** -0.5)
        qb = q.astype(BF16)
        kb = k.astype(BF16)
        v = v_ref[:, h * A_DV:(h + 1) * A_DV]

        li_c = gcol_ref[:, h:h + 1]
        f_c = gcol_ref[:, A_HEADS + h:A_HEADS + h + 1]
        li_r = grow_ref[h:h + 1, :]
        f_r = grow_ref[A_HEADS + h:A_HEADS + h + 1, :]
        f_prev = fprev_scr[0:1, A_HEADS + h:A_HEADS + h + 1]
        f_end = gcol_ref[L - 1:L, A_HEADS + h:A_HEADS + h + 1]
        m_st = m_scr[h, 0:1, 0:1]
        c_st = c_scr[h]
        n_st = n_scr[h]

        dmat = jnp.where(causal, (f_c - f_r) + li_r, NEG_BIG)
        inter = (f_c - f_prev) + m_st
        m_row = jnp.maximum(inter, jnp.max(dmat, axis=-1, keepdims=True))
        w_intra = jnp.exp(dmat - m_row)
        w_inter = jnp.exp(inter - m_row)
        s = lax.dot_general(qb, kb, (((1,), (1,)), ((), ())), preferred_element_type=F32)
        scores = s * w_intra
        num = (jnp.dot(scores.astype(BF16), v, preferred_element_type=F32)
               + w_inter * jnp.dot(qb, c_st.astype(BF16), preferred_element_type=F32))
        den = (jnp.sum(scores, axis=-1, keepdims=True)
               + w_inter * jnp.sum(q * n_st, axis=-1, keepdims=True))
        hh = num * (1.0 / jnp.maximum(jnp.abs(den), jnp.exp(-m_row)))

        g_tot = f_end - f_prev
        to_end = (f_end - f_c) + li_c
        m_new = jnp.maximum(g_tot + m_st, jnp.max(to_end, axis=0, keepdims=True))
        w_k = jnp.exp(to_end - m_new)
        decay = jnp.exp(g_tot + m_st - m_new)
        kw = k * w_k
        c_scr[h] = decay * c_st + jnp.dot(kw.T.astype(BF16), v, preferred_element_type=F32)
        n_scr[h] = decay * n_st + jnp.sum(kw, axis=0, keepdims=True)
        m_scr[h] = jnp.broadcast_to(m_new, (SUBLANES, LANES))

        hn = hh * lax.rsqrt(jnp.mean(hh * hh, axis=-1, keepdims=True) + EPS)
        hn = hn * hg_ref[:, h * A_DV:(h + 1) * A_DV]
        og = _sigmoid(o_ref[:, h * A_DV:(h + 1) * A_DV].astype(F32))
        zz = _silu(z_ref[:, h * A_DV:(h + 1) * A_DV].astype(F32))
        out_ref[:, h * A_DV:(h + 1) * A_DV] = ((og * hn) * zz).astype(BF16)
        after_head()

    fprev_scr[...] = gcol_ref[L - 1:L, :]


FOX_ACC_ROWS = B_DH + BF16_ROWS


def _fox_kernel(q_ref, k_ref, v_ref, z_ref, pc_ref, out_ref, vt_scr, acc_scr, rhs_scr, m_scr,
                s2_scr, cm2_scr):
    s_scr = (s2_scr.at[0], s2_scr.at[1])
    cm_scr = (cm2_scr.at[0], cm2_scr.at[1])
    hg = pl.program_id(1)
    qi = pl.program_id(2)
    TQ, TK = FOX_TQ, FOX_TK
    seq = k_ref.shape[0]
    nh = 2 * FOX_G

    @pl.when(qi == 0)
    def _():
        for g in range(FOX_G):
            vt = v_ref[:, g * LANES:(g + 1) * LANES].astype(F32).T
            for hh in range(2):
                vt_scr[2 * g + hh, 0:B_DH, :] = vt[hh * B_DH:(hh + 1) * B_DH, :].astype(BF16)
                vt_scr[2 * g + hh, B_DH:FOX_ACC_ROWS, :] = jnp.ones((BF16_ROWS, seq), BF16)

    row = lax.broadcasted_iota(jnp.int32, (LANES, TQ), 0)
    for g in range(FOX_G):
        qt = (q_ref[:, g * LANES:(g + 1) * LANES].astype(F32) * (B_DH ** -0.5 * LOG2E)).T
        for hh in range(2):
            h = 2 * g + hh
            head = hg * nh + h
            qm = jnp.where((row >= hh * B_DH) & (row < (hh + 1) * B_DH), qt, 0.0)
            sel = jnp.where((row == PIECE_OFFS[0] + head) | (row == PIECE_OFFS[1] + head)
                            | (row == PIECE_OFFS[2] + head), 1.0, 0.0)
            rhs_scr[h, 0:LANES, :] = qm.astype(BF16)
            rhs_scr[h, LANES:2 * LANES, :] = sel.astype(BF16)

    acc_scr[...] = jnp.zeros_like(acc_scr)
    m_scr[...] = jnp.full(m_scr.shape, NEG_BIG, F32)

    def key_block(kj):
        k0 = pl.multiple_of(kj * TK, TK)
        pcs = pc_ref[pl.ds(k0, TK), :]
        return [jnp.concatenate([k_ref[pl.ds(k0, TK), g * LANES:(g + 1) * LANES], pcs], axis=1)
                for g in range(FOX_G)]

    def scores_head(h, lhs, slot):
        s = jnp.dot(lhs[h // 2], rhs_scr[h], preferred_element_type=F32)
        s_scr[slot][h] = s
        cm_scr[slot][h] = jnp.broadcast_to(jnp.max(s, axis=0, keepdims=True), (SUBLANES, TQ))

    def scores(kj, slot):
        lhs = key_block(kj)
        for h in range(nh):
            scores_head(h, lhs, slot)

    def softmax_pv(h, kj, s, cmax, lo):
        k0 = pl.multiple_of(kj * TK, TK)
        m_old = m_scr[h, 0:1, lo:TQ]
        m_new = jnp.maximum(m_old, cmax)
        alpha = jnp.exp2(m_old - m_new)
        p = jnp.exp2(s - m_new).astype(BF16)
        pv = jnp.dot(vt_scr[h, :, pl.ds(k0, TK)], p, preferred_element_type=F32)
        acc_scr[h, :, lo:TQ] = alpha * acc_scr[h, :, lo:TQ] + pv
        m_scr[h, :, lo:TQ] = jnp.broadcast_to(m_new, (SUBLANES, TQ - lo))

    def overlapped(kj_next, slot_next, kj, slot):
        lhs = key_block(kj_next)
        for h in range(nh):
            scores_head(h, lhs, slot_next)
            softmax_pv(h, kj, s_scr[slot][h], cm_scr[slot][h, 0:1, :], 0)

    def pair(i, carry):
        overlapped(2 * i + 1, 1, 2 * i, 0)
        overlapped(2 * i + 2, 0, 2 * i + 1, 1)
        return carry

    scores(0, 0)
    lax.fori_loop(0, qi, pair, 0)

    half = TQ - TK
    lhs_b = key_block(2 * qi + 1)
    sb = [jnp.dot(lhs_b[h // 2], rhs_scr[h, :, half:TQ], preferred_element_type=F32)
          for h in range(nh)]
    r = lax.broadcasted_iota(jnp.int32, (TK, TQ), 0)
    c = lax.broadcasted_iota(jnp.int32, (TK, TQ), 1)
    rb = lax.broadcasted_iota(jnp.int32, (TK, TK), 0)
    cb = lax.broadcasted_iota(jnp.int32, (TK, TK), 1)
    for h in range(nh):
        s = jnp.where(c >= r, s_scr[0][h], NEG_BIG)
        softmax_pv(h, 2 * qi, s, jnp.max(s, axis=0, keepdims=True), 0)
    for h in range(nh):
        s = jnp.where(cb >= rb, sb[h], NEG_BIG)
        softmax_pv(h, 2 * qi + 1, s, jnp.max(s, axis=0, keepdims=True), half)

    for g in range(FOX_G):
        parts = []
        for hh in range(2):
            a = acc_scr[2 * g + hh]
            parts.append(a[0:B_DH, :] * (1.0 / a[B_DH:B_DH + 1, :]))
        o = jnp.concatenate(parts, axis=0).T
        zz = _silu(z_ref[:, g * LANES:(g + 1) * LANES].astype(F32))
        out_ref[:, g * LANES:(g + 1) * LANES] = (o * zz).astype(BF16)


def _fox(proj, pieces, bsz, seq):
    m = proj.shape[0]
    nq = seq // FOX_TQ
    nh = 2 * FOX_G
    w = FOX_G * LANES
    ngrp = (B_HEADS * B_DH) // w
    sec = SECTION_W // w
    col0 = SEC_BQ * sec
    assert (SEC_BK, SEC_BV, SEC_BZ) == (SEC_BQ + 1, SEC_BQ + 2, SEC_BQ + 3)
    return pl.pallas_call(
        _fox_kernel,
        grid=(bsz, ngrp, nq),
        in_specs=[
            pl.BlockSpec((FOX_TQ, w), lambda b, hg, qi: (b * nq + qi, col0 + hg)),
            pl.BlockSpec((seq, w), lambda b, hg, qi: (b, col0 + sec + hg)),
            pl.BlockSpec((seq, w), lambda b, hg, qi: (b, col0 + 2 * sec + hg)),
            pl.BlockSpec((FOX_TQ, w), lambda b, hg, qi: (b * nq + qi, col0 + 3 * sec + hg)),
            pl.BlockSpec((None, seq, LANES), lambda b, hg, qi: (b, 0, 0)),
        ],
        out_specs=pl.BlockSpec((FOX_TQ, w), lambda b, hg, qi: (b * nq + qi, hg)),
        out_shape=jax.ShapeDtypeStruct((m, B_HEADS * B_DH), BF16),
        scratch_shapes=[
            pltpu.VMEM((nh, FOX_ACC_ROWS, seq), BF16),
            pltpu.VMEM((nh, FOX_ACC_ROWS, FOX_TQ), F32),
            pltpu.VMEM((nh, 2 * LANES, FOX_TQ), BF16),
            pltpu.VMEM((nh, SUBLANES, FOX_TQ), F32),
            pltpu.VMEM((2, nh, FOX_TK, FOX_TQ), F32),
            pltpu.VMEM((2, nh, SUBLANES, FOX_TQ), F32),
        ],
        compiler_params=pltpu.CompilerParams(
            dimension_semantics=("arbitrary", "arbitrary", "arbitrary"),
            vmem_limit_bytes=VMEM_LIMIT),
        name="fox",
    )(proj, proj, proj, proj, pieces)


def _merge_stages(ha_ref, hb_ref, ga_ref, gb_ref, x_ref, p_ref, wa_ref, wb_ref, wo_ref, wg_ref,
                  wp_ref, png_ref, fng_ref, out_ref):
    ya = jnp.dot(ha_ref[...], wa_ref[...], preferred_element_type=F32)
    yb = jnp.dot(hb_ref[...], wb_ref[...], preferred_element_type=F32)
    yield
    merged = (_sigmoid(ga_ref[...].astype(F32)) * ya + _sigmoid(gb_ref[...].astype(F32)) * yb)
    x1 = x_ref[...] + jnp.dot(merged.astype(BF16), wo_ref[...], preferred_element_type=F32)
    yield
    r = _rms_norm(x1, png_ref[...]).astype(BF16)
    gate = _sigmoid(jnp.dot(r, wg_ref[...], preferred_element_type=F32))
    yield
    pp = jnp.dot(p_ref[...].astype(BF16), wp_ref[...], preferred_element_type=F32)
    x2 = x1 + gate * pp
    out_ref[...] = _rms_norm(x2, fng_ref[...])
    yield


def _mlstm_merge_kernel(nt, n_chunks,
                        qk_ref, v_ref, o_ref, z_ref, gcol_ref, grow_ref, cw_ref, cb_ref, hg_ref,
                        hb_ref, ga_ref, gb_ref, x_ref, p_ref, wa_ref, wb_ref, wo_ref, wg_ref,
                        wp_ref, png_ref, fng_ref, out_ref,
                        ha_scr, xpad_scr, c_scr, n_scr, m_scr, fprev_scr):
    s = pl.program_id(0)
    chunk = jnp.minimum(s, n_chunks - 1)
    state = (xpad_scr, c_scr, n_scr, m_scr, fprev_scr)

    @pl.when(s == 0)
    def _():
        ha_scr[...] = jnp.zeros_like(ha_scr)

    @pl.when(chunk % nt == 0)
    def _():
        _mlstm_reset(*state)

    stages = _merge_stages(ha_scr, hb_ref, ga_ref, gb_ref, x_ref, p_ref, wa_ref, wb_ref, wo_ref,
                           wg_ref, wp_ref, png_ref, fng_ref, out_ref)
    next(stages)
    _mlstm_chunk(qk_ref, v_ref, o_ref, z_ref, gcol_ref, grow_ref, cw_ref, cb_ref, hg_ref,
                 ha_scr, *state, after_head=lambda: next(stages, None))


def _mlstm_merge(proj, gcol, grow, conv_w, conv_b, head_g, hb, x2, p2, wa, wb, wo, wg, wp,
                 png, fng, bsz, seq):
    m, d = x2.shape
    pd = p2.shape[1]
    width = SECTION_W
    nt = seq // A_CHUNK
    n_chunks = bsz * nt
    cur = lambda s: jnp.minimum(s, n_chunks - 1)
    prev = lambda s: jnp.maximum(s - 1, 0)
    a_blk = lambda sec: pl.BlockSpec((A_CHUNK, width), lambda s: (cur(s), sec))
    m_blk = lambda w, sec: pl.BlockSpec((A_CHUNK, w), lambda s: (prev(s), sec))
    full = lambda r, c: pl.BlockSpec((r, c), lambda s: (0, 0))
    return pl.pallas_call(
        functools.partial(_mlstm_merge_kernel, nt, n_chunks),
        grid=(n_chunks + 1,),
        in_specs=[
            a_blk(SEC_QK), a_blk(SEC_AV), a_blk(SEC_AO), a_blk(SEC_AZ),
            pl.BlockSpec((None, A_CHUNK, LANES), lambda s: (cur(s) // nt, cur(s) % nt, 0)),
            pl.BlockSpec((None, SUBLANES, A_CHUNK), lambda s: (cur(s) // nt, 0, cur(s) % nt)),
            full(CONV_K, width), full(1, width), full(1, width),
            m_blk(d, 0), m_blk(d, SEC_GA), m_blk(d, SEC_GB), m_blk(d, 0), m_blk(pd, 0),
            full(d, d), full(d, d), full(d, d), full(d, d), full(pd, d),
            full(1, d), full(1, d),
        ],
        out_specs=pl.BlockSpec((A_CHUNK, d), lambda s: (prev(s), 0)),
        out_shape=jax.ShapeDtypeStruct((m, d), F32),
        scratch_shapes=[
            pltpu.VMEM((A_CHUNK, width), BF16),
            pltpu.VMEM((A_CHUNK + SUBLANES, width), F32),
            pltpu.VMEM((A_HEADS, A_DQK, A_DV), F32),
            pltpu.VMEM((A_HEADS, 1, A_DQK), F32),
            pltpu.VMEM((A_HEADS, SUBLANES, LANES), F32),
            pltpu.VMEM((1, LANES), F32),
        ],
        compiler_params=pltpu.CompilerParams(
            dimension_semantics=("arbitrary",), vmem_limit_bytes=VMEM_LIMIT),
        name="mlstm_merge",
    )(proj, proj, proj, proj, gcol, grow, conv_w, conv_b, head_g,
      hb, proj, proj, x2, p2, wa, wb, wo, wg, wp, png, fng)


def _split_w_in(w):
    qkw = A_HEADS * A_DQK
    aw = A_HEADS * A_DV
    bw = B_HEADS * B_DH
    d = w.shape[0]
    o_ai = 2 * qkw + aw
    o_ao = o_ai + 2 * A_HEADS
    o_bf = o_ao + 2 * aw + 3 * bw
    o_bz = o_bf + B_HEADS
    seg_cols = (o_ai, o_bf - o_ao, w.shape[1] - o_bz)
    shifts = (0, o_ao - o_ai, o_ao - o_ai + o_bz - o_bf)
    assert all(c % IN_TN == 0 for c in seg_cols) and all(s % SUBLANES == 0 for s in shifts)
    seg_tiles = tuple(c // IN_TN for c in seg_cols)
    w_t = w.T
    w_gate_t = jnp.concatenate(
        [w_t[o_ai:o_ao], w_t[o_bf:o_bz], jnp.zeros((LANES - N_GATE, d), w.dtype)], axis=0)
    return w_t, w_gate_t, seg_tiles, shifts


def _layer(x, p_i, attn_norm_g, w_in, conv_w, conv_b, a_bias_i, a_bias_f, a_head_norm_g, b_bias_f,
           w_branch_a, w_branch_b, w_out, ple_norm_g, w_ple_gate, w_ple_proj, out_norm_g):
    bsz, seq, d = x.shape
    m = bsz * seq
    x2 = x.reshape(m, d)
    w_t, w_gate_t, seg_tiles, shifts = _split_w_in(w_in)
    proj, gates = _in_proj(x2, attn_norm_g.reshape(1, d), w_t, w_gate_t, seg_tiles, shifts)

    bias = jnp.concatenate([a_bias_i, a_bias_f, b_bias_f, jnp.zeros((LANES - N_GATE,), F32)])
    gcol, grow, pieces = _gates(gates.reshape(bsz, seq, LANES), bias.reshape(1, LANES))

    hb = _fox(proj, pieces, bsz, seq)
    out = _mlstm_merge(proj, gcol, grow, conv_w, conv_b.reshape(1, -1),
                       a_head_norm_g.reshape(1, -1), hb, x2, p_i.reshape(m, -1),
                       w_branch_a.astype(BF16), w_branch_b.astype(BF16), w_out.astype(BF16),
                       w_ple_gate.astype(BF16), w_ple_proj.astype(BF16),
                       ple_norm_g.reshape(1, d), out_norm_g.reshape(1, d), bsz, seq)
    return out.reshape(bsz, seq, d)


def kernel(x, p, attn_norm_g, w_in, conv_w, conv_b, a_bias_i, a_bias_f, a_head_norm_g, b_bias_f,
           w_branch_a, w_branch_b, w_out, ple_norm_g, w_ple_gate, w_ple_proj, final_norm_g):
    depth = w_in.shape[0]
    assert depth == 1, "the final norm is fused into the single layer's merge kernel"
    return _layer(x, p[0], attn_norm_g[0], w_in[0], conv_w[0], conv_b[0], a_bias_i[0], a_bias_f[0],
                  a_head_norm_g[0], b_bias_f[0], w_branch_a[0], w_branch_b[0], w_out[0],
                  ple_norm_g[0], w_ple_gate[0], w_ple_proj[0], final_norm_g)
```

```python
import functools
import math

import jax
import jax.numpy as jnp
from jax import lax
from jax.experimental import pallas as pl
from jax.experimental.pallas import tpu as pltpu

F32 = jnp.float32
BF16 = jnp.bfloat16

EPS = 1e-6
A_HEADS = 4
A_DQK = 128
A_DV = 256
CONV_K = 4
B_HEADS = 16
B_DH = 64
LANES = 128
SUBLANES = 8
BF16_ROWS = 16
NEG_BIG = -1e30
LOG2E = math.log2(math.e)

IN_TM = 2048
IN_TN = 1024
CUM_BLK = 256
A_CHUNK = 256
FOX_TK = 256
FOX_TQ = 2 * FOX_TK
FOX_G = 4
MERGE_TM = 512
VMEM_LIMIT = 56 * 1024 * 1024

SECTION_W = A_HEADS * A_DV
SEC_QK, SEC_AV, SEC_AO, SEC_AZ, SEC_BQ, SEC_BK, SEC_BV, SEC_BZ, SEC_GA, SEC_GB = range(10)

N_GATE = 2 * A_HEADS + B_HEADS
B_LANE0 = 2 * A_HEADS
PIECE_OFFS = (B_LANE0, B_LANE0 + B_HEADS, B_LANE0 + 2 * B_HEADS)


def _sigmoid(x):
    return 1.0 / (1.0 + jnp.exp2(x * (-LOG2E)))


def _silu(x):
    return x * _sigmoid(x)


def _rms_norm(x, g):
    ms = jnp.mean(x * x, axis=-1, keepdims=True)
    return (x * lax.rsqrt(ms + EPS)) * g


def _split3(x):
    x1 = x.astype(BF16)
    r1 = x - x1.astype(F32)
    x2 = r1.astype(BF16)
    x3 = (r1 - x2.astype(F32)).astype(BF16)
    return x1, x2, x3


_NT = (((1,), (1,)), ((), ()))


def _in_proj_kernel(x_ref, g_ref, wt_ref, wgt_ref, proj_ref, gates_ref, h_scr):
    @pl.when(pl.program_id(1) == 0)
    def _():
        h = _rms_norm(x_ref[...], g_ref[...]).astype(BF16)
        h_scr[...] = h
        gates_ref[...] = lax.dot_general(h, wgt_ref[...].astype(BF16), _NT,
                                         preferred_element_type=F32)

    proj_ref[...] = lax.dot_general(h_scr[...], wt_ref[...].astype(BF16), _NT,
                                    preferred_element_type=F32).astype(BF16)


def _in_proj(x2, g, w_t, w_gate_t, seg_tiles, shifts):
    m, d = x2.shape
    n = sum(seg_tiles) * IN_TN

    def w_rows(i, j):
        shift = shifts[0] // SUBLANES
        lo = 0
        for n_tiles, s in zip(seg_tiles[:-1], shifts[1:]):
            lo += n_tiles
            shift = jnp.where(j >= lo, s // SUBLANES, shift)
        return ((j * (IN_TN // SUBLANES) + shift) * SUBLANES, 0)

    return pl.pallas_call(
        _in_proj_kernel,
        grid=(m // IN_TM, n // IN_TN),
        in_specs=[
            pl.BlockSpec((IN_TM, d), lambda i, j: (i, 0)),
            pl.BlockSpec((1, d), lambda i, j: (0, 0)),
            pl.BlockSpec((pl.Element(IN_TN), pl.Element(d)), w_rows),
            pl.BlockSpec((LANES, d), lambda i, j: (0, 0)),
        ],
        out_specs=[
            pl.BlockSpec((IN_TM, IN_TN), lambda i, j: (i, j)),
            pl.BlockSpec((IN_TM, LANES), lambda i, j: (i, 0)),
        ],
        out_shape=[
            jax.ShapeDtypeStruct((m, n), BF16),
            jax.ShapeDtypeStruct((m, LANES), F32),
        ],
        scratch_shapes=[pltpu.VMEM((IN_TM, d), BF16)],
        compiler_params=pltpu.CompilerParams(
            dimension_semantics=("arbitrary", "arbitrary"),
            vmem_limit_bytes=VMEM_LIMIT),
        name="in_proj",
    )(x2, g, w_t, w_gate_t)


def _gates_kernel(g_ref, bias_ref, col_ref, row_ref, pc_ref):
    x = g_ref[...] + bias_ref[...]
    s = x.shape[0]
    ls = jnp.minimum(x, 0.0) - jnp.log1p(jnp.exp(-jnp.abs(x)))
    r = lax.broadcasted_iota(jnp.int32, (CUM_BLK, CUM_BLK), 0)
    c = lax.broadcasted_iota(jnp.int32, (CUM_BLK, CUM_BLK), 1)
    tri = jnp.where(r >= c, 1.0, 0.0).astype(BF16)
    carry = jnp.zeros((1, LANES), F32)
    blocks = []
    for blk in range(s // CUM_BLK):
        x1, x2, x3 = _split3(ls[blk * CUM_BLK:(blk + 1) * CUM_BLK])
        cs = (jnp.dot(tri, x3, preferred_element_type=F32)
              + jnp.dot(tri, x2, preferred_element_type=F32)
              + jnp.dot(tri, x1, preferred_element_type=F32)) + carry
        carry = cs[CUM_BLK - 1:CUM_BLK, :]
        blocks.append(cs)
    cum = jnp.concatenate(blocks, axis=0)
    lane = lax.broadcasted_iota(jnp.int32, x.shape, 1)
    res = jnp.where(lane < A_HEADS, x, cum)
    res = res * LOG2E
    col_ref[...] = res
    row_ref[...] = res.T[0:SUBLANES, :]

    in_b = (lane >= B_LANE0) & (lane < B_LANE0 + B_HEADS)
    p1, p2, p3 = _split3(jnp.where(in_b, cum * (-LOG2E), 0.0))
    pieces = (p1.astype(F32)
              + pltpu.roll(p2.astype(F32), PIECE_OFFS[1] - B_LANE0, axis=1)
              + pltpu.roll(p3.astype(F32), PIECE_OFFS[2] - B_LANE0, axis=1))
    pc_ref[...] = pieces.astype(BF16)


def _gates(gates3, bias):
    b, s, _ = gates3.shape
    return pl.pallas_call(
        _gates_kernel,
        grid=(b,),
        in_specs=[
            pl.BlockSpec((None, s, LANES), lambda i: (i, 0, 0)),
            pl.BlockSpec((1, LANES), lambda i: (0, 0)),
        ],
        out_specs=[
            pl.BlockSpec((None, s, LANES), lambda i: (i, 0, 0)),
            pl.BlockSpec((None, SUBLANES, s), lambda i: (i, 0, 0)),
            pl.BlockSpec((None, s, LANES), lambda i: (i, 0, 0)),
        ],
        out_shape=[
            jax.ShapeDtypeStruct((b, s, LANES), F32),
            jax.ShapeDtypeStruct((b, SUBLANES, s), F32),
            jax.ShapeDtypeStruct((b, s, LANES), BF16),
        ],
        compiler_params=pltpu.CompilerParams(
            dimension_semantics=("arbitrary",), vmem_limit_bytes=VMEM_LIMIT),
        name="gates",
    )(gates3, bias)


def _mlstm_reset(xpad_scr, c_scr, n_scr, m_scr, fprev_scr):
    xpad_scr[0:SUBLANES, :] = jnp.zeros((SUBLANES, xpad_scr.shape[1]), F32)
    c_scr[...] = jnp.zeros_like(c_scr)
    n_scr[...] = jnp.zeros_like(n_scr)
    m_scr[...] = jnp.zeros_like(m_scr)
    fprev_scr[...] = jnp.zeros_like(fprev_scr)


def _mlstm_chunk(qk_ref, v_ref, o_ref, z_ref, gcol_ref, grow_ref, cw_ref, cb_ref, hg_ref,
                 out_ref, xpad_scr, c_scr, n_scr, m_scr, fprev_scr, after_head=lambda: None):
    L = A_CHUNK
    qkw = A_HEADS * A_DQK

    xpad_scr[SUBLANES:SUBLANES + L, :] = qk_ref[...].astype(F32)
    y = cb_ref[...] + xpad_scr[SUBLANES:SUBLANES + L, :] * cw_ref[CONV_K - 1:CONV_K, :]
    for d in range(1, CONV_K):
        y = y + xpad_scr[SUBLANES - d:SUBLANES - d + L, :] * cw_ref[CONV_K - 1 - d:CONV_K - d, :]
    xpad_scr[0:SUBLANES, :] = xpad_scr[L:L + SUBLANES, :]
    qk = _silu(y)

    row = lax.broadcasted_iota(jnp.int32, (L, L), 0)
    col = lax.broadcasted_iota(jnp.int32, (L, L), 1)
    causal = row >= col

    for h in range(A_HEADS):
        q = qk[:, h * A_DQK:(h + 1) * A_DQK]
        k = qk[:, qkw + h * A_DQK:qkw + (h + 1) * A_DQK] * (A_DQK ** -0.5)
        qb = q.astype(BF16)
        kb = k.astype(BF16)
        v = v_ref[:, h * A_DV:(h + 1) * A_DV]

        li_c = gcol_ref[:, h:h + 1]
        f_c = gcol_ref[:, A_HEADS + h:A_HEADS + h + 1]
        li_r = grow_ref[h:h + 1, :]
        f_r = grow_ref[A_HEADS + h:A_HEADS + h + 1, :]
        f_prev = fprev_scr[0:1, A_HEADS + h:A_HEADS + h + 1]
        f_end = gcol_ref[L - 1:L, A_HEADS + h:A_HEADS + h + 1]
        m_st = m_scr[h, 0:1, 0:1]
        c_st = c_scr[h]
        n_st = n_scr[h]

        dmat = jnp.where(causal, (f_c - f_r) + li_r, NEG_BIG)
        inter = (f_c - f_prev) + m_st
        m_row = jnp.maximum(inter, jnp.max(dmat, axis=-1, keepdims=True))
        w_intra = jnp.exp2(dmat - m_row)
        w_inter = jnp.exp2(inter - m_row)
        s = lax.dot_general(qb, kb, (((1,), (1,)), ((), ())), preferred_element_type=F32)
        scores = s * w_intra
        num = (jnp.dot(scores.astype(BF16), v, preferred_element_type=F32)
               + w_inter * jnp.dot(qb, c_st.astype(BF16), preferred_element_type=F32))
        den = (jnp.sum(scores, axis=-1, keepdims=True)
               + w_inter * jnp.sum(q * n_st, axis=-1, keepdims=True))
        hh = num * (1.0 / jnp.maximum(jnp.abs(den), jnp.exp2(-m_row)))

        g_tot = f_end - f_prev
        to_end = (f_end - f_c) + li_c
        m_new = jnp.maximum(g_tot + m_st, jnp.max(to_end, axis=0, keepdims=True))
        w_k = jnp.exp2(to_end - m_new)
        decay = jnp.exp2(g_tot + m_st - m_new)
        kw = k * w_k
        c_scr[h] = decay * c_st + jnp.dot(kw.T.astype(BF16), v, preferred_element_type=F32)
        n_scr[h] = decay * n_st + jnp.sum(kw, axis=0, keepdims=True)
        m_scr[h] = jnp.broadcast_to(m_new, (SUBLANES, LANES))

        hn = hh * lax.rsqrt(jnp.mean(hh * hh, axis=-1, keepdims=True) + EPS)
        hn = hn * hg_ref[:, h * A_DV:(h + 1) * A_DV]
        og = _sigmoid(o_ref[:, h * A_DV:(h + 1) * A_DV].astype(F32))
        zz = _silu(z_ref[:, h * A_DV:(h + 1) * A_DV].astype(F32))
        out_ref[:, h * A_DV:(h + 1) * A_DV] = ((og * hn) * zz).astype(BF16)
        after_head()

    fprev_scr[...] = gcol_ref[L - 1:L, :]


FOX_ACC_ROWS = B_DH + BF16_ROWS


def _fox_kernel(q_ref, k_ref, v_ref, z_ref, pc_ref, out_ref, vt_scr, acc_scr, rhs_scr, m_scr,
                s2_scr, cm2_scr):
    s_scr = (s2_scr.at[0], s2_scr.at[1])
    cm_scr = (cm2_scr.at[0], cm2_scr.at[1])
    hg = pl.program_id(1)
    qi = pl.program_id(2)
    TQ, TK = FOX_TQ, FOX_TK
    seq = k_ref.shape[0]
    nh = 2 * FOX_G

    @pl.when(qi == 0)
    def _():
        for g in range(FOX_G):
            vt = v_ref[:, g * LANES:(g + 1) * LANES].astype(F32).T
            for hh in range(2):
                vt_scr[2 * g + hh, 0:B_DH, :] = vt[hh * B_DH:(hh + 1) * B_DH, :].astype(BF16)
                vt_scr[2 * g + hh, B_DH:FOX_ACC_ROWS, :] = jnp.ones((BF16_ROWS, seq), BF16)

    row = lax.broadcasted_iota(jnp.int32, (LANES, TQ), 0)
    for g in range(FOX_G):
        qt = (q_ref[:, g * LANES:(g + 1) * LANES].astype(F32) * (B_DH ** -0.5 * LOG2E)).T
        for hh in range(2):
            h = 2 * g + hh
            head = hg * nh + h
            qm = jnp.where((row >= hh * B_DH) & (row < (hh + 1) * B_DH), qt, 0.0)
            sel = jnp.where((row == PIECE_OFFS[0] + head) | (row == PIECE_OFFS[1] + head)
                            | (row == PIECE_OFFS[2] + head), 1.0, 0.0)
            rhs_scr[h, 0:LANES, :] = qm.astype(BF16)
            rhs_scr[h, LANES:2 * LANES, :] = sel.astype(BF16)

    acc_scr[...] = jnp.zeros_like(acc_scr)
    m_scr[...] = jnp.full(m_scr.shape, NEG_BIG, F32)

    def key_block(kj):
        k0 = pl.multiple_of(kj * TK, TK)
        pcs = pc_ref[pl.ds(k0, TK), :]
        return [jnp.concatenate([k_ref[pl.ds(k0, TK), g * LANES:(g + 1) * LANES], pcs], axis=1)
                for g in range(FOX_G)]

    def scores_head(h, lhs, slot):
        s = jnp.dot(lhs[h // 2], rhs_scr[h], preferred_element_type=F32)
        s_scr[slot][h] = s
        cm_scr[slot][h] = jnp.broadcast_to(jnp.max(s, axis=0, keepdims=True), (SUBLANES, TQ))

    def scores(kj, slot):
        lhs = key_block(kj)
        for h in range(nh):
            scores_head(h, lhs, slot)

    def softmax_pv(h, kj, s, cmax, lo):
        k0 = pl.multiple_of(kj * TK, TK)
        m_old = m_scr[h, 0:1, lo:TQ]
        m_new = jnp.maximum(m_old, cmax)
        alpha = jnp.exp2(m_old - m_new)
        p = jnp.exp2(s - m_new).astype(BF16)
        pv = jnp.dot(vt_scr[h, :, pl.ds(k0, TK)], p, preferred_element_type=F32)
        acc_scr[h, :, lo:TQ] = alpha * acc_scr[h, :, lo:TQ] + pv
        m_scr[h, :, lo:TQ] = jnp.broadcast_to(m_new, (SUBLANES, TQ - lo))

    def overlapped(kj_next, slot_next, kj, slot):
        lhs = key_block(kj_next)
        for h in range(nh):
            scores_head(h, lhs, slot_next)
            softmax_pv(h, kj, s_scr[slot][h], cm_scr[slot][h, 0:1, :], 0)

    def pair(i, carry):
        overlapped(2 * i + 1, 1, 2 * i, 0)
        overlapped(2 * i + 2, 0, 2 * i + 1, 1)
        return carry

    scores(0, 0)
    lax.fori_loop(0, qi, pair, 0)

    half = TQ - TK
    lhs_b = key_block(2 * qi + 1)
    r = lax.broadcasted_iota(jnp.int32, (TK, TQ), 0)
    c = lax.broadcasted_iota(jnp.int32, (TK, TQ), 1)
    rb = lax.broadcasted_iota(jnp.int32, (TK, TK), 0)
    cb = lax.broadcasted_iota(jnp.int32, (TK, TK), 1)
    for h in range(nh):
        sb = jnp.dot(lhs_b[h // 2], rhs_scr[h, :, half:TQ], preferred_element_type=F32)
        s_scr[1][h, :, half:TQ] = jnp.where(cb >= rb, sb, NEG_BIG)
        s = jnp.where(c >= r, s_scr[0][h], NEG_BIG)
        softmax_pv(h, 2 * qi, s, jnp.max(s, axis=0, keepdims=True), 0)

    for g in range(FOX_G):
        parts = []
        for hh in range(2):
            h = 2 * g + hh
            s = s_scr[1][h, :, half:TQ]
            softmax_pv(h, 2 * qi + 1, s, jnp.max(s, axis=0, keepdims=True), half)
            a = acc_scr[h]
            parts.append(a[0:B_DH, :] * (1.0 / a[B_DH:B_DH + 1, :]))
        o = jnp.concatenate(parts, axis=0).T
        zz = _silu(z_ref[:, g * LANES:(g + 1) * LANES].astype(F32))
        out_ref[:, g * LANES:(g + 1) * LANES] = (o * zz).astype(BF16)


def _fox(proj, pieces, bsz, seq):
    m = proj.shape[0]
    nq = seq // FOX_TQ
    nh = 2 * FOX_G
    w = FOX_G * LANES
    ngrp = (B_HEADS * B_DH) // w
    sec = SECTION_W // w
    col0 = SEC_BQ * sec
    assert (SEC_BK, SEC_BV, SEC_BZ) == (SEC_BQ + 1, SEC_BQ + 2, SEC_BQ + 3)
    return pl.pallas_call(
        _fox_kernel,
        grid=(bsz, ngrp, nq),
        in_specs=[
            pl.BlockSpec((FOX_TQ, w), lambda b, hg, qi: (b * nq + qi, col0 + hg)),
            pl.BlockSpec((seq, w), lambda b, hg, qi: (b, col0 + sec + hg)),
            pl.BlockSpec((seq, w), lambda b, hg, qi: (b, col0 + 2 * sec + hg)),
            pl.BlockSpec((FOX_TQ, w), lambda b, hg, qi: (b * nq + qi, col0 + 3 * sec + hg)),
            pl.BlockSpec((None, seq, LANES), lambda b, hg, qi: (b, 0, 0)),
        ],
        out_specs=pl.BlockSpec((FOX_TQ, w), lambda b, hg, qi: (b * nq + qi, hg)),
        out_shape=jax.ShapeDtypeStruct((m, B_HEADS * B_DH), BF16),
        scratch_shapes=[
            pltpu.VMEM((nh, FOX_ACC_ROWS, seq), BF16),
            pltpu.VMEM((nh, FOX_ACC_ROWS, FOX_TQ), F32),
            pltpu.VMEM((nh, 2 * LANES, FOX_TQ), BF16),
            pltpu.VMEM((nh, SUBLANES, FOX_TQ), F32),
            pltpu.VMEM((2, nh, FOX_TK, FOX_TQ), F32),
            pltpu.VMEM((2, nh, SUBLANES, FOX_TQ), F32),
        ],
        compiler_params=pltpu.CompilerParams(
            dimension_semantics=("arbitrary", "arbitrary", "arbitrary"),
            vmem_limit_bytes=VMEM_LIMIT),
        name="fox",
    )(proj, proj, proj, proj, pieces)


def _merge_stages(ha_ref, hb_ref, ga_ref, gb_ref, x_ref, p_ref, wa_ref, wb_ref, wo_ref, wg_ref,
                  wp_ref, png_ref, fng_ref, out_ref):
    ya = jnp.dot(ha_ref[...], wa_ref[...], preferred_element_type=F32)
    yb = jnp.dot(hb_ref[...], wb_ref[...], preferred_element_type=F32)
    yield
    merged = (_sigmoid(ga_ref[...].astype(F32)) * ya + _sigmoid(gb_ref[...].astype(F32)) * yb)
    x1 = x_ref[...] + jnp.dot(merged.astype(BF16), wo_ref[...], preferred_element_type=F32)
    yield
    r = _rms_norm(x1, png_ref[...]).astype(BF16)
    gate = _sigmoid(jnp.dot(r, wg_ref[...], preferred_element_type=F32))
    yield
    pp = jnp.dot(p_ref[...].astype(BF16), wp_ref[...], preferred_element_type=F32)
    x2 = x1 + gate * pp
    out_ref[...] = _rms_norm(x2, fng_ref[...])
    yield


def _mlstm_merge_kernel(nt, n_chunks,
                        qk_ref, v_ref, o_ref, z_ref, gcol_ref, grow_ref, cw_ref, cb_ref, hg_ref,
                        hb_ref, ga_ref, gb_ref, x_ref, p_ref, wa_ref, wb_ref, wo_ref, wg_ref,
                        wp_ref, png_ref, fng_ref, out_ref,
                        ha_scr, xpad_scr, c_scr, n_scr, m_scr, fprev_scr):
    s = pl.program_id(0)
    chunk = jnp.minimum(s, n_chunks - 1)
    state = (xpad_scr, c_scr, n_scr, m_scr, fprev_scr)

    @pl.when(s == 0)
    def _():
        ha_scr[...] = jnp.zeros_like(ha_scr)

    @pl.when(chunk % nt == 0)
    def _():
        _mlstm_reset(*state)

    stages = _merge_stages(ha_scr, hb_ref, ga_ref, gb_ref, x_ref, p_ref, wa_ref, wb_ref, wo_ref,
                           wg_ref, wp_ref, png_ref, fng_ref, out_ref)
    next(stages)
    _mlstm_chunk(qk_ref, v_ref, o_ref, z_ref, gcol_ref, grow_ref, cw_ref, cb_ref, hg_ref,
                 ha_scr, *state, after_head=lambda: next(stages, None))


def _mlstm_merge(proj, gcol, grow, conv_w, conv_b, head_g, hb, x2, p2, wa, wb, wo, wg, wp,
                 png, fng, bsz, seq):
    m, d = x2.shape
    pd = p2.shape[1]
    width = SECTION_W
    nt = seq // A_CHUNK
    n_chunks = bsz * nt
    cur = lambda s: jnp.minimum(s, n_chunks - 1)
    prev = lambda s: jnp.maximum(s - 1, 0)
    a_blk = lambda sec: pl.BlockSpec((A_CHUNK, width), lambda s: (cur(s), sec))
    m_blk = lambda w, sec: pl.BlockSpec((A_CHUNK, w), lambda s: (prev(s), sec))
    full = lambda r, c: pl.BlockSpec((r, c), lambda s: (0, 0))
    return pl.pallas_call(
        functools.partial(_mlstm_merge_kernel, nt, n_chunks),
        grid=(n_chunks + 1,),
        in_specs=[
            a_blk(SEC_QK), a_blk(SEC_AV), a_blk(SEC_AO), a_blk(SEC_AZ),
            pl.BlockSpec((None, A_CHUNK, LANES), lambda s: (cur(s) // nt, cur(s) % nt, 0)),
            pl.BlockSpec((None, SUBLANES, A_CHUNK), lambda s: (cur(s) // nt, 0, cur(s) % nt)),
            full(CONV_K, width), full(1, width), full(1, width),
            m_blk(d, 0), m_blk(d, SEC_GA), m_blk(d, SEC_GB), m_blk(d, 0), m_blk(pd, 0),
            full(d, d), full(d, d), full(d, d), full(d, d), full(pd, d),
            full(1, d), full(1, d),
        ],
        out_specs=pl.BlockSpec((A_CHUNK, d), lambda s: (prev(s), 0)),
        out_shape=jax.ShapeDtypeStruct((m, d), F32),
        scratch_shapes=[
            pltpu.VMEM((A_CHUNK, width), BF16),
            pltpu.VMEM((A_CHUNK + SUBLANES, width), F32),
            pltpu.VMEM((A_HEADS, A_DQK, A_DV), F32),
            pltpu.VMEM((A_HEADS, 1, A_DQK), F32),
            pltpu.VMEM((A_HEADS, SUBLANES, LANES), F32),
            pltpu.VMEM((1, LANES), F32),
        ],
        compiler_params=pltpu.CompilerParams(
            dimension_semantics=("arbitrary",), vmem_limit_bytes=VMEM_LIMIT),
        name="mlstm_merge",
    )(proj, proj, proj, proj, gcol, grow, conv_w, conv_b, head_g,
      hb, proj, proj, x2, p2, wa, wb, wo, wg, wp, png, fng)


def _split_w_in(w):
    qkw = A_HEADS * A_DQK
    aw = A_HEADS * A_DV
    bw = B_HEADS * B_DH
    d = w.shape[0]
    o_ai = 2 * qkw + aw
    o_ao = o_ai + 2 * A_HEADS
    o_bf = o_ao + 2 * aw + 3 * bw
    o_bz = o_bf + B_HEADS
    seg_cols = (o_ai, o_bf - o_ao, w.shape[1] - o_bz)
    shifts = (0, o_ao - o_ai, o_ao - o_ai + o_bz - o_bf)
    assert all(c % IN_TN == 0 for c in seg_cols) and all(s % SUBLANES == 0 for s in shifts)
    seg_tiles = tuple(c // IN_TN for c in seg_cols)
    w_t = w.T
    w_gate_t = jnp.concatenate(
        [w_t[o_ai:o_ao], w_t[o_bf:o_bz], jnp.zeros((LANES - N_GATE, d), w.dtype)], axis=0)
    return w_t, w_gate_t, seg_tiles, shifts


def _layer(x, p_i, attn_norm_g, w_in, conv_w, conv_b, a_bias_i, a_bias_f, a_head_norm_g, b_bias_f,
           w_branch_a, w_branch_b, w_out, ple_norm_g, w_ple_gate, w_ple_proj, out_norm_g):
    bsz, seq, d = x.shape
    m = bsz * seq
    x2 = x.reshape(m, d)
    w_t, w_gate_t, seg_tiles, shifts = _split_w_in(w_in)
    proj, gates = _in_proj(x2, attn_norm_g.reshape(1, d), w_t, w_gate_t, seg_tiles, shifts)

    bias = jnp.concatenate([a_bias_i, a_bias_f, b_bias_f, jnp.zeros((LANES - N_GATE,), F32)])
    gcol, grow, pieces = _gates(gates.reshape(bsz, seq, LANES), bias.reshape(1, LANES))

    hb = _fox(proj, pieces, bsz, seq)
    out = _mlstm_merge(proj, gcol, grow, conv_w, conv_b.reshape(1, -1),
                       a_head_norm_g.reshape(1, -1), hb, x2, p_i.reshape(m, -1),
                       w_branch_a.astype(BF16), w_branch_b.astype(BF16), w_out.astype(BF16),
                       w_ple_gate.astype(BF16), w_ple_proj.astype(BF16),
                       ple_norm_g.reshape(1, d), out_norm_g.reshape(1, d), bsz, seq)
    return out.reshape(bsz, seq, d)


def kernel(x, p, attn_norm_g, w_in, conv_w, conv_b, a_bias_i, a_bias_f, a_head_norm_g, b_bias_f,
           w_branch_a, w_branch_b, w_out, ple_norm_g, w_ple_gate, w_ple_proj, final_norm_g):
    depth = w_in.shape[0]
    assert depth == 1, "the final norm is fused into the single layer's merge kernel"
    return _layer(x, p[0], attn_norm_g[0], w_in[0], conv_w[0], conv_b[0], a_bias_i[0], a_bias_f[0],
                  a_head_norm_g[0], b_bias_f[0], w_branch_a[0], w_branch_b[0], w_out[0],
                  ple_norm_g[0], w_ple_gate[0], w_ple_proj[0], final_norm_g)
```

```python
import functools
import math

import jax
import jax.numpy as jnp
from jax import lax
from jax.experimental import pallas as pl
from jax.experimental.pallas import tpu as pltpu

F32 = jnp.float32
BF16 = jnp.bfloat16

EPS = 1e-6
A_HEADS = 4
A_DQK = 128
A_DV = 256
CONV_K = 4
B_HEADS = 16
B_DH = 64
LANES = 128
SUBLANES = 8
BF16_ROWS = 16
NEG_BIG = -1e30
LOG2E = math.log2(math.e)

IN_TM = 2048
IN_TN = 1024
IN_CHUNKS = 8
CUM_BLK = 256
A_CHUNK = 256
FOX_TK = 256
FOX_TQ = 2 * FOX_TK
FOX_G = 4
MERGE_TM = 512
VMEM_LIMIT = 56 * 1024 * 1024

SECTION_W = A_HEADS * A_DV
SEC_QK, SEC_AV, SEC_AO, SEC_AZ, SEC_BQ, SEC_BK, SEC_BV, SEC_BZ, SEC_GA, SEC_GB = range(10)

N_GATE = 2 * A_HEADS + B_HEADS
B_LANE0 = 2 * A_HEADS
PIECE_OFFS = (B_LANE0, B_LANE0 + B_HEADS, B_LANE0 + 2 * B_HEADS)


def _sigmoid(x):
    return 1.0 / (1.0 + jnp.exp2(x * (-LOG2E)))


def _silu(x):
    return x * _sigmoid(x)


def _rms_norm(x, g):
    ms = jnp.mean(x * x, axis=-1, keepdims=True)
    return (x * lax.rsqrt(ms + EPS)) * g


def _split3(x):
    x1 = x.astype(BF16)
    r1 = x - x1.astype(F32)
    x2 = r1.astype(BF16)
    x3 = (r1 - x2.astype(F32)).astype(BF16)
    return x1, x2, x3


_NT = (((1,), (1,)), ((), ()))


def _in_proj_kernel(x_ref, g_ref, wt_ref, wgt_ref, cw_ref, cb_ref, cs_ref,
                    proj_ref, gates_ref, qk_ref, h_scr, qkraw_scr, xpad_scr):
    i = pl.program_id(0)
    j = pl.program_id(1)
    tm = x_ref.shape[0]

    @pl.when((i == 0) & (j == 0))
    def _():
        qkraw_scr[...] = jnp.zeros_like(qkraw_scr)
        xpad_scr[0:SUBLANES, :] = jnp.zeros((SUBLANES, LANES), F32)

    @pl.when(j == 0)
    def _():
        h = _rms_norm(x_ref[...], g_ref[...]).astype(BF16)
        h_scr[...] = h
        gates_ref[...] = lax.dot_general(h, wgt_ref[...].astype(BF16), _NT,
                                         preferred_element_type=F32)

    n_strips = qkraw_scr.shape[1] // LANES
    lane0 = pl.multiple_of(jnp.clip(j - 1, 0, n_strips - 1) * LANES, LANES)
    xpad_scr[SUBLANES:SUBLANES + tm, :] = qkraw_scr[:, pl.ds(lane0, LANES)].astype(F32)

    w = wt_ref[...].astype(BF16)
    ch = tm // IN_CHUNKS
    for c in range(IN_CHUNKS):
        r0 = c * ch
        proj_ref[r0:r0 + ch, :] = lax.dot_general(
            h_scr[r0:r0 + ch, :], w, _NT, preferred_element_type=F32).astype(BF16)
        y = cb_ref[...]
        for d in range(CONV_K):
            y = y + (xpad_scr[SUBLANES + r0 - d:SUBLANES + r0 - d + ch, :]
                     * cw_ref[CONV_K - 1 - d:CONV_K - d, :])
        qk_ref[r0:r0 + ch, :] = (_silu(y) * cs_ref[...]).astype(BF16)

    @pl.when(j == 0)
    def _():
        qkraw_scr[...] = proj_ref[...]


def _in_proj(x2, g, w_t, w_gate_t, seg_tiles, shifts, conv_w, conv_b, conv_scale, seq):
    m, d = x2.shape
    n = sum(seg_tiles) * IN_TN
    n_tiles = n // IN_TN
    n_strips = SECTION_W // LANES
    assert IN_TM == seq and IN_TN == SECTION_W and SEC_QK == 0 and n_tiles > n_strips
    strip = lambda i, j: (0, jnp.clip(j - 1, 0, n_strips - 1))

    def w_rows(i, j):
        shift = shifts[0] // SUBLANES
        lo = 0
        for n_tiles, s in zip(seg_tiles[:-1], shifts[1:]):
            lo += n_tiles
            shift = jnp.where(j >= lo, s // SUBLANES, shift)
        return ((j * (IN_TN // SUBLANES) + shift) * SUBLANES, 0)

    return pl.pallas_call(
        _in_proj_kernel,
        grid=(m // IN_TM, n // IN_TN),
        in_specs=[
            pl.BlockSpec((IN_TM, d), lambda i, j: (i, 0)),
            pl.BlockSpec((1, d), lambda i, j: (0, 0)),
            pl.BlockSpec((pl.Element(IN_TN), pl.Element(d)), w_rows),
            pl.BlockSpec((LANES, d), lambda i, j: (0, 0)),
            pl.BlockSpec((CONV_K, LANES), strip),
            pl.BlockSpec((1, LANES), strip),
            pl.BlockSpec((1, LANES), strip),
        ],
        out_specs=[
            pl.BlockSpec((IN_TM, IN_TN), lambda i, j: (i, j)),
            pl.BlockSpec((IN_TM, LANES), lambda i, j: (i, 0)),
            pl.BlockSpec((IN_TM, LANES), lambda i, j: (i, strip(i, j)[1])),
        ],
        out_shape=[
            jax.ShapeDtypeStruct((m, n), BF16),
            jax.ShapeDtypeStruct((m, LANES), F32),
            jax.ShapeDtypeStruct((m, SECTION_W), BF16),
        ],
        scratch_shapes=[pltpu.VMEM((IN_TM, d), BF16),
                        pltpu.VMEM((IN_TM, SECTION_W), BF16),
                        pltpu.VMEM((IN_TM + SUBLANES, LANES), F32)],
        compiler_params=pltpu.CompilerParams(
            dimension_semantics=("arbitrary", "arbitrary"),
            vmem_limit_bytes=VMEM_LIMIT),
        name="in_proj",
    )(x2, g, w_t, w_gate_t, conv_w, conv_b, conv_scale)


def _gates_kernel(g_ref, bias_ref, col_ref, row_ref, pc_ref):
    x = g_ref[...] + bias_ref[...]
    s = x.shape[0]
    ls = jnp.minimum(x, 0.0) - jnp.log1p(jnp.exp(-jnp.abs(x)))
    r = lax.broadcasted_iota(jnp.int32, (CUM_BLK, CUM_BLK), 0)
    c = lax.broadcasted_iota(jnp.int32, (CUM_BLK, CUM_BLK), 1)
    tri = jnp.where(r >= c, 1.0, 0.0).astype(BF16)
    carry = jnp.zeros((1, LANES), F32)
    blocks = []
    for blk in range(s // CUM_BLK):
        x1, x2, x3 = _split3(ls[blk * CUM_BLK:(blk + 1) * CUM_BLK])
        cs = (jnp.dot(tri, x3, preferred_element_type=F32)
              + jnp.dot(tri, x2, preferred_element_type=F32)
              + jnp.dot(tri, x1, preferred_element_type=F32)) + carry
        carry = cs[CUM_BLK - 1:CUM_BLK, :]
        blocks.append(cs)
    cum = jnp.concatenate(blocks, axis=0)
    lane = lax.broadcasted_iota(jnp.int32, x.shape, 1)
    res = jnp.where(lane < A_HEADS, x, cum)
    res = res * LOG2E
    col_ref[...] = res
    row_ref[...] = res.T[0:SUBLANES, :]

    in_b = (lane >= B_LANE0) & (lane < B_LANE0 + B_HEADS)
    p1, p2, p3 = _split3(jnp.where(in_b, cum * (-LOG2E), 0.0))
    pieces = (p1.astype(F32)
              + pltpu.roll(p2.astype(F32), PIECE_OFFS[1] - B_LANE0, axis=1)
              + pltpu.roll(p3.astype(F32), PIECE_OFFS[2] - B_LANE0, axis=1))
    pc_ref[...] = pieces.astype(BF16)


def _gates(gates3, bias):
    b, s, _ = gates3.shape
    return pl.pallas_call(
        _gates_kernel,
        grid=(b,),
        in_specs=[
            pl.BlockSpec((None, s, LANES), lambda i: (i, 0, 0)),
            pl.BlockSpec((1, LANES), lambda i: (0, 0)),
        ],
        out_specs=[
            pl.BlockSpec((None, s, LANES), lambda i: (i, 0, 0)),
            pl.BlockSpec((None, SUBLANES, s), lambda i: (i, 0, 0)),
            pl.BlockSpec((None, s, LANES), lambda i: (i, 0, 0)),
        ],
        out_shape=[
            jax.ShapeDtypeStruct((b, s, LANES), F32),
            jax.ShapeDtypeStruct((b, SUBLANES, s), F32),
            jax.ShapeDtypeStruct((b, s, LANES), BF16),
        ],
        compiler_params=pltpu.CompilerParams(
            dimension_semantics=("arbitrary",), vmem_limit_bytes=VMEM_LIMIT),
        name="gates",
    )(gates3, bias)


def _mlstm_reset(c_scr, n_scr, m_scr, fprev_scr):
    c_scr[...] = jnp.zeros_like(c_scr)
    n_scr[...] = jnp.zeros_like(n_scr)
    m_scr[...] = jnp.zeros_like(m_scr)
    fprev_scr[...] = jnp.zeros_like(fprev_scr)


def _mlstm_chunk(qk_ref, v_ref, o_ref, z_ref, gcol_ref, grow_ref, hg_ref,
                 out_ref, c_scr, n_scr, m_scr, fprev_scr, after_head=lambda: None):
    L = A_CHUNK
    qkw = A_HEADS * A_DQK

    row = lax.broadcasted_iota(jnp.int32, (L, L), 0)
    col = lax.broadcasted_iota(jnp.int32, (L, L), 1)
    causal = row >= col

    for h in range(A_HEADS):
        qb = qk_ref[:, h * A_DQK:(h + 1) * A_DQK]
        kb = qk_ref[:, qkw + h * A_DQK:qkw + (h + 1) * A_DQK]
        q = qb.astype(F32)
        k = kb.astype(F32)
        v = v_ref[:, h * A_DV:(h + 1) * A_DV]

        li_c = gcol_ref[:, h:h + 1]
        f_c = gcol_ref[:, A_HEADS + h:A_HEADS + h + 1]
        li_r = grow_ref[h:h + 1, :]
        f_r = grow_ref[A_HEADS + h:A_HEADS + h + 1, :]
        f_prev = fprev_scr[0:1, A_HEADS + h:A_HEADS + h + 1]
        f_end = gcol_ref[L - 1:L, A_HEADS + h:A_HEADS + h + 1]
        m_st = m_scr[h, 0:1, 0:1]
        c_st = c_scr[h]
        n_st = n_scr[h]

        dmat = jnp.where(causal, (f_c - f_r) + li_r, NEG_BIG)
        inter = (f_c - f_prev) + m_st
        m_row = jnp.maximum(inter, jnp.max(dmat, axis=-1, keepdims=True))
        w_intra = jnp.exp2(dmat - m_row)
        w_inter = jnp.exp2(inter - m_row)
        s = lax.dot_general(qb, kb, (((1,), (1,)), ((), ())), preferred_element_type=F32)
        scores = s * w_intra
        num = (jnp.dot(scores.astype(BF16), v, preferred_element_type=F32)
               + w_inter * jnp.dot(qb, c_st.astype(BF16), preferred_element_type=F32))
        den = (jnp.sum(scores, axis=-1, keepdims=True)
               + w_inter * jnp.sum(q * n_st, axis=-1, keepdims=True))
        hh = num * (1.0 / jnp.maximum(jnp.abs(den), jnp.exp2(-m_row)))

        g_tot = f_end - f_prev
        to_end = (f_end - f_c) + li_c
        m_new = jnp.maximum(g_tot + m_st, jnp.max(to_end, axis=0, keepdims=True))
        w_k = jnp.exp2(to_end - m_new)
        decay = jnp.exp2(g_tot + m_st - m_new)
        kw = k * w_k
        c_scr[h] = decay * c_st + jnp.dot(kw.T.astype(BF16), v, preferred_element_type=F32)
        n_scr[h] = decay * n_st + jnp.sum(kw, axis=0, keepdims=True)
        m_scr[h] = jnp.broadcast_to(m_new, (SUBLANES, LANES))

        hn = hh * lax.rsqrt(jnp.mean(hh * hh, axis=-1, keepdims=True) + EPS)
        hn = hn * hg_ref[:, h * A_DV:(h + 1) * A_DV]
        og = _sigmoid(o_ref[:, h * A_DV:(h + 1) * A_DV].astype(F32))
        zz = _silu(z_ref[:, h * A_DV:(h + 1) * A_DV].astype(F32))
        out_ref[:, h * A_DV:(h + 1) * A_DV] = ((og * hn) * zz).astype(BF16)
        after_head()

    fprev_scr[...] = gcol_ref[L - 1:L, :]


FOX_ACC_ROWS = B_DH + BF16_ROWS


def _fox_kernel(q_ref, k_ref, v_ref, z_ref, pc_ref, out_ref, vt_scr, acc_scr, rhs_scr, m_scr,
                s2_scr, cm2_scr):
    s_scr = (s2_scr.at[0], s2_scr.at[1])
    cm_scr = (cm2_scr.at[0], cm2_scr.at[1])
    hg = pl.program_id(1)
    qi = pl.program_id(2)
    TQ, TK = FOX_TQ, FOX_TK
    seq = k_ref.shape[0]
    nh = 2 * FOX_G

    @pl.when(qi == 0)
    def _():
        for g in range(FOX_G):
            vt = v_ref[:, g * LANES:(g + 1) * LANES].astype(F32).T
            for hh in range(2):
                vt_scr[2 * g + hh, 0:B_DH, :] = vt[hh * B_DH:(hh + 1) * B_DH, :].astype(BF16)
                vt_scr[2 * g + hh, B_DH:FOX_ACC_ROWS, :] = jnp.ones((BF16_ROWS, seq), BF16)

    row = lax.broadcasted_iota(jnp.int32, (LANES, TQ), 0)
    for g in range(FOX_G):
        qt = (q_ref[:, g * LANES:(g + 1) * LANES].astype(F32) * (B_DH ** -0.5 * LOG2E)).T
        for hh in range(2):
            h = 2 * g + hh
            head = hg * nh + h
            qm = jnp.where((row >= hh * B_DH) & (row < (hh + 1) * B_DH), qt, 0.0)
            sel = jnp.where((row == PIECE_OFFS[0] + head) | (row == PIECE_OFFS[1] + head)
                            | (row == PIECE_OFFS[2] + head), 1.0, 0.0)
            rhs_scr[h, 0:LANES, :] = qm.astype(BF16)
            rhs_scr[h, LANES:2 * LANES, :] = sel.astype(BF16)

    acc_scr[...] = jnp.zeros_like(acc_scr)
    m_scr[...] = jnp.full(m_scr.shape, NEG_BIG, F32)

    def key_block(kj):
        k0 = pl.multiple_of(kj * TK, TK)
        pcs = pc_ref[pl.ds(k0, TK), :]
        return [jnp.concatenate([k_ref[pl.ds(k0, TK), g * LANES:(g + 1) * LANES], pcs], axis=1)
                for g in range(FOX_G)]

    def scores_head(h, lhs, slot):
        s = jnp.dot(lhs[h // 2], rhs_scr[h], preferred_element_type=F32)
        s_scr[slot][h] = s
        cm_scr[slot][h] = jnp.broadcast_to(jnp.max(s, axis=0, keepdims=True), (SUBLANES, TQ))

    def scores(kj, slot):
        lhs = key_block(kj)
        for h in range(nh):
            scores_head(h, lhs, slot)

    def softmax_pv(h, kj, s, cmax, lo):
        k0 = pl.multiple_of(kj * TK, TK)
        m_old = m_scr[h, 0:1, lo:TQ]
        m_new = jnp.maximum(m_old, cmax)
        alpha = jnp.exp2(m_old - m_new)
        p = jnp.exp2(s - m_new).astype(BF16)
        pv = jnp.dot(vt_scr[h, :, pl.ds(k0, TK)], p, preferred_element_type=F32)
        acc_scr[h, :, lo:TQ] = alpha * acc_scr[h, :, lo:TQ] + pv
        m_scr[h, :, lo:TQ] = jnp.broadcast_to(m_new, (SUBLANES, TQ - lo))

    def overlapped(kj_next, slot_next, kj, slot):
        lhs = key_block(kj_next)
        for h in range(nh):
            scores_head(h, lhs, slot_next)
            softmax_pv(h, kj, s_scr[slot][h], cm_scr[slot][h, 0:1, :], 0)

    def pair(i, carry):
        overlapped(2 * i + 1, 1, 2 * i, 0)
        overlapped(2 * i + 2, 0, 2 * i + 1, 1)
        return carry

    scores(0, 0)
    lax.fori_loop(0, qi, pair, 0)

    half = TQ - TK
    lhs_b = key_block(2 * qi + 1)
    r = lax.broadcasted_iota(jnp.int32, (TK, TQ), 0)
    c = lax.broadcasted_iota(jnp.int32, (TK, TQ), 1)
    rb = lax.broadcasted_iota(jnp.int32, (TK, TK), 0)
    cb = lax.broadcasted_iota(jnp.int32, (TK, TK), 1)
    for h in range(nh):
        sb = jnp.dot(lhs_b[h // 2], rhs_scr[h, :, half:TQ], preferred_element_type=F32)
        s_scr[1][h, :, half:TQ] = jnp.where(cb >= rb, sb, NEG_BIG)
        s = jnp.where(c >= r, s_scr[0][h], NEG_BIG)
        softmax_pv(h, 2 * qi, s, jnp.max(s, axis=0, keepdims=True), 0)

    for g in range(FOX_G):
        parts = []
        for hh in range(2):
            h = 2 * g + hh
            s = s_scr[1][h, :, half:TQ]
            softmax_pv(h, 2 * qi + 1, s, jnp.max(s, axis=0, keepdims=True), half)
            a = acc_scr[h]
            parts.append(a[0:B_DH, :] * (1.0 / a[B_DH:B_DH + 1, :]))
        o = jnp.concatenate(parts, axis=0).T
        zz = _silu(z_ref[:, g * LANES:(g + 1) * LANES].astype(F32))
        out_ref[:, g * LANES:(g + 1) * LANES] = (o * zz).astype(BF16)


def _fox(proj, pieces, bsz, seq):
    m = proj.shape[0]
    nq = seq // FOX_TQ
    nh = 2 * FOX_G
    w = FOX_G * LANES
    ngrp = (B_HEADS * B_DH) // w
    sec = SECTION_W // w
    col0 = SEC_BQ * sec
    assert (SEC_BK, SEC_BV, SEC_BZ) == (SEC_BQ + 1, SEC_BQ + 2, SEC_BQ + 3)
    return pl.pallas_call(
        _fox_kernel,
        grid=(bsz, ngrp, nq),
        in_specs=[
            pl.BlockSpec((FOX_TQ, w), lambda b, hg, qi: (b * nq + qi, col0 + hg)),
            pl.BlockSpec((seq, w), lambda b, hg, qi: (b, col0 + sec + hg)),
            pl.BlockSpec((seq, w), lambda b, hg, qi: (b, col0 + 2 * sec + hg)),
            pl.BlockSpec((FOX_TQ, w), lambda b, hg, qi: (b * nq + qi, col0 + 3 * sec + hg)),
            pl.BlockSpec((None, seq, LANES), lambda b, hg, qi: (b, 0, 0)),
        ],
        out_specs=pl.BlockSpec((FOX_TQ, w), lambda b, hg, qi: (b * nq + qi, hg)),
        out_shape=jax.ShapeDtypeStruct((m, B_HEADS * B_DH), BF16),
        scratch_shapes=[
            pltpu.VMEM((nh, FOX_ACC_ROWS, seq), BF16),
            pltpu.VMEM((nh, FOX_ACC_ROWS, FOX_TQ), F32),
            pltpu.VMEM((nh, 2 * LANES, FOX_TQ), BF16),
            pltpu.VMEM((nh, SUBLANES, FOX_TQ), F32),
            pltpu.VMEM((2, nh, FOX_TK, FOX_TQ), F32),
            pltpu.VMEM((2, nh, SUBLANES, FOX_TQ), F32),
        ],
        compiler_params=pltpu.CompilerParams(
            dimension_semantics=("arbitrary", "arbitrary", "arbitrary"),
            vmem_limit_bytes=VMEM_LIMIT),
        name="fox",
    )(proj, proj, proj, proj, pieces)


def _merge_stages(ha_ref, hb_ref, ga_ref, gb_ref, x_ref, p_ref, wa_ref, wb_ref, wo_ref, wg_ref,
                  wp_ref, png_ref, fng_ref, out_ref):
    ya = jnp.dot(ha_ref[...], wa_ref[...], preferred_element_type=F32)
    yb = jnp.dot(hb_ref[...], wb_ref[...], preferred_element_type=F32)
    yield
    merged = (_sigmoid(ga_ref[...].astype(F32)) * ya + _sigmoid(gb_ref[...].astype(F32)) * yb)
    x1 = x_ref[...] + jnp.dot(merged.astype(BF16), wo_ref[...], preferred_element_type=F32)
    yield
    r = _rms_norm(x1, png_ref[...]).astype(BF16)
    gate = _sigmoid(jnp.dot(r, wg_ref[...], preferred_element_type=F32))
    yield
    pp = jnp.dot(p_ref[...].astype(BF16), wp_ref[...], preferred_element_type=F32)
    x2 = x1 + gate * pp
    out_ref[...] = _rms_norm(x2, fng_ref[...])
    yield


def _mlstm_merge_kernel(nt, n_chunks,
                        qk_ref, v_ref, o_ref, z_ref, gcol_ref, grow_ref, hg_ref,
                        hb_ref, ga_ref, gb_ref, x_ref, p_ref, wa_ref, wb_ref, wo_ref, wg_ref,
                        wp_ref, png_ref, fng_ref, out_ref,
                        ha_scr, c_scr, n_scr, m_scr, fprev_scr):
    s = pl.program_id(0)
    chunk = jnp.minimum(s, n_chunks - 1)
    state = (c_scr, n_scr, m_scr, fprev_scr)

    @pl.when(s == 0)
    def _():
        ha_scr[...] = jnp.zeros_like(ha_scr)

    @pl.when(chunk % nt == 0)
    def _():
        _mlstm_reset(*state)

    stages = _merge_stages(ha_scr, hb_ref, ga_ref, gb_ref, x_ref, p_ref, wa_ref, wb_ref, wo_ref,
                           wg_ref, wp_ref, png_ref, fng_ref, out_ref)
    next(stages)
    _mlstm_chunk(qk_ref, v_ref, o_ref, z_ref, gcol_ref, grow_ref, hg_ref,
                 ha_scr, *state, after_head=lambda: next(stages, None))


def _mlstm_merge(proj, qk_act, gcol, grow, head_g, hb, x2, p2, wa, wb, wo, wg, wp,
                 png, fng, bsz, seq):
    m, d = x2.shape
    pd = p2.shape[1]
    width = SECTION_W
    nt = seq // A_CHUNK
    n_chunks = bsz * nt
    cur = lambda s: jnp.minimum(s, n_chunks - 1)
    prev = lambda s: jnp.maximum(s - 1, 0)
    a_blk = lambda sec: pl.BlockSpec((A_CHUNK, width), lambda s: (cur(s), sec))
    m_blk = lambda w, sec: pl.BlockSpec((A_CHUNK, w), lambda s: (prev(s), sec))
    full = lambda r, c: pl.BlockSpec((r, c), lambda s: (0, 0))
    return pl.pallas_call(
        functools.partial(_mlstm_merge_kernel, nt, n_chunks),
        grid=(n_chunks + 1,),
        in_specs=[
            a_blk(0), a_blk(SEC_AV), a_blk(SEC_AO), a_blk(SEC_AZ),
            pl.BlockSpec((None, A_CHUNK, LANES), lambda s: (cur(s) // nt, cur(s) % nt, 0)),
            pl.BlockSpec((None, SUBLANES, A_CHUNK), lambda s: (cur(s) // nt, 0, cur(s) % nt)),
            full(1, width),
            m_blk(d, 0), m_blk(d, SEC_GA), m_blk(d, SEC_GB), m_blk(d, 0), m_blk(pd, 0),
            full(d, d), full(d, d), full(d, d), full(d, d), full(pd, d),
            full(1, d), full(1, d),
        ],
        out_specs=pl.BlockSpec((A_CHUNK, d), lambda s: (prev(s), 0)),
        out_shape=jax.ShapeDtypeStruct((m, d), F32),
        scratch_shapes=[
            pltpu.VMEM((A_CHUNK, width), BF16),
            pltpu.VMEM((A_HEADS, A_DQK, A_DV), F32),
            pltpu.VMEM((A_HEADS, 1, A_DQK), F32),
            pltpu.VMEM((A_HEADS, SUBLANES, LANES), F32),
            pltpu.VMEM((1, LANES), F32),
        ],
        compiler_params=pltpu.CompilerParams(
            dimension_semantics=("arbitrary",), vmem_limit_bytes=VMEM_LIMIT),
        name="mlstm_merge",
    )(qk_act, proj, proj, proj, gcol, grow, head_g,
      hb, proj, proj, x2, p2, wa, wb, wo, wg, wp, png, fng)


def _split_w_in(w):
    qkw = A_HEADS * A_DQK
    aw = A_HEADS * A_DV
    bw = B_HEADS * B_DH
    d = w.shape[0]
    o_ai = 2 * qkw + aw
    o_ao = o_ai + 2 * A_HEADS
    o_bf = o_ao + 2 * aw + 3 * bw
    o_bz = o_bf + B_HEADS
    seg_cols = (o_ai, o_bf - o_ao, w.shape[1] - o_bz)
    shifts = (0, o_ao - o_ai, o_ao - o_ai + o_bz - o_bf)
    assert all(c % IN_TN == 0 for c in seg_cols) and all(s % SUBLANES == 0 for s in shifts)
    seg_tiles = tuple(c // IN_TN for c in seg_cols)
    w_t = w.T
    w_gate_t = jnp.concatenate(
        [w_t[o_ai:o_ao], w_t[o_bf:o_bz], jnp.zeros((LANES - N_GATE, d), w.dtype)], axis=0)
    return w_t, w_gate_t, seg_tiles, shifts


def _layer(x, p_i, attn_norm_g, w_in, conv_w, conv_b, a_bias_i, a_bias_f, a_head_norm_g, b_bias_f,
           w_branch_a, w_branch_b, w_out, ple_norm_g, w_ple_gate, w_ple_proj, out_norm_g):
    bsz, seq, d = x.shape
    m = bsz * seq
    x2 = x.reshape(m, d)
    w_t, w_gate_t, seg_tiles, shifts = _split_w_in(w_in)
    qkw = A_HEADS * A_DQK
    conv_scale = jnp.concatenate([jnp.ones((1, qkw), F32), jnp.full((1, qkw), A_DQK ** -0.5, F32)],
                                 axis=1)
    proj, gates, qk_act = _in_proj(x2, attn_norm_g.reshape(1, d), w_t, w_gate_t, seg_tiles, shifts,
                                   conv_w, conv_b.reshape(1, -1), conv_scale, seq)

    bias = jnp.concatenate([a_bias_i, a_bias_f, b_bias_f, jnp.zeros((LANES - N_GATE,), F32)])
    gcol, grow, pieces = _gates(gates.reshape(bsz, seq, LANES), bias.reshape(1, LANES))

    hb = _fox(proj, pieces, bsz, seq)
    out = _mlstm_merge(proj, qk_act, gcol, grow,
                       a_head_norm_g.reshape(1, -1), hb, x2, p_i.reshape(m, -1),
                       w_branch_a.astype(BF16), w_branch_b.astype(BF16), w_out.astype(BF16),
                       w_ple_gate.astype(BF16), w_ple_proj.astype(BF16),
                       ple_norm_g.reshape(1, d), out_norm_g.reshape(1, d), bsz, seq)
    return out.reshape(bsz, seq, d)


def kernel(x, p, attn_norm_g, w_in, conv_w, conv_b, a_bias_i, a_bias_f, a_head_norm_g, b_bias_f,
           w_branch_a, w_branch_b, w_out, ple_norm_g, w_ple_gate, w_ple_proj, final_norm_g):
    depth = w_in.shape[0]
    assert depth == 1, "the final norm is fused into the single layer's merge kernel"
    return _layer(x, p[0], attn_norm_g[0], w_in[0], conv_w[0], conv_b[0], a_bias_i[0], a_bias_f[0],
                  a_head_norm_g[0], b_bias_f[0], w_branch_a[0], w_branch_b[0], w_out[0],
                  ple_norm_g[0], w_ple_gate[0], w_ple_proj[0], final_norm_g)
```

```python
import functools
import math

import jax
import jax.numpy as jnp
from jax import lax
from jax.experimental import pallas as pl
from jax.experimental.pallas import tpu as pltpu

F32 = jnp.float32
BF16 = jnp.bfloat16

EPS = 1e-6
A_HEADS = 4
A_DQK = 128
A_DV = 256
CONV_K = 4
B_HEADS = 16
B_DH = 64
LANES = 128
SUBLANES = 8
BF16_ROWS = 16
NEG_BIG = -1e30
LOG2E = math.log2(math.e)

IN_TM = 2048
IN_TN = 1024
IN_CHUNKS = 8
CUM_BLK = 256
A_CHUNK = 256
FOX_TK = 256
FOX_TQ = 2 * FOX_TK
FOX_G = 4
MERGE_TM = 512
VMEM_LIMIT = 56 * 1024 * 1024

SECTION_W = A_HEADS * A_DV
SEC_QK, SEC_AV, SEC_AO, SEC_AZ, SEC_BQ, SEC_BK, SEC_BV, SEC_BZ, SEC_GA, SEC_GB = range(10)

N_GATE = 2 * A_HEADS + B_HEADS
B_LANE0 = 2 * A_HEADS
PIECE_OFFS = (B_LANE0, B_LANE0 + B_HEADS, B_LANE0 + 2 * B_HEADS)


def _sigmoid(x):
    return 1.0 / (1.0 + jnp.exp2(x * (-LOG2E)))


def _silu(x):
    return x * _sigmoid(x)


def _rms_norm(x, g):
    ms = jnp.mean(x * x, axis=-1, keepdims=True)
    return (x * lax.rsqrt(ms + EPS)) * g


def _split3(x):
    x1 = x.astype(BF16)
    r1 = x - x1.astype(F32)
    x2 = r1.astype(BF16)
    x3 = (r1 - x2.astype(F32)).astype(BF16)
    return x1, x2, x3


_NT = (((1,), (1,)), ((), ()))


def _in_proj_kernel(x_ref, g_ref, wt_ref, wgt_ref, cw_ref, cb_ref, cs_ref,
                    proj_ref, gates_ref, qk_ref, h_scr, qkraw_scr, xpad_scr):
    i = pl.program_id(0)
    j = pl.program_id(1)
    tm = x_ref.shape[0]

    @pl.when((i == 0) & (j == 0))
    def _():
        qkraw_scr[...] = jnp.zeros_like(qkraw_scr)
        xpad_scr[0:SUBLANES, :] = jnp.zeros((SUBLANES, LANES), F32)

    @pl.when(j == 0)
    def _():
        h = _rms_norm(x_ref[...], g_ref[...]).astype(BF16)
        h_scr[...] = h
        gates_ref[...] = lax.dot_general(h, wgt_ref[...].astype(BF16), _NT,
                                         preferred_element_type=F32)

    n_strips = qkraw_scr.shape[1] // LANES
    lane0 = pl.multiple_of(jnp.clip(j - 1, 0, n_strips - 1) * LANES, LANES)
    xpad_scr[SUBLANES:SUBLANES + tm, :] = qkraw_scr[:, pl.ds(lane0, LANES)].astype(F32)

    w = wt_ref[...].astype(BF16)
    ch = tm // IN_CHUNKS
    sec = j // (SECTION_W // IN_TN)
    is_sigmoid = (sec == SEC_AO) | (sec == SEC_GA) | (sec == SEC_GB)
    is_silu = (sec == SEC_AZ) | (sec == SEC_BZ)
    for c in range(IN_CHUNKS):
        r0 = c * ch
        acc = lax.dot_general(h_scr[r0:r0 + ch, :], w, _NT, preferred_element_type=F32)
        sg = _sigmoid(acc)
        act = jnp.where(is_silu, acc * sg, jnp.where(is_sigmoid, sg, acc))
        proj_ref[r0:r0 + ch, :] = act.astype(BF16)
        y = cb_ref[...]
        for d in range(CONV_K):
            y = y + (xpad_scr[SUBLANES + r0 - d:SUBLANES + r0 - d + ch, :]
                     * cw_ref[CONV_K - 1 - d:CONV_K - d, :])
        qk_ref[r0:r0 + ch, :] = (_silu(y) * cs_ref[...]).astype(BF16)

    @pl.when(j == 0)
    def _():
        qkraw_scr[...] = proj_ref[...]


def _in_proj(x2, g, w_t, w_gate_t, seg_tiles, shifts, conv_w, conv_b, conv_scale, seq):
    m, d = x2.shape
    n = sum(seg_tiles) * IN_TN
    n_tiles = n // IN_TN
    n_strips = SECTION_W // LANES
    assert IN_TM == seq and IN_TN == SECTION_W and SEC_QK == 0 and n_tiles > n_strips
    strip = lambda i, j: (0, jnp.clip(j - 1, 0, n_strips - 1))

    def w_rows(i, j):
        shift = shifts[0] // SUBLANES
        lo = 0
        for n_tiles, s in zip(seg_tiles[:-1], shifts[1:]):
            lo += n_tiles
            shift = jnp.where(j >= lo, s // SUBLANES, shift)
        return ((j * (IN_TN // SUBLANES) + shift) * SUBLANES, 0)

    return pl.pallas_call(
        _in_proj_kernel,
        grid=(m // IN_TM, n // IN_TN),
        in_specs=[
            pl.BlockSpec((IN_TM, d), lambda i, j: (i, 0)),
            pl.BlockSpec((1, d), lambda i, j: (0, 0)),
            pl.BlockSpec((pl.Element(IN_TN), pl.Element(d)), w_rows),
            pl.BlockSpec((LANES, d), lambda i, j: (0, 0)),
            pl.BlockSpec((CONV_K, LANES), strip),
            pl.BlockSpec((1, LANES), strip),
            pl.BlockSpec((1, LANES), strip),
        ],
        out_specs=[
            pl.BlockSpec((IN_TM, IN_TN), lambda i, j: (i, j)),
            pl.BlockSpec((IN_TM, LANES), lambda i, j: (i, 0)),
            pl.BlockSpec((IN_TM, LANES), lambda i, j: (i, strip(i, j)[1])),
        ],
        out_shape=[
            jax.ShapeDtypeStruct((m, n), BF16),
            jax.ShapeDtypeStruct((m, LANES), F32),
            jax.ShapeDtypeStruct((m, SECTION_W), BF16),
        ],
        scratch_shapes=[pltpu.VMEM((IN_TM, d), BF16),
                        pltpu.VMEM((IN_TM, SECTION_W), BF16),
                        pltpu.VMEM((IN_TM + SUBLANES, LANES), F32)],
        compiler_params=pltpu.CompilerParams(
            dimension_semantics=("arbitrary", "arbitrary"),
            vmem_limit_bytes=VMEM_LIMIT),
        name="in_proj",
    )(x2, g, w_t, w_gate_t, conv_w, conv_b, conv_scale)


def _gates_kernel(g_ref, bias_ref, col_ref, row_ref, pc_ref):
    x = g_ref[...] + bias_ref[...]
    s = x.shape[0]
    ls = jnp.minimum(x, 0.0) - jnp.log1p(jnp.exp(-jnp.abs(x)))
    r = lax.broadcasted_iota(jnp.int32, (CUM_BLK, CUM_BLK), 0)
    c = lax.broadcasted_iota(jnp.int32, (CUM_BLK, CUM_BLK), 1)
    tri = jnp.where(r >= c, 1.0, 0.0).astype(BF16)
    carry = jnp.zeros((1, LANES), F32)
    blocks = []
    for blk in range(s // CUM_BLK):
        x1, x2, x3 = _split3(ls[blk * CUM_BLK:(blk + 1) * CUM_BLK])
        cs = (jnp.dot(tri, x3, preferred_element_type=F32)
              + jnp.dot(tri, x2, preferred_element_type=F32)
              + jnp.dot(tri, x1, preferred_element_type=F32)) + carry
        carry = cs[CUM_BLK - 1:CUM_BLK, :]
        blocks.append(cs)
    cum = jnp.concatenate(blocks, axis=0)
    lane = lax.broadcasted_iota(jnp.int32, x.shape, 1)
    res = jnp.where(lane < A_HEADS, x, cum)
    res = res * LOG2E
    col_ref[...] = res
    row_ref[...] = res.T[0:SUBLANES, :]

    in_b = (lane >= B_LANE0) & (lane < B_LANE0 + B_HEADS)
    p1, p2, p3 = _split3(jnp.where(in_b, cum * (-LOG2E), 0.0))
    pieces = (p1.astype(F32)
              + pltpu.roll(p2.astype(F32), PIECE_OFFS[1] - B_LANE0, axis=1)
              + pltpu.roll(p3.astype(F32), PIECE_OFFS[2] - B_LANE0, axis=1))
    pc_ref[...] = pieces.astype(BF16)


def _gates(gates3, bias):
    b, s, _ = gates3.shape
    return pl.pallas_call(
        _gates_kernel,
        grid=(b,),
        in_specs=[
            pl.BlockSpec((None, s, LANES), lambda i: (i, 0, 0)),
            pl.BlockSpec((1, LANES), lambda i: (0, 0)),
        ],
        out_specs=[
            pl.BlockSpec((None, s, LANES), lambda i: (i, 0, 0)),
            pl.BlockSpec((None, SUBLANES, s), lambda i: (i, 0, 0)),
            pl.BlockSpec((None, s, LANES), lambda i: (i, 0, 0)),
        ],
        out_shape=[
            jax.ShapeDtypeStruct((b, s, LANES), F32),
            jax.ShapeDtypeStruct((b, SUBLANES, s), F32),
            jax.ShapeDtypeStruct((b, s, LANES), BF16),
        ],
        compiler_params=pltpu.CompilerParams(
            dimension_semantics=("arbitrary",), vmem_limit_bytes=VMEM_LIMIT),
        name="gates",
    )(gates3, bias)


def _mlstm_reset(c_scr, n_scr, m_scr, fprev_scr):
    c_scr[...] = jnp.zeros_like(c_scr)
    n_scr[...] = jnp.zeros_like(n_scr)
    m_scr[...] = jnp.zeros_like(m_scr)
    fprev_scr[...] = jnp.zeros_like(fprev_scr)


def _mlstm_chunk(qk_ref, v_ref, o_ref, z_ref, gcol_ref, grow_ref, hg_ref,
                 out_ref, c_scr, n_scr, m_scr, fprev_scr, after_head=lambda: None):
    L = A_CHUNK
    qkw = A_HEADS * A_DQK

    row = lax.broadcasted_iota(jnp.int32, (L, L), 0)
    col = lax.broadcasted_iota(jnp.int32, (L, L), 1)
    causal = row >= col

    for h in range(A_HEADS):
        qb = qk_ref[:, h * A_DQK:(h + 1) * A_DQK]
        kb = qk_ref[:, qkw + h * A_DQK:qkw + (h + 1) * A_DQK]
        q = qb.astype(F32)
        k = kb.astype(F32)
        v = v_ref[:, h * A_DV:(h + 1) * A_DV]

        li_c = gcol_ref[:, h:h + 1]
        f_c = gcol_ref[:, A_HEADS + h:A_HEADS + h + 1]
        li_r = grow_ref[h:h + 1, :]
        f_r = grow_ref[A_HEADS + h:A_HEADS + h + 1, :]
        f_prev = fprev_scr[0:1, A_HEADS + h:A_HEADS + h + 1]
        f_end = gcol_ref[L - 1:L, A_HEADS + h:A_HEADS + h + 1]
        m_st = m_scr[h, 0:1, 0:1]
        c_st = c_scr[h]
        n_st = n_scr[h]

        dmat = jnp.where(causal, (f_c - f_r) + li_r, NEG_BIG)
        inter = (f_c - f_prev) + m_st
        m_row = jnp.maximum(inter, jnp.max(dmat, axis=-1, keepdims=True))
        w_intra = jnp.exp2(dmat - m_row)
        w_inter = jnp.exp2(inter - m_row)
        s = lax.dot_general(qb, kb, (((1,), (1,)), ((), ())), preferred_element_type=F32)
        scores = s * w_intra
        num = (jnp.dot(scores.astype(BF16), v, preferred_element_type=F32)
               + w_inter * jnp.dot(qb, c_st.astype(BF16), preferred_element_type=F32))
        den = (jnp.sum(scores, axis=-1, keepdims=True)
               + w_inter * jnp.sum(q * n_st, axis=-1, keepdims=True))
        hh = num * (1.0 / jnp.maximum(jnp.abs(den), jnp.exp2(-m_row)))

        g_tot = f_end - f_prev
        to_end = (f_end - f_c) + li_c
        m_new = jnp.maximum(g_tot + m_st, jnp.max(to_end, axis=0, keepdims=True))
        w_k = jnp.exp2(to_end - m_new)
        decay = jnp.exp2(g_tot + m_st - m_new)
        kw = k * w_k
        c_scr[h] = decay * c_st + jnp.dot(kw.T.astype(BF16), v, preferred_element_type=F32)
        n_scr[h] = decay * n_st + jnp.sum(kw, axis=0, keepdims=True)
        m_scr[h] = jnp.broadcast_to(m_new, (SUBLANES, LANES))

        hn = hh * lax.rsqrt(jnp.mean(hh * hh, axis=-1, keepdims=True) + EPS)
        hn = hn * hg_ref[:, h * A_DV:(h + 1) * A_DV]
        og = o_ref[:, h * A_DV:(h + 1) * A_DV].astype(F32)
        zz = z_ref[:, h * A_DV:(h + 1) * A_DV].astype(F32)
        out_ref[:, h * A_DV:(h + 1) * A_DV] = ((og * hn) * zz).astype(BF16)
        after_head()

    fprev_scr[...] = gcol_ref[L - 1:L, :]


FOX_ACC_ROWS = B_DH + BF16_ROWS


def _fox_kernel(q_ref, k_ref, v_ref, z_ref, pc_ref, out_ref, vt_scr, acc_scr, rhs_scr, m_scr,
                s2_scr, cm2_scr):
    s_scr = (s2_scr.at[0], s2_scr.at[1])
    cm_scr = (cm2_scr.at[0], cm2_scr.at[1])
    hg = pl.program_id(1)
    qi = pl.program_id(2)
    TQ, TK = FOX_TQ, FOX_TK
    seq = k_ref.shape[0]
    nh = 2 * FOX_G

    @pl.when(qi == 0)
    def _():
        for g in range(FOX_G):
            vt = v_ref[:, g * LANES:(g + 1) * LANES].astype(F32).T
            for hh in range(2):
                vt_scr[2 * g + hh, 0:B_DH, :] = vt[hh * B_DH:(hh + 1) * B_DH, :].astype(BF16)
                vt_scr[2 * g + hh, B_DH:FOX_ACC_ROWS, :] = jnp.ones((BF16_ROWS, seq), BF16)

    row = lax.broadcasted_iota(jnp.int32, (LANES, TQ), 0)
    for g in range(FOX_G):
        qt = (q_ref[:, g * LANES:(g + 1) * LANES].astype(F32) * (B_DH ** -0.5 * LOG2E)).T
        for hh in range(2):
            h = 2 * g + hh
            head = hg * nh + h
            qm = jnp.where((row >= hh * B_DH) & (row < (hh + 1) * B_DH), qt, 0.0)
            sel = jnp.where((row == PIECE_OFFS[0] + head) | (row == PIECE_OFFS[1] + head)
                            | (row == PIECE_OFFS[2] + head), 1.0, 0.0)
            rhs_scr[h, 0:LANES, :] = qm.astype(BF16)
            rhs_scr[h, LANES:2 * LANES, :] = sel.astype(BF16)

    acc_scr[...] = jnp.zeros_like(acc_scr)
    m_scr[...] = jnp.full(m_scr.shape, NEG_BIG, F32)

    def key_block(kj):
        k0 = pl.multiple_of(kj * TK, TK)
        pcs = pc_ref[pl.ds(k0, TK), :]
        return [jnp.concatenate([k_ref[pl.ds(k0, TK), g * LANES:(g + 1) * LANES], pcs], axis=1)
                for g in range(FOX_G)]

    def scores_head(h, lhs, slot):
        s = jnp.dot(lhs[h // 2], rhs_scr[h], preferred_element_type=F32)
        s_scr[slot][h] = s
        cm_scr[slot][h] = jnp.broadcast_to(jnp.max(s, axis=0, keepdims=True), (SUBLANES, TQ))

    def scores(kj, slot):
        lhs = key_block(kj)
        for h in range(nh):
            scores_head(h, lhs, slot)

    def softmax_pv(h, kj, s, cmax, lo):
        k0 = pl.multiple_of(kj * TK, TK)
        m_old = m_scr[h, 0:1, lo:TQ]
        m_new = jnp.maximum(m_old, cmax)
        alpha = jnp.exp2(m_old - m_new)
        p = jnp.exp2(s - m_new).astype(BF16)
        pv = jnp.dot(vt_scr[h, :, pl.ds(k0, TK)], p, preferred_element_type=F32)
        acc_scr[h, :, lo:TQ] = alpha * acc_scr[h, :, lo:TQ] + pv
        m_scr[h, :, lo:TQ] = jnp.broadcast_to(m_new, (SUBLANES, TQ - lo))

    def overlapped(kj_next, slot_next, kj, slot):
        lhs = key_block(kj_next)
        for h in range(nh):
            scores_head(h, lhs, slot_next)
            softmax_pv(h, kj, s_scr[slot][h], cm_scr[slot][h, 0:1, :], 0)

    def pair(i, carry):
        overlapped(2 * i + 1, 1, 2 * i, 0)
        overlapped(2 * i + 2, 0, 2 * i + 1, 1)
        return carry

    scores(0, 0)
    lax.fori_loop(0, qi, pair, 0)

    half = TQ - TK
    lhs_b = key_block(2 * qi + 1)
    r = lax.broadcasted_iota(jnp.int32, (TK, TQ), 0)
    c = lax.broadcasted_iota(jnp.int32, (TK, TQ), 1)
    rb = lax.broadcasted_iota(jnp.int32, (TK, TK), 0)
    cb = lax.broadcasted_iota(jnp.int32, (TK, TK), 1)
    for h in range(nh):
        sb = jnp.dot(lhs_b[h // 2], rhs_scr[h, :, half:TQ], preferred_element_type=F32)
        s_scr[1][h, :, half:TQ] = jnp.where(cb >= rb, sb, NEG_BIG)
        s = jnp.where(c >= r, s_scr[0][h], NEG_BIG)
        softmax_pv(h, 2 * qi, s, jnp.max(s, axis=0, keepdims=True), 0)

    for g in range(FOX_G):
        parts = []
        for hh in range(2):
            h = 2 * g + hh
            s = s_scr[1][h, :, half:TQ]
            softmax_pv(h, 2 * qi + 1, s, jnp.max(s, axis=0, keepdims=True), half)
            a = acc_scr[h]
            parts.append(a[0:B_DH, :] * (1.0 / a[B_DH:B_DH + 1, :]))
        o = jnp.concatenate(parts, axis=0).T
        zz = z_ref[:, g * LANES:(g + 1) * LANES].astype(F32)
        out_ref[:, g * LANES:(g + 1) * LANES] = (o * zz).astype(BF16)


def _fox(proj, pieces, bsz, seq):
    m = proj.shape[0]
    nq = seq // FOX_TQ
    nh = 2 * FOX_G
    w = FOX_G * LANES
    ngrp = (B_HEADS * B_DH) // w
    sec = SECTION_W // w
    col0 = SEC_BQ * sec
    assert (SEC_BK, SEC_BV, SEC_BZ) == (SEC_BQ + 1, SEC_BQ + 2, SEC_BQ + 3)
    return pl.pallas_call(
        _fox_kernel,
        grid=(bsz, ngrp, nq),
        in_specs=[
            pl.BlockSpec((FOX_TQ, w), lambda b, hg, qi: (b * nq + qi, col0 + hg)),
            pl.BlockSpec((seq, w), lambda b, hg, qi: (b, col0 + sec + hg)),
            pl.BlockSpec((seq, w), lambda b, hg, qi: (b, col0 + 2 * sec + hg)),
            pl.BlockSpec((FOX_TQ, w), lambda b, hg, qi: (b * nq + qi, col0 + 3 * sec + hg)),
            pl.BlockSpec((None, seq, LANES), lambda b, hg, qi: (b, 0, 0)),
        ],
        out_specs=pl.BlockSpec((FOX_TQ, w), lambda b, hg, qi: (b * nq + qi, hg)),
        out_shape=jax.ShapeDtypeStruct((m, B_HEADS * B_DH), BF16),
        scratch_shapes=[
            pltpu.VMEM((nh, FOX_ACC_ROWS, seq), BF16),
            pltpu.VMEM((nh, FOX_ACC_ROWS, FOX_TQ), F32),
            pltpu.VMEM((nh, 2 * LANES, FOX_TQ), BF16),
            pltpu.VMEM((nh, SUBLANES, FOX_TQ), F32),
            pltpu.VMEM((2, nh, FOX_TK, FOX_TQ), F32),
            pltpu.VMEM((2, nh, SUBLANES, FOX_TQ), F32),
        ],
        compiler_params=pltpu.CompilerParams(
            dimension_semantics=("arbitrary", "arbitrary", "arbitrary"),
            vmem_limit_bytes=VMEM_LIMIT),
        name="fox",
    )(proj, proj, proj, proj, pieces)


def _merge_stages(ha_ref, hb_ref, ga_ref, gb_ref, x_ref, p_ref, wa_ref, wb_ref, wo_ref, wg_ref,
                  wp_ref, png_ref, fng_ref, out_ref):
    ya = jnp.dot(ha_ref[...], wa_ref[...], preferred_element_type=F32)
    yb = jnp.dot(hb_ref[...], wb_ref[...], preferred_element_type=F32)
    yield
    merged = ga_ref[...].astype(F32) * ya + gb_ref[...].astype(F32) * yb
    x1 = x_ref[...] + jnp.dot(merged.astype(BF16), wo_ref[...], preferred_element_type=F32)
    yield
    r = _rms_norm(x1, png_ref[...]).astype(BF16)
    gate = _sigmoid(jnp.dot(r, wg_ref[...], preferred_element_type=F32))
    yield
    pp = jnp.dot(p_ref[...].astype(BF16), wp_ref[...], preferred_element_type=F32)
    x2 = x1 + gate * pp
    out_ref[...] = _rms_norm(x2, fng_ref[...])
    yield


def _mlstm_merge_kernel(nt, n_chunks,
                        qk_ref, v_ref, o_ref, z_ref, gcol_ref, grow_ref, hg_ref,
                        hb_ref, ga_ref, gb_ref, x_ref, p_ref, wa_ref, wb_ref, wo_ref, wg_ref,
                        wp_ref, png_ref, fng_ref, out_ref,
                        ha_scr, c_scr, n_scr, m_scr, fprev_scr):
    s = pl.program_id(0)
    chunk = jnp.minimum(s, n_chunks - 1)
    state = (c_scr, n_scr, m_scr, fprev_scr)

    @pl.when(s == 0)
    def _():
        ha_scr[...] = jnp.zeros_like(ha_scr)

    @pl.when(chunk % nt == 0)
    def _():
        _mlstm_reset(*state)

    stages = _merge_stages(ha_scr, hb_ref, ga_ref, gb_ref, x_ref, p_ref, wa_ref, wb_ref, wo_ref,
                           wg_ref, wp_ref, png_ref, fng_ref, out_ref)
    next(stages)
    _mlstm_chunk(qk_ref, v_ref, o_ref, z_ref, gcol_ref, grow_ref, hg_ref,
                 ha_scr, *state, after_head=lambda: next(stages, None))


def _mlstm_merge(proj, qk_act, gcol, grow, head_g, hb, x2, p2, wa, wb, wo, wg, wp,
                 png, fng, bsz, seq):
    m, d = x2.shape
    pd = p2.shape[1]
    width = SECTION_W
    nt = seq // A_CHUNK
    n_chunks = bsz * nt
    cur = lambda s: jnp.minimum(s, n_chunks - 1)
    prev = lambda s: jnp.maximum(s - 1, 0)
    a_blk = lambda sec: pl.BlockSpec((A_CHUNK, width), lambda s: (cur(s), sec))
    m_blk = lambda w, sec: pl.BlockSpec((A_CHUNK, w), lambda s: (prev(s), sec))
    full = lambda r, c: pl.BlockSpec((r, c), lambda s: (0, 0))
    return pl.pallas_call(
        functools.partial(_mlstm_merge_kernel, nt, n_chunks),
        grid=(n_chunks + 1,),
        in_specs=[
            a_blk(0), a_blk(SEC_AV), a_blk(SEC_AO), a_blk(SEC_AZ),
            pl.BlockSpec((None, A_CHUNK, LANES), lambda s: (cur(s) // nt, cur(s) % nt, 0)),
            pl.BlockSpec((None, SUBLANES, A_CHUNK), lambda s: (cur(s) // nt, 0, cur(s) % nt)),
            full(1, width),
            m_blk(d, 0), m_blk(d, SEC_GA), m_blk(d, SEC_GB), m_blk(d, 0), m_blk(pd, 0),
            full(d, d), full(d, d), full(d, d), full(d, d), full(pd, d),
            full(1, d), full(1, d),
        ],
        out_specs=pl.BlockSpec((A_CHUNK, d), lambda s: (prev(s), 0)),
        out_shape=jax.ShapeDtypeStruct((m, d), F32),
        scratch_shapes=[
            pltpu.VMEM((A_CHUNK, width), BF16),
            pltpu.VMEM((A_HEADS, A_DQK, A_DV), F32),
            pltpu.VMEM((A_HEADS, 1, A_DQK), F32),
            pltpu.VMEM((A_HEADS, SUBLANES, LANES), F32),
            pltpu.VMEM((1, LANES), F32),
        ],
        compiler_params=pltpu.CompilerParams(
            dimension_semantics=("arbitrary",), vmem_limit_bytes=VMEM_LIMIT),
        name="mlstm_merge",
    )(qk_act, proj, proj, proj, gcol, grow, head_g,
      hb, proj, proj, x2, p2, wa, wb, wo, wg, wp, png, fng)


def _split_w_in(w):
    qkw = A_HEADS * A_DQK
    aw = A_HEADS * A_DV
    bw = B_HEADS * B_DH
    d = w.shape[0]
    o_ai = 2 * qkw + aw
    o_ao = o_ai + 2 * A_HEADS
    o_bf = o_ao + 2 * aw + 3 * bw
    o_bz = o_bf + B_HEADS
    seg_cols = (o_ai, o_bf - o_ao, w.shape[1] - o_bz)
    shifts = (0, o_ao - o_ai, o_ao - o_ai + o_bz - o_bf)
    assert all(c % IN_TN == 0 for c in seg_cols) and all(s % SUBLANES == 0 for s in shifts)
    seg_tiles = tuple(c // IN_TN for c in seg_cols)
    w_t = w.T
    w_gate_t = jnp.concatenate(
        [w_t[o_ai:o_ao], w_t[o_bf:o_bz], jnp.zeros((LANES - N_GATE, d), w.dtype)], axis=0)
    return w_t, w_gate_t, seg_tiles, shifts


def _layer(x, p_i, attn_norm_g, w_in, conv_w, conv_b, a_bias_i, a_bias_f, a_head_norm_g, b_bias_f,
           w_branch_a, w_branch_b, w_out, ple_norm_g, w_ple_gate, w_ple_proj, out_norm_g):
    bsz, seq, d = x.shape
    m = bsz * seq
    x2 = x.reshape(m, d)
    w_t, w_gate_t, seg_tiles, shifts = _split_w_in(w_in)
    qkw = A_HEADS * A_DQK
    conv_scale = jnp.concatenate([jnp.ones((1, qkw), F32), jnp.full((1, qkw), A_DQK ** -0.5, F32)],
                                 axis=1)
    proj, gates, qk_act = _in_proj(x2, attn_norm_g.reshape(1, d), w_t, w_gate_t, seg_tiles, shifts,
                                   conv_w, conv_b.reshape(1, -1), conv_scale, seq)

    bias = jnp.concatenate([a_bias_i, a_bias_f, b_bias_f, jnp.zeros((LANES - N_GATE,), F32)])
    gcol, grow, pieces = _gates(gates.reshape(bsz, seq, LANES), bias.reshape(1, LANES))

    hb = _fox(proj, pieces, bsz, seq)
    out = _mlstm_merge(proj, qk_act, gcol, grow,
                       a_head_norm_g.reshape(1, -1), hb, x2, p_i.reshape(m, -1),
                       w_branch_a.astype(BF16), w_branch_b.astype(BF16), w_out.astype(BF16),
                       w_ple_gate.astype(BF16), w_ple_proj.astype(BF16),
                       ple_norm_g.reshape(1, d), out_norm_g.reshape(1, d), bsz, seq)
    return out.reshape(bsz, seq, d)


def kernel(x, p, attn_norm_g, w_in, conv_w, conv_b, a_bias_i, a_bias_f, a_head_norm_g, b_bias_f,
           w_branch_a, w_branch_b, w_out, ple_norm_g, w_ple_gate, w_ple_proj, final_norm_g):
    depth = w_in.shape[0]
    assert depth == 1, "the final norm is fused into the single layer's merge kernel"
    return _layer(x, p[0], attn_norm_g[0], w_in[0], conv_w[0], conv_b[0], a_bias_i[0], a_bias_f[0],
                  a_head_norm_g[0], b_bias_f[0], w_branch_a[0], w_branch_b[0], w_out[0],
                  ple_norm_g[0], w_ple_gate[0], w_ple_proj[0], final_norm_g)
```

```python
import functools
import math

import jax
import jax.numpy as jnp
from jax import lax
from jax.experimental import pallas as pl
from jax.experimental.pallas import tpu as pltpu

F32 = jnp.float32
BF16 = jnp.bfloat16

EPS = 1e-6
A_HEADS = 4
A_DQK = 128
A_DV = 256
CONV_K = 4
B_HEADS = 16
B_DH = 64
LANES = 128
SUBLANES = 8
BF16_ROWS = 16
NEG_BIG = -1e30
LOG2E = math.log2(math.e)

IN_TM = 2048
IN_TN = 1024
IN_CHUNKS = 8
CUM_BLK = 256
A_CHUNK = 256
FOX_TK = 256
FOX_TQ = 2 * FOX_TK
FOX_G = 4
MERGE_LAG = 1
VMEM_LIMIT = 56 * 1024 * 1024

SECTION_W = A_HEADS * A_DV
SEC_QK, SEC_AV, SEC_AO, SEC_AZ, SEC_BQ, SEC_BK, SEC_BV, SEC_BZ, SEC_GA, SEC_GB = range(10)

N_GATE = 2 * A_HEADS + B_HEADS
B_LANE0 = 2 * A_HEADS
PIECE_OFFS = (B_LANE0, B_LANE0 + B_HEADS, B_LANE0 + 2 * B_HEADS)


def _sigmoid(x):
    return 1.0 / (1.0 + jnp.exp2(x * (-LOG2E)))


def _silu(x):
    return x * _sigmoid(x)


def _rms_norm(x, g):
    ms = jnp.mean(x * x, axis=-1, keepdims=True)
    return (x * lax.rsqrt(ms + EPS)) * g


def _split3(x):
    x1 = x.astype(BF16)
    r1 = x - x1.astype(F32)
    x2 = r1.astype(BF16)
    x3 = (r1 - x2.astype(F32)).astype(BF16)
    return x1, x2, x3


_NT = (((1,), (1,)), ((), ()))


def _in_proj_kernel(x_ref, g_ref, wt_ref, wgt_ref, cw_ref, cb_ref, cs_ref,
                    proj_ref, gates_ref, qk_ref, h_scr, qkraw_scr, xpad_scr):
    i = pl.program_id(0)
    j = pl.program_id(1)
    tm = x_ref.shape[0]

    @pl.when((i == 0) & (j == 0))
    def _():
        qkraw_scr[...] = jnp.zeros_like(qkraw_scr)
        xpad_scr[0:SUBLANES, :] = jnp.zeros((SUBLANES, LANES), F32)

    @pl.when(j == 0)
    def _():
        h = _rms_norm(x_ref[...], g_ref[...]).astype(BF16)
        h_scr[...] = h
        gates_ref[...] = lax.dot_general(h, wgt_ref[...].astype(BF16), _NT,
                                         preferred_element_type=F32)

    n_strips = qkraw_scr.shape[1] // LANES
    lane0 = pl.multiple_of(jnp.clip(j - 1, 0, n_strips - 1) * LANES, LANES)
    xpad_scr[SUBLANES:SUBLANES + tm, :] = qkraw_scr[:, pl.ds(lane0, LANES)].astype(F32)

    w = wt_ref[...].astype(BF16)
    ch = tm // IN_CHUNKS
    for c in range(IN_CHUNKS):
        r0 = c * ch
        proj_ref[r0:r0 + ch, :] = lax.dot_general(
            h_scr[r0:r0 + ch, :], w, _NT, preferred_element_type=F32).astype(BF16)
        y = cb_ref[...]
        for d in range(CONV_K):
            y = y + (xpad_scr[SUBLANES + r0 - d:SUBLANES + r0 - d + ch, :]
                     * cw_ref[CONV_K - 1 - d:CONV_K - d, :])
        qk_ref[r0:r0 + ch, :] = (_silu(y) * cs_ref[...]).astype(BF16)

    @pl.when(j == 0)
    def _():
        qkraw_scr[...] = proj_ref[...]


def _in_proj(x2, g, w_t, w_gate_t, seg_tiles, shifts, conv_w, conv_b, conv_scale, seq):
    m, d = x2.shape
    n = sum(seg_tiles) * IN_TN
    n_tiles = n // IN_TN
    n_strips = SECTION_W // LANES
    assert IN_TM == seq and IN_TN == SECTION_W and SEC_QK == 0 and n_tiles > n_strips
    strip = lambda i, j: (0, jnp.clip(j - 1, 0, n_strips - 1))

    def w_rows(i, j):
        shift = shifts[0] // SUBLANES
        lo = 0
        for n_tiles, s in zip(seg_tiles[:-1], shifts[1:]):
            lo += n_tiles
            shift = jnp.where(j >= lo, s // SUBLANES, shift)
        return ((j * (IN_TN // SUBLANES) + shift) * SUBLANES, 0)

    return pl.pallas_call(
        _in_proj_kernel,
        grid=(m // IN_TM, n // IN_TN),
        in_specs=[
            pl.BlockSpec((IN_TM, d), lambda i, j: (i, 0)),
            pl.BlockSpec((1, d), lambda i, j: (0, 0)),
            pl.BlockSpec((pl.Element(IN_TN), pl.Element(d)), w_rows),
            pl.BlockSpec((LANES, d), lambda i, j: (0, 0)),
            pl.BlockSpec((CONV_K, LANES), strip),
            pl.BlockSpec((1, LANES), strip),
            pl.BlockSpec((1, LANES), strip),
        ],
        out_specs=[
            pl.BlockSpec((IN_TM, IN_TN), lambda i, j: (i, j)),
            pl.BlockSpec((IN_TM, LANES), lambda i, j: (i, 0)),
            pl.BlockSpec((IN_TM, LANES), lambda i, j: (i, strip(i, j)[1])),
        ],
        out_shape=[
            jax.ShapeDtypeStruct((m, n), BF16),
            jax.ShapeDtypeStruct((m, LANES), F32),
            jax.ShapeDtypeStruct((m, SECTION_W), BF16),
        ],
        scratch_shapes=[pltpu.VMEM((IN_TM, d), BF16),
                        pltpu.VMEM((IN_TM, SECTION_W), BF16),
                        pltpu.VMEM((IN_TM + SUBLANES, LANES), F32)],
        compiler_params=pltpu.CompilerParams(
            dimension_semantics=("arbitrary", "arbitrary"),
            vmem_limit_bytes=VMEM_LIMIT),
        name="in_proj",
    )(x2, g, w_t, w_gate_t, conv_w, conv_b, conv_scale)


def _gates_kernel(g_ref, bias_ref, col_ref, row_ref, pc_ref):
    x = g_ref[...] + bias_ref[...]
    s = x.shape[0]
    ls = jnp.minimum(x, 0.0) - jnp.log1p(jnp.exp(-jnp.abs(x)))
    r = lax.broadcasted_iota(jnp.int32, (CUM_BLK, CUM_BLK), 0)
    c = lax.broadcasted_iota(jnp.int32, (CUM_BLK, CUM_BLK), 1)
    tri = jnp.where(r >= c, 1.0, 0.0).astype(BF16)
    carry = jnp.zeros((1, LANES), F32)
    blocks = []
    for blk in range(s // CUM_BLK):
        x1, x2, x3 = _split3(ls[blk * CUM_BLK:(blk + 1) * CUM_BLK])
        cs = (jnp.dot(tri, x3, preferred_element_type=F32)
              + jnp.dot(tri, x2, preferred_element_type=F32)
              + jnp.dot(tri, x1, preferred_element_type=F32)) + carry
        carry = cs[CUM_BLK - 1:CUM_BLK, :]
        blocks.append(cs)
    cum = jnp.concatenate(blocks, axis=0)
    lane = lax.broadcasted_iota(jnp.int32, x.shape, 1)
    res = jnp.where(lane < A_HEADS, x, cum)
    res = res * LOG2E
    col_ref[...] = res
    row_ref[...] = res.T[0:SUBLANES, :]

    in_b = (lane >= B_LANE0) & (lane < B_LANE0 + B_HEADS)
    p1, p2, p3 = _split3(jnp.where(in_b, cum * (-LOG2E), 0.0))
    pieces = (p1.astype(F32)
              + pltpu.roll(p2.astype(F32), PIECE_OFFS[1] - B_LANE0, axis=1)
              + pltpu.roll(p3.astype(F32), PIECE_OFFS[2] - B_LANE0, axis=1))
    pc_ref[...] = pieces.astype(BF16)


def _gates(gates3, bias):
    b, s, _ = gates3.shape
    return pl.pallas_call(
        _gates_kernel,
        grid=(b,),
        in_specs=[
            pl.BlockSpec((None, s, LANES), lambda i: (i, 0, 0)),
            pl.BlockSpec((1, LANES), lambda i: (0, 0)),
        ],
        out_specs=[
            pl.BlockSpec((None, s, LANES), lambda i: (i, 0, 0)),
            pl.BlockSpec((None, SUBLANES, s), lambda i: (i, 0, 0)),
            pl.BlockSpec((None, s, LANES), lambda i: (i, 0, 0)),
        ],
        out_shape=[
            jax.ShapeDtypeStruct((b, s, LANES), F32),
            jax.ShapeDtypeStruct((b, SUBLANES, s), F32),
            jax.ShapeDtypeStruct((b, s, LANES), BF16),
        ],
        compiler_params=pltpu.CompilerParams(
            dimension_semantics=("arbitrary",), vmem_limit_bytes=VMEM_LIMIT),
        name="gates",
    )(gates3, bias)


def _mlstm_reset(c_scr, n_scr, m_scr, fprev_scr):
    c_scr[...] = jnp.zeros_like(c_scr)
    n_scr[...] = jnp.zeros_like(n_scr)
    m_scr[...] = jnp.zeros_like(m_scr)
    fprev_scr[...] = jnp.zeros_like(fprev_scr)


def _mlstm_chunk(qk_ref, v_ref, o_ref, z_ref, gcol_ref, grow_ref, hg_ref,
                 out_ref, c_scr, n_scr, m_scr, fprev_scr, after_head=lambda: None):
    L = A_CHUNK
    qkw = A_HEADS * A_DQK

    row = lax.broadcasted_iota(jnp.int32, (L, L), 0)
    col = lax.broadcasted_iota(jnp.int32, (L, L), 1)
    causal = row >= col

    for h in range(A_HEADS):
        qb = qk_ref[:, h * A_DQK:(h + 1) * A_DQK]
        kb = qk_ref[:, qkw + h * A_DQK:qkw + (h + 1) * A_DQK]
        q = qb.astype(F32)
        k = kb.astype(F32)
        v = v_ref[:, h * A_DV:(h + 1) * A_DV]

        li_c = gcol_ref[:, h:h + 1]
        f_c = gcol_ref[:, A_HEADS + h:A_HEADS + h + 1]
        li_r = grow_ref[h:h + 1, :]
        f_r = grow_ref[A_HEADS + h:A_HEADS + h + 1, :]
        f_prev = fprev_scr[0:1, A_HEADS + h:A_HEADS + h + 1]
        f_end = gcol_ref[L - 1:L, A_HEADS + h:A_HEADS + h + 1]
        m_st = m_scr[h, 0:1, 0:1]
        c_st = c_scr[h]
        n_st = n_scr[h]

        dmat = jnp.where(causal, (f_c - f_r) + li_r, NEG_BIG)
        inter = (f_c - f_prev) + m_st
        m_row = jnp.maximum(inter, jnp.max(dmat, axis=-1, keepdims=True))
        w_intra = jnp.exp2(dmat - m_row)
        w_inter = jnp.exp2(inter - m_row)
        s = lax.dot_general(qb, kb, (((1,), (1,)), ((), ())), preferred_element_type=F32)
        scores = s * w_intra
        num = (jnp.dot(scores.astype(BF16), v, preferred_element_type=F32)
               + w_inter * jnp.dot(qb, c_st.astype(BF16), preferred_element_type=F32))
        den = (jnp.sum(scores, axis=-1, keepdims=True)
               + w_inter * jnp.sum(q * n_st, axis=-1, keepdims=True))
        hh = num * (1.0 / jnp.maximum(jnp.abs(den), jnp.exp2(-m_row)))

        g_tot = f_end - f_prev
        to_end = (f_end - f_c) + li_c
        m_new = jnp.maximum(g_tot + m_st, jnp.max(to_end, axis=0, keepdims=True))
        w_k = jnp.exp2(to_end - m_new)
        decay = jnp.exp2(g_tot + m_st - m_new)
        kw = k * w_k
        c_scr[h] = decay * c_st + jnp.dot(kw.T.astype(BF16), v, preferred_element_type=F32)
        n_scr[h] = decay * n_st + jnp.sum(kw, axis=0, keepdims=True)
        m_scr[h] = jnp.broadcast_to(m_new, (SUBLANES, LANES))

        hn = hh * lax.rsqrt(jnp.mean(hh * hh, axis=-1, keepdims=True) + EPS)
        hn = hn * hg_ref[:, h * A_DV:(h + 1) * A_DV]
        og = _sigmoid(o_ref[:, h * A_DV:(h + 1) * A_DV].astype(F32))
        zz = _silu(z_ref[:, h * A_DV:(h + 1) * A_DV].astype(F32))
        out_ref[:, h * A_DV:(h + 1) * A_DV] = ((og * hn) * zz).astype(BF16)
        after_head()

    fprev_scr[...] = gcol_ref[L - 1:L, :]


FOX_ACC_ROWS = B_DH + BF16_ROWS


def _fox_kernel(q_ref, k_ref, v_ref, z_ref, pc_ref, out_ref, vt_scr, acc_scr, rhs_scr, m_scr,
                s2_scr, cm2_scr):
    s_scr = (s2_scr.at[0], s2_scr.at[1])
    cm_scr = (cm2_scr.at[0], cm2_scr.at[1])
    hg = pl.program_id(1)
    qi = pl.program_id(2)
    TQ, TK = FOX_TQ, FOX_TK
    seq = k_ref.shape[0]
    nh = 2 * FOX_G

    @pl.when(qi == 0)
    def _():
        for g in range(FOX_G):
            vt = v_ref[:, g * LANES:(g + 1) * LANES].astype(F32).T
            for hh in range(2):
                vt_scr[2 * g + hh, 0:B_DH, :] = vt[hh * B_DH:(hh + 1) * B_DH, :].astype(BF16)
                vt_scr[2 * g + hh, B_DH:FOX_ACC_ROWS, :] = jnp.ones((BF16_ROWS, seq), BF16)

    row = lax.broadcasted_iota(jnp.int32, (LANES, TQ), 0)
    for g in range(FOX_G):
        qt = (q_ref[:, g * LANES:(g + 1) * LANES].astype(F32) * (B_DH ** -0.5 * LOG2E)).T
        for hh in range(2):
            h = 2 * g + hh
            head = hg * nh + h
            qm = jnp.where((row >= hh * B_DH) & (row < (hh + 1) * B_DH), qt, 0.0)
            sel = jnp.where((row == PIECE_OFFS[0] + head) | (row == PIECE_OFFS[1] + head)
                            | (row == PIECE_OFFS[2] + head), 1.0, 0.0)
            rhs_scr[h, 0:LANES, :] = qm.astype(BF16)
            rhs_scr[h, LANES:2 * LANES, :] = sel.astype(BF16)

    acc_scr[...] = jnp.zeros_like(acc_scr)
    m_scr[...] = jnp.full(m_scr.shape, NEG_BIG, F32)

    def key_block(kj):
        k0 = pl.multiple_of(kj * TK, TK)
        pcs = pc_ref[pl.ds(k0, TK), :]
        return [jnp.concatenate([k_ref[pl.ds(k0, TK), g * LANES:(g + 1) * LANES], pcs], axis=1)
                for g in range(FOX_G)]

    def scores_head(h, lhs, slot):
        s = jnp.dot(lhs[h // 2], rhs_scr[h], preferred_element_type=F32)
        s_scr[slot][h] = s
        cm_scr[slot][h] = jnp.broadcast_to(jnp.max(s, axis=0, keepdims=True), (SUBLANES, TQ))

    def scores(kj, slot):
        lhs = key_block(kj)
        for h in range(nh):
            scores_head(h, lhs, slot)

    def softmax_pv(h, kj, s, cmax, lo):
        k0 = pl.multiple_of(kj * TK, TK)
        m_old = m_scr[h, 0:1, lo:TQ]
        m_new = jnp.maximum(m_old, cmax)
        alpha = jnp.exp2(m_old - m_new)
        p = jnp.exp2(s - m_new).astype(BF16)
        pv = jnp.dot(vt_scr[h, :, pl.ds(k0, TK)], p, preferred_element_type=F32)
        acc_scr[h, :, lo:TQ] = alpha * acc_scr[h, :, lo:TQ] + pv
        m_scr[h, :, lo:TQ] = jnp.broadcast_to(m_new, (SUBLANES, TQ - lo))

    def overlapped(kj_next, slot_next, kj, slot):
        lhs = key_block(kj_next)
        for h in range(nh):
            scores_head(h, lhs, slot_next)
            softmax_pv(h, kj, s_scr[slot][h], cm_scr[slot][h, 0:1, :], 0)

    def pair(i, carry):
        overlapped(2 * i + 1, 1, 2 * i, 0)
        overlapped(2 * i + 2, 0, 2 * i + 1, 1)
        return carry

    scores(0, 0)
    lax.fori_loop(0, qi, pair, 0)

    half = TQ - TK
    lhs_b = key_block(2 * qi + 1)
    r = lax.broadcasted_iota(jnp.int32, (TK, TQ), 0)
    c = lax.broadcasted_iota(jnp.int32, (TK, TQ), 1)
    rb = lax.broadcasted_iota(jnp.int32, (TK, TK), 0)
    cb = lax.broadcasted_iota(jnp.int32, (TK, TK), 1)
    for h in range(nh):
        sb = jnp.dot(lhs_b[h // 2], rhs_scr[h, :, half:TQ], preferred_element_type=F32)
        s_scr[1][h, :, half:TQ] = jnp.where(cb >= rb, sb, NEG_BIG)
        s = jnp.where(c >= r, s_scr[0][h], NEG_BIG)
        softmax_pv(h, 2 * qi, s, jnp.max(s, axis=0, keepdims=True), 0)

    for g in range(FOX_G):
        parts = []
        for hh in range(2):
            h = 2 * g + hh
            s = s_scr[1][h, :, half:TQ]
            softmax_pv(h, 2 * qi + 1, s, jnp.max(s, axis=0, keepdims=True), half)
            a = acc_scr[h]
            parts.append(a[0:B_DH, :] * (1.0 / a[B_DH:B_DH + 1, :]))
        o = jnp.concatenate(parts, axis=0).T
        zz = _silu(z_ref[:, g * LANES:(g + 1) * LANES].astype(F32))
        out_ref[:, g * LANES:(g + 1) * LANES] = (o * zz).astype(BF16)


def _fox(proj, pieces, bsz, seq):
    m = proj.shape[0]
    nq = seq // FOX_TQ
    nh = 2 * FOX_G
    w = FOX_G * LANES
    ngrp = (B_HEADS * B_DH) // w
    sec = SECTION_W // w
    col0 = SEC_BQ * sec
    assert (SEC_BK, SEC_BV, SEC_BZ) == (SEC_BQ + 1, SEC_BQ + 2, SEC_BQ + 3)
    return pl.pallas_call(
        _fox_kernel,
        grid=(bsz, ngrp, nq),
        in_specs=[
            pl.BlockSpec((FOX_TQ, w), lambda b, hg, qi: (b * nq + qi, col0 + hg)),
            pl.BlockSpec((seq, w), lambda b, hg, qi: (b, col0 + sec + hg)),
            pl.BlockSpec((seq, w), lambda b, hg, qi: (b, col0 + 2 * sec + hg)),
            pl.BlockSpec((FOX_TQ, w), lambda b, hg, qi: (b * nq + qi, col0 + 3 * sec + hg)),
            pl.BlockSpec((None, seq, LANES), lambda b, hg, qi: (b, 0, 0)),
        ],
        out_specs=pl.BlockSpec((FOX_TQ, w), lambda b, hg, qi: (b * nq + qi, hg)),
        out_shape=jax.ShapeDtypeStruct((m, B_HEADS * B_DH), BF16),
        scratch_shapes=[
            pltpu.VMEM((nh, FOX_ACC_ROWS, seq), BF16),
            pltpu.VMEM((nh, FOX_ACC_ROWS, FOX_TQ), F32),
            pltpu.VMEM((nh, 2 * LANES, FOX_TQ), BF16),
            pltpu.VMEM((nh, SUBLANES, FOX_TQ), F32),
            pltpu.VMEM((2, nh, FOX_TK, FOX_TQ), F32),
            pltpu.VMEM((2, nh, SUBLANES, FOX_TQ), F32),
        ],
        compiler_params=pltpu.CompilerParams(
            dimension_semantics=("arbitrary", "arbitrary", "arbitrary"),
            vmem_limit_bytes=VMEM_LIMIT),
        name="fox",
    )(proj, proj, proj, proj, pieces)


def _merge_stages(ha_ref, hb_ref, ga_ref, gb_ref, x_ref, p_ref, wa_ref, wb_ref, wo_ref, wg_ref,
                  wp_ref, png_ref, fng_ref, out_ref):
    ya = jnp.dot(ha_ref[...], wa_ref[...], preferred_element_type=F32)
    yb = jnp.dot(hb_ref[...], wb_ref[...], preferred_element_type=F32)
    yield
    merged = (_sigmoid(ga_ref[...].astype(F32)) * ya + _sigmoid(gb_ref[...].astype(F32)) * yb)
    x1 = x_ref[...] + jnp.dot(merged.astype(BF16), wo_ref[...], preferred_element_type=F32)
    yield
    r = _rms_norm(x1, png_ref[...]).astype(BF16)
    gate = _sigmoid(jnp.dot(r, wg_ref[...], preferred_element_type=F32))
    yield
    pp = jnp.dot(p_ref[...].astype(BF16), wp_ref[...], preferred_element_type=F32)
    x2 = x1 + gate * pp
    out_ref[...] = _rms_norm(x2, fng_ref[...])
    yield


def _mlstm_merge_kernel(nt, n_chunks,
                        qk_ref, v_ref, o_ref, z_ref, gcol_ref, grow_ref, hg_ref,
                        hb_ref, ga_ref, gb_ref, x_ref, p_ref, wa_ref, wb_ref, wo_ref, wg_ref,
                        wp_ref, png_ref, fng_ref, out_ref,
                        ha_scr, c_scr, n_scr, m_scr, fprev_scr):
    s = pl.program_id(0)
    chunk = jnp.minimum(s, n_chunks - 1)
    state = (c_scr, n_scr, m_scr, fprev_scr)

    @pl.when(s == 0)
    def _():
        ha_scr[...] = jnp.zeros_like(ha_scr)

    @pl.when(chunk % nt == 0)
    def _():
        _mlstm_reset(*state)

    stages = _merge_stages(ha_scr, hb_ref, ga_ref, gb_ref, x_ref, p_ref, wa_ref, wb_ref, wo_ref,
                           wg_ref, wp_ref, png_ref, fng_ref, out_ref)
    next(stages)
    heads_done = []

    def after_head():
        heads_done.append(None)
        if len(heads_done) > MERGE_LAG:
            next(stages, None)

    _mlstm_chunk(qk_ref, v_ref, o_ref, z_ref, gcol_ref, grow_ref, hg_ref,
                 ha_scr, *state, after_head=after_head)
    for _ in stages:
        pass


def _mlstm_merge(proj, qk_act, gcol, grow, head_g, hb, x2, p2, wa, wb, wo, wg, wp,
                 png, fng, bsz, seq):
    m, d = x2.shape
    pd = p2.shape[1]
    width = SECTION_W
    nt = seq // A_CHUNK
    n_chunks = bsz * nt
    cur = lambda s: jnp.minimum(s, n_chunks - 1)
    prev = lambda s: jnp.maximum(s - 1, 0)
    a_blk = lambda sec: pl.BlockSpec((A_CHUNK, width), lambda s: (cur(s), sec))
    m_blk = lambda w, sec: pl.BlockSpec((A_CHUNK, w), lambda s: (prev(s), sec))
    full = lambda r, c: pl.BlockSpec((r, c), lambda s: (0, 0))
    return pl.pallas_call(
        functools.partial(_mlstm_merge_kernel, nt, n_chunks),
        grid=(n_chunks + 1,),
        in_specs=[
            a_blk(0), a_blk(SEC_AV), a_blk(SEC_AO), a_blk(SEC_AZ),
            pl.BlockSpec((None, A_CHUNK, LANES), lambda s: (cur(s) // nt, cur(s) % nt, 0)),
            pl.BlockSpec((None, SUBLANES, A_CHUNK), lambda s: (cur(s) // nt, 0, cur(s) % nt)),
            full(1, width),
            m_blk(d, 0), m_blk(d, SEC_GA), m_blk(d, SEC_GB), m_blk(d, 0), m_blk(pd, 0),
            full(d, d), full(d, d), full(d, d), full(d, d), full(pd, d),
            full(1, d), full(1, d),
        ],
        out_specs=pl.BlockSpec((A_CHUNK, d), lambda s: (prev(s), 0)),
        out_shape=jax.ShapeDtypeStruct((m, d), F32),
        scratch_shapes=[
            pltpu.VMEM((A_CHUNK, width), BF16),
            pltpu.VMEM((A_HEADS, A_DQK, A_DV), F32),
            pltpu.VMEM((A_HEADS, 1, A_DQK), F32),
            pltpu.VMEM((A_HEADS, SUBLANES, LANES), F32),
            pltpu.VMEM((1, LANES), F32),
        ],
        compiler_params=pltpu.CompilerParams(
            dimension_semantics=("arbitrary",), vmem_limit_bytes=VMEM_LIMIT),
        name="mlstm_merge",
    )(qk_act, proj, proj, proj, gcol, grow, head_g,
      hb, proj, proj, x2, p2, wa, wb, wo, wg, wp, png, fng)


def _split_w_in(w):
    qkw = A_HEADS * A_DQK
    aw = A_HEADS * A_DV
    bw = B_HEADS * B_DH
    d = w.shape[0]
    o_ai = 2 * qkw + aw
    o_ao = o_ai + 2 * A_HEADS
    o_bf = o_ao + 2 * aw + 3 * bw
    o_bz = o_bf + B_HEADS
    seg_cols = (o_ai, o_bf - o_ao, w.shape[1] - o_bz)
    shifts = (0, o_ao - o_ai, o_ao - o_ai + o_bz - o_bf)
    assert all(c % IN_TN == 0 for c in seg_cols) and all(s % SUBLANES == 0 for s in shifts)
    seg_tiles = tuple(c // IN_TN for c in seg_cols)
    w_t = w.T
    w_gate_t = jnp.concatenate(
        [w_t[o_ai:o_ao], w_t[o_bf:o_bz], jnp.zeros((LANES - N_GATE, d), w.dtype)], axis=0)
    return w_t, w_gate_t, seg_tiles, shifts


def _layer(x, p_i, attn_norm_g, w_in, conv_w, conv_b, a_bias_i, a_bias_f, a_head_norm_g, b_bias_f,
           w_branch_a, w_branch_b, w_out, ple_norm_g, w_ple_gate, w_ple_proj, out_norm_g):
    bsz, seq, d = x.shape
    m = bsz * seq
    x2 = x.reshape(m, d)
    w_t, w_gate_t, seg_tiles, shifts = _split_w_in(w_in)
    qkw = A_HEADS * A_DQK
    conv_scale = jnp.concatenate([jnp.ones((1, qkw), F32), jnp.full((1, qkw), A_DQK ** -0.5, F32)],
                                 axis=1)
    proj, gates, qk_act = _in_proj(x2, attn_norm_g.reshape(1, d), w_t, w_gate_t, seg_tiles, shifts,
                                   conv_w, conv_b.reshape(1, -1), conv_scale, seq)

    bias = jnp.concatenate([a_bias_i, a_bias_f, b_bias_f, jnp.zeros((LANES - N_GATE,), F32)])
    gcol, grow, pieces = _gates(gates.reshape(bsz, seq, LANES), bias.reshape(1, LANES))

    hb = _fox(proj, pieces, bsz, seq)
    out = _mlstm_merge(proj, qk_act, gcol, grow,
                       a_head_norm_g.reshape(1, -1), hb, x2, p_i.reshape(m, -1),
                       w_branch_a.astype(BF16), w_branch_b.astype(BF16), w_out.astype(BF16),
                       w_ple_gate.astype(BF16), w_ple_proj.astype(BF16),
                       ple_norm_g.reshape(1, d), out_norm_g.reshape(1, d), bsz, seq)
    return out.reshape(bsz, seq, d)


def kernel(x, p, attn_norm_g, w_in, conv_w, conv_b, a_bias_i, a_bias_f, a_head_norm_g, b_bias_f,
           w_branch_a, w_branch_b, w_out, ple_norm_g, w_ple_gate, w_ple_proj, final_norm_g):
    depth = w_in.shape[0]
    assert depth == 1, "the final norm is fused into the single layer's merge kernel"
    return _layer(x, p[0], attn_norm_g[0], w_in[0], conv_w[0], conv_b[0], a_bias_i[0], a_bias_f[0],
                  a_head_norm_g[0], b_bias_f[0], w_branch_a[0], w_branch_b[0], w_out[0],
                  ple_norm_g[0], w_ple_gate[0], w_ple_proj[0], final_norm_g)
```

```python
import functools
import math

import jax
import jax.numpy as jnp
from jax import lax
from jax.experimental import pallas as pl
from jax.experimental.pallas import tpu as pltpu

F32 = jnp.float32
BF16 = jnp.bfloat16

EPS = 1e-6
A_HEADS = 4
A_DQK = 128
A_DV = 256
CONV_K = 4
B_HEADS = 16
B_DH = 64
LANES = 128
SUBLANES = 8
BF16_ROWS = 16
NEG_BIG = -1e30
LOG2E = math.log2(math.e)

IN_TM = 2048
IN_TN = 1024
IN_CHUNKS = 8
CUM_BLK = 256
A_CHUNK = 256
FOX_TK = 256
FOX_TQ = 2 * FOX_TK
FOX_G = 4
MERGE_LAG = 1
V7X_VMEM_BYTES = 64 * 1024 * 1024
VMEM_LIMIT = V7X_VMEM_BYTES * 7 // 8

SECTION_W = A_HEADS * A_DV
SEC_QK, SEC_AV, SEC_AO, SEC_AZ, SEC_BQ, SEC_BK, SEC_BV, SEC_BZ, SEC_GA, SEC_GB = range(10)

N_GATE = 2 * A_HEADS + B_HEADS
B_LANE0 = 2 * A_HEADS
PIECE_OFFS = (B_LANE0, B_LANE0 + B_HEADS, B_LANE0 + 2 * B_HEADS)


def _sigmoid(x):
    return 1.0 / (1.0 + jnp.exp2(x * (-LOG2E)))


def _silu(x):
    return x * _sigmoid(x)


def _rms_norm(x, g):
    ms = jnp.mean(x * x, axis=-1, keepdims=True)
    return (x * lax.rsqrt(ms + EPS)) * g


def _split3(x):
    x1 = x.astype(BF16)
    r1 = x - x1.astype(F32)
    x2 = r1.astype(BF16)
    x3 = (r1 - x2.astype(F32)).astype(BF16)
    return x1, x2, x3


_NT = (((1,), (1,)), ((), ()))


def _in_proj_kernel(x_ref, g_ref, wt_ref, wgt_ref, cw_ref, cb_ref, cs_ref,
                    proj_ref, gates_ref, qk_ref, h_scr, qkraw_scr, xpad_scr):
    i = pl.program_id(0)
    j = pl.program_id(1)
    tm = x_ref.shape[0]

    @pl.when((i == 0) & (j == 0))
    def _():
        qkraw_scr[...] = jnp.zeros_like(qkraw_scr)
        xpad_scr[0:SUBLANES, :] = jnp.zeros((SUBLANES, LANES), F32)

    @pl.when(j == 0)
    def _():
        h = _rms_norm(x_ref[...], g_ref[...]).astype(BF16)
        h_scr[...] = h
        gates_ref[...] = lax.dot_general(h, wgt_ref[...].astype(BF16), _NT,
                                         preferred_element_type=F32)

    n_strips = qkraw_scr.shape[1] // LANES
    lane0 = pl.multiple_of(jnp.clip(j - 1, 0, n_strips - 1) * LANES, LANES)
    xpad_scr[SUBLANES:SUBLANES + tm, :] = qkraw_scr[:, pl.ds(lane0, LANES)].astype(F32)

    w = wt_ref[...].astype(BF16)
    ch = tm // IN_CHUNKS
    for c in range(IN_CHUNKS):
        r0 = c * ch
        proj_ref[r0:r0 + ch, :] = lax.dot_general(
            h_scr[r0:r0 + ch, :], w, _NT, preferred_element_type=F32).astype(BF16)
        y = cb_ref[...]
        for d in range(CONV_K):
            y = y + (xpad_scr[SUBLANES + r0 - d:SUBLANES + r0 - d + ch, :]
                     * cw_ref[CONV_K - 1 - d:CONV_K - d, :])
        qk_ref[r0:r0 + ch, :] = (_silu(y) * cs_ref[...]).astype(BF16)

    @pl.when(j == 0)
    def _():
        qkraw_scr[...] = proj_ref[...]


def _in_proj(x2, g, w_t, w_gate_t, seg_tiles, shifts, conv_w, conv_b, conv_scale, seq):
    m, d = x2.shape
    n = sum(seg_tiles) * IN_TN
    n_tiles = n // IN_TN
    n_strips = SECTION_W // LANES
    assert IN_TM == seq and IN_TN == SECTION_W and SEC_QK == 0 and n_tiles > n_strips
    strip = lambda i, j: (0, jnp.clip(j - 1, 0, n_strips - 1))

    def w_rows(i, j):
        shift = shifts[0] // SUBLANES
        lo = 0
        for n_tiles, s in zip(seg_tiles[:-1], shifts[1:]):
            lo += n_tiles
            shift = jnp.where(j >= lo, s // SUBLANES, shift)
        return ((j * (IN_TN // SUBLANES) + shift) * SUBLANES, 0)

    return pl.pallas_call(
        _in_proj_kernel,
        grid=(m // IN_TM, n // IN_TN),
        in_specs=[
            pl.BlockSpec((IN_TM, d), lambda i, j: (i, 0)),
            pl.BlockSpec((1, d), lambda i, j: (0, 0)),
            pl.BlockSpec((pl.Element(IN_TN), pl.Element(d)), w_rows),
            pl.BlockSpec((LANES, d), lambda i, j: (0, 0)),
            pl.BlockSpec((CONV_K, LANES), strip),
            pl.BlockSpec((1, LANES), strip),
            pl.BlockSpec((1, LANES), strip),
        ],
        out_specs=[
            pl.BlockSpec((IN_TM, IN_TN), lambda i, j: (i, j)),
            pl.BlockSpec((IN_TM, LANES), lambda i, j: (i, 0)),
            pl.BlockSpec((IN_TM, LANES), lambda i, j: (i, strip(i, j)[1])),
        ],
        out_shape=[
            jax.ShapeDtypeStruct((m, n), BF16),
            jax.ShapeDtypeStruct((m, LANES), F32),
            jax.ShapeDtypeStruct((m, SECTION_W), BF16),
        ],
        scratch_shapes=[pltpu.VMEM((IN_TM, d), BF16),
                        pltpu.VMEM((IN_TM, SECTION_W), BF16),
                        pltpu.VMEM((IN_TM + SUBLANES, LANES), F32)],
        compiler_params=pltpu.CompilerParams(
            dimension_semantics=("arbitrary", "arbitrary"),
            vmem_limit_bytes=VMEM_LIMIT),
        name="in_proj",
    )(x2, g, w_t, w_gate_t, conv_w, conv_b, conv_scale)


def _gates_kernel(g_ref, bias_ref, col_ref, row_ref, pc_ref):
    x = g_ref[...] + bias_ref[...]
    s = x.shape[0]
    ls = jnp.minimum(x, 0.0) - jnp.log1p(jnp.exp(-jnp.abs(x)))
    r = lax.broadcasted_iota(jnp.int32, (CUM_BLK, CUM_BLK), 0)
    c = lax.broadcasted_iota(jnp.int32, (CUM_BLK, CUM_BLK), 1)
    tri = jnp.where(r >= c, 1.0, 0.0).astype(BF16)
    carry = jnp.zeros((1, LANES), F32)
    blocks = []
    for blk in range(s // CUM_BLK):
        x1, x2, x3 = _split3(ls[blk * CUM_BLK:(blk + 1) * CUM_BLK])
        cs = (jnp.dot(tri, x3, preferred_element_type=F32)
              + jnp.dot(tri, x2, preferred_element_type=F32)
              + jnp.dot(tri, x1, preferred_element_type=F32)) + carry
        carry = cs[CUM_BLK - 1:CUM_BLK, :]
        blocks.append(cs)
    cum = jnp.concatenate(blocks, axis=0)
    lane = lax.broadcasted_iota(jnp.int32, x.shape, 1)
    res = jnp.where(lane < A_HEADS, x, cum)
    res = res * LOG2E
    col_ref[...] = res
    row_ref[...] = res.T[0:SUBLANES, :]

    in_b = (lane >= B_LANE0) & (lane < B_LANE0 + B_HEADS)
    p1, p2, p3 = _split3(jnp.where(in_b, cum * (-LOG2E), 0.0))
    pieces = (p1.astype(F32)
              + pltpu.roll(p2.astype(F32), PIECE_OFFS[1] - B_LANE0, axis=1)
              + pltpu.roll(p3.astype(F32), PIECE_OFFS[2] - B_LANE0, axis=1))
    pc_ref[...] = pieces.astype(BF16)


def _gates(gates3, bias):
    b, s, _ = gates3.shape
    return pl.pallas_call(
        _gates_kernel,
        grid=(b,),
        in_specs=[
            pl.BlockSpec((None, s, LANES), lambda i: (i, 0, 0)),
            pl.BlockSpec((1, LANES), lambda i: (0, 0)),
        ],
        out_specs=[
            pl.BlockSpec((None, s, LANES), lambda i: (i, 0, 0)),
            pl.BlockSpec((None, SUBLANES, s), lambda i: (i, 0, 0)),
            pl.BlockSpec((None, s, LANES), lambda i: (i, 0, 0)),
        ],
        out_shape=[
            jax.ShapeDtypeStruct((b, s, LANES), F32),
            jax.ShapeDtypeStruct((b, SUBLANES, s), F32),
            jax.ShapeDtypeStruct((b, s, LANES), BF16),
        ],
        compiler_params=pltpu.CompilerParams(
            dimension_semantics=("arbitrary",), vmem_limit_bytes=VMEM_LIMIT),
        name="gates",
    )(gates3, bias)


def _mlstm_reset(c_scr, n_scr, m_scr, fprev_scr):
    c_scr[...] = jnp.zeros_like(c_scr)
    n_scr[...] = jnp.zeros_like(n_scr)
    m_scr[...] = jnp.zeros_like(m_scr)
    fprev_scr[...] = jnp.zeros_like(fprev_scr)


def _mlstm_chunk(qk_ref, v_ref, o_ref, z_ref, gcol_ref, grow_ref, hg_ref,
                 out_ref, c_scr, n_scr, m_scr, fprev_scr, after_head=lambda: None):
    L = A_CHUNK
    qkw = A_HEADS * A_DQK

    row = lax.broadcasted_iota(jnp.int32, (L, L), 0)
    col = lax.broadcasted_iota(jnp.int32, (L, L), 1)
    causal = row >= col

    for h in range(A_HEADS):
        qb = qk_ref[:, h * A_DQK:(h + 1) * A_DQK]
        kb = qk_ref[:, qkw + h * A_DQK:qkw + (h + 1) * A_DQK]
        q = qb.astype(F32)
        k = kb.astype(F32)
        v = v_ref[:, h * A_DV:(h + 1) * A_DV]

        li_c = gcol_ref[:, h:h + 1]
        f_c = gcol_ref[:, A_HEADS + h:A_HEADS + h + 1]
        li_r = grow_ref[h:h + 1, :]
        f_r = grow_ref[A_HEADS + h:A_HEADS + h + 1, :]
        f_prev = fprev_scr[0:1, A_HEADS + h:A_HEADS + h + 1]
        f_end = gcol_ref[L - 1:L, A_HEADS + h:A_HEADS + h + 1]
        m_st = m_scr[h, 0:1, 0:1]
        c_st = c_scr[h]
        n_st = n_scr[h]

        dmat = jnp.where(causal, (f_c - f_r) + li_r, NEG_BIG)
        inter = (f_c - f_prev) + m_st
        m_row = jnp.maximum(inter, jnp.max(dmat, axis=-1, keepdims=True))
        w_intra = jnp.exp2(dmat - m_row)
        w_inter = jnp.exp2(inter - m_row)
        s = lax.dot_general(qb, kb, (((1,), (1,)), ((), ())), preferred_element_type=F32)
        scores = s * w_intra
        num = (jnp.dot(scores.astype(BF16), v, preferred_element_type=F32)
               + w_inter * jnp.dot(qb, c_st.astype(BF16), preferred_element_type=F32))
        den = (jnp.sum(scores, axis=-1, keepdims=True)
               + w_inter * jnp.sum(q * n_st, axis=-1, keepdims=True))
        hh = num * (1.0 / jnp.maximum(jnp.abs(den), jnp.exp2(-m_row)))

        g_tot = f_end - f_prev
        to_end = (f_end - f_c) + li_c
        m_new = jnp.maximum(g_tot + m_st, jnp.max(to_end, axis=0, keepdims=True))
        w_k = jnp.exp2(to_end - m_new)
        decay = jnp.exp2(g_tot + m_st - m_new)
        kw = k * w_k
        c_scr[h] = decay * c_st + jnp.dot(kw.T.astype(BF16), v, preferred_element_type=F32)
        n_scr[h] = decay * n_st + jnp.sum(kw, axis=0, keepdims=True)
        m_scr[h] = jnp.broadcast_to(m_new, (SUBLANES, LANES))

        hn = hh * lax.rsqrt(jnp.mean(hh * hh, axis=-1, keepdims=True) + EPS)
        hn = hn * hg_ref[:, h * A_DV:(h + 1) * A_DV]
        og = _sigmoid(o_ref[:, h * A_DV:(h + 1) * A_DV].astype(F32))
        zz = _silu(z_ref[:, h * A_DV:(h + 1) * A_DV].astype(F32))
        out_ref[:, h * A_DV:(h + 1) * A_DV] = ((og * hn) * zz).astype(BF16)
        after_head()

    fprev_scr[...] = gcol_ref[L - 1:L, :]


FOX_ACC_ROWS = B_DH + BF16_ROWS


def _fox_kernel(q_ref, k_ref, v_ref, z_ref, pc_ref, out_ref, vt_scr, acc_scr, rhs_scr, m_scr,
                s2_scr, cm2_scr):
    s_scr = (s2_scr.at[0], s2_scr.at[1])
    cm_scr = (cm2_scr.at[0], cm2_scr.at[1])
    hg = pl.program_id(1)
    qi = pl.program_id(2)
    TQ, TK = FOX_TQ, FOX_TK
    seq = k_ref.shape[0]
    nh = 2 * FOX_G

    @pl.when(qi == 0)
    def _():
        for g in range(FOX_G):
            vt = v_ref[:, g * LANES:(g + 1) * LANES].astype(F32).T
            for hh in range(2):
                vt_scr[2 * g + hh, 0:B_DH, :] = vt[hh * B_DH:(hh + 1) * B_DH, :].astype(BF16)
                vt_scr[2 * g + hh, B_DH:FOX_ACC_ROWS, :] = jnp.ones((BF16_ROWS, seq), BF16)

    row = lax.broadcasted_iota(jnp.int32, (LANES, TQ), 0)
    for g in range(FOX_G):
        qt = (q_ref[:, g * LANES:(g + 1) * LANES].astype(F32) * (B_DH ** -0.5 * LOG2E)).T
        for hh in range(2):
            h = 2 * g + hh
            head = hg * nh + h
            qm = jnp.where((row >= hh * B_DH) & (row < (hh + 1) * B_DH), qt, 0.0)
            sel = jnp.where((row == PIECE_OFFS[0] + head) | (row == PIECE_OFFS[1] + head)
                            | (row == PIECE_OFFS[2] + head), 1.0, 0.0)
            rhs_scr[h, 0:LANES, :] = qm.astype(BF16)
            rhs_scr[h, LANES:2 * LANES, :] = sel.astype(BF16)

    acc_scr[...] = jnp.zeros_like(acc_scr)
    m_scr[...] = jnp.full(m_scr.shape, NEG_BIG, F32)

    def key_block(kj):
        k0 = pl.multiple_of(kj * TK, TK)
        pcs = pc_ref[pl.ds(k0, TK), :]
        return [jnp.concatenate([k_ref[pl.ds(k0, TK), g * LANES:(g + 1) * LANES], pcs], axis=1)
                for g in range(FOX_G)]

    def scores_head(h, lhs, slot):
        s = jnp.dot(lhs[h // 2], rhs_scr[h], preferred_element_type=F32)
        s_scr[slot][h] = s
        cm_scr[slot][h] = jnp.broadcast_to(jnp.max(s, axis=0, keepdims=True), (SUBLANES, TQ))

    def scores(kj, slot):
        lhs = key_block(kj)
        for h in range(nh):
            scores_head(h, lhs, slot)

    def softmax_pv(h, kj, s, cmax, lo):
        k0 = pl.multiple_of(kj * TK, TK)
        m_old = m_scr[h, 0:1, lo:TQ]
        m_new = jnp.maximum(m_old, cmax)
        alpha = jnp.exp2(m_old - m_new)
        p = jnp.exp2(s - m_new).astype(BF16)
        pv = jnp.dot(vt_scr[h, :, pl.ds(k0, TK)], p, preferred_element_type=F32)
        acc_scr[h, :, lo:TQ] = alpha * acc_scr[h, :, lo:TQ] + pv
        m_scr[h, :, lo:TQ] = jnp.broadcast_to(m_new, (SUBLANES, TQ - lo))

    def overlapped(kj_next, slot_next, kj, slot):
        lhs = key_block(kj_next)
        for h in range(nh):
            scores_head(h, lhs, slot_next)
            softmax_pv(h, kj, s_scr[slot][h], cm_scr[slot][h, 0:1, :], 0)

    def pair(i, carry):
        overlapped(2 * i + 1, 1, 2 * i, 0)
        overlapped(2 * i + 2, 0, 2 * i + 1, 1)
        return carry

    scores(0, 0)
    lax.fori_loop(0, qi, pair, 0)

    half = TQ - TK
    lhs_b = key_block(2 * qi + 1)
    r = lax.broadcasted_iota(jnp.int32, (TK, TQ), 0)
    c = lax.broadcasted_iota(jnp.int32, (TK, TQ), 1)
    rb = lax.broadcasted_iota(jnp.int32, (TK, TK), 0)
    cb = lax.broadcasted_iota(jnp.int32, (TK, TK), 1)
    for h in range(nh):
        sb = jnp.dot(lhs_b[h // 2], rhs_scr[h, :, half:TQ], preferred_element_type=F32)
        s_scr[1][h, :, half:TQ] = jnp.where(cb >= rb, sb, NEG_BIG)
        s = jnp.where(c >= r, s_scr[0][h], NEG_BIG)
        softmax_pv(h, 2 * qi, s, jnp.max(s, axis=0, keepdims=True), 0)

    for g in range(FOX_G):
        parts = []
        for hh in range(2):
            h = 2 * g + hh
            s = s_scr[1][h, :, half:TQ]
            softmax_pv(h, 2 * qi + 1, s, jnp.max(s, axis=0, keepdims=True), half)
            a = acc_scr[h]
            parts.append(a[0:B_DH, :] * (1.0 / a[B_DH:B_DH + 1, :]))
        o = jnp.concatenate(parts, axis=0).T
        zz = _silu(z_ref[:, g * LANES:(g + 1) * LANES].astype(F32))
        out_ref[:, g * LANES:(g + 1) * LANES] = (o * zz).astype(BF16)


def _fox(proj, pieces, bsz, seq):
    m = proj.shape[0]
    nq = seq // FOX_TQ
    nh = 2 * FOX_G
    w = FOX_G * LANES
    ngrp = (B_HEADS * B_DH) // w
    sec = SECTION_W // w
    col0 = SEC_BQ * sec
    assert (SEC_BK, SEC_BV, SEC_BZ) == (SEC_BQ + 1, SEC_BQ + 2, SEC_BQ + 3)
    return pl.pallas_call(
        _fox_kernel,
        grid=(bsz, ngrp, nq),
        in_specs=[
            pl.BlockSpec((FOX_TQ, w), lambda b, hg, qi: (b * nq + qi, col0 + hg)),
            pl.BlockSpec((seq, w), lambda b, hg, qi: (b, col0 + sec + hg)),
            pl.BlockSpec((seq, w), lambda b, hg, qi: (b, col0 + 2 * sec + hg)),
            pl.BlockSpec((FOX_TQ, w), lambda b, hg, qi: (b * nq + qi, col0 + 3 * sec + hg)),
            pl.BlockSpec((None, seq, LANES), lambda b, hg, qi: (b, 0, 0)),
        ],
        out_specs=pl.BlockSpec((FOX_TQ, w), lambda b, hg, qi: (b * nq + qi, hg)),
        out_shape=jax.ShapeDtypeStruct((m, B_HEADS * B_DH), BF16),
        scratch_shapes=[
            pltpu.VMEM((nh, FOX_ACC_ROWS, seq), BF16),
            pltpu.VMEM((nh, FOX_ACC_ROWS, FOX_TQ), F32),
            pltpu.VMEM((nh, 2 * LANES, FOX_TQ), BF16),
            pltpu.VMEM((nh, SUBLANES, FOX_TQ), F32),
            pltpu.VMEM((2, nh, FOX_TK, FOX_TQ), F32),
            pltpu.VMEM((2, nh, SUBLANES, FOX_TQ), F32),
        ],
        compiler_params=pltpu.CompilerParams(
            dimension_semantics=("arbitrary", "arbitrary", "arbitrary"),
            vmem_limit_bytes=VMEM_LIMIT),
        name="fox",
    )(proj, proj, proj, proj, pieces)


def _merge_stages(ha_ref, hb_ref, ga_ref, gb_ref, x_ref, p_ref, wa_ref, wb_ref, wo_ref, wg_ref,
                  wp_ref, png_ref, fng_ref, out_ref):
    ya = jnp.dot(ha_ref[...], wa_ref[...], preferred_element_type=F32)
    yb = jnp.dot(hb_ref[...], wb_ref[...], preferred_element_type=F32)
    yield
    merged = (_sigmoid(ga_ref[...].astype(F32)) * ya + _sigmoid(gb_ref[...].astype(F32)) * yb)
    x1 = x_ref[...] + jnp.dot(merged.astype(BF16), wo_ref[...], preferred_element_type=F32)
    yield
    r = _rms_norm(x1, png_ref[...]).astype(BF16)
    gate = _sigmoid(jnp.dot(r, wg_ref[...], preferred_element_type=F32))
    yield
    pp = jnp.dot(p_ref[...].astype(BF16), wp_ref[...], preferred_element_type=F32)
    x2 = x1 + gate * pp
    out_ref[...] = _rms_norm(x2, fng_ref[...])
    yield


def _mlstm_merge_kernel(nt, n_chunks,
                        qk_ref, v_ref, o_ref, z_ref, gcol_ref, grow_ref, hg_ref,
                        hb_ref, ga_ref, gb_ref, x_ref, p_ref, wa_ref, wb_ref, wo_ref, wg_ref,
                        wp_ref, png_ref, fng_ref, out_ref,
                        ha_scr, c_scr, n_scr, m_scr, fprev_scr):
    s = pl.program_id(0)
    chunk = jnp.minimum(s, n_chunks - 1)
    state = (c_scr, n_scr, m_scr, fprev_scr)

    @pl.when(s == 0)
    def _():
        ha_scr[...] = jnp.zeros_like(ha_scr)

    @pl.when(chunk % nt == 0)
    def _():
        _mlstm_reset(*state)

    stages = _merge_stages(ha_scr, hb_ref, ga_ref, gb_ref, x_ref, p_ref, wa_ref, wb_ref, wo_ref,
                           wg_ref, wp_ref, png_ref, fng_ref, out_ref)
    next(stages)
    heads_done = []

    def after_head():
        heads_done.append(None)
        if len(heads_done) > MERGE_LAG:
            next(stages, None)

    _mlstm_chunk(qk_ref, v_ref, o_ref, z_ref, gcol_ref, grow_ref, hg_ref,
                 ha_scr, *state, after_head=after_head)
    for _ in stages:
        pass


def _mlstm_merge(proj, qk_act, gcol, grow, head_g, hb, x2, p2, wa, wb, wo, wg, wp,
                 png, fng, bsz, seq):
    m, d = x2.shape
    pd = p2.shape[1]
    width = SECTION_W
    nt = seq // A_CHUNK
    n_chunks = bsz * nt
    cur = lambda s: jnp.minimum(s, n_chunks - 1)
    prev = lambda s: jnp.maximum(s - 1, 0)
    a_blk = lambda sec: pl.BlockSpec((A_CHUNK, width), lambda s: (cur(s), sec))
    m_blk = lambda w, sec: pl.BlockSpec((A_CHUNK, w), lambda s: (prev(s), sec))
    full = lambda r, c: pl.BlockSpec((r, c), lambda s: (0, 0))
    return pl.pallas_call(
        functools.partial(_mlstm_merge_kernel, nt, n_chunks),
        grid=(n_chunks + 1,),
        in_specs=[
            a_blk(0), a_blk(SEC_AV), a_blk(SEC_AO), a_blk(SEC_AZ),
            pl.BlockSpec((None, A_CHUNK, LANES), lambda s: (cur(s) // nt, cur(s) % nt, 0)),
            pl.BlockSpec((None, SUBLANES, A_CHUNK), lambda s: (cur(s) // nt, 0, cur(s) % nt)),
            full(1, width),
            m_blk(d, 0), m_blk(d, SEC_GA), m_blk(d, SEC_GB), m_blk(d, 0), m_blk(pd, 0),
            full(d, d), full(d, d), full(d, d), full(d, d), full(pd, d),
            full(1, d), full(1, d),
        ],
        out_specs=pl.BlockSpec((A_CHUNK, d), lambda s: (prev(s), 0)),
        out_shape=jax.ShapeDtypeStruct((m, d), F32),
        scratch_shapes=[
            pltpu.VMEM((A_CHUNK, width), BF16),
            pltpu.VMEM((A_HEADS, A_DQK, A_DV), F32),
            pltpu.VMEM((A_HEADS, 1, A_DQK), F32),
            pltpu.VMEM((A_HEADS, SUBLANES, LANES), F32),
            pltpu.VMEM((1, LANES), F32),
        ],
        compiler_params=pltpu.CompilerParams(
            dimension_semantics=("arbitrary",), vmem_limit_bytes=VMEM_LIMIT),
        name="mlstm_merge",
    )(qk_act, proj, proj, proj, gcol, grow, head_g,
      hb, proj, proj, x2, p2, wa, wb, wo, wg, wp, png, fng)


def _split_w_in(w):
    qkw = A_HEADS * A_DQK
    aw = A_HEADS * A_DV
    bw = B_HEADS * B_DH
    d = w.shape[0]
    o_ai = 2 * qkw + aw
    o_ao = o_ai + 2 * A_HEADS
    o_bf = o_ao + 2 * aw + 3 * bw
    o_bz = o_bf + B_HEADS
    seg_cols = (o_ai, o_bf - o_ao, w.shape[1] - o_bz)
    shifts = (0, o_ao - o_ai, o_ao - o_ai + o_bz - o_bf)
    assert all(c % IN_TN == 0 for c in seg_cols) and all(s % SUBLANES == 0 for s in shifts)
    seg_tiles = tuple(c // IN_TN for c in seg_cols)
    w_t = w.T
    w_gate_t = jnp.concatenate(
        [w_t[o_ai:o_ao], w_t[o_bf:o_bz], jnp.zeros((LANES - N_GATE, d), w.dtype)], axis=0)
    return w_t, w_gate_t, seg_tiles, shifts


def _layer(x, p_i, attn_norm_g, w_in, conv_w, conv_b, a_bias_i, a_bias_f, a_head_norm_g, b_bias_f,
           w_branch_a, w_branch_b, w_out, ple_norm_g, w_ple_gate, w_ple_proj, out_norm_g):
    bsz, seq, d = x.shape
    m = bsz * seq
    x2 = x.reshape(m, d)
    w_t, w_gate_t, seg_tiles, shifts = _split_w_in(w_in)
    qkw = A_HEADS * A_DQK
    conv_scale = jnp.concatenate([jnp.ones((1, qkw), F32), jnp.full((1, qkw), A_DQK ** -0.5, F32)],
                                 axis=1)
    proj, gates, qk_act = _in_proj(x2, attn_norm_g.reshape(1, d), w_t, w_gate_t, seg_tiles, shifts,
                                   conv_w, conv_b.reshape(1, -1), conv_scale, seq)

    bias = jnp.concatenate([a_bias_i, a_bias_f, b_bias_f, jnp.zeros((LANES - N_GATE,), F32)])
    gcol, grow, pieces = _gates(gates.reshape(bsz, seq, LANES), bias.reshape(1, LANES))

    hb = _fox(proj, pieces, bsz, seq)
    out = _mlstm_merge(proj, qk_act, gcol, grow,
                       a_head_norm_g.reshape(1, -1), hb, x2, p_i.reshape(m, -1),
                       w_branch_a.astype(BF16), w_branch_b.astype(BF16), w_out.astype(BF16),
                       w_ple_gate.astype(BF16), w_ple_proj.astype(BF16),
                       ple_norm_g.reshape(1, d), out_norm_g.reshape(1, d), bsz, seq)
    return out.reshape(bsz, seq, d)


def kernel(x, p, attn_norm_g, w_in, conv_w, conv_b, a_bias_i, a_bias_f, a_head_norm_g, b_bias_f,
           w_branch_a, w_branch_b, w_out, ple_norm_g, w_ple_gate, w_ple_proj, final_norm_g):
    depth = w_in.shape[0]
    assert depth == 1, "the final norm is fused into the single layer's merge kernel"
    return _layer(x, p[0], attn_norm_g[0], w_in[0], conv_w[0], conv_b[0], a_bias_i[0], a_bias_f[0],
                  a_head_norm_g[0], b_bias_f[0], w_branch_a[0], w_branch_b[0], w_out[0],
                  ple_norm_g[0], w_ple_gate[0], w_ple_proj[0], final_norm_g)
```

```python
import functools
import math

import jax
import jax.numpy as jnp
from jax import lax
from jax.experimental import pallas as pl
from jax.experimental.pallas import tpu as pltpu

F32 = jnp.float32
BF16 = jnp.bfloat16

EPS = 1e-6
A_HEADS = 4
A_DQK = 128
A_DV = 256
CONV_K = 4
B_HEADS = 16
B_DH = 64
LANES = 128
SUBLANES = 8
BF16_ROWS = 16
NEG_BIG = -1e30
LOG2E = math.log2(math.e)

IN_TM = 2048
IN_TN = 1024
IN_CHUNKS = 8
CUM_BLK = 256
A_CHUNK = 256
FOX_TK = 256
FOX_TQ = 2 * FOX_TK
FOX_G = 4
MERGE_LAG = 1
V7X_VMEM_BYTES = 64 * 1024 * 1024
VMEM_LIMIT = V7X_VMEM_BYTES * 7 // 8

SECTION_W = A_HEADS * A_DV
SEC_QK, SEC_AV, SEC_AO, SEC_AZ, SEC_BQ, SEC_BK, SEC_BV, SEC_BZ, SEC_GA, SEC_GB = range(10)

N_GATE = 2 * A_HEADS + B_HEADS
B_LANE0 = 2 * A_HEADS
PIECE_OFFS = (B_LANE0, B_LANE0 + B_HEADS, B_LANE0 + 2 * B_HEADS)


def _sigmoid(x):
    return 1.0 / (1.0 + jnp.exp2(x * (-LOG2E)))


def _silu(x):
    return x * _sigmoid(x)


def _rms_norm(x, g):
    ms = jnp.mean(x * x, axis=-1, keepdims=True)
    return (x * lax.rsqrt(ms + EPS)) * g


def _split3(x):
    x1 = x.astype(BF16)
    r1 = x - x1.astype(F32)
    x2 = r1.astype(BF16)
    x3 = (r1 - x2.astype(F32)).astype(BF16)
    return x1, x2, x3


_NT = (((1,), (1,)), ((), ()))


def _in_proj_kernel(x_ref, g_ref, wt_ref, wgt_ref, cw_ref, cb_ref, cs_ref,
                    proj_ref, gates_ref, qk_ref, h_scr, qkraw_scr, xpad_scr):
    i = pl.program_id(0)
    j = pl.program_id(1)
    tm = x_ref.shape[0]

    @pl.when((i == 0) & (j == 0))
    def _():
        qkraw_scr[...] = jnp.zeros_like(qkraw_scr)
        xpad_scr[0:SUBLANES, :] = jnp.zeros((SUBLANES, LANES), F32)

    @pl.when(j == 0)
    def _():
        h = _rms_norm(x_ref[...], g_ref[...]).astype(BF16)
        h_scr[...] = h
        gates_ref[...] = lax.dot_general(h, wgt_ref[...].astype(BF16), _NT,
                                         preferred_element_type=F32)

    n_strips = qkraw_scr.shape[1] // LANES
    lane0 = pl.multiple_of(jnp.clip(j - 1, 0, n_strips - 1) * LANES, LANES)
    xpad_scr[SUBLANES:SUBLANES + tm, :] = qkraw_scr[:, pl.ds(lane0, LANES)].astype(F32)

    w = wt_ref[...].astype(BF16)
    ch = tm // IN_CHUNKS
    for c in range(IN_CHUNKS):
        r0 = c * ch
        proj_ref[r0:r0 + ch, :] = lax.dot_general(
            h_scr[r0:r0 + ch, :], w, _NT, preferred_element_type=F32).astype(BF16)
        y = cb_ref[...]
        for d in range(CONV_K):
            y = y + (xpad_scr[SUBLANES + r0 - d:SUBLANES + r0 - d + ch, :]
                     * cw_ref[CONV_K - 1 - d:CONV_K - d, :])
        qk_ref[r0:r0 + ch, :] = (_silu(y) * cs_ref[...]).astype(BF16)

    @pl.when(j == 0)
    def _():
        qkraw_scr[...] = proj_ref[...]


def _in_proj(x2, g, w_t, w_gate_t, seg_tiles, shifts, conv_w, conv_b, conv_scale, seq):
    m, d = x2.shape
    n = sum(seg_tiles) * IN_TN
    n_tiles = n // IN_TN
    n_strips = SECTION_W // LANES
    assert IN_TM == seq and IN_TN == SECTION_W and SEC_QK == 0 and n_tiles > n_strips
    strip = lambda i, j: (0, jnp.clip(j - 1, 0, n_strips - 1))

    def w_rows(i, j):
        shift = shifts[0] // SUBLANES
        lo = 0
        for n_tiles, s in zip(seg_tiles[:-1], shifts[1:]):
            lo += n_tiles
            shift = jnp.where(j >= lo, s // SUBLANES, shift)
        return ((j * (IN_TN // SUBLANES) + shift) * SUBLANES, 0)

    return pl.pallas_call(
        _in_proj_kernel,
        grid=(m // IN_TM, n // IN_TN),
        in_specs=[
            pl.BlockSpec((IN_TM, d), lambda i, j: (i, 0)),
            pl.BlockSpec((1, d), lambda i, j: (0, 0)),
            pl.BlockSpec((pl.Element(IN_TN), pl.Element(d)), w_rows),
            pl.BlockSpec((LANES, d), lambda i, j: (0, 0)),
            pl.BlockSpec((CONV_K, LANES), strip),
            pl.BlockSpec((1, LANES), strip),
            pl.BlockSpec((1, LANES), strip),
        ],
        out_specs=[
            pl.BlockSpec((IN_TM, IN_TN), lambda i, j: (i, j)),
            pl.BlockSpec((IN_TM, LANES), lambda i, j: (i, 0)),
            pl.BlockSpec((IN_TM, LANES), lambda i, j: (i, strip(i, j)[1])),
        ],
        out_shape=[
            jax.ShapeDtypeStruct((m, n), BF16),
            jax.ShapeDtypeStruct((m, LANES), F32),
            jax.ShapeDtypeStruct((m, SECTION_W), BF16),
        ],
        scratch_shapes=[pltpu.VMEM((IN_TM, d), BF16),
                        pltpu.VMEM((IN_TM, SECTION_W), BF16),
                        pltpu.VMEM((IN_TM + SUBLANES, LANES), F32)],
        compiler_params=pltpu.CompilerParams(
            dimension_semantics=("arbitrary", "arbitrary"),
            vmem_limit_bytes=VMEM_LIMIT),
        name="in_proj",
    )(x2, g, w_t, w_gate_t, conv_w, conv_b, conv_scale)


def _gates_kernel(g_ref, bias_ref, col_ref, row_ref, pc_ref):
    x = g_ref[...] + bias_ref[...]
    s = x.shape[0]
    ls = jnp.minimum(x, 0.0) - jnp.log1p(jnp.exp(-jnp.abs(x)))
    r = lax.broadcasted_iota(jnp.int32, (CUM_BLK, CUM_BLK), 0)
    c = lax.broadcasted_iota(jnp.int32, (CUM_BLK, CUM_BLK), 1)
    tri = jnp.where(r >= c, 1.0, 0.0).astype(BF16)
    carry = jnp.zeros((1, LANES), F32)
    blocks = []
    for blk in range(s // CUM_BLK):
        x1, x2, x3 = _split3(ls[blk * CUM_BLK:(blk + 1) * CUM_BLK])
        cs = (jnp.dot(tri, x3, preferred_element_type=F32)
              + jnp.dot(tri, x2, preferred_element_type=F32)
              + jnp.dot(tri, x1, preferred_element_type=F32)) + carry
        carry = cs[CUM_BLK - 1:CUM_BLK, :]
        blocks.append(cs)
    cum = jnp.concatenate(blocks, axis=0)
    lane = lax.broadcasted_iota(jnp.int32, x.shape, 1)
    res = jnp.where(lane < A_HEADS, x, cum)
    res = res * LOG2E
    col_ref[...] = res
    row_ref[...] = res.T[0:SUBLANES, :]

    in_b = (lane >= B_LANE0) & (lane < B_LANE0 + B_HEADS)
    p1, p2, p3 = _split3(jnp.where(in_b, cum * (-LOG2E), 0.0))
    pieces = (p1.astype(F32)
              + pltpu.roll(p2.astype(F32), PIECE_OFFS[1] - B_LANE0, axis=1)
              + pltpu.roll(p3.astype(F32), PIECE_OFFS[2] - B_LANE0, axis=1))
    pc_ref[...] = pieces.astype(BF16)


def _gates(gates3, bias):
    b, s, _ = gates3.shape
    return pl.pallas_call(
        _gates_kernel,
        grid=(b,),
        in_specs=[
            pl.BlockSpec((None, s, LANES), lambda i: (i, 0, 0)),
            pl.BlockSpec((1, LANES), lambda i: (0, 0)),
        ],
        out_specs=[
            pl.BlockSpec((None, s, LANES), lambda i: (i, 0, 0)),
            pl.BlockSpec((None, SUBLANES, s), lambda i: (i, 0, 0)),
            pl.BlockSpec((None, s, LANES), lambda i: (i, 0, 0)),
        ],
        out_shape=[
            jax.ShapeDtypeStruct((b, s, LANES), F32),
            jax.ShapeDtypeStruct((b, SUBLANES, s), F32),
            jax.ShapeDtypeStruct((b, s, LANES), BF16),
        ],
        compiler_params=pltpu.CompilerParams(
            dimension_semantics=("arbitrary",), vmem_limit_bytes=VMEM_LIMIT),
        name="gates",
    )(gates3, bias)


def _mlstm_reset(c_scr, n_scr, m_scr, fprev_scr):
    c_scr[...] = jnp.zeros_like(c_scr)
    n_scr[...] = jnp.zeros_like(n_scr)
    m_scr[...] = jnp.zeros_like(m_scr)
    fprev_scr[...] = jnp.zeros_like(fprev_scr)


def _mlstm_chunk(qk_ref, v_ref, o_ref, z_ref, gcol_ref, grow_ref, hg_ref,
                 out_ref, c_scr, n_scr, m_scr, fprev_scr, after_head=lambda: None):
    L = A_CHUNK
    qkw = A_HEADS * A_DQK

    row = lax.broadcasted_iota(jnp.int32, (L, L), 0)
    col = lax.broadcasted_iota(jnp.int32, (L, L), 1)
    causal = row >= col

    for h in range(A_HEADS):
        qb = qk_ref[:, h * A_DQK:(h + 1) * A_DQK]
        kb = qk_ref[:, qkw + h * A_DQK:qkw + (h + 1) * A_DQK]
        q = qb.astype(F32)
        k = kb.astype(F32)
        v = v_ref[:, h * A_DV:(h + 1) * A_DV]

        li_c = gcol_ref[:, h:h + 1]
        f_c = gcol_ref[:, A_HEADS + h:A_HEADS + h + 1]
        li_r = grow_ref[h:h + 1, :]
        f_r = grow_ref[A_HEADS + h:A_HEADS + h + 1, :]
        f_prev = fprev_scr[0:1, A_HEADS + h:A_HEADS + h + 1]
        f_end = gcol_ref[L - 1:L, A_HEADS + h:A_HEADS + h + 1]
        m_st = m_scr[h, 0:1, 0:1]
        c_st = c_scr[h]
        n_st = n_scr[h]

        dmat = jnp.where(causal, (f_c - f_r) + li_r, NEG_BIG)
        inter = (f_c - f_prev) + m_st
        m_row = jnp.maximum(inter, jnp.max(dmat, axis=-1, keepdims=True))
        w_intra = jnp.exp2(dmat - m_row)
        w_inter = jnp.exp2(inter - m_row)
        s = lax.dot_general(qb, kb, (((1,), (1,)), ((), ())), preferred_element_type=F32)
        scores = s * w_intra
        num = (jnp.dot(scores.astype(BF16), v, preferred_element_type=F32)
               + w_inter * jnp.dot(qb, c_st.astype(BF16), preferred_element_type=F32))
        den = (jnp.sum(scores, axis=-1, keepdims=True)
               + w_inter * jnp.sum(q * n_st, axis=-1, keepdims=True))
        hh = num * (1.0 / jnp.maximum(jnp.abs(den), jnp.exp2(-m_row)))

        g_tot = f_end - f_prev
        to_end = (f_end - f_c) + li_c
        m_new = jnp.maximum(g_tot + m_st, jnp.max(to_end, axis=0, keepdims=True))
        w_k = jnp.exp2(to_end - m_new)
        decay = jnp.exp2(g_tot + m_st - m_new)
        kw = k * w_k
        c_scr[h] = decay * c_st + jnp.dot(kw.T.astype(BF16), v, preferred_element_type=F32)
        n_scr[h] = decay * n_st + jnp.sum(kw, axis=0, keepdims=True)
        m_scr[h] = jnp.broadcast_to(m_new, (SUBLANES, LANES))

        hn = hh * lax.rsqrt(jnp.mean(hh * hh, axis=-1, keepdims=True) + EPS)
        hn = hn * hg_ref[:, h * A_DV:(h + 1) * A_DV]
        og = _sigmoid(o_ref[:, h * A_DV:(h + 1) * A_DV].astype(F32))
        zz = _silu(z_ref[:, h * A_DV:(h + 1) * A_DV].astype(F32))
        out_ref[:, h * A_DV:(h + 1) * A_DV] = ((og * hn) * zz).astype(BF16)
        after_head()

    fprev_scr[...] = gcol_ref[L - 1:L, :]


FOX_ACC_ROWS = B_DH + BF16_ROWS


def _fox_kernel(q_ref, k_ref, v_ref, z_ref, pc_ref, out_ref, vt_scr, acc_scr, rhs_scr, m_scr,
                s2_scr, cm2_scr):
    s_scr = (s2_scr.at[0], s2_scr.at[1])
    cm_scr = (cm2_scr.at[0], cm2_scr.at[1])
    hg = pl.program_id(1)
    qi = pl.program_id(2)
    TQ, TK = FOX_TQ, FOX_TK
    seq = k_ref.shape[0]
    nh = 2 * FOX_G

    @pl.when(qi == 0)
    def _():
        for g in range(FOX_G):
            vt = v_ref[:, g * LANES:(g + 1) * LANES].astype(F32).T
            for hh in range(2):
                vt_scr[2 * g + hh, 0:B_DH, :] = vt[hh * B_DH:(hh + 1) * B_DH, :].astype(BF16)
                vt_scr[2 * g + hh, B_DH:FOX_ACC_ROWS, :] = jnp.ones((BF16_ROWS, seq), BF16)

    row = lax.broadcasted_iota(jnp.int32, (LANES, TQ), 0)
    for g in range(FOX_G):
        qt = (q_ref[:, g * LANES:(g + 1) * LANES].astype(F32) * (B_DH ** -0.5 * LOG2E)).T
        for hh in range(2):
            h = 2 * g + hh
            head = hg * nh + h
            qm = jnp.where((row >= hh * B_DH) & (row < (hh + 1) * B_DH), qt, 0.0)
            sel = jnp.where((row == PIECE_OFFS[0] + head) | (row == PIECE_OFFS[1] + head)
                            | (row == PIECE_OFFS[2] + head), 1.0, 0.0)
            rhs_scr[h, 0:LANES, :] = qm.astype(BF16)
            rhs_scr[h, LANES:2 * LANES, :] = sel.astype(BF16)

    acc_scr[...] = jnp.zeros_like(acc_scr)
    m_scr[...] = jnp.full(m_scr.shape, NEG_BIG, F32)

    def key_block(kj):
        k0 = pl.multiple_of(kj * TK, TK)
        pcs = pc_ref[pl.ds(k0, TK), :]
        return [jnp.concatenate([k_ref[pl.ds(k0, TK), g * LANES:(g + 1) * LANES], pcs], axis=1)
                for g in range(FOX_G)]

    def scores_head(h, lhs, slot):
        s = jnp.dot(lhs[h // 2], rhs_scr[h], preferred_element_type=F32)
        s_scr[slot][h] = s
        cm_scr[slot][h] = jnp.broadcast_to(jnp.max(s, axis=0, keepdims=True), (SUBLANES, TQ))

    def scores(kj, slot):
        lhs = key_block(kj)
        for h in range(nh):
            scores_head(h, lhs, slot)

    def softmax_pv(h, kj, s, cmax, lo):
        k0 = pl.multiple_of(kj * TK, TK)
        m_old = m_scr[h, 0:1, lo:TQ]
        m_new = jnp.maximum(m_old, cmax)
        alpha = jnp.exp2(m_old - m_new)
        p = jnp.exp2(s - m_new).astype(BF16)
        pv = jnp.dot(vt_scr[h, :, pl.ds(k0, TK)], p, preferred_element_type=F32)
        acc_scr[h, :, lo:TQ] = alpha * acc_scr[h, :, lo:TQ] + pv
        m_scr[h, :, lo:TQ] = jnp.broadcast_to(m_new, (SUBLANES, TQ - lo))

    def overlapped(kj_next, slot_next, kj, slot):
        lhs = key_block(kj_next)
        for h in range(nh):
            scores_head(h, lhs, slot_next)
            softmax_pv(h, kj, s_scr[slot][h], cm_scr[slot][h, 0:1, :], 0)

    def pair(i, carry):
        overlapped(2 * i + 1, 1, 2 * i, 0)
        overlapped(2 * i + 2, 0, 2 * i + 1, 1)
        return carry

    scores(0, 0)
    lax.fori_loop(0, qi, pair, 0)

    half = TQ - TK
    lhs_b = key_block(2 * qi + 1)
    r = lax.broadcasted_iota(jnp.int32, (TK, TQ), 0)
    c = lax.broadcasted_iota(jnp.int32, (TK, TQ), 1)
    rb = lax.broadcasted_iota(jnp.int32, (TK, TK), 0)
    cb = lax.broadcasted_iota(jnp.int32, (TK, TK), 1)
    for h in range(nh):
        sb = jnp.dot(lhs_b[h // 2], rhs_scr[h, :, half:TQ], preferred_element_type=F32)
        s_scr[1][h, :, half:TQ] = jnp.where(cb >= rb, sb, NEG_BIG)
        s = jnp.where(c >= r, s_scr[0][h], NEG_BIG)
        softmax_pv(h, 2 * qi, s, jnp.max(s, axis=0, keepdims=True), 0)

    for g in range(FOX_G):
        parts = []
        for hh in range(2):
            h = 2 * g + hh
            s = s_scr[1][h, :, half:TQ]
            softmax_pv(h, 2 * qi + 1, s, jnp.max(s, axis=0, keepdims=True), half)
            a = acc_scr[h]
            parts.append(a[0:B_DH, :] * (1.0 / a[B_DH:B_DH + 1, :]))
        o = jnp.concatenate(parts, axis=0).T
        zz = _silu(z_ref[:, g * LANES:(g + 1) * LANES].astype(F32))
        out_ref[:, g * LANES:(g + 1) * LANES] = (o * zz).astype(BF16)


def _fox(proj, pieces, bsz, seq):
    m = proj.shape[0]
    nq = seq // FOX_TQ
    nh = 2 * FOX_G
    w = FOX_G * LANES
    ngrp = (B_HEADS * B_DH) // w
    sec = SECTION_W // w
    col0 = SEC_BQ * sec
    assert (SEC_BK, SEC_BV, SEC_BZ) == (SEC_BQ + 1, SEC_BQ + 2, SEC_BQ + 3)
    return pl.pallas_call(
        _fox_kernel,
        grid=(bsz, ngrp, nq),
        in_specs=[
            pl.BlockSpec((FOX_TQ, w), lambda b, hg, qi: (b * nq + qi, col0 + hg)),
            pl.BlockSpec((seq, w), lambda b, hg, qi: (b, col0 + sec + hg)),
            pl.BlockSpec((seq, w), lambda b, hg, qi: (b, col0 + 2 * sec + hg)),
            pl.BlockSpec((FOX_TQ, w), lambda b, hg, qi: (b * nq + qi, col0 + 3 * sec + hg)),
            pl.BlockSpec((None, seq, LANES), lambda b, hg, qi: (b, 0, 0)),
        ],
        out_specs=pl.BlockSpec((FOX_TQ, w), lambda b, hg, qi: (b * nq + qi, hg)),
        out_shape=jax.ShapeDtypeStruct((m, B_HEADS * B_DH), BF16),
        scratch_shapes=[
            pltpu.VMEM((nh, FOX_ACC_ROWS, seq), BF16),
            pltpu.VMEM((nh, FOX_ACC_ROWS, FOX_TQ), F32),
            pltpu.VMEM((nh, 2 * LANES, FOX_TQ), BF16),
            pltpu.VMEM((nh, SUBLANES, FOX_TQ), F32),
            pltpu.VMEM((2, nh, FOX_TK, FOX_TQ), F32),
            pltpu.VMEM((2, nh, SUBLANES, FOX_TQ), F32),
        ],
        compiler_params=pltpu.CompilerParams(
            dimension_semantics=("arbitrary", "arbitrary", "arbitrary"),
            vmem_limit_bytes=VMEM_LIMIT),
        name="fox",
    )(proj, proj, proj, proj, pieces)


def _merge_stages(ha_ref, hb_ref, ga_ref, gb_ref, x_ref, p_ref, wa_ref, wb_ref, wo_ref, wg_ref,
                  wp_ref, png_ref, fng_ref, out_ref):
    ya = jnp.dot(ha_ref[...], wa_ref[...], preferred_element_type=F32)
    yb = jnp.dot(hb_ref[...], wb_ref[...], preferred_element_type=F32)
    yield
    merged = (_sigmoid(ga_ref[...].astype(F32)) * ya + _sigmoid(gb_ref[...].astype(F32)) * yb)
    x1 = x_ref[...] + jnp.dot(merged.astype(BF16), wo_ref[...], preferred_element_type=F32)
    yield
    r = _rms_norm(x1, png_ref[...]).astype(BF16)
    gate = _sigmoid(jnp.dot(r, wg_ref[...], preferred_element_type=F32))
    yield
    pp = jnp.dot(p_ref[...].astype(BF16), wp_ref[...], preferred_element_type=F32)
    x2 = x1 + gate * pp
    out_ref[...] = _rms_norm(x2, fng_ref[...])
    yield


def _mlstm_merge_kernel(nt, n_chunks,
                        qk_ref, v_ref, o_ref, z_ref, gcol_ref, grow_ref, hg_ref,
                        hb_ref, ga_ref, gb_ref, x_ref, p_ref, wa_ref, wb_ref, wo_ref, wg_ref,
                        wp_ref, png_ref, fng_ref, out_ref,
                        ha_scr, c_scr, n_scr, m_scr, fprev_scr,
                        wa_scr, wb_scr, wo_scr, wg_scr, wp_scr):
    s = pl.program_id(0)
    chunk = jnp.minimum(s, n_chunks - 1)
    state = (c_scr, n_scr, m_scr, fprev_scr)

    @pl.when(s == 0)
    def _():
        ha_scr[...] = jnp.zeros_like(ha_scr)
        for src, dst in ((wa_ref, wa_scr), (wb_ref, wb_scr), (wo_ref, wo_scr), (wg_ref, wg_scr),
                         (wp_ref, wp_scr)):
            dst[...] = src[...].astype(BF16)

    @pl.when(chunk % nt == 0)
    def _():
        _mlstm_reset(*state)

    stages = _merge_stages(ha_scr, hb_ref, ga_ref, gb_ref, x_ref, p_ref, wa_scr, wb_scr, wo_scr,
                           wg_scr, wp_scr, png_ref, fng_ref, out_ref)
    next(stages)
    heads_done = []

    def after_head():
        heads_done.append(None)
        if len(heads_done) > MERGE_LAG:
            next(stages, None)

    _mlstm_chunk(qk_ref, v_ref, o_ref, z_ref, gcol_ref, grow_ref, hg_ref,
                 ha_scr, *state, after_head=after_head)
    for _ in stages:
        pass


def _mlstm_merge(proj, qk_act, gcol, grow, head_g, hb, x2, p2, wa, wb, wo, wg, wp,
                 png, fng, bsz, seq):
    m, d = x2.shape
    pd = p2.shape[1]
    width = SECTION_W
    nt = seq // A_CHUNK
    n_chunks = bsz * nt
    cur = lambda s: jnp.minimum(s, n_chunks - 1)
    prev = lambda s: jnp.maximum(s - 1, 0)
    a_blk = lambda sec: pl.BlockSpec((A_CHUNK, width), lambda s: (cur(s), sec))
    m_blk = lambda w, sec: pl.BlockSpec((A_CHUNK, w), lambda s: (prev(s), sec))
    full = lambda r, c: pl.BlockSpec((r, c), lambda s: (0, 0))
    once = lambda r, c: pl.BlockSpec((r, c), lambda s: (0, 0), pipeline_mode=pl.Buffered(1))
    return pl.pallas_call(
        functools.partial(_mlstm_merge_kernel, nt, n_chunks),
        grid=(n_chunks + 1,),
        in_specs=[
            a_blk(0), a_blk(SEC_AV), a_blk(SEC_AO), a_blk(SEC_AZ),
            pl.BlockSpec((None, A_CHUNK, LANES), lambda s: (cur(s) // nt, cur(s) % nt, 0)),
            pl.BlockSpec((None, SUBLANES, A_CHUNK), lambda s: (cur(s) // nt, 0, cur(s) % nt)),
            full(1, width),
            m_blk(d, 0), m_blk(d, SEC_GA), m_blk(d, SEC_GB), m_blk(d, 0), m_blk(pd, 0),
            once(d, d), once(d, d), once(d, d), once(d, d), once(pd, d),
            full(1, d), full(1, d),
        ],
        out_specs=pl.BlockSpec((A_CHUNK, d), lambda s: (prev(s), 0)),
        out_shape=jax.ShapeDtypeStruct((m, d), F32),
        scratch_shapes=[
            pltpu.VMEM((A_CHUNK, width), BF16),
            pltpu.VMEM((A_HEADS, A_DQK, A_DV), F32),
            pltpu.VMEM((A_HEADS, 1, A_DQK), F32),
            pltpu.VMEM((A_HEADS, SUBLANES, LANES), F32),
            pltpu.VMEM((1, LANES), F32),
            pltpu.VMEM((d, d), BF16), pltpu.VMEM((d, d), BF16), pltpu.VMEM((d, d), BF16),
            pltpu.VMEM((d, d), BF16), pltpu.VMEM((pd, d), BF16),
        ],
        compiler_params=pltpu.CompilerParams(
            dimension_semantics=("arbitrary",), vmem_limit_bytes=VMEM_LIMIT),
        name="mlstm_merge",
    )(qk_act, proj, proj, proj, gcol, grow, head_g,
      hb, proj, proj, x2, p2, wa, wb, wo, wg, wp, png, fng)


def _split_w_in(w):
    qkw = A_HEADS * A_DQK
    aw = A_HEADS * A_DV
    bw = B_HEADS * B_DH
    d = w.shape[0]
    o_ai = 2 * qkw + aw
    o_ao = o_ai + 2 * A_HEADS
    o_bf = o_ao + 2 * aw + 3 * bw
    o_bz = o_bf + B_HEADS
    seg_cols = (o_ai, o_bf - o_ao, w.shape[1] - o_bz)
    shifts = (0, o_ao - o_ai, o_ao - o_ai + o_bz - o_bf)
    assert all(c % IN_TN == 0 for c in seg_cols) and all(s % SUBLANES == 0 for s in shifts)
    seg_tiles = tuple(c // IN_TN for c in seg_cols)
    w_t = w.T
    w_gate_t = jnp.concatenate(
        [w_t[o_ai:o_ao], w_t[o_bf:o_bz], jnp.zeros((LANES - N_GATE, d), w.dtype)], axis=0)
    return w_t, w_gate_t, seg_tiles, shifts


def _layer(x, p_i, attn_norm_g, w_in, conv_w, conv_b, a_bias_i, a_bias_f, a_head_norm_g, b_bias_f,
           w_branch_a, w_branch_b, w_out, ple_norm_g, w_ple_gate, w_ple_proj, out_norm_g):
    bsz, seq, d = x.shape
    m = bsz * seq
    x2 = x.reshape(m, d)
    w_t, w_gate_t, seg_tiles, shifts = _split_w_in(w_in)
    qkw = A_HEADS * A_DQK
    conv_scale = jnp.concatenate([jnp.ones((1, qkw), F32), jnp.full((1, qkw), A_DQK ** -0.5, F32)],
                                 axis=1)
    proj, gates, qk_act = _in_proj(x2, attn_norm_g.reshape(1, d), w_t, w_gate_t, seg_tiles, shifts,
                                   conv_w, conv_b.reshape(1, -1), conv_scale, seq)

    bias = jnp.concatenate([a_bias_i, a_bias_f, b_bias_f, jnp.zeros((LANES - N_GATE,), F32)])
    gcol, grow, pieces = _gates(gates.reshape(bsz, seq, LANES), bias.reshape(1, LANES))

    hb = _fox(proj, pieces, bsz, seq)
    out = _mlstm_merge(proj, qk_act, gcol, grow,
                       a_head_norm_g.reshape(1, -1), hb, x2, p_i.reshape(m, -1),
                       w_branch_a, w_branch_b, w_out, w_ple_gate, w_ple_proj,
                       ple_norm_g.reshape(1, d), out_norm_g.reshape(1, d), bsz, seq)
    return out.reshape(bsz, seq, d)


def kernel(x, p, attn_norm_g, w_in, conv_w, conv_b, a_bias_i, a_bias_f, a_head_norm_g, b_bias_f,
           w_branch_a, w_branch_b, w_out, ple_norm_g, w_ple_gate, w_ple_proj, final_norm_g):
    depth = w_in.shape[0]
    assert depth == 1, "the final norm is fused into the single layer's merge kernel"
    return _layer(x, p[0], attn_norm_g[0], w_in[0], conv_w[0], conv_b[0], a_bias_i[0], a_bias_f[0],
                  a_head_norm_g[0], b_bias_f[0], w_branch_a[0], w_branch_b[0], w_out[0],
                  ple_norm_g[0], w_ple_gate[0], w_ple_proj[0], final_norm_g)
```

```python
import functools
import math

import jax
import jax.numpy as jnp
from jax import lax
from jax.experimental import pallas as pl
from jax.experimental.pallas import tpu as pltpu

F32 = jnp.float32
BF16 = jnp.bfloat16

EPS = 1e-6
A_HEADS = 4
A_DQK = 128
A_DV = 256
CONV_K = 4
B_HEADS = 16
B_DH = 64
LANES = 128
SUBLANES = 8
BF16_ROWS = 16
NEG_BIG = -1e30
LOG2E = math.log2(math.e)

IN_TM = 2048
IN_TN = 1024
IN_CHUNKS = 8
CUM_BLK = 256
A_CHUNK = 256
FOX_TK = 256
FOX_TQ = 2 * FOX_TK
FOX_G = 4
MERGE_LAG = 1
V7X_VMEM_BYTES = 64 * 1024 * 1024
VMEM_LIMIT = V7X_VMEM_BYTES * 7 // 8

SECTION_W = A_HEADS * A_DV
SEC_AV, SEC_AO, SEC_AZ, SEC_BQ, SEC_BK, SEC_BV, SEC_BZ, SEC_GA, SEC_GB = range(9)

N_GATE = 2 * A_HEADS + B_HEADS
B_LANE0 = 2 * A_HEADS
PIECE_OFFS = (B_LANE0, B_LANE0 + B_HEADS, B_LANE0 + 2 * B_HEADS)


def _sigmoid(x):
    return 1.0 / (1.0 + jnp.exp2(x * (-LOG2E)))


def _silu(x):
    return x * _sigmoid(x)


def _rms_norm(x, g):
    ms = jnp.mean(x * x, axis=-1, keepdims=True)
    return (x * lax.rsqrt(ms + EPS)) * g


def _split3(x):
    x1 = x.astype(BF16)
    r1 = x - x1.astype(F32)
    x2 = r1.astype(BF16)
    x3 = (r1 - x2.astype(F32)).astype(BF16)
    return x1, x2, x3


_NT = (((1,), (1,)), ((), ()))


def _in_proj_kernel(x_ref, g_ref, wt_ref, wgt_ref, cw_ref, cb_ref, cs_ref,
                    proj_ref, gates_ref, qk_ref, h_scr, qkraw_scr, xpad_scr):
    i = pl.program_id(0)
    j = pl.program_id(1)
    tm = x_ref.shape[0]

    @pl.when((i == 0) & (j == 0))
    def _():
        qkraw_scr[...] = jnp.zeros_like(qkraw_scr)
        xpad_scr[0:SUBLANES, :] = jnp.zeros((SUBLANES, LANES), F32)

    @pl.when(j == 0)
    def _():
        h = _rms_norm(x_ref[...], g_ref[...]).astype(BF16)
        h_scr[...] = h
        gates_ref[...] = lax.dot_general(h, wgt_ref[...].astype(BF16), _NT,
                                         preferred_element_type=F32)

    n_strips = qkraw_scr.shape[1] // LANES
    lane0 = pl.multiple_of(jnp.clip(j - 1, 0, n_strips - 1) * LANES, LANES)
    xpad_scr[SUBLANES:SUBLANES + tm, :] = qkraw_scr[:, pl.ds(lane0, LANES)].astype(F32)

    w = wt_ref[...].astype(BF16)
    ch = tm // IN_CHUNKS
    for c in range(IN_CHUNKS):
        r0 = c * ch
        proj_ref[r0:r0 + ch, :] = lax.dot_general(
            h_scr[r0:r0 + ch, :], w, _NT, preferred_element_type=F32).astype(BF16)
        y = cb_ref[...]
        for d in range(CONV_K):
            y = y + (xpad_scr[SUBLANES + r0 - d:SUBLANES + r0 - d + ch, :]
                     * cw_ref[CONV_K - 1 - d:CONV_K - d, :])
        qk_ref[r0:r0 + ch, :] = (_silu(y) * cs_ref[...]).astype(BF16)

    @pl.when(j == 0)
    def _():
        qkraw_scr[...] = proj_ref[...]


def _in_proj(x2, g, w_t, w_gate_t, seg_tiles, shifts, conv_w, conv_b, conv_scale, seq):
    m, d = x2.shape
    n = sum(seg_tiles) * IN_TN
    n_tiles = n // IN_TN
    n_strips = SECTION_W // LANES
    assert IN_TM == seq and IN_TN == SECTION_W and n_tiles > n_strips
    strip = lambda i, j: (0, jnp.clip(j - 1, 0, n_strips - 1))

    def w_rows(i, j):
        shift = shifts[0] // SUBLANES
        lo = 0
        for n_tiles, s in zip(seg_tiles[:-1], shifts[1:]):
            lo += n_tiles
            shift = jnp.where(j >= lo, s // SUBLANES, shift)
        return ((j * (IN_TN // SUBLANES) + shift) * SUBLANES, 0)

    return pl.pallas_call(
        _in_proj_kernel,
        grid=(m // IN_TM, n // IN_TN),
        in_specs=[
            pl.BlockSpec((IN_TM, d), lambda i, j: (i, 0)),
            pl.BlockSpec((1, d), lambda i, j: (0, 0)),
            pl.BlockSpec((pl.Element(IN_TN), pl.Element(d)), w_rows),
            pl.BlockSpec((LANES, d), lambda i, j: (0, 0)),
            pl.BlockSpec((CONV_K, LANES), strip),
            pl.BlockSpec((1, LANES), strip),
            pl.BlockSpec((1, LANES), strip),
        ],
        out_specs=[
            pl.BlockSpec((IN_TM, IN_TN), lambda i, j: (i, jnp.maximum(j - 1, 0))),
            pl.BlockSpec((IN_TM, LANES), lambda i, j: (i, 0)),
            pl.BlockSpec((IN_TM, LANES), lambda i, j: (i, strip(i, j)[1])),
        ],
        out_shape=[
            jax.ShapeDtypeStruct((m, n - IN_TN), BF16),
            jax.ShapeDtypeStruct((m, LANES), F32),
            jax.ShapeDtypeStruct((m, SECTION_W), BF16),
        ],
        scratch_shapes=[pltpu.VMEM((IN_TM, d), BF16),
                        pltpu.VMEM((IN_TM, SECTION_W), BF16),
                        pltpu.VMEM((IN_TM + SUBLANES, LANES), F32)],
        compiler_params=pltpu.CompilerParams(
            dimension_semantics=("arbitrary", "arbitrary"),
            vmem_limit_bytes=VMEM_LIMIT),
        name="in_proj",
    )(x2, g, w_t, w_gate_t, conv_w, conv_b, conv_scale)


def _gates_kernel(g_ref, bias_ref, col_ref, row_ref, pc_ref):
    x = g_ref[...] + bias_ref[...]
    s = x.shape[0]
    ls = jnp.minimum(x, 0.0) - jnp.log1p(jnp.exp(-jnp.abs(x)))
    r = lax.broadcasted_iota(jnp.int32, (CUM_BLK, CUM_BLK), 0)
    c = lax.broadcasted_iota(jnp.int32, (CUM_BLK, CUM_BLK), 1)
    tri = jnp.where(r >= c, 1.0, 0.0).astype(BF16)
    carry = jnp.zeros((1, LANES), F32)
    blocks = []
    for blk in range(s // CUM_BLK):
        x1, x2, x3 = _split3(ls[blk * CUM_BLK:(blk + 1) * CUM_BLK])
        cs = (jnp.dot(tri, x3, preferred_element_type=F32)
              + jnp.dot(tri, x2, preferred_element_type=F32)
              + jnp.dot(tri, x1, preferred_element_type=F32)) + carry
        carry = cs[CUM_BLK - 1:CUM_BLK, :]
        blocks.append(cs)
    cum = jnp.concatenate(blocks, axis=0)
    lane = lax.broadcasted_iota(jnp.int32, x.shape, 1)
    res = jnp.where(lane < A_HEADS, x, cum)
    res = res * LOG2E
    col_ref[...] = res
    row_ref[...] = res.T[0:SUBLANES, :]

    in_b = (lane >= B_LANE0) & (lane < B_LANE0 + B_HEADS)
    p1, p2, p3 = _split3(jnp.where(in_b, cum * (-LOG2E), 0.0))
    pieces = (p1.astype(F32)
              + pltpu.roll(p2.astype(F32), PIECE_OFFS[1] - B_LANE0, axis=1)
              + pltpu.roll(p3.astype(F32), PIECE_OFFS[2] - B_LANE0, axis=1))
    pc_ref[...] = pieces.astype(BF16)


def _gates(gates3, bias):
    b, s, _ = gates3.shape
    return pl.pallas_call(
        _gates_kernel,
        grid=(b,),
        in_specs=[
            pl.BlockSpec((None, s, LANES), lambda i: (i, 0, 0)),
            pl.BlockSpec((1, LANES), lambda i: (0, 0)),
        ],
        out_specs=[
            pl.BlockSpec((None, s, LANES), lambda i: (i, 0, 0)),
            pl.BlockSpec((None, SUBLANES, s), lambda i: (i, 0, 0)),
            pl.BlockSpec((None, s, LANES), lambda i: (i, 0, 0)),
        ],
        out_shape=[
            jax.ShapeDtypeStruct((b, s, LANES), F32),
            jax.ShapeDtypeStruct((b, SUBLANES, s), F32),
            jax.ShapeDtypeStruct((b, s, LANES), BF16),
        ],
        compiler_params=pltpu.CompilerParams(
            dimension_semantics=("arbitrary",), vmem_limit_bytes=VMEM_LIMIT),
        name="gates",
    )(gates3, bias)


def _mlstm_reset(c_scr, n_scr, m_scr, fprev_scr):
    c_scr[...] = jnp.zeros_like(c_scr)
    n_scr[...] = jnp.zeros_like(n_scr)
    m_scr[...] = jnp.zeros_like(m_scr)
    fprev_scr[...] = jnp.zeros_like(fprev_scr)


def _mlstm_chunk(qk_ref, v_ref, o_ref, z_ref, gcol_ref, grow_ref, hg_ref,
                 out_ref, c_scr, n_scr, m_scr, fprev_scr, after_head=lambda: None):
    L = A_CHUNK
    qkw = A_HEADS * A_DQK

    row = lax.broadcasted_iota(jnp.int32, (L, L), 0)
    col = lax.broadcasted_iota(jnp.int32, (L, L), 1)
    causal = row >= col

    for h in range(A_HEADS):
        qb = qk_ref[:, h * A_DQK:(h + 1) * A_DQK]
        kb = qk_ref[:, qkw + h * A_DQK:qkw + (h + 1) * A_DQK]
        q = qb.astype(F32)
        k = kb.astype(F32)
        v = v_ref[:, h * A_DV:(h + 1) * A_DV]

        li_c = gcol_ref[:, h:h + 1]
        f_c = gcol_ref[:, A_HEADS + h:A_HEADS + h + 1]
        li_r = grow_ref[h:h + 1, :]
        f_r = grow_ref[A_HEADS + h:A_HEADS + h + 1, :]
        f_prev = fprev_scr[0:1, A_HEADS + h:A_HEADS + h + 1]
        f_end = gcol_ref[L - 1:L, A_HEADS + h:A_HEADS + h + 1]
        m_st = m_scr[h, 0:1, 0:1]
        c_st = c_scr[h]
        n_st = n_scr[h]

        dmat = jnp.where(causal, (f_c - f_r) + li_r, NEG_BIG)
        inter = (f_c - f_prev) + m_st
        m_row = jnp.maximum(inter, jnp.max(dmat, axis=-1, keepdims=True))
        w_intra = jnp.exp2(dmat - m_row)
        w_inter = jnp.exp2(inter - m_row)
        s = lax.dot_general(qb, kb, (((1,), (1,)), ((), ())), preferred_element_type=F32)
        scores = s * w_intra
        num = (jnp.dot(scores.astype(BF16), v, preferred_element_type=F32)
               + w_inter * jnp.dot(qb, c_st.astype(BF16), preferred_element_type=F32))
        den = (jnp.sum(scores, axis=-1, keepdims=True)
               + w_inter * jnp.sum(q * n_st, axis=-1, keepdims=True))
        hh = num * (1.0 / jnp.maximum(jnp.abs(den), jnp.exp2(-m_row)))

        g_tot = f_end - f_prev
        to_end = (f_end - f_c) + li_c
        m_new = jnp.maximum(g_tot + m_st, jnp.max(to_end, axis=0, keepdims=True))
        w_k = jnp.exp2(to_end - m_new)
        decay = jnp.exp2(g_tot + m_st - m_new)
        kw = k * w_k
        c_scr[h] = decay * c_st + jnp.dot(kw.T.astype(BF16), v, preferred_element_type=F32)
        n_scr[h] = decay * n_st + jnp.sum(kw, axis=0, keepdims=True)
        m_scr[h] = jnp.broadcast_to(m_new, (SUBLANES, LANES))

        hn = hh * lax.rsqrt(jnp.mean(hh * hh, axis=-1, keepdims=True) + EPS)
        hn = hn * hg_ref[:, h * A_DV:(h + 1) * A_DV]
        og = _sigmoid(o_ref[:, h * A_DV:(h + 1) * A_DV].astype(F32))
        zz = _silu(z_ref[:, h * A_DV:(h + 1) * A_DV].astype(F32))
        out_ref[:, h * A_DV:(h + 1) * A_DV] = ((og * hn) * zz).astype(BF16)
        after_head()

    fprev_scr[...] = gcol_ref[L - 1:L, :]


FOX_ACC_ROWS = B_DH + BF16_ROWS


def _fox_kernel(q_ref, k_ref, v_ref, z_ref, pc_ref, out_ref, vt_scr, acc_scr, rhs_scr, m_scr,
                s2_scr, cm2_scr):
    s_scr = (s2_scr.at[0], s2_scr.at[1])
    cm_scr = (cm2_scr.at[0], cm2_scr.at[1])
    hg = pl.program_id(1)
    qi = pl.program_id(2)
    TQ, TK = FOX_TQ, FOX_TK
    seq = k_ref.shape[0]
    nh = 2 * FOX_G

    @pl.when(qi == 0)
    def _():
        for g in range(FOX_G):
            vt = v_ref[:, g * LANES:(g + 1) * LANES].astype(F32).T
            for hh in range(2):
                vt_scr[2 * g + hh, 0:B_DH, :] = vt[hh * B_DH:(hh + 1) * B_DH, :].astype(BF16)
                vt_scr[2 * g + hh, B_DH:FOX_ACC_ROWS, :] = jnp.ones((BF16_ROWS, seq), BF16)

    row = lax.broadcasted_iota(jnp.int32, (LANES, TQ), 0)
    for g in range(FOX_G):
        qt = (q_ref[:, g * LANES:(g + 1) * LANES].astype(F32) * (B_DH ** -0.5 * LOG2E)).T
        for hh in range(2):
            h = 2 * g + hh
            head = hg * nh + h
            qm = jnp.where((row >= hh * B_DH) & (row < (hh + 1) * B_DH), qt, 0.0)
            sel = jnp.where((row == PIECE_OFFS[0] + head) | (row == PIECE_OFFS[1] + head)
                            | (row == PIECE_OFFS[2] + head), 1.0, 0.0)
            rhs_scr[h, 0:LANES, :] = qm.astype(BF16)
            rhs_scr[h, LANES:2 * LANES, :] = sel.astype(BF16)

    acc_scr[...] = jnp.zeros_like(acc_scr)
    m_scr[...] = jnp.full(m_scr.shape, NEG_BIG, F32)

    def key_block(kj):
        k0 = pl.multiple_of(kj * TK, TK)
        pcs = pc_ref[pl.ds(k0, TK), :]
        return [jnp.concatenate([k_ref[pl.ds(k0, TK), g * LANES:(g + 1) * LANES], pcs], axis=1)
                for g in range(FOX_G)]

    def scores_head(h, lhs, slot):
        s = jnp.dot(lhs[h // 2], rhs_scr[h], preferred_element_type=F32)
        s_scr[slot][h] = s
        cm_scr[slot][h] = jnp.broadcast_to(jnp.max(s, axis=0, keepdims=True), (SUBLANES, TQ))

    def scores(kj, slot):
        lhs = key_block(kj)
        for h in range(nh):
            scores_head(h, lhs, slot)

    def softmax_pv(h, kj, s, cmax, lo):
        k0 = pl.multiple_of(kj * TK, TK)
        m_old = m_scr[h, 0:1, lo:TQ]
        m_new = jnp.maximum(m_old, cmax)
        alpha = jnp.exp2(m_old - m_new)
        p = jnp.exp2(s - m_new).astype(BF16)
        pv = jnp.dot(vt_scr[h, :, pl.ds(k0, TK)], p, preferred_element_type=F32)
        acc_scr[h, :, lo:TQ] = alpha * acc_scr[h, :, lo:TQ] + pv
        m_scr[h, :, lo:TQ] = jnp.broadcast_to(m_new, (SUBLANES, TQ - lo))

    def overlapped(kj_next, slot_next, kj, slot):
        lhs = key_block(kj_next)
        for h in range(nh):
            scores_head(h, lhs, slot_next)
            softmax_pv(h, kj, s_scr[slot][h], cm_scr[slot][h, 0:1, :], 0)

    def pair(i, carry):
        overlapped(2 * i + 1, 1, 2 * i, 0)
        overlapped(2 * i + 2, 0, 2 * i + 1, 1)
        return carry

    scores(0, 0)
    lax.fori_loop(0, qi, pair, 0)

    half = TQ - TK
    lhs_b = key_block(2 * qi + 1)
    r = lax.broadcasted_iota(jnp.int32, (TK, TQ), 0)
    c = lax.broadcasted_iota(jnp.int32, (TK, TQ), 1)
    rb = lax.broadcasted_iota(jnp.int32, (TK, TK), 0)
    cb = lax.broadcasted_iota(jnp.int32, (TK, TK), 1)
    for h in range(nh):
        sb = jnp.dot(lhs_b[h // 2], rhs_scr[h, :, half:TQ], preferred_element_type=F32)
        s_scr[1][h, :, half:TQ] = jnp.where(cb >= rb, sb, NEG_BIG)
        s = jnp.where(c >= r, s_scr[0][h], NEG_BIG)
        softmax_pv(h, 2 * qi, s, jnp.max(s, axis=0, keepdims=True), 0)

    for g in range(FOX_G):
        parts = []
        for hh in range(2):
            h = 2 * g + hh
            s = s_scr[1][h, :, half:TQ]
            softmax_pv(h, 2 * qi + 1, s, jnp.max(s, axis=0, keepdims=True), half)
            a = acc_scr[h]
            parts.append(a[0:B_DH, :] * (1.0 / a[B_DH:B_DH + 1, :]))
        o = jnp.concatenate(parts, axis=0).T
        zz = _silu(z_ref[:, g * LANES:(g + 1) * LANES].astype(F32))
        out_ref[:, g * LANES:(g + 1) * LANES] = (o * zz).astype(BF16)


def _fox(proj, pieces, bsz, seq):
    m = proj.shape[0]
    nq = seq // FOX_TQ
    nh = 2 * FOX_G
    w = FOX_G * LANES
    ngrp = (B_HEADS * B_DH) // w
    sec = SECTION_W // w
    col0 = SEC_BQ * sec
    assert (SEC_BK, SEC_BV, SEC_BZ) == (SEC_BQ + 1, SEC_BQ + 2, SEC_BQ + 3)
    return pl.pallas_call(
        _fox_kernel,
        grid=(bsz, ngrp, nq),
        in_specs=[
            pl.BlockSpec((FOX_TQ, w), lambda b, hg, qi: (b * nq + qi, col0 + hg)),
            pl.BlockSpec((seq, w), lambda b, hg, qi: (b, col0 + sec + hg)),
            pl.BlockSpec((seq, w), lambda b, hg, qi: (b, col0 + 2 * sec + hg)),
            pl.BlockSpec((FOX_TQ, w), lambda b, hg, qi: (b * nq + qi, col0 + 3 * sec + hg)),
            pl.BlockSpec((None, seq, LANES), lambda b, hg, qi: (b, 0, 0)),
        ],
        out_specs=pl.BlockSpec((FOX_TQ, w), lambda b, hg, qi: (b * nq + qi, hg)),
        out_shape=jax.ShapeDtypeStruct((m, B_HEADS * B_DH), BF16),
        scratch_shapes=[
            pltpu.VMEM((nh, FOX_ACC_ROWS, seq), BF16),
            pltpu.VMEM((nh, FOX_ACC_ROWS, FOX_TQ), F32),
            pltpu.VMEM((nh, 2 * LANES, FOX_TQ), BF16),
            pltpu.VMEM((nh, SUBLANES, FOX_TQ), F32),
            pltpu.VMEM((2, nh, FOX_TK, FOX_TQ), F32),
            pltpu.VMEM((2, nh, SUBLANES, FOX_TQ), F32),
        ],
        compiler_params=pltpu.CompilerParams(
            dimension_semantics=("arbitrary", "arbitrary", "arbitrary"),
            vmem_limit_bytes=VMEM_LIMIT),
        name="fox",
    )(proj, proj, proj, proj, pieces)


def _merge_stages(ha_ref, hb_ref, ga_ref, gb_ref, x_ref, p_ref, wa_ref, wb_ref, wo_ref, wg_ref,
                  wp_ref, png_ref, fng_ref, out_ref):
    ya = jnp.dot(ha_ref[...], wa_ref[...], preferred_element_type=F32)
    yb = jnp.dot(hb_ref[...], wb_ref[...], preferred_element_type=F32)
    yield
    merged = (_sigmoid(ga_ref[...].astype(F32)) * ya + _sigmoid(gb_ref[...].astype(F32)) * yb)
    x1 = x_ref[...] + jnp.dot(merged.astype(BF16), wo_ref[...], preferred_element_type=F32)
    yield
    r = _rms_norm(x1, png_ref[...]).astype(BF16)
    gate = _sigmoid(jnp.dot(r, wg_ref[...], preferred_element_type=F32))
    yield
    pp = jnp.dot(p_ref[...].astype(BF16), wp_ref[...], preferred_element_type=F32)
    x2 = x1 + gate * pp
    out_ref[...] = _rms_norm(x2, fng_ref[...])
    yield


def _mlstm_merge_kernel(nt, n_chunks,
                        qk_ref, v_ref, o_ref, z_ref, gcol_ref, grow_ref, hg_ref,
                        hb_ref, ga_ref, gb_ref, x_ref, p_ref, wa_ref, wb_ref, wo_ref, wg_ref,
                        wp_ref, png_ref, fng_ref, out_ref,
                        ha_scr, c_scr, n_scr, m_scr, fprev_scr,
                        wa_scr, wb_scr, wo_scr, wg_scr, wp_scr):
    s = pl.program_id(0)
    chunk = jnp.minimum(s, n_chunks - 1)
    state = (c_scr, n_scr, m_scr, fprev_scr)

    @pl.when(s == 0)
    def _():
        ha_scr[...] = jnp.zeros_like(ha_scr)
        for src, dst in ((wa_ref, wa_scr), (wb_ref, wb_scr), (wo_ref, wo_scr), (wg_ref, wg_scr),
                         (wp_ref, wp_scr)):
            dst[...] = src[...].astype(BF16)

    @pl.when(chunk % nt == 0)
    def _():
        _mlstm_reset(*state)

    stages = _merge_stages(ha_scr, hb_ref, ga_ref, gb_ref, x_ref, p_ref, wa_scr, wb_scr, wo_scr,
                           wg_scr, wp_scr, png_ref, fng_ref, out_ref)
    next(stages)
    heads_done = []

    def after_head():
        heads_done.append(None)
        if len(heads_done) > MERGE_LAG:
            next(stages, None)

    _mlstm_chunk(qk_ref, v_ref, o_ref, z_ref, gcol_ref, grow_ref, hg_ref,
                 ha_scr, *state, after_head=after_head)
    for _ in stages:
        pass


def _mlstm_merge(proj, qk_act, gcol, grow, head_g, hb, x2, p2, wa, wb, wo, wg, wp,
                 png, fng, bsz, seq):
    m, d = x2.shape
    pd = p2.shape[1]
    width = SECTION_W
    nt = seq // A_CHUNK
    n_chunks = bsz * nt
    cur = lambda s: jnp.minimum(s, n_chunks - 1)
    prev = lambda s: jnp.maximum(s - 1, 0)
    a_blk = lambda sec: pl.BlockSpec((A_CHUNK, width), lambda s: (cur(s), sec))
    m_blk = lambda w, sec: pl.BlockSpec((A_CHUNK, w), lambda s: (prev(s), sec))
    full = lambda r, c: pl.BlockSpec((r, c), lambda s: (0, 0))
    once = lambda r, c: pl.BlockSpec((r, c), lambda s: (0, 0), pipeline_mode=pl.Buffered(1))
    return pl.pallas_call(
        functools.partial(_mlstm_merge_kernel, nt, n_chunks),
        grid=(n_chunks + 1,),
        in_specs=[
            a_blk(0), a_blk(SEC_AV), a_blk(SEC_AO), a_blk(SEC_AZ),
            pl.BlockSpec((None, A_CHUNK, LANES), lambda s: (cur(s) // nt, cur(s) % nt, 0)),
            pl.BlockSpec((None, SUBLANES, A_CHUNK), lambda s: (cur(s) // nt, 0, cur(s) % nt)),
            full(1, width),
            m_blk(d, 0), m_blk(d, SEC_GA), m_blk(d, SEC_GB), m_blk(d, 0), m_blk(pd, 0),
            once(d, d), once(d, d), once(d, d), once(d, d), once(pd, d),
            full(1, d), full(1, d),
        ],
        out_specs=pl.BlockSpec((A_CHUNK, d), lambda s: (prev(s), 0)),
        out_shape=jax.ShapeDtypeStruct((m, d), F32),
        scratch_shapes=[
            pltpu.VMEM((A_CHUNK, width), BF16),
            pltpu.VMEM((A_HEADS, A_DQK, A_DV), F32),
            pltpu.VMEM((A_HEADS, 1, A_DQK), F32),
            pltpu.VMEM((A_HEADS, SUBLANES, LANES), F32),
            pltpu.VMEM((1, LANES), F32),
            pltpu.VMEM((d, d), BF16), pltpu.VMEM((d, d), BF16), pltpu.VMEM((d, d), BF16),
            pltpu.VMEM((d, d), BF16), pltpu.VMEM((pd, d), BF16),
        ],
        compiler_params=pltpu.CompilerParams(
            dimension_semantics=("arbitrary",), vmem_limit_bytes=VMEM_LIMIT),
        name="mlstm_merge",
    )(qk_act, proj, proj, proj, gcol, grow, head_g,
      hb, proj, proj, x2, p2, wa, wb, wo, wg, wp, png, fng)


def _split_w_in(w):
    qkw = A_HEADS * A_DQK
    aw = A_HEADS * A_DV
    bw = B_HEADS * B_DH
    d = w.shape[0]
    o_ai = 2 * qkw + aw
    o_ao = o_ai + 2 * A_HEADS
    o_bf = o_ao + 2 * aw + 3 * bw
    o_bz = o_bf + B_HEADS
    seg_cols = (o_ai, o_bf - o_ao, w.shape[1] - o_bz)
    shifts = (0, o_ao - o_ai, o_ao - o_ai + o_bz - o_bf)
    assert all(c % IN_TN == 0 for c in seg_cols) and all(s % SUBLANES == 0 for s in shifts)
    seg_tiles = tuple(c // IN_TN for c in seg_cols)
    w_t = w.T
    w_gate_t = jnp.concatenate(
        [w_t[o_ai:o_ao], w_t[o_bf:o_bz], jnp.zeros((LANES - N_GATE, d), w.dtype)], axis=0)
    return w_t, w_gate_t, seg_tiles, shifts


def _layer(x, p_i, attn_norm_g, w_in, conv_w, conv_b, a_bias_i, a_bias_f, a_head_norm_g, b_bias_f,
           w_branch_a, w_branch_b, w_out, ple_norm_g, w_ple_gate, w_ple_proj, out_norm_g):
    bsz, seq, d = x.shape
    m = bsz * seq
    x2 = x.reshape(m, d)
    w_t, w_gate_t, seg_tiles, shifts = _split_w_in(w_in)
    qkw = A_HEADS * A_DQK
    conv_scale = jnp.concatenate([jnp.ones((1, qkw), F32), jnp.full((1, qkw), A_DQK ** -0.5, F32)],
                                 axis=1)
    proj, gates, qk_act = _in_proj(x2, attn_norm_g.reshape(1, d), w_t, w_gate_t, seg_tiles, shifts,
                                   conv_w, conv_b.reshape(1, -1), conv_scale, seq)

    bias = jnp.concatenate([a_bias_i, a_bias_f, b_bias_f, jnp.zeros((LANES - N_GATE,), F32)])
    gcol, grow, pieces = _gates(gates.reshape(bsz, seq, LANES), bias.reshape(1, LANES))

    hb = _fox(proj, pieces, bsz, seq)
    out = _mlstm_merge(proj, qk_act, gcol, grow,
                       a_head_norm_g.reshape(1, -1), hb, x2, p_i.reshape(m, -1),
                       w_branch_a, w_branch_b, w_out, w_ple_gate, w_ple_proj,
                       ple_norm_g.reshape(1, d), out_norm_g.reshape(1, d), bsz, seq)
    return out.reshape(bsz, seq, d)


def kernel(x, p, attn_norm_g, w_in, conv_w, conv_b, a_bias_i, a_bias_f, a_head_norm_g, b_bias_f,
           w_branch_a, w_branch_b, w_out, ple_norm_g, w_ple_gate, w_ple_proj, final_norm_g):
    depth = w_in.shape[0]
    assert depth == 1, "the final norm is fused into the single layer's merge kernel"
    return _layer(x, p[0], attn_norm_g[0], w_in[0], conv_w[0], conv_b[0], a_bias_i[0], a_bias_f[0],
                  a_head_norm_g[0], b_bias_f[0], w_branch_a[0], w_branch_b[0], w_out[0],
                  ple_norm_g[0], w_ple_gate[0], w_ple_proj[0], final_norm_g)
```

```python
import functools
import math

import jax
import jax.numpy as jnp
from jax import lax
from jax.experimental import pallas as pl
from jax.experimental.pallas import tpu as pltpu

F32 = jnp.float32
BF16 = jnp.bfloat16

EPS = 1e-6
A_HEADS = 4
A_DQK = 128
A_DV = 256
CONV_K = 4
B_HEADS = 16
B_DH = 64
LANES = 128
SUBLANES = 8
BF16_ROWS = 16
NEG_BIG = -1e30
LOG2E = math.log2(math.e)

IN_TM = 2048
IN_TN = 1024
IN_CHUNKS = 8
CUM_BLK = 256
A_CHUNK = 256
FOX_TK = 256
FOX_TQ = 2 * FOX_TK
FOX_G = 4
GB_COLS = 256
MERGE_LAG = 1
V7X_VMEM_BYTES = 64 * 1024 * 1024
VMEM_LIMIT = V7X_VMEM_BYTES * 7 // 8

SECTION_W = A_HEADS * A_DV
SEC_QK, SEC_AV, SEC_AO, SEC_AZ, SEC_BQ, SEC_BK, SEC_BV, SEC_BZ, SEC_GA = range(9)

N_GATE = 2 * A_HEADS + B_HEADS
B_LANE0 = 2 * A_HEADS
PIECE_OFFS = (B_LANE0, B_LANE0 + B_HEADS, B_LANE0 + 2 * B_HEADS)


def _sigmoid(x):
    return 1.0 / (1.0 + jnp.exp2(x * (-LOG2E)))


def _silu(x):
    return x * _sigmoid(x)


def _rms_norm(x, g):
    ms = jnp.mean(x * x, axis=-1, keepdims=True)
    return (x * lax.rsqrt(ms + EPS)) * g


def _split3(x):
    x1 = x.astype(BF16)
    r1 = x - x1.astype(F32)
    x2 = r1.astype(BF16)
    x3 = (r1 - x2.astype(F32)).astype(BF16)
    return x1, x2, x3


_NT = (((1,), (1,)), ((), ()))


def _in_proj_kernel(x_ref, g_ref, wt_ref, wgt_ref, cw_ref, cb_ref, cs_ref,
                    proj_ref, gates_ref, qk_ref, hout_ref, h_scr, qkraw_scr, xpad_scr):
    i = pl.program_id(0)
    j = pl.program_id(1)
    tm = x_ref.shape[0]

    @pl.when((i == 0) & (j == 0))
    def _():
        qkraw_scr[...] = jnp.zeros_like(qkraw_scr)
        xpad_scr[0:SUBLANES, :] = jnp.zeros((SUBLANES, LANES), F32)

    @pl.when(j == 0)
    def _():
        h = _rms_norm(x_ref[...], g_ref[...]).astype(BF16)
        h_scr[...] = h
        gates_ref[...] = lax.dot_general(h, wgt_ref[...].astype(BF16), _NT,
                                         preferred_element_type=F32)

    n_strips = qkraw_scr.shape[1] // LANES
    lane0 = pl.multiple_of(jnp.clip(j - 1, 0, n_strips - 1) * LANES, LANES)
    xpad_scr[SUBLANES:SUBLANES + tm, :] = qkraw_scr[:, pl.ds(lane0, LANES)].astype(F32)
    hl0 = pl.multiple_of(jnp.minimum(j, h_scr.shape[1] // LANES - 1) * LANES, LANES)
    hout_ref[...] = h_scr[:, pl.ds(hl0, LANES)]

    w = wt_ref[...].astype(BF16)
    ch = tm // IN_CHUNKS
    for c in range(IN_CHUNKS):
        r0 = c * ch
        proj_ref[r0:r0 + ch, :] = lax.dot_general(
            h_scr[r0:r0 + ch, :], w, _NT, preferred_element_type=F32).astype(BF16)
        y = cb_ref[...]
        for d in range(CONV_K):
            y = y + (xpad_scr[SUBLANES + r0 - d:SUBLANES + r0 - d + ch, :]
                     * cw_ref[CONV_K - 1 - d:CONV_K - d, :])
        qk_ref[r0:r0 + ch, :] = (_silu(y) * cs_ref[...]).astype(BF16)

    @pl.when(j == 0)
    def _():
        qkraw_scr[...] = proj_ref[...]


def _in_proj(x2, g, w_t, w_gate_t, seg_tiles, shifts, conv_w, conv_b, conv_scale, seq):
    m, d = x2.shape
    n = sum(seg_tiles) * IN_TN
    n_tiles = n // IN_TN
    n_strips = SECTION_W // LANES
    assert IN_TM == seq and IN_TN == SECTION_W and SEC_QK == 0 and n_tiles > n_strips
    strip = lambda i, j: (0, jnp.clip(j - 1, 0, n_strips - 1))

    def w_rows(i, j):
        shift = shifts[0] // SUBLANES
        lo = 0
        for n_tiles, s in zip(seg_tiles[:-1], shifts[1:]):
            lo += n_tiles
            shift = jnp.where(j >= lo, s // SUBLANES, shift)
        return ((j * (IN_TN // SUBLANES) + shift) * SUBLANES, 0)

    return pl.pallas_call(
        _in_proj_kernel,
        grid=(m // IN_TM, n // IN_TN),
        in_specs=[
            pl.BlockSpec((IN_TM, d), lambda i, j: (i, 0)),
            pl.BlockSpec((1, d), lambda i, j: (0, 0)),
            pl.BlockSpec((pl.Element(IN_TN), pl.Element(d)), w_rows),
            pl.BlockSpec((LANES, d), lambda i, j: (0, 0)),
            pl.BlockSpec((CONV_K, LANES), strip),
            pl.BlockSpec((1, LANES), strip),
            pl.BlockSpec((1, LANES), strip),
        ],
        out_specs=[
            pl.BlockSpec((IN_TM, IN_TN), lambda i, j: (i, j)),
            pl.BlockSpec((IN_TM, LANES), lambda i, j: (i, 0)),
            pl.BlockSpec((IN_TM, LANES), lambda i, j: (i, strip(i, j)[1])),
            pl.BlockSpec((IN_TM, LANES), lambda i, j: (i, jnp.minimum(j, d // LANES - 1))),
        ],
        out_shape=[
            jax.ShapeDtypeStruct((m, n), BF16),
            jax.ShapeDtypeStruct((m, LANES), F32),
            jax.ShapeDtypeStruct((m, SECTION_W), BF16),
            jax.ShapeDtypeStruct((m, d), BF16),
        ],
        scratch_shapes=[pltpu.VMEM((IN_TM, d), BF16),
                        pltpu.VMEM((IN_TM, SECTION_W), BF16),
                        pltpu.VMEM((IN_TM + SUBLANES, LANES), F32)],
        compiler_params=pltpu.CompilerParams(
            dimension_semantics=("arbitrary", "arbitrary"),
            vmem_limit_bytes=VMEM_LIMIT),
        name="in_proj",
    )(x2, g, w_t, w_gate_t, conv_w, conv_b, conv_scale)


def _gates_kernel(g_ref, bias_ref, col_ref, row_ref, pc_ref):
    x = g_ref[...] + bias_ref[...]
    s = x.shape[0]
    ls = jnp.minimum(x, 0.0) - jnp.log1p(jnp.exp(-jnp.abs(x)))
    r = lax.broadcasted_iota(jnp.int32, (CUM_BLK, CUM_BLK), 0)
    c = lax.broadcasted_iota(jnp.int32, (CUM_BLK, CUM_BLK), 1)
    tri = jnp.where(r >= c, 1.0, 0.0).astype(BF16)
    carry = jnp.zeros((1, LANES), F32)
    blocks = []
    for blk in range(s // CUM_BLK):
        x1, x2, x3 = _split3(ls[blk * CUM_BLK:(blk + 1) * CUM_BLK])
        cs = (jnp.dot(tri, x3, preferred_element_type=F32)
              + jnp.dot(tri, x2, preferred_element_type=F32)
              + jnp.dot(tri, x1, preferred_element_type=F32)) + carry
        carry = cs[CUM_BLK - 1:CUM_BLK, :]
        blocks.append(cs)
    cum = jnp.concatenate(blocks, axis=0)
    lane = lax.broadcasted_iota(jnp.int32, x.shape, 1)
    res = jnp.where(lane < A_HEADS, x, cum)
    res = res * LOG2E
    col_ref[...] = res
    row_ref[...] = res.T[0:SUBLANES, :]

    in_b = (lane >= B_LANE0) & (lane < B_LANE0 + B_HEADS)
    p1, p2, p3 = _split3(jnp.where(in_b, cum * (-LOG2E), 0.0))
    pieces = (p1.astype(F32)
              + pltpu.roll(p2.astype(F32), PIECE_OFFS[1] - B_LANE0, axis=1)
              + pltpu.roll(p3.astype(F32), PIECE_OFFS[2] - B_LANE0, axis=1))
    pc_ref[...] = pieces.astype(BF16)


def _gates(gates3, bias):
    b, s, _ = gates3.shape
    return pl.pallas_call(
        _gates_kernel,
        grid=(b,),
        in_specs=[
            pl.BlockSpec((None, s, LANES), lambda i: (i, 0, 0)),
            pl.BlockSpec((1, LANES), lambda i: (0, 0)),
        ],
        out_specs=[
            pl.BlockSpec((None, s, LANES), lambda i: (i, 0, 0)),
            pl.BlockSpec((None, SUBLANES, s), lambda i: (i, 0, 0)),
            pl.BlockSpec((None, s, LANES), lambda i: (i, 0, 0)),
        ],
        out_shape=[
            jax.ShapeDtypeStruct((b, s, LANES), F32),
            jax.ShapeDtypeStruct((b, SUBLANES, s), F32),
            jax.ShapeDtypeStruct((b, s, LANES), BF16),
        ],
        compiler_params=pltpu.CompilerParams(
            dimension_semantics=("arbitrary",), vmem_limit_bytes=VMEM_LIMIT),
        name="gates",
    )(gates3, bias)


def _mlstm_reset(c_scr, n_scr, m_scr, fprev_scr):
    c_scr[...] = jnp.zeros_like(c_scr)
    n_scr[...] = jnp.zeros_like(n_scr)
    m_scr[...] = jnp.zeros_like(m_scr)
    fprev_scr[...] = jnp.zeros_like(fprev_scr)


def _mlstm_chunk(qk_ref, v_ref, o_ref, z_ref, gcol_ref, grow_ref, hg_ref,
                 out_ref, c_scr, n_scr, m_scr, fprev_scr, after_head=lambda: None):
    L = A_CHUNK
    qkw = A_HEADS * A_DQK

    row = lax.broadcasted_iota(jnp.int32, (L, L), 0)
    col = lax.broadcasted_iota(jnp.int32, (L, L), 1)
    causal = row >= col

    for h in range(A_HEADS):
        qb = qk_ref[:, h * A_DQK:(h + 1) * A_DQK]
        kb = qk_ref[:, qkw + h * A_DQK:qkw + (h + 1) * A_DQK]
        q = qb.astype(F32)
        k = kb.astype(F32)
        v = v_ref[:, h * A_DV:(h + 1) * A_DV]

        li_c = gcol_ref[:, h:h + 1]
        f_c = gcol_ref[:, A_HEADS + h:A_HEADS + h + 1]
        li_r = grow_ref[h:h + 1, :]
        f_r = grow_ref[A_HEADS + h:A_HEADS + h + 1, :]
        f_prev = fprev_scr[0:1, A_HEADS + h:A_HEADS + h + 1]
        f_end = gcol_ref[L - 1:L, A_HEADS + h:A_HEADS + h + 1]
        m_st = m_scr[h, 0:1, 0:1]
        c_st = c_scr[h]
        n_st = n_scr[h]

        dmat = jnp.where(causal, (f_c - f_r) + li_r, NEG_BIG)
        inter = (f_c - f_prev) + m_st
        m_row = jnp.maximum(inter, jnp.max(dmat, axis=-1, keepdims=True))
        w_intra = jnp.exp2(dmat - m_row)
        w_inter = jnp.exp2(inter - m_row)
        s = lax.dot_general(qb, kb, (((1,), (1,)), ((), ())), preferred_element_type=F32)
        scores = s * w_intra
        num = (jnp.dot(scores.astype(BF16), v, preferred_element_type=F32)
               + w_inter * jnp.dot(qb, c_st.astype(BF16), preferred_element_type=F32))
        den = (jnp.sum(scores, axis=-1, keepdims=True)
               + w_inter * jnp.sum(q * n_st, axis=-1, keepdims=True))
        hh = num * (1.0 / jnp.maximum(jnp.abs(den), jnp.exp2(-m_row)))

        g_tot = f_end - f_prev
        to_end = (f_end - f_c) + li_c
        m_new = jnp.maximum(g_tot + m_st, jnp.max(to_end, axis=0, keepdims=True))
        w_k = jnp.exp2(to_end - m_new)
        decay = jnp.exp2(g_tot + m_st - m_new)
        kw = k * w_k
        c_scr[h] = decay * c_st + jnp.dot(kw.T.astype(BF16), v, preferred_element_type=F32)
        n_scr[h] = decay * n_st + jnp.sum(kw, axis=0, keepdims=True)
        m_scr[h] = jnp.broadcast_to(m_new, (SUBLANES, LANES))

        hn = hh * lax.rsqrt(jnp.mean(hh * hh, axis=-1, keepdims=True) + EPS)
        hn = hn * hg_ref[:, h * A_DV:(h + 1) * A_DV]
        og = _sigmoid(o_ref[:, h * A_DV:(h + 1) * A_DV].astype(F32))
        zz = _silu(z_ref[:, h * A_DV:(h + 1) * A_DV].astype(F32))
        out_ref[:, h * A_DV:(h + 1) * A_DV] = ((og * hn) * zz).astype(BF16)
        after_head()

    fprev_scr[...] = gcol_ref[L - 1:L, :]


FOX_ACC_ROWS = B_DH + BF16_ROWS


def _fox_kernel(q_ref, k_ref, v_ref, z_ref, pc_ref, hin_ref, wgb_ref, out_ref, gb_ref,
                vt_scr, acc_scr, rhs_scr, m_scr, s2_scr, cm2_scr, wgb_scr):
    s_scr = (s2_scr.at[0], s2_scr.at[1])
    cm_scr = (cm2_scr.at[0], cm2_scr.at[1])
    hg = pl.program_id(1)
    qi = pl.program_id(2)
    TQ, TK = FOX_TQ, FOX_TK
    seq = k_ref.shape[0]
    nh = 2 * FOX_G

    @pl.when(qi == 0)
    def _():
        for g in range(FOX_G):
            vt = v_ref[:, g * LANES:(g + 1) * LANES].astype(F32).T
            for hh in range(2):
                vt_scr[2 * g + hh, 0:B_DH, :] = vt[hh * B_DH:(hh + 1) * B_DH, :].astype(BF16)
                vt_scr[2 * g + hh, B_DH:FOX_ACC_ROWS, :] = jnp.ones((BF16_ROWS, seq), BF16)
        wgb_scr[...] = wgb_ref[...].astype(BF16)

    row = lax.broadcasted_iota(jnp.int32, (LANES, TQ), 0)
    for g in range(FOX_G):
        qt = (q_ref[:, g * LANES:(g + 1) * LANES].astype(F32) * (B_DH ** -0.5 * LOG2E)).T
        for hh in range(2):
            h = 2 * g + hh
            head = hg * nh + h
            qm = jnp.where((row >= hh * B_DH) & (row < (hh + 1) * B_DH), qt, 0.0)
            sel = jnp.where((row == PIECE_OFFS[0] + head) | (row == PIECE_OFFS[1] + head)
                            | (row == PIECE_OFFS[2] + head), 1.0, 0.0)
            rhs_scr[h, 0:LANES, :] = qm.astype(BF16)
            rhs_scr[h, LANES:2 * LANES, :] = sel.astype(BF16)

    acc_scr[...] = jnp.zeros_like(acc_scr)
    m_scr[...] = jnp.full(m_scr.shape, NEG_BIG, F32)

    def key_block(kj):
        k0 = pl.multiple_of(kj * TK, TK)
        pcs = pc_ref[pl.ds(k0, TK), :]
        return [jnp.concatenate([k_ref[pl.ds(k0, TK), g * LANES:(g + 1) * LANES], pcs], axis=1)
                for g in range(FOX_G)]

    def scores_head(h, lhs, slot):
        s = jnp.dot(lhs[h // 2], rhs_scr[h], preferred_element_type=F32)
        s_scr[slot][h] = s
        cm_scr[slot][h] = jnp.broadcast_to(jnp.max(s, axis=0, keepdims=True), (SUBLANES, TQ))

    def scores(kj, slot):
        lhs = key_block(kj)
        for h in range(nh):
            scores_head(h, lhs, slot)

    def softmax_pv(h, kj, s, cmax, lo):
        k0 = pl.multiple_of(kj * TK, TK)
        m_old = m_scr[h, 0:1, lo:TQ]
        m_new = jnp.maximum(m_old, cmax)
        alpha = jnp.exp2(m_old - m_new)
        p = jnp.exp2(s - m_new).astype(BF16)
        pv = jnp.dot(vt_scr[h, :, pl.ds(k0, TK)], p, preferred_element_type=F32)
        acc_scr[h, :, lo:TQ] = alpha * acc_scr[h, :, lo:TQ] + pv
        m_scr[h, :, lo:TQ] = jnp.broadcast_to(m_new, (SUBLANES, TQ - lo))

    def overlapped(kj_next, slot_next, kj, slot):
        lhs = key_block(kj_next)
        for h in range(nh):
            scores_head(h, lhs, slot_next)
            softmax_pv(h, kj, s_scr[slot][h], cm_scr[slot][h, 0:1, :], 0)

    def pair(i, carry):
        overlapped(2 * i + 1, 1, 2 * i, 0)
        overlapped(2 * i + 2, 0, 2 * i + 1, 1)
        return carry

    scores(0, 0)
    lax.fori_loop(0, qi, pair, 0)

    half = TQ - TK
    lhs_b = key_block(2 * qi + 1)
    r = lax.broadcasted_iota(jnp.int32, (TK, TQ), 0)
    c = lax.broadcasted_iota(jnp.int32, (TK, TQ), 1)
    rb = lax.broadcasted_iota(jnp.int32, (TK, TK), 0)
    cb = lax.broadcasted_iota(jnp.int32, (TK, TK), 1)
    for h in range(nh):
        sb = jnp.dot(lhs_b[h // 2], rhs_scr[h, :, half:TQ], preferred_element_type=F32)
        s_scr[1][h, :, half:TQ] = jnp.where(cb >= rb, sb, NEG_BIG)
        s = jnp.where(c >= r, s_scr[0][h], NEG_BIG)
        softmax_pv(h, 2 * qi, s, jnp.max(s, axis=0, keepdims=True), 0)
        if h % 2 == 1 and h // 2 < gb_ref.shape[1] // GB_COLS:
            c0 = (h // 2) * GB_COLS
            gb_ref[:, c0:c0 + GB_COLS] = lax.dot_general(
                hin_ref[...], wgb_scr[c0:c0 + GB_COLS, :], _NT,
                preferred_element_type=F32).astype(BF16)

    for g in range(FOX_G):
        parts = []
        for hh in range(2):
            h = 2 * g + hh
            s = s_scr[1][h, :, half:TQ]
            softmax_pv(h, 2 * qi + 1, s, jnp.max(s, axis=0, keepdims=True), half)
            a = acc_scr[h]
            parts.append(a[0:B_DH, :] * (1.0 / a[B_DH:B_DH + 1, :]))
        o = jnp.concatenate(parts, axis=0).T
        zz = _silu(z_ref[:, g * LANES:(g + 1) * LANES].astype(F32))
        out_ref[:, g * LANES:(g + 1) * LANES] = (o * zz).astype(BF16)


def _fox(proj, pieces, h_in, w_gb_t, bsz, seq):
    m = proj.shape[0]
    d = h_in.shape[1]
    assert w_gb_t.shape == (B_HEADS * B_DH, d) and (FOX_G * LANES) % GB_COLS == 0
    nq = seq // FOX_TQ
    nh = 2 * FOX_G
    w = FOX_G * LANES
    ngrp = (B_HEADS * B_DH) // w
    sec = SECTION_W // w
    col0 = SEC_BQ * sec
    assert (SEC_BK, SEC_BV, SEC_BZ) == (SEC_BQ + 1, SEC_BQ + 2, SEC_BQ + 3)
    return pl.pallas_call(
        _fox_kernel,
        grid=(bsz, ngrp, nq),
        in_specs=[
            pl.BlockSpec((FOX_TQ, w), lambda b, hg, qi: (b * nq + qi, col0 + hg)),
            pl.BlockSpec((seq, w), lambda b, hg, qi: (b, col0 + sec + hg)),
            pl.BlockSpec((seq, w), lambda b, hg, qi: (b, col0 + 2 * sec + hg)),
            pl.BlockSpec((FOX_TQ, w), lambda b, hg, qi: (b * nq + qi, col0 + 3 * sec + hg)),
            pl.BlockSpec((None, seq, LANES), lambda b, hg, qi: (b, 0, 0)),
            pl.BlockSpec((FOX_TQ, d), lambda b, hg, qi: (b * nq + qi, 0)),
            pl.BlockSpec((w, d), lambda b, hg, qi: (hg, 0)),
        ],
        out_specs=[pl.BlockSpec((FOX_TQ, w), lambda b, hg, qi: (b * nq + qi, hg)),
                   pl.BlockSpec((FOX_TQ, w), lambda b, hg, qi: (b * nq + qi, hg))],
        out_shape=[jax.ShapeDtypeStruct((m, B_HEADS * B_DH), BF16),
                   jax.ShapeDtypeStruct((m, d), BF16)],
        scratch_shapes=[
            pltpu.VMEM((nh, FOX_ACC_ROWS, seq), BF16),
            pltpu.VMEM((nh, FOX_ACC_ROWS, FOX_TQ), F32),
            pltpu.VMEM((nh, 2 * LANES, FOX_TQ), BF16),
            pltpu.VMEM((nh, SUBLANES, FOX_TQ), F32),
            pltpu.VMEM((2, nh, FOX_TK, FOX_TQ), F32),
            pltpu.VMEM((2, nh, SUBLANES, FOX_TQ), F32),
            pltpu.VMEM((w, d), BF16),
        ],
        compiler_params=pltpu.CompilerParams(
            dimension_semantics=("arbitrary", "arbitrary", "arbitrary"),
            vmem_limit_bytes=VMEM_LIMIT),
        name="fox",
    )(proj, proj, proj, proj, pieces, h_in, w_gb_t)


def _merge_stages(ha_ref, hb_ref, ga_ref, gb_ref, x_ref, p_ref, wa_ref, wb_ref, wo_ref, wg_ref,
                  wp_ref, png_ref, fng_ref, out_ref):
    ya = jnp.dot(ha_ref[...], wa_ref[...], preferred_element_type=F32)
    yb = jnp.dot(hb_ref[...], wb_ref[...], preferred_element_type=F32)
    yield
    merged = (_sigmoid(ga_ref[...].astype(F32)) * ya + _sigmoid(gb_ref[...].astype(F32)) * yb)
    x1 = x_ref[...] + jnp.dot(merged.astype(BF16), wo_ref[...], preferred_element_type=F32)
    yield
    r = _rms_norm(x1, png_ref[...]).astype(BF16)
    gate = _sigmoid(jnp.dot(r, wg_ref[...], preferred_element_type=F32))
    yield
    pp = jnp.dot(p_ref[...].astype(BF16), wp_ref[...], preferred_element_type=F32)
    x2 = x1 + gate * pp
    out_ref[...] = _rms_norm(x2, fng_ref[...])
    yield


def _mlstm_merge_kernel(nt, n_chunks,
                        qk_ref, v_ref, o_ref, z_ref, gcol_ref, grow_ref, hg_ref,
                        hb_ref, ga_ref, gb_ref, x_ref, p_ref, wa_ref, wb_ref, wo_ref, wg_ref,
                        wp_ref, png_ref, fng_ref, out_ref,
                        ha_scr, c_scr, n_scr, m_scr, fprev_scr,
                        wa_scr, wb_scr, wo_scr, wg_scr, wp_scr):
    s = pl.program_id(0)
    chunk = jnp.minimum(s, n_chunks - 1)
    state = (c_scr, n_scr, m_scr, fprev_scr)

    @pl.when(s == 0)
    def _():
        ha_scr[...] = jnp.zeros_like(ha_scr)
        for src, dst in ((wa_ref, wa_scr), (wb_ref, wb_scr), (wo_ref, wo_scr), (wg_ref, wg_scr),
                         (wp_ref, wp_scr)):
            dst[...] = src[...].astype(BF16)

    @pl.when(chunk % nt == 0)
    def _():
        _mlstm_reset(*state)

    stages = _merge_stages(ha_scr, hb_ref, ga_ref, gb_ref, x_ref, p_ref, wa_scr, wb_scr, wo_scr,
                           wg_scr, wp_scr, png_ref, fng_ref, out_ref)
    next(stages)
    heads_done = []

    def after_head():
        heads_done.append(None)
        if len(heads_done) > MERGE_LAG:
            next(stages, None)

    _mlstm_chunk(qk_ref, v_ref, o_ref, z_ref, gcol_ref, grow_ref, hg_ref,
                 ha_scr, *state, after_head=after_head)
    for _ in stages:
        pass


def _mlstm_merge(proj, qk_act, gcol, grow, head_g, hb, g_b, x2, p2, wa, wb, wo, wg, wp,
                 png, fng, bsz, seq):
    m, d = x2.shape
    pd = p2.shape[1]
    width = SECTION_W
    nt = seq // A_CHUNK
    n_chunks = bsz * nt
    cur = lambda s: jnp.minimum(s, n_chunks - 1)
    prev = lambda s: jnp.maximum(s - 1, 0)
    a_blk = lambda sec: pl.BlockSpec((A_CHUNK, width), lambda s: (cur(s), sec))
    m_blk = lambda w, sec: pl.BlockSpec((A_CHUNK, w), lambda s: (prev(s), sec))
    full = lambda r, c: pl.BlockSpec((r, c), lambda s: (0, 0))
    once = lambda r, c: pl.BlockSpec((r, c), lambda s: (0, 0), pipeline_mode=pl.Buffered(1))
    return pl.pallas_call(
        functools.partial(_mlstm_merge_kernel, nt, n_chunks),
        grid=(n_chunks + 1,),
        in_specs=[
            a_blk(0), a_blk(SEC_AV), a_blk(SEC_AO), a_blk(SEC_AZ),
            pl.BlockSpec((None, A_CHUNK, LANES), lambda s: (cur(s) // nt, cur(s) % nt, 0)),
            pl.BlockSpec((None, SUBLANES, A_CHUNK), lambda s: (cur(s) // nt, 0, cur(s) % nt)),
            full(1, width),
            m_blk(d, 0), m_blk(d, SEC_GA), m_blk(d, 0), m_blk(d, 0), m_blk(pd, 0),
            once(d, d), once(d, d), once(d, d), once(d, d), once(pd, d),
            full(1, d), full(1, d),
        ],
        out_specs=pl.BlockSpec((A_CHUNK, d), lambda s: (prev(s), 0)),
        out_shape=jax.ShapeDtypeStruct((m, d), F32),
        scratch_shapes=[
            pltpu.VMEM((A_CHUNK, width), BF16),
            pltpu.VMEM((A_HEADS, A_DQK, A_DV), F32),
            pltpu.VMEM((A_HEADS, 1, A_DQK), F32),
            pltpu.VMEM((A_HEADS, SUBLANES, LANES), F32),
            pltpu.VMEM((1, LANES), F32),
            pltpu.VMEM((d, d), BF16), pltpu.VMEM((d, d), BF16), pltpu.VMEM((d, d), BF16),
            pltpu.VMEM((d, d), BF16), pltpu.VMEM((pd, d), BF16),
        ],
        compiler_params=pltpu.CompilerParams(
            dimension_semantics=("arbitrary",), vmem_limit_bytes=VMEM_LIMIT),
        name="mlstm_merge",
    )(qk_act, proj, proj, proj, gcol, grow, head_g,
      hb, proj, g_b, x2, p2, wa, wb, wo, wg, wp, png, fng)


def _split_w_in(w):
    qkw = A_HEADS * A_DQK
    aw = A_HEADS * A_DV
    bw = B_HEADS * B_DH
    d = w.shape[0]
    o_ai = 2 * qkw + aw
    o_ao = o_ai + 2 * A_HEADS
    o_bf = o_ao + 2 * aw + 3 * bw
    o_bz = o_bf + B_HEADS
    o_gb = w.shape[1] - d
    seg_cols = (o_ai, o_bf - o_ao, o_gb - o_bz)
    shifts = (0, o_ao - o_ai, o_ao - o_ai + o_bz - o_bf)
    assert all(c % IN_TN == 0 for c in seg_cols) and all(s % SUBLANES == 0 for s in shifts)
    seg_tiles = tuple(c // IN_TN for c in seg_cols)
    w_t = w.T
    w_gate_t = jnp.concatenate(
        [w_t[o_ai:o_ao], w_t[o_bf:o_bz], jnp.zeros((LANES - N_GATE, d), w.dtype)], axis=0)
    return w_t, w_gate_t, seg_tiles, shifts, w_t[o_gb:]


def _layer(x, p_i, attn_norm_g, w_in, conv_w, conv_b, a_bias_i, a_bias_f, a_head_norm_g, b_bias_f,
           w_branch_a, w_branch_b, w_out, ple_norm_g, w_ple_gate, w_ple_proj, out_norm_g):
    bsz, seq, d = x.shape
    m = bsz * seq
    x2 = x.reshape(m, d)
    w_t, w_gate_t, seg_tiles, shifts, w_gb_t = _split_w_in(w_in)
    qkw = A_HEADS * A_DQK
    conv_scale = jnp.concatenate([jnp.ones((1, qkw), F32), jnp.full((1, qkw), A_DQK ** -0.5, F32)],
                                 axis=1)
    proj, gates, qk_act, h_in = _in_proj(x2, attn_norm_g.reshape(1, d), w_t, w_gate_t, seg_tiles, shifts,
                                   conv_w, conv_b.reshape(1, -1), conv_scale, seq)

    bias = jnp.concatenate([a_bias_i, a_bias_f, b_bias_f, jnp.zeros((LANES - N_GATE,), F32)])
    gcol, grow, pieces = _gates(gates.reshape(bsz, seq, LANES), bias.reshape(1, LANES))

    hb, g_b = _fox(proj, pieces, h_in, w_gb_t, bsz, seq)
    out = _mlstm_merge(proj, qk_act, gcol, grow,
                       a_head_norm_g.reshape(1, -1), hb, g_b, x2, p_i.reshape(m, -1),
                       w_branch_a, w_branch_b, w_out, w_ple_gate, w_ple_proj,
                       ple_norm_g.reshape(1, d), out_norm_g.reshape(1, d), bsz, seq)
    return out.reshape(bsz, seq, d)


def kernel(x, p, attn_norm_g, w_in, conv_w, conv_b, a_bias_i, a_bias_f, a_head_norm_g, b_bias_f,
           w_branch_a, w_branch_b, w_out, ple_norm_g, w_ple_gate, w_ple_proj, final_norm_g):
    depth = w_in.shape[0]
    assert depth == 1, "the final norm is fused into the single layer's merge kernel"
    return _layer(x, p[0], attn_norm_g[0], w_in[0], conv_w[0], conv_b[0], a_bias_i[0], a_bias_f[0],
                  a_head_norm_g[0], b_bias_f[0], w_branch_a[0], w_branch_b[0], w_out[0],
                  ple_norm_g[0], w_ple_gate[0], w_ple_proj[0], final_norm_g)
```

```python
import functools
import math

import jax
import jax.numpy as jnp
from jax import lax
from jax.experimental import pallas as pl
from jax.experimental.pallas import tpu as pltpu

F32 = jnp.float32
BF16 = jnp.bfloat16

EPS = 1e-6
A_HEADS = 4
A_DQK = 128
A_DV = 256
CONV_K = 4
B_HEADS = 16
B_DH = 64
LANES = 128
SUBLANES = 8
BF16_ROWS = 16
NEG_BIG = -1e30
LOG2E = math.log2(math.e)

IN_TM = 2048
IN_TN = 1024
IN_CHUNKS = 8
CUM_BLK = 256
A_CHUNK = 256
FOX_TK = 256
FOX_TQ = 2 * FOX_TK
FOX_G = 4
MERGE_LAG = 1
V7X_VMEM_BYTES = 64 * 1024 * 1024
VMEM_LIMIT = V7X_VMEM_BYTES * 7 // 8

SECTION_W = A_HEADS * A_DV
SEC_QK, SEC_AV, SEC_AO, SEC_AZ, SEC_BQ, SEC_BK, SEC_BV, SEC_BZ, SEC_GA, SEC_GB = range(10)

N_GATE = 2 * A_HEADS + B_HEADS
B_LANE0 = 2 * A_HEADS
PIECE_OFFS = (B_LANE0, B_LANE0 + B_HEADS, B_LANE0 + 2 * B_HEADS)


def _sigmoid(x):
    return 1.0 / (1.0 + jnp.exp2(x * (-LOG2E)))


def _silu(x):
    return x * _sigmoid(x)


def _rms_norm(x, g):
    ms = jnp.mean(x * x, axis=-1, keepdims=True)
    return (x * lax.rsqrt(ms + EPS)) * g


def _split3(x):
    x1 = x.astype(BF16)
    r1 = x - x1.astype(F32)
    x2 = r1.astype(BF16)
    x3 = (r1 - x2.astype(F32)).astype(BF16)
    return x1, x2, x3


_NT = (((1,), (1,)), ((), ()))


def _in_proj_kernel(x_ref, g_ref, wt_ref, wga_ref, wgb_ref, cw_ref, cb_ref, cs_ref,
                    proj_ref, gates_ref, qk_ref, h_scr, qkraw_scr, xpad_scr):
    i = pl.program_id(0)
    j = pl.program_id(1)
    tm = x_ref.shape[0]

    @pl.when((i == 0) & (j == 0))
    def _():
        qkraw_scr[...] = jnp.zeros_like(qkraw_scr)
        xpad_scr[0:SUBLANES, :] = jnp.zeros((SUBLANES, LANES), F32)

    @pl.when(j == 0)
    def _():
        h = _rms_norm(x_ref[...], g_ref[...]).astype(BF16)
        h_scr[...] = h
        n_gate = wga_ref.shape[0] + wgb_ref.shape[0]
        wg = jnp.concatenate([wga_ref[...], wgb_ref[...],
                              jnp.zeros((LANES - n_gate, wga_ref.shape[1]), F32)], axis=0)
        gates_ref[...] = lax.dot_general(h, wg.astype(BF16), _NT, preferred_element_type=F32)

    n_strips = qkraw_scr.shape[1] // LANES
    lane0 = pl.multiple_of(jnp.clip(j - 1, 0, n_strips - 1) * LANES, LANES)
    xpad_scr[SUBLANES:SUBLANES + tm, :] = qkraw_scr[:, pl.ds(lane0, LANES)].astype(F32)

    w = wt_ref[...].astype(BF16)
    ch = tm // IN_CHUNKS
    for c in range(IN_CHUNKS):
        r0 = c * ch
        proj_ref[r0:r0 + ch, :] = lax.dot_general(
            h_scr[r0:r0 + ch, :], w, _NT, preferred_element_type=F32).astype(BF16)
        y = cb_ref[...]
        for d in range(CONV_K):
            y = y + (xpad_scr[SUBLANES + r0 - d:SUBLANES + r0 - d + ch, :]
                     * cw_ref[CONV_K - 1 - d:CONV_K - d, :])
        qk_ref[r0:r0 + ch, :] = (_silu(y) * cs_ref[...]).astype(BF16)

    @pl.when(j == 0)
    def _():
        qkraw_scr[...] = proj_ref[...]


def _in_proj(x2, g, w_t, gate_rows, seg_tiles, shifts, conv_w, conv_b, conv_scale, seq):
    m, d = x2.shape
    n = sum(seg_tiles) * IN_TN
    n_tiles = n // IN_TN
    n_strips = SECTION_W // LANES
    assert IN_TM == seq and IN_TN == SECTION_W and SEC_QK == 0 and n_tiles > n_strips
    strip = lambda i, j: (0, jnp.clip(j - 1, 0, n_strips - 1))

    def w_rows(i, j):
        shift = shifts[0] // SUBLANES
        lo = 0
        for n_tiles, s in zip(seg_tiles[:-1], shifts[1:]):
            lo += n_tiles
            shift = jnp.where(j >= lo, s // SUBLANES, shift)
        return ((j * (IN_TN // SUBLANES) + shift) * SUBLANES, 0)

    return pl.pallas_call(
        _in_proj_kernel,
        grid=(m // IN_TM, n // IN_TN),
        in_specs=[
            pl.BlockSpec((IN_TM, d), lambda i, j: (i, 0)),
            pl.BlockSpec((1, d), lambda i, j: (0, 0)),
            pl.BlockSpec((pl.Element(IN_TN), pl.Element(d)), w_rows),
            pl.BlockSpec((pl.Element(gate_rows[0][1]), pl.Element(d)),
                         lambda i, j: (gate_rows[0][0], 0)),
            pl.BlockSpec((pl.Element(gate_rows[1][1]), pl.Element(d)),
                         lambda i, j: (gate_rows[1][0], 0)),
            pl.BlockSpec((CONV_K, LANES), strip),
            pl.BlockSpec((1, LANES), strip),
            pl.BlockSpec((1, LANES), strip),
        ],
        out_specs=[
            pl.BlockSpec((IN_TM, IN_TN), lambda i, j: (i, j)),
            pl.BlockSpec((IN_TM, LANES), lambda i, j: (i, 0)),
            pl.BlockSpec((IN_TM, LANES), lambda i, j: (i, strip(i, j)[1])),
        ],
        out_shape=[
            jax.ShapeDtypeStruct((m, n), BF16),
            jax.ShapeDtypeStruct((m, LANES), F32),
            jax.ShapeDtypeStruct((m, SECTION_W), BF16),
        ],
        scratch_shapes=[pltpu.VMEM((IN_TM, d), BF16),
                        pltpu.VMEM((IN_TM, SECTION_W), BF16),
                        pltpu.VMEM((IN_TM + SUBLANES, LANES), F32)],
        compiler_params=pltpu.CompilerParams(
            dimension_semantics=("arbitrary", "arbitrary"),
            vmem_limit_bytes=VMEM_LIMIT),
        name="in_proj",
    )(x2, g, w_t, w_t, w_t, conv_w, conv_b, conv_scale)


def _gates_kernel(g_ref, bi_ref, bfa_ref, bfb_ref, col_ref, row_ref, pc_ref):
    lane1 = lax.broadcasted_iota(jnp.int32, (1, LANES), 1)
    bias = jnp.zeros((1, LANES), F32)
    lane0 = 0
    for ref in (bi_ref, bfa_ref, bfb_ref):
        for t in range(ref.shape[0]):
            bias = jnp.where(lane1 == lane0 + t, ref[t], bias)
        lane0 += ref.shape[0]
    x = g_ref[...] + bias
    s = x.shape[0]
    ls = jnp.minimum(x, 0.0) - jnp.log1p(jnp.exp(-jnp.abs(x)))
    r = lax.broadcasted_iota(jnp.int32, (CUM_BLK, CUM_BLK), 0)
    c = lax.broadcasted_iota(jnp.int32, (CUM_BLK, CUM_BLK), 1)
    tri = jnp.where(r >= c, 1.0, 0.0).astype(BF16)
    carry = jnp.zeros((1, LANES), F32)
    blocks = []
    for blk in range(s // CUM_BLK):
        x1, x2, x3 = _split3(ls[blk * CUM_BLK:(blk + 1) * CUM_BLK])
        cs = (jnp.dot(tri, x3, preferred_element_type=F32)
              + jnp.dot(tri, x2, preferred_element_type=F32)
              + jnp.dot(tri, x1, preferred_element_type=F32)) + carry
        carry = cs[CUM_BLK - 1:CUM_BLK, :]
        blocks.append(cs)
    cum = jnp.concatenate(blocks, axis=0)
    lane = lax.broadcasted_iota(jnp.int32, x.shape, 1)
    res = jnp.where(lane < A_HEADS, x, cum)
    res = res * LOG2E
    col_ref[...] = res
    row_ref[...] = res.T[0:SUBLANES, :]

    in_b = (lane >= B_LANE0) & (lane < B_LANE0 + B_HEADS)
    p1, p2, p3 = _split3(jnp.where(in_b, cum * (-LOG2E), 0.0))
    pieces = (p1.astype(F32)
              + pltpu.roll(p2.astype(F32), PIECE_OFFS[1] - B_LANE0, axis=1)
              + pltpu.roll(p3.astype(F32), PIECE_OFFS[2] - B_LANE0, axis=1))
    pc_ref[...] = pieces.astype(BF16)


def _gates(gates3, bias_i, bias_fa, bias_fb):
    b, s, _ = gates3.shape
    assert bias_i.shape[0] + bias_fa.shape[0] + bias_fb.shape[0] == N_GATE
    smem = pl.BlockSpec(memory_space=pltpu.SMEM)
    return pl.pallas_call(
        _gates_kernel,
        grid=(b,),
        in_specs=[
            pl.BlockSpec((None, s, LANES), lambda i: (i, 0, 0)),
            smem, smem, smem,
        ],
        out_specs=[
            pl.BlockSpec((None, s, LANES), lambda i: (i, 0, 0)),
            pl.BlockSpec((None, SUBLANES, s), lambda i: (i, 0, 0)),
            pl.BlockSpec((None, s, LANES), lambda i: (i, 0, 0)),
        ],
        out_shape=[
            jax.ShapeDtypeStruct((b, s, LANES), F32),
            jax.ShapeDtypeStruct((b, SUBLANES, s), F32),
            jax.ShapeDtypeStruct((b, s, LANES), BF16),
        ],
        compiler_params=pltpu.CompilerParams(
            dimension_semantics=("arbitrary",), vmem_limit_bytes=VMEM_LIMIT),
        name="gates",
    )(gates3, bias_i, bias_fa, bias_fb)


def _mlstm_reset(c_scr, n_scr, m_scr, fprev_scr):
    c_scr[...] = jnp.zeros_like(c_scr)
    n_scr[...] = jnp.zeros_like(n_scr)
    m_scr[...] = jnp.zeros_like(m_scr)
    fprev_scr[...] = jnp.zeros_like(fprev_scr)


def _mlstm_chunk(qk_ref, v_ref, o_ref, z_ref, gcol_ref, grow_ref, hg_ref,
                 out_ref, c_scr, n_scr, m_scr, fprev_scr, after_head=lambda: None):
    L = A_CHUNK
    qkw = A_HEADS * A_DQK

    row = lax.broadcasted_iota(jnp.int32, (L, L), 0)
    col = lax.broadcasted_iota(jnp.int32, (L, L), 1)
    causal = row >= col

    for h in range(A_HEADS):
        qb = qk_ref[:, h * A_DQK:(h + 1) * A_DQK]
        kb = qk_ref[:, qkw + h * A_DQK:qkw + (h + 1) * A_DQK]
        q = qb.astype(F32)
        k = kb.astype(F32)
        v = v_ref[:, h * A_DV:(h + 1) * A_DV]

        li_c = gcol_ref[:, h:h + 1]
        f_c = gcol_ref[:, A_HEADS + h:A_HEADS + h + 1]
        li_r = grow_ref[h:h + 1, :]
        f_r = grow_ref[A_HEADS + h:A_HEADS + h + 1, :]
        f_prev = fprev_scr[0:1, A_HEADS + h:A_HEADS + h + 1]
        f_end = gcol_ref[L - 1:L, A_HEADS + h:A_HEADS + h + 1]
        m_st = m_scr[h, 0:1, 0:1]
        c_st = c_scr[h]
        n_st = n_scr[h]

        dmat = jnp.where(causal, (f_c - f_r) + li_r, NEG_BIG)
        inter = (f_c - f_prev) + m_st
        m_row = jnp.maximum(inter, jnp.max(dmat, axis=-1, keepdims=True))
        w_intra = jnp.exp2(dmat - m_row)
        w_inter = jnp.exp2(inter - m_row)
        s = lax.dot_general(qb, kb, (((1,), (1,)), ((), ())), preferred_element_type=F32)
        scores = s * w_intra
        num = (jnp.dot(scores.astype(BF16), v, preferred_element_type=F32)
               + w_inter * jnp.dot(qb, c_st.astype(BF16), preferred_element_type=F32))
        den = (jnp.sum(scores, axis=-1, keepdims=True)
               + w_inter * jnp.sum(q * n_st, axis=-1, keepdims=True))
        hh = num * (1.0 / jnp.maximum(jnp.abs(den), jnp.exp2(-m_row)))

        g_tot = f_end - f_prev
        to_end = (f_end - f_c) + li_c
        m_new = jnp.maximum(g_tot + m_st, jnp.max(to_end, axis=0, keepdims=True))
        w_k = jnp.exp2(to_end - m_new)
        decay = jnp.exp2(g_tot + m_st - m_new)
        kw = k * w_k
        c_scr[h] = decay * c_st + jnp.dot(kw.T.astype(BF16), v, preferred_element_type=F32)
        n_scr[h] = decay * n_st + jnp.sum(kw, axis=0, keepdims=True)
        m_scr[h] = jnp.broadcast_to(m_new, (SUBLANES, LANES))

        hn = hh * lax.rsqrt(jnp.mean(hh * hh, axis=-1, keepdims=True) + EPS)
        hn = hn * hg_ref[:, h * A_DV:(h + 1) * A_DV]
        og = _sigmoid(o_ref[:, h * A_DV:(h + 1) * A_DV].astype(F32))
        zz = _silu(z_ref[:, h * A_DV:(h + 1) * A_DV].astype(F32))
        out_ref[:, h * A_DV:(h + 1) * A_DV] = ((og * hn) * zz).astype(BF16)
        after_head()

    fprev_scr[...] = gcol_ref[L - 1:L, :]


FOX_ACC_ROWS = B_DH + BF16_ROWS


def _fox_kernel(q_ref, k_ref, v_ref, z_ref, pc_ref, out_ref, vt_scr, acc_scr, rhs_scr, m_scr,
                s2_scr, cm2_scr):
    s_scr = (s2_scr.at[0], s2_scr.at[1])
    cm_scr = (cm2_scr.at[0], cm2_scr.at[1])
    hg = pl.program_id(1)
    qi = pl.program_id(2)
    TQ, TK = FOX_TQ, FOX_TK
    seq = k_ref.shape[0]
    nh = 2 * FOX_G

    @pl.when(qi == 0)
    def _():
        for g in range(FOX_G):
            vt = v_ref[:, g * LANES:(g + 1) * LANES].astype(F32).T
            for hh in range(2):
                vt_scr[2 * g + hh, 0:B_DH, :] = vt[hh * B_DH:(hh + 1) * B_DH, :].astype(BF16)
                vt_scr[2 * g + hh, B_DH:FOX_ACC_ROWS, :] = jnp.ones((BF16_ROWS, seq), BF16)

    row = lax.broadcasted_iota(jnp.int32, (LANES, TQ), 0)
    for g in range(FOX_G):
        qt = (q_ref[:, g * LANES:(g + 1) * LANES].astype(F32) * (B_DH ** -0.5 * LOG2E)).T
        for hh in range(2):
            h = 2 * g + hh
            head = hg * nh + h
            qm = jnp.where((row >= hh * B_DH) & (row < (hh + 1) * B_DH), qt, 0.0)
            sel = jnp.where((row == PIECE_OFFS[0] + head) | (row == PIECE_OFFS[1] + head)
                            | (row == PIECE_OFFS[2] + head), 1.0, 0.0)
            rhs_scr[h, 0:LANES, :] = qm.astype(BF16)
            rhs_scr[h, LANES:2 * LANES, :] = sel.astype(BF16)

    acc_scr[...] = jnp.zeros_like(acc_scr)
    m_scr[...] = jnp.full(m_scr.shape, NEG_BIG, F32)

    def key_block(kj):
        k0 = pl.multiple_of(kj * TK, TK)
        pcs = pc_ref[pl.ds(k0, TK), :]
        return [jnp.concatenate([k_ref[pl.ds(k0, TK), g * LANES:(g + 1) * LANES], pcs], axis=1)
                for g in range(FOX_G)]

    def scores_head(h, lhs, slot):
        s = jnp.dot(lhs[h // 2], rhs_scr[h], preferred_element_type=F32)
        s_scr[slot][h] = s
        cm_scr[slot][h] = jnp.broadcast_to(jnp.max(s, axis=0, keepdims=True), (SUBLANES, TQ))

    def scores(kj, slot):
        lhs = key_block(kj)
        for h in range(nh):
            scores_head(h, lhs, slot)

    def softmax_pv(h, kj, s, cmax, lo):
        k0 = pl.multiple_of(kj * TK, TK)
        m_old = m_scr[h, 0:1, lo:TQ]
        m_new = jnp.maximum(m_old, cmax)
        alpha = jnp.exp2(m_old - m_new)
        p = jnp.exp2(s - m_new).astype(BF16)
        pv = jnp.dot(vt_scr[h, :, pl.ds(k0, TK)], p, preferred_element_type=F32)
        acc_scr[h, :, lo:TQ] = alpha * acc_scr[h, :, lo:TQ] + pv
        m_scr[h, :, lo:TQ] = jnp.broadcast_to(m_new, (SUBLANES, TQ - lo))

    def overlapped(kj_next, slot_next, kj, slot):
        lhs = key_block(kj_next)
        for h in range(nh):
            scores_head(h, lhs, slot_next)
            softmax_pv(h, kj, s_scr[slot][h], cm_scr[slot][h, 0:1, :], 0)

    def pair(i, carry):
        overlapped(2 * i + 1, 1, 2 * i, 0)
        overlapped(2 * i + 2, 0, 2 * i + 1, 1)
        return carry

    scores(0, 0)
    lax.fori_loop(0, qi, pair, 0)

    half = TQ - TK
    lhs_b = key_block(2 * qi + 1)
    r = lax.broadcasted_iota(jnp.int32, (TK, TQ), 0)
    c = lax.broadcasted_iota(jnp.int32, (TK, TQ), 1)
    rb = lax.broadcasted_iota(jnp.int32, (TK, TK), 0)
    cb = lax.broadcasted_iota(jnp.int32, (TK, TK), 1)
    for h in range(nh):
        sb = jnp.dot(lhs_b[h // 2], rhs_scr[h, :, half:TQ], preferred_element_type=F32)
        s_scr[1][h, :, half:TQ] = jnp.where(cb >= rb, sb, NEG_BIG)
        s = jnp.where(c >= r, s_scr[0][h], NEG_BIG)
        softmax_pv(h, 2 * qi, s, jnp.max(s, axis=0, keepdims=True), 0)

    for g in range(FOX_G):
        parts = []
        for hh in range(2):
            h = 2 * g + hh
            s = s_scr[1][h, :, half:TQ]
            softmax_pv(h, 2 * qi + 1, s, jnp.max(s, axis=0, keepdims=True), half)
            a = acc_scr[h]
            parts.append(a[0:B_DH, :] * (1.0 / a[B_DH:B_DH + 1, :]))
        o = jnp.concatenate(parts, axis=0).T
        zz = _silu(z_ref[:, g * LANES:(g + 1) * LANES].astype(F32))
        out_ref[:, g * LANES:(g + 1) * LANES] = (o * zz).astype(BF16)


def _fox(proj, pieces, bsz, seq):
    m = proj.shape[0]
    nq = seq // FOX_TQ
    nh = 2 * FOX_G
    w = FOX_G * LANES
    ngrp = (B_HEADS * B_DH) // w
    sec = SECTION_W // w
    col0 = SEC_BQ * sec
    assert (SEC_BK, SEC_BV, SEC_BZ) == (SEC_BQ + 1, SEC_BQ + 2, SEC_BQ + 3)
    return pl.pallas_call(
        _fox_kernel,
        grid=(bsz, ngrp, nq),
        in_specs=[
            pl.BlockSpec((FOX_TQ, w), lambda b, hg, qi: (b * nq + qi, col0 + hg)),
            pl.BlockSpec((seq, w), lambda b, hg, qi: (b, col0 + sec + hg)),
            pl.BlockSpec((seq, w), lambda b, hg, qi: (b, col0 + 2 * sec + hg)),
            pl.BlockSpec((FOX_TQ, w), lambda b, hg, qi: (b * nq + qi, col0 + 3 * sec + hg)),
            pl.BlockSpec((None, seq, LANES), lambda b, hg, qi: (b, 0, 0)),
        ],
        out_specs=pl.BlockSpec((FOX_TQ, w), lambda b, hg, qi: (b * nq + qi, hg)),
        out_shape=jax.ShapeDtypeStruct((m, B_HEADS * B_DH), BF16),
        scratch_shapes=[
            pltpu.VMEM((nh, FOX_ACC_ROWS, seq), BF16),
            pltpu.VMEM((nh, FOX_ACC_ROWS, FOX_TQ), F32),
            pltpu.VMEM((nh, 2 * LANES, FOX_TQ), BF16),
            pltpu.VMEM((nh, SUBLANES, FOX_TQ), F32),
            pltpu.VMEM((2, nh, FOX_TK, FOX_TQ), F32),
            pltpu.VMEM((2, nh, SUBLANES, FOX_TQ), F32),
        ],
        compiler_params=pltpu.CompilerParams(
            dimension_semantics=("arbitrary", "arbitrary", "arbitrary"),
            vmem_limit_bytes=VMEM_LIMIT),
        name="fox",
    )(proj, proj, proj, proj, pieces)


def _merge_stages(ha_ref, hb_ref, ga_ref, gb_ref, x_ref, p_ref, wa_ref, wb_ref, wo_ref, wg_ref,
                  wp_ref, png_ref, fng_ref, out_ref):
    ya = jnp.dot(ha_ref[...], wa_ref[...], preferred_element_type=F32)
    yb = jnp.dot(hb_ref[...], wb_ref[...], preferred_element_type=F32)
    yield
    merged = (_sigmoid(ga_ref[...].astype(F32)) * ya + _sigmoid(gb_ref[...].astype(F32)) * yb)
    x1 = x_ref[...] + jnp.dot(merged.astype(BF16), wo_ref[...], preferred_element_type=F32)
    yield
    r = _rms_norm(x1, png_ref[...]).astype(BF16)
    gate = _sigmoid(jnp.dot(r, wg_ref[...], preferred_element_type=F32))
    yield
    pp = jnp.dot(p_ref[...].astype(BF16), wp_ref[...], preferred_element_type=F32)
    x2 = x1 + gate * pp
    out_ref[...] = _rms_norm(x2, fng_ref[...])
    yield


def _mlstm_merge_kernel(nt, n_chunks,
                        qk_ref, v_ref, o_ref, z_ref, gcol_ref, grow_ref, hg_ref,
                        hb_ref, ga_ref, gb_ref, x_ref, p_ref, wa_ref, wb_ref, wo_ref, wg_ref,
                        wp_ref, png_ref, fng_ref, out_ref,
                        ha_scr, c_scr, n_scr, m_scr, fprev_scr,
                        wa_scr, wb_scr, wo_scr, wg_scr, wp_scr):
    s = pl.program_id(0)
    chunk = jnp.minimum(s, n_chunks - 1)
    state = (c_scr, n_scr, m_scr, fprev_scr)

    @pl.when(s == 0)
    def _():
        ha_scr[...] = jnp.zeros_like(ha_scr)
        for src, dst in ((wa_ref, wa_scr), (wb_ref, wb_scr), (wo_ref, wo_scr), (wg_ref, wg_scr),
                         (wp_ref, wp_scr)):
            dst[...] = src[...].astype(BF16)

    @pl.when(chunk % nt == 0)
    def _():
        _mlstm_reset(*state)

    stages = _merge_stages(ha_scr, hb_ref, ga_ref, gb_ref, x_ref, p_ref, wa_scr, wb_scr, wo_scr,
                           wg_scr, wp_scr, png_ref, fng_ref, out_ref)
    next(stages)
    heads_done = []

    def after_head():
        heads_done.append(None)
        if len(heads_done) > MERGE_LAG:
            next(stages, None)

    _mlstm_chunk(qk_ref, v_ref, o_ref, z_ref, gcol_ref, grow_ref, hg_ref,
                 ha_scr, *state, after_head=after_head)
    for _ in stages:
        pass


def _mlstm_merge(proj, qk_act, gcol, grow, head_g, hb, x2, p2, wa, wb, wo, wg, wp,
                 png, fng, bsz, seq):
    m, d = x2.shape
    pd = p2.shape[1]
    width = SECTION_W
    nt = seq // A_CHUNK
    n_chunks = bsz * nt
    cur = lambda s: jnp.minimum(s, n_chunks - 1)
    prev = lambda s: jnp.maximum(s - 1, 0)
    a_blk = lambda sec: pl.BlockSpec((A_CHUNK, width), lambda s: (cur(s), sec))
    m_blk = lambda w, sec: pl.BlockSpec((A_CHUNK, w), lambda s: (prev(s), sec))
    full = lambda r, c: pl.BlockSpec((r, c), lambda s: (0, 0))
    once = lambda r, c: pl.BlockSpec((r, c), lambda s: (0, 0), pipeline_mode=pl.Buffered(1))
    return pl.pallas_call(
        functools.partial(_mlstm_merge_kernel, nt, n_chunks),
        grid=(n_chunks + 1,),
        in_specs=[
            a_blk(0), a_blk(SEC_AV), a_blk(SEC_AO), a_blk(SEC_AZ),
            pl.BlockSpec((None, A_CHUNK, LANES), lambda s: (cur(s) // nt, cur(s) % nt, 0)),
            pl.BlockSpec((None, SUBLANES, A_CHUNK), lambda s: (cur(s) // nt, 0, cur(s) % nt)),
            full(1, width),
            m_blk(d, 0), m_blk(d, SEC_GA), m_blk(d, SEC_GB), m_blk(d, 0), m_blk(pd, 0),
            once(d, d), once(d, d), once(d, d), once(d, d), once(pd, d),
            full(1, d), full(1, d),
        ],
        out_specs=pl.BlockSpec((A_CHUNK, d), lambda s: (prev(s), 0)),
        out_shape=jax.ShapeDtypeStruct((m, d), F32),
        scratch_shapes=[
            pltpu.VMEM((A_CHUNK, width), BF16),
            pltpu.VMEM((A_HEADS, A_DQK, A_DV), F32),
            pltpu.VMEM((A_HEADS, 1, A_DQK), F32),
            pltpu.VMEM((A_HEADS, SUBLANES, LANES), F32),
            pltpu.VMEM((1, LANES), F32),
            pltpu.VMEM((d, d), BF16), pltpu.VMEM((d, d), BF16), pltpu.VMEM((d, d), BF16),
            pltpu.VMEM((d, d), BF16), pltpu.VMEM((pd, d), BF16),
        ],
        compiler_params=pltpu.CompilerParams(
            dimension_semantics=("arbitrary",), vmem_limit_bytes=VMEM_LIMIT),
        name="mlstm_merge",
    )(qk_act, proj, proj, proj, gcol, grow, head_g,
      hb, proj, proj, x2, p2, wa, wb, wo, wg, wp, png, fng)


def _split_w_in(w):
    qkw = A_HEADS * A_DQK
    aw = A_HEADS * A_DV
    bw = B_HEADS * B_DH
    d = w.shape[0]
    o_ai = 2 * qkw + aw
    o_ao = o_ai + 2 * A_HEADS
    o_bf = o_ao + 2 * aw + 3 * bw
    o_bz = o_bf + B_HEADS
    seg_cols = (o_ai, o_bf - o_ao, w.shape[1] - o_bz)
    shifts = (0, o_ao - o_ai, o_ao - o_ai + o_bz - o_bf)
    assert all(c % IN_TN == 0 for c in seg_cols) and all(s % SUBLANES == 0 for s in shifts)
    seg_tiles = tuple(c // IN_TN for c in seg_cols)
    gate_rows = ((o_ai, o_ao - o_ai), (o_bf, o_bz - o_bf))
    assert all(r % SUBLANES == 0 and n % SUBLANES == 0 for r, n in gate_rows)
    return w.T, gate_rows, seg_tiles, shifts


def _layer(x, p_i, attn_norm_g, w_in, conv_w, conv_b, a_bias_i, a_bias_f, a_head_norm_g, b_bias_f,
           w_branch_a, w_branch_b, w_out, ple_norm_g, w_ple_gate, w_ple_proj, out_norm_g):
    bsz, seq, d = x.shape
    m = bsz * seq
    x2 = x.reshape(m, d)
    w_t, gate_rows, seg_tiles, shifts = _split_w_in(w_in)
    qkw = A_HEADS * A_DQK
    conv_scale = jnp.concatenate([jnp.ones((1, qkw), F32), jnp.full((1, qkw), A_DQK ** -0.5, F32)],
                                 axis=1)
    proj, gates, qk_act = _in_proj(x2, attn_norm_g.reshape(1, d), w_t, gate_rows, seg_tiles, shifts,
                                   conv_w, conv_b.reshape(1, -1), conv_scale, seq)

    gcol, grow, pieces = _gates(gates.reshape(bsz, seq, LANES), a_bias_i, a_bias_f, b_bias_f)

    hb = _fox(proj, pieces, bsz, seq)
    out = _mlstm_merge(proj, qk_act, gcol, grow,
                       a_head_norm_g.reshape(1, -1), hb, x2, p_i.reshape(m, -1),
                       w_branch_a, w_branch_b, w_out, w_ple_gate, w_ple_proj,
                       ple_norm_g.reshape(1, d), out_norm_g.reshape(1, d), bsz, seq)
    return out.reshape(bsz, seq, d)


def kernel(x, p, attn_norm_g, w_in, conv_w, conv_b, a_bias_i, a_bias_f, a_head_norm_g, b_bias_f,
           w_branch_a, w_branch_b, w_out, ple_norm_g, w_ple_gate, w_ple_proj, final_norm_g):
    depth = w_in.shape[0]
    assert depth == 1, "the final norm is fused into the single layer's merge kernel"
    return _layer(x, p[0], attn_norm_g[0], w_in[0], conv_w[0], conv_b[0], a_bias_i[0], a_bias_f[0],
                  a_head_norm_g[0], b_bias_f[0], w_branch_a[0], w_branch_b[0], w_out[0],
                  ple_norm_g[0], w_ple_gate[0], w_ple_proj[0], final_norm_g)
```

```python
import functools
import math

import jax
import jax.numpy as jnp
from jax import lax
from jax.experimental import pallas as pl
from jax.experimental.pallas import tpu as pltpu

F32 = jnp.float32
BF16 = jnp.bfloat16

EPS = 1e-6
A_HEADS = 4
A_DQK = 128
A_DV = 256
CONV_K = 4
B_HEADS = 16
B_DH = 64
LANES = 128
SUBLANES = 8
BF16_ROWS = 16
NEG_BIG = -1e30
LOG2E = math.log2(math.e)

IN_TM = 2048
IN_TN = 1024
IN_CHUNKS = 8
CUM_BLK = 256
A_CHUNK = 256
FOX_TK = 256
FOX_TQ = 2 * FOX_TK
FOX_G = 4
MERGE_LAG = 1
V7X_VMEM_BYTES = 64 * 1024 * 1024
VMEM_LIMIT = V7X_VMEM_BYTES * 7 // 8

SECTION_W = A_HEADS * A_DV
SEC_QK, SEC_AV, SEC_AO, SEC_AZ, SEC_BQ, SEC_BK, SEC_BV, SEC_BZ, SEC_GA, SEC_GB = range(10)

N_GATE = 2 * A_HEADS + B_HEADS
B_LANE0 = 2 * A_HEADS
PIECE_OFFS = (B_LANE0, B_LANE0 + B_HEADS, B_LANE0 + 2 * B_HEADS)


def _sigmoid(x):
    return 1.0 / (1.0 + jnp.exp2(x * (-LOG2E)))


def _silu(x):
    return x * _sigmoid(x)


def _rms_norm(x, g):
    ms = jnp.mean(x * x, axis=-1, keepdims=True)
    return (x * lax.rsqrt(ms + EPS)) * g


def _split3(x):
    x1 = x.astype(BF16)
    r1 = x - x1.astype(F32)
    x2 = r1.astype(BF16)
    x3 = (r1 - x2.astype(F32)).astype(BF16)
    return x1, x2, x3


_NT = (((1,), (1,)), ((), ()))


def _in_proj_kernel(x_ref, g_ref, wt_ref, wga_ref, wgb_ref, cw_ref, cb_ref, cs_ref,
                    proj_ref, gates_ref, qk_ref, h_scr, qkraw_scr, xpad_scr, xbuf_scr, x_sem):
    i = pl.program_id(0)
    j = pl.program_id(1)
    tm = xbuf_scr.shape[1]
    n_seq = pl.num_programs(0)

    def x_copy(seq, slot):
        return pltpu.make_async_copy(x_ref.at[pl.ds(seq * tm, tm), :], xbuf_scr.at[slot],
                                     x_sem.at[slot])

    @pl.when((i == 0) & (j == 0))
    def _():
        qkraw_scr[...] = jnp.zeros_like(qkraw_scr)
        xpad_scr[0:SUBLANES, :] = jnp.zeros((SUBLANES, LANES), F32)
        x_copy(0, 0).start()

    @pl.when((j == 1) & (i + 1 < n_seq))
    def _():
        x_copy(i + 1, (i + 1) % 2).start()

    @pl.when(j == 0)
    def _():
        x_copy(i, i % 2).wait()
        h = _rms_norm(xbuf_scr[i % 2], g_ref[...]).astype(BF16)
        h_scr[...] = h
        n_gate = wga_ref.shape[0] + wgb_ref.shape[0]
        wg = jnp.concatenate([wga_ref[...], wgb_ref[...],
                              jnp.zeros((LANES - n_gate, wga_ref.shape[1]), F32)], axis=0)
        gates_ref[...] = lax.dot_general(h, wg.astype(BF16), _NT, preferred_element_type=F32)

    n_strips = qkraw_scr.shape[1] // LANES
    lane0 = pl.multiple_of(jnp.clip(j - 1, 0, n_strips - 1) * LANES, LANES)
    xpad_scr[SUBLANES:SUBLANES + tm, :] = qkraw_scr[:, pl.ds(lane0, LANES)].astype(F32)

    w = wt_ref[...].astype(BF16)
    ch = tm // IN_CHUNKS
    for c in range(IN_CHUNKS):
        r0 = c * ch
        proj_ref[r0:r0 + ch, :] = lax.dot_general(
            h_scr[r0:r0 + ch, :], w, _NT, preferred_element_type=F32).astype(BF16)
        y = cb_ref[...]
        for d in range(CONV_K):
            y = y + (xpad_scr[SUBLANES + r0 - d:SUBLANES + r0 - d + ch, :]
                     * cw_ref[CONV_K - 1 - d:CONV_K - d, :])
        qk_ref[r0:r0 + ch, :] = (_silu(y) * cs_ref[...]).astype(BF16)

    @pl.when(j == 0)
    def _():
        qkraw_scr[...] = proj_ref[...]


def _in_proj(x2, g, w_t, gate_rows, seg_tiles, shifts, conv_w, conv_b, conv_scale, seq):
    m, d = x2.shape
    n = sum(seg_tiles) * IN_TN
    n_tiles = n // IN_TN
    n_strips = SECTION_W // LANES
    assert IN_TM == seq and IN_TN == SECTION_W and SEC_QK == 0 and n_tiles > n_strips
    strip = lambda i, j: (0, jnp.clip(j - 1, 0, n_strips - 1))

    def w_rows(i, j):
        shift = shifts[0] // SUBLANES
        lo = 0
        for n_tiles, s in zip(seg_tiles[:-1], shifts[1:]):
            lo += n_tiles
            shift = jnp.where(j >= lo, s // SUBLANES, shift)
        return ((j * (IN_TN // SUBLANES) + shift) * SUBLANES, 0)

    return pl.pallas_call(
        _in_proj_kernel,
        grid=(m // IN_TM, n // IN_TN),
        in_specs=[
            pl.BlockSpec(memory_space=pl.ANY),
            pl.BlockSpec((1, d), lambda i, j: (0, 0)),
            pl.BlockSpec((pl.Element(IN_TN), pl.Element(d)), w_rows),
            pl.BlockSpec((pl.Element(gate_rows[0][1]), pl.Element(d)),
                         lambda i, j: (gate_rows[0][0], 0)),
            pl.BlockSpec((pl.Element(gate_rows[1][1]), pl.Element(d)),
                         lambda i, j: (gate_rows[1][0], 0)),
            pl.BlockSpec((CONV_K, LANES), strip),
            pl.BlockSpec((1, LANES), strip),
            pl.BlockSpec((1, LANES), strip),
        ],
        out_specs=[
            pl.BlockSpec((IN_TM, IN_TN), lambda i, j: (i, j)),
            pl.BlockSpec((IN_TM, LANES), lambda i, j: (i, 0)),
            pl.BlockSpec((IN_TM, LANES), lambda i, j: (i, strip(i, j)[1])),
        ],
        out_shape=[
            jax.ShapeDtypeStruct((m, n), BF16),
            jax.ShapeDtypeStruct((m, LANES), F32),
            jax.ShapeDtypeStruct((m, SECTION_W), BF16),
        ],
        scratch_shapes=[pltpu.VMEM((IN_TM, d), BF16),
                        pltpu.VMEM((IN_TM, SECTION_W), BF16),
                        pltpu.VMEM((IN_TM + SUBLANES, LANES), F32),
                        pltpu.VMEM((2, IN_TM, d), F32),
                        pltpu.SemaphoreType.DMA((2,))],
        compiler_params=pltpu.CompilerParams(
            dimension_semantics=("arbitrary", "arbitrary"),
            vmem_limit_bytes=VMEM_LIMIT),
        name="in_proj",
    )(x2, g, w_t, w_t, w_t, conv_w, conv_b, conv_scale)


def _gates_kernel(g_ref, bi_ref, bfa_ref, bfb_ref, col_ref, row_ref, pc_ref):
    lane1 = lax.broadcasted_iota(jnp.int32, (1, LANES), 1)
    bias = jnp.zeros((1, LANES), F32)
    lane0 = 0
    for ref in (bi_ref, bfa_ref, bfb_ref):
        for t in range(ref.shape[0]):
            bias = jnp.where(lane1 == lane0 + t, ref[t], bias)
        lane0 += ref.shape[0]
    x = g_ref[...] + bias
    s = x.shape[0]
    ls = jnp.minimum(x, 0.0) - jnp.log1p(jnp.exp(-jnp.abs(x)))
    r = lax.broadcasted_iota(jnp.int32, (CUM_BLK, CUM_BLK), 0)
    c = lax.broadcasted_iota(jnp.int32, (CUM_BLK, CUM_BLK), 1)
    tri = jnp.where(r >= c, 1.0, 0.0).astype(BF16)
    carry = jnp.zeros((1, LANES), F32)
    blocks = []
    for blk in range(s // CUM_BLK):
        x1, x2, x3 = _split3(ls[blk * CUM_BLK:(blk + 1) * CUM_BLK])
        cs = (jnp.dot(tri, x3, preferred_element_type=F32)
              + jnp.dot(tri, x2, preferred_element_type=F32)
              + jnp.dot(tri, x1, preferred_element_type=F32)) + carry
        carry = cs[CUM_BLK - 1:CUM_BLK, :]
        blocks.append(cs)
    cum = jnp.concatenate(blocks, axis=0)
    lane = lax.broadcasted_iota(jnp.int32, x.shape, 1)
    res = jnp.where(lane < A_HEADS, x, cum)
    res = res * LOG2E
    col_ref[...] = res
    row_ref[...] = res.T[0:SUBLANES, :]

    in_b = (lane >= B_LANE0) & (lane < B_LANE0 + B_HEADS)
    p1, p2, p3 = _split3(jnp.where(in_b, cum * (-LOG2E), 0.0))
    pieces = (p1.astype(F32)
              + pltpu.roll(p2.astype(F32), PIECE_OFFS[1] - B_LANE0, axis=1)
              + pltpu.roll(p3.astype(F32), PIECE_OFFS[2] - B_LANE0, axis=1))
    pc_ref[...] = pieces.astype(BF16)


def _gates(gates3, bias_i, bias_fa, bias_fb):
    b, s, _ = gates3.shape
    assert bias_i.shape[0] + bias_fa.shape[0] + bias_fb.shape[0] == N_GATE
    smem = pl.BlockSpec(memory_space=pltpu.SMEM)
    return pl.pallas_call(
        _gates_kernel,
        grid=(b,),
        in_specs=[
            pl.BlockSpec((None, s, LANES), lambda i: (i, 0, 0)),
            smem, smem, smem,
        ],
        out_specs=[
            pl.BlockSpec((None, s, LANES), lambda i: (i, 0, 0)),
            pl.BlockSpec((None, SUBLANES, s), lambda i: (i, 0, 0)),
            pl.BlockSpec((None, s, LANES), lambda i: (i, 0, 0)),
        ],
        out_shape=[
            jax.ShapeDtypeStruct((b, s, LANES), F32),
            jax.ShapeDtypeStruct((b, SUBLANES, s), F32),
            jax.ShapeDtypeStruct((b, s, LANES), BF16),
        ],
        compiler_params=pltpu.CompilerParams(
            dimension_semantics=("arbitrary",), vmem_limit_bytes=VMEM_LIMIT),
        name="gates",
    )(gates3, bias_i, bias_fa, bias_fb)


def _mlstm_reset(c_scr, n_scr, m_scr, fprev_scr):
    c_scr[...] = jnp.zeros_like(c_scr)
    n_scr[...] = jnp.zeros_like(n_scr)
    m_scr[...] = jnp.zeros_like(m_scr)
    fprev_scr[...] = jnp.zeros_like(fprev_scr)


def _mlstm_chunk(qk_ref, v_ref, o_ref, z_ref, gcol_ref, grow_ref, hg_ref,
                 out_ref, c_scr, n_scr, m_scr, fprev_scr, after_head=lambda: None):
    L = A_CHUNK
    qkw = A_HEADS * A_DQK

    row = lax.broadcasted_iota(jnp.int32, (L, L), 0)
    col = lax.broadcasted_iota(jnp.int32, (L, L), 1)
    causal = row >= col

    for h in range(A_HEADS):
        qb = qk_ref[:, h * A_DQK:(h + 1) * A_DQK]
        kb = qk_ref[:, qkw + h * A_DQK:qkw + (h + 1) * A_DQK]
        q = qb.astype(F32)
        k = kb.astype(F32)
        v = v_ref[:, h * A_DV:(h + 1) * A_DV]

        li_c = gcol_ref[:, h:h + 1]
        f_c = gcol_ref[:, A_HEADS + h:A_HEADS + h + 1]
        li_r = grow_ref[h:h + 1, :]
        f_r = grow_ref[A_HEADS + h:A_HEADS + h + 1, :]
        f_prev = fprev_scr[0:1, A_HEADS + h:A_HEADS + h + 1]
        f_end = gcol_ref[L - 1:L, A_HEADS + h:A_HEADS + h + 1]
        m_st = m_scr[h, 0:1, 0:1]
        c_st = c_scr[h]
        n_st = n_scr[h]

        dmat = jnp.where(causal, (f_c - f_r) + li_r, NEG_BIG)
        inter = (f_c - f_prev) + m_st
        m_row = jnp.maximum(inter, jnp.max(dmat, axis=-1, keepdims=True))
        w_intra = jnp.exp2(dmat - m_row)
        w_inter = jnp.exp2(inter - m_row)
        s = lax.dot_general(qb, kb, (((1,), (1,)), ((), ())), preferred_element_type=F32)
        scores = s * w_intra
        num = (jnp.dot(scores.astype(BF16), v, preferred_element_type=F32)
               + w_inter * jnp.dot(qb, c_st.astype(BF16), preferred_element_type=F32))
        den = (jnp.sum(scores, axis=-1, keepdims=True)
               + w_inter * jnp.sum(q * n_st, axis=-1, keepdims=True))
        hh = num * (1.0 / jnp.maximum(jnp.abs(den), jnp.exp2(-m_row)))

        g_tot = f_end - f_prev
        to_end = (f_end - f_c) + li_c
        m_new = jnp.maximum(g_tot + m_st, jnp.max(to_end, axis=0, keepdims=True))
        w_k = jnp.exp2(to_end - m_new)
        decay = jnp.exp2(g_tot + m_st - m_new)
        kw = k * w_k
        c_scr[h] = decay * c_st + jnp.dot(kw.T.astype(BF16), v, preferred_element_type=F32)
        n_scr[h] = decay * n_st + jnp.sum(kw, axis=0, keepdims=True)
        m_scr[h] = jnp.broadcast_to(m_new, (SUBLANES, LANES))

        hn = hh * lax.rsqrt(jnp.mean(hh * hh, axis=-1, keepdims=True) + EPS)
        hn = hn * hg_ref[:, h * A_DV:(h + 1) * A_DV]
        og = _sigmoid(o_ref[:, h * A_DV:(h + 1) * A_DV].astype(F32))
        zz = _silu(z_ref[:, h * A_DV:(h + 1) * A_DV].astype(F32))
        out_ref[:, h * A_DV:(h + 1) * A_DV] = ((og * hn) * zz).astype(BF16)
        after_head()

    fprev_scr[...] = gcol_ref[L - 1:L, :]


FOX_ACC_ROWS = B_DH + BF16_ROWS


def _fox_kernel(q_ref, k_ref, v_ref, z_ref, pc_ref, out_ref, vt_scr, acc_scr, rhs_scr, m_scr,
                s2_scr, cm2_scr):
    s_scr = (s2_scr.at[0], s2_scr.at[1])
    cm_scr = (cm2_scr.at[0], cm2_scr.at[1])
    hg = pl.program_id(1)
    qi = pl.program_id(2)
    TQ, TK = FOX_TQ, FOX_TK
    seq = k_ref.shape[0]
    nh = 2 * FOX_G

    @pl.when(qi == 0)
    def _():
        for g in range(FOX_G):
            vt = v_ref[:, g * LANES:(g + 1) * LANES].astype(F32).T
            for hh in range(2):
                vt_scr[2 * g + hh, 0:B_DH, :] = vt[hh * B_DH:(hh + 1) * B_DH, :].astype(BF16)
                vt_scr[2 * g + hh, B_DH:FOX_ACC_ROWS, :] = jnp.ones((BF16_ROWS, seq), BF16)

    row = lax.broadcasted_iota(jnp.int32, (LANES, TQ), 0)
    for g in range(FOX_G):
        qt = (q_ref[:, g * LANES:(g + 1) * LANES].astype(F32) * (B_DH ** -0.5 * LOG2E)).T
        for hh in range(2):
            h = 2 * g + hh
            head = hg * nh + h
            qm = jnp.where((row >= hh * B_DH) & (row < (hh + 1) * B_DH), qt, 0.0)
            sel = jnp.where((row == PIECE_OFFS[0] + head) | (row == PIECE_OFFS[1] + head)
                            | (row == PIECE_OFFS[2] + head), 1.0, 0.0)
            rhs_scr[h, 0:LANES, :] = qm.astype(BF16)
            rhs_scr[h, LANES:2 * LANES, :] = sel.astype(BF16)

    acc_scr[...] = jnp.zeros_like(acc_scr)
    m_scr[...] = jnp.full(m_scr.shape, NEG_BIG, F32)

    def key_block(kj):
        k0 = pl.multiple_of(kj * TK, TK)
        pcs = pc_ref[pl.ds(k0, TK), :]
        return [jnp.concatenate([k_ref[pl.ds(k0, TK), g * LANES:(g + 1) * LANES], pcs], axis=1)
                for g in range(FOX_G)]

    def scores_head(h, lhs, slot):
        s = jnp.dot(lhs[h // 2], rhs_scr[h], preferred_element_type=F32)
        s_scr[slot][h] = s
        cm_scr[slot][h] = jnp.broadcast_to(jnp.max(s, axis=0, keepdims=True), (SUBLANES, TQ))

    def scores(kj, slot):
        lhs = key_block(kj)
        for h in range(nh):
            scores_head(h, lhs, slot)

    def softmax_pv(h, kj, s, cmax, lo):
        k0 = pl.multiple_of(kj * TK, TK)
        m_old = m_scr[h, 0:1, lo:TQ]
        m_new = jnp.maximum(m_old, cmax)
        alpha = jnp.exp2(m_old - m_new)
        p = jnp.exp2(s - m_new).astype(BF16)
        pv = jnp.dot(vt_scr[h, :, pl.ds(k0, TK)], p, preferred_element_type=F32)
        acc_scr[h, :, lo:TQ] = alpha * acc_scr[h, :, lo:TQ] + pv
        m_scr[h, :, lo:TQ] = jnp.broadcast_to(m_new, (SUBLANES, TQ - lo))

    def overlapped(kj_next, slot_next, kj, slot):
        lhs = key_block(kj_next)
        for h in range(nh):
            scores_head(h, lhs, slot_next)
            softmax_pv(h, kj, s_scr[slot][h], cm_scr[slot][h, 0:1, :], 0)

    def pair(i, carry):
        overlapped(2 * i + 1, 1, 2 * i, 0)
        overlapped(2 * i + 2, 0, 2 * i + 1, 1)
        return carry

    scores(0, 0)
    lax.fori_loop(0, qi, pair, 0)

    half = TQ - TK
    lhs_b = key_block(2 * qi + 1)
    r = lax.broadcasted_iota(jnp.int32, (TK, TQ), 0)
    c = lax.broadcasted_iota(jnp.int32, (TK, TQ), 1)
    rb = lax.broadcasted_iota(jnp.int32, (TK, TK), 0)
    cb = lax.broadcasted_iota(jnp.int32, (TK, TK), 1)
    for h in range(nh):
        sb = jnp.dot(lhs_b[h // 2], rhs_scr[h, :, half:TQ], preferred_element_type=F32)
        s_scr[1][h, :, half:TQ] = jnp.where(cb >= rb, sb, NEG_BIG)
        s = jnp.where(c >= r, s_scr[0][h], NEG_BIG)
        softmax_pv(h, 2 * qi, s, jnp.max(s, axis=0, keepdims=True), 0)

    for g in range(FOX_G):
        parts = []
        for hh in range(2):
            h = 2 * g + hh
            s = s_scr[1][h, :, half:TQ]
            softmax_pv(h, 2 * qi + 1, s, jnp.max(s, axis=0, keepdims=True), half)
            a = acc_scr[h]
            parts.append(a[0:B_DH, :] * (1.0 / a[B_DH:B_DH + 1, :]))
        o = jnp.concatenate(parts, axis=0).T
        zz = _silu(z_ref[:, g * LANES:(g + 1) * LANES].astype(F32))
        out_ref[:, g * LANES:(g + 1) * LANES] = (o * zz).astype(BF16)


def _fox(proj, pieces, bsz, seq):
    m = proj.shape[0]
    nq = seq // FOX_TQ
    nh = 2 * FOX_G
    w = FOX_G * LANES
    ngrp = (B_HEADS * B_DH) // w
    sec = SECTION_W // w
    col0 = SEC_BQ * sec
    assert (SEC_BK, SEC_BV, SEC_BZ) == (SEC_BQ + 1, SEC_BQ + 2, SEC_BQ + 3)
    return pl.pallas_call(
        _fox_kernel,
        grid=(bsz, ngrp, nq),
        in_specs=[
            pl.BlockSpec((FOX_TQ, w), lambda b, hg, qi: (b * nq + qi, col0 + hg)),
            pl.BlockSpec((seq, w), lambda b, hg, qi: (b, col0 + sec + hg)),
            pl.BlockSpec((seq, w), lambda b, hg, qi: (b, col0 + 2 * sec + hg)),
            pl.BlockSpec((FOX_TQ, w), lambda b, hg, qi: (b * nq + qi, col0 + 3 * sec + hg)),
            pl.BlockSpec((None, seq, LANES), lambda b, hg, qi: (b, 0, 0)),
        ],
        out_specs=pl.BlockSpec((FOX_TQ, w), lambda b, hg, qi: (b * nq + qi, hg)),
        out_shape=jax.ShapeDtypeStruct((m, B_HEADS * B_DH), BF16),
        scratch_shapes=[
            pltpu.VMEM((nh, FOX_ACC_ROWS, seq), BF16),
            pltpu.VMEM((nh, FOX_ACC_ROWS, FOX_TQ), F32),
            pltpu.VMEM((nh, 2 * LANES, FOX_TQ), BF16),
            pltpu.VMEM((nh, SUBLANES, FOX_TQ), F32),
            pltpu.VMEM((2, nh, FOX_TK, FOX_TQ), F32),
            pltpu.VMEM((2, nh, SUBLANES, FOX_TQ), F32),
        ],
        compiler_params=pltpu.CompilerParams(
            dimension_semantics=("arbitrary", "arbitrary", "arbitrary"),
            vmem_limit_bytes=VMEM_LIMIT),
        name="fox",
    )(proj, proj, proj, proj, pieces)


def _merge_stages(ha_ref, hb_ref, ga_ref, gb_ref, x_ref, p_ref, wa_ref, wb_ref, wo_ref, wg_ref,
                  wp_ref, png_ref, fng_ref, out_ref):
    ya = jnp.dot(ha_ref[...], wa_ref[...], preferred_element_type=F32)
    yb = jnp.dot(hb_ref[...], wb_ref[...], preferred_element_type=F32)
    yield
    merged = (_sigmoid(ga_ref[...].astype(F32)) * ya + _sigmoid(gb_ref[...].astype(F32)) * yb)
    x1 = x_ref[...] + jnp.dot(merged.astype(BF16), wo_ref[...], preferred_element_type=F32)
    yield
    r = _rms_norm(x1, png_ref[...]).astype(BF16)
    gate = _sigmoid(jnp.dot(r, wg_ref[...], preferred_element_type=F32))
    yield
    pp = jnp.dot(p_ref[...].astype(BF16), wp_ref[...], preferred_element_type=F32)
    x2 = x1 + gate * pp
    out_ref[...] = _rms_norm(x2, fng_ref[...])
    yield


def _mlstm_merge_kernel(nt, n_chunks,
                        qk_ref, v_ref, o_ref, z_ref, gcol_ref, grow_ref, hg_ref,
                        hb_ref, ga_ref, gb_ref, x_ref, p_ref, wa_ref, wb_ref, wo_ref, wg_ref,
                        wp_ref, png_ref, fng_ref, out_ref,
                        ha_scr, c_scr, n_scr, m_scr, fprev_scr,
                        wa_scr, wb_scr, wo_scr, wg_scr, wp_scr):
    s = pl.program_id(0)
    chunk = jnp.minimum(s, n_chunks - 1)
    state = (c_scr, n_scr, m_scr, fprev_scr)

    @pl.when(s == 0)
    def _():
        ha_scr[...] = jnp.zeros_like(ha_scr)
        for src, dst in ((wa_ref, wa_scr), (wb_ref, wb_scr), (wo_ref, wo_scr), (wg_ref, wg_scr),
                         (wp_ref, wp_scr)):
            dst[...] = src[...].astype(BF16)

    @pl.when(chunk % nt == 0)
    def _():
        _mlstm_reset(*state)

    stages = _merge_stages(ha_scr, hb_ref, ga_ref, gb_ref, x_ref, p_ref, wa_scr, wb_scr, wo_scr,
                           wg_scr, wp_scr, png_ref, fng_ref, out_ref)
    next(stages)
    heads_done = []

    def after_head():
        heads_done.append(None)
        if len(heads_done) > MERGE_LAG:
            next(stages, None)

    _mlstm_chunk(qk_ref, v_ref, o_ref, z_ref, gcol_ref, grow_ref, hg_ref,
                 ha_scr, *state, after_head=after_head)
    for _ in stages:
        pass


def _mlstm_merge(proj, qk_act, gcol, grow, head_g, hb, x2, p2, wa, wb, wo, wg, wp,
                 png, fng, bsz, seq):
    m, d = x2.shape
    pd = p2.shape[1]
    width = SECTION_W
    nt = seq // A_CHUNK
    n_chunks = bsz * nt
    cur = lambda s: jnp.minimum(s, n_chunks - 1)
    prev = lambda s: jnp.maximum(s - 1, 0)
    a_blk = lambda sec: pl.BlockSpec((A_CHUNK, width), lambda s: (cur(s), sec))
    m_blk = lambda w, sec: pl.BlockSpec((A_CHUNK, w), lambda s: (prev(s), sec))
    full = lambda r, c: pl.BlockSpec((r, c), lambda s: (0, 0))
    once = lambda r, c: pl.BlockSpec((r, c), lambda s: (0, 0), pipeline_mode=pl.Buffered(1))
    return pl.pallas_call(
        functools.partial(_mlstm_merge_kernel, nt, n_chunks),
        grid=(n_chunks + 1,),
        in_specs=[
            a_blk(0), a_blk(SEC_AV), a_blk(SEC_AO), a_blk(SEC_AZ),
            pl.BlockSpec((None, A_CHUNK, LANES), lambda s: (cur(s) // nt, cur(s) % nt, 0)),
            pl.BlockSpec((None, SUBLANES, A_CHUNK), lambda s: (cur(s) // nt, 0, cur(s) % nt)),
            full(1, width),
            m_blk(d, 0), m_blk(d, SEC_GA), m_blk(d, SEC_GB), m_blk(d, 0), m_blk(pd, 0),
            once(d, d), once(d, d), once(d, d), once(d, d), once(pd, d),
            full(1, d), full(1, d),
        ],
        out_specs=pl.BlockSpec((A_CHUNK, d), lambda s: (prev(s), 0)),
        out_shape=jax.ShapeDtypeStruct((m, d), F32),
        scratch_shapes=[
            pltpu.VMEM((A_CHUNK, width), BF16),
            pltpu.VMEM((A_HEADS, A_DQK, A_DV), F32),
            pltpu.VMEM((A_HEADS, 1, A_DQK), F32),
            pltpu.VMEM((A_HEADS, SUBLANES, LANES), F32),
            pltpu.VMEM((1, LANES), F32),
            pltpu.VMEM((d, d), BF16), pltpu.VMEM((d, d), BF16), pltpu.VMEM((d, d), BF16),
            pltpu.VMEM((d, d), BF16), pltpu.VMEM((pd, d), BF16),
        ],
        compiler_params=pltpu.CompilerParams(
            dimension_semantics=("arbitrary",), vmem_limit_bytes=VMEM_LIMIT),
        name="mlstm_merge",
    )(qk_act, proj, proj, proj, gcol, grow, head_g,
      hb, proj, proj, x2, p2, wa, wb, wo, wg, wp, png, fng)


def _split_w_in(w):
    qkw = A_HEADS * A_DQK
    aw = A_HEADS * A_DV
    bw = B_HEADS * B_DH
    d = w.shape[0]
    o_ai = 2 * qkw + aw
    o_ao = o_ai + 2 * A_HEADS
    o_bf = o_ao + 2 * aw + 3 * bw
    o_bz = o_bf + B_HEADS
    seg_cols = (o_ai, o_bf - o_ao, w.shape[1] - o_bz)
    shifts = (0, o_ao - o_ai, o_ao - o_ai + o_bz - o_bf)
    assert all(c % IN_TN == 0 for c in seg_cols) and all(s % SUBLANES == 0 for s in shifts)
    seg_tiles = tuple(c // IN_TN for c in seg_cols)
    gate_rows = ((o_ai, o_ao - o_ai), (o_bf, o_bz - o_bf))
    assert all(r % SUBLANES == 0 and n % SUBLANES == 0 for r, n in gate_rows)
    return w.T, gate_rows, seg_tiles, shifts


def _layer(x, p_i, attn_norm_g, w_in, conv_w, conv_b, a_bias_i, a_bias_f, a_head_norm_g, b_bias_f,
           w_branch_a, w_branch_b, w_out, ple_norm_g, w_ple_gate, w_ple_proj, out_norm_g):
    bsz, seq, d = x.shape
    m = bsz * seq
    x2 = x.reshape(m, d)
    w_t, gate_rows, seg_tiles, shifts = _split_w_in(w_in)
    qkw = A_HEADS * A_DQK
    conv_scale = jnp.concatenate([jnp.ones((1, qkw), F32), jnp.full((1, qkw), A_DQK ** -0.5, F32)],
                                 axis=1)
    proj, gates, qk_act = _in_proj(x2, attn_norm_g.reshape(1, d), w_t, gate_rows, seg_tiles, shifts,
                                   conv_w, conv_b.reshape(1, -1), conv_scale, seq)

    gcol, grow, pieces = _gates(gates.reshape(bsz, seq, LANES), a_bias_i, a_bias_f, b_bias_f)

    hb = _fox(proj, pieces, bsz, seq)
    out = _mlstm_merge(proj, qk_act, gcol, grow,
                       a_head_norm_g.reshape(1, -1), hb, x2, p_i.reshape(m, -1),
                       w_branch_a, w_branch_b, w_out, w_ple_gate, w_ple_proj,
                       ple_norm_g.reshape(1, d), out_norm_g.reshape(1, d), bsz, seq)
    return out.reshape(bsz, seq, d)


def kernel(x, p, attn_norm_g, w_in, conv_w, conv_b, a_bias_i, a_bias_f, a_head_norm_g, b_bias_f,
           w_branch_a, w_branch_b, w_out, ple_norm_g, w_ple_gate, w_ple_proj, final_norm_g):
    depth = w_in.shape[0]
    assert depth == 1, "the final norm is fused into the single layer's merge kernel"
    return _layer(x, p[0], attn_norm_g[0], w_in[0], conv_w[0], conv_b[0], a_bias_i[0], a_bias_f[0],
                  a_head_norm_g[0], b_bias_f[0], w_branch_a[0], w_branch_b[0], w_out[0],
                  ple_norm_g[0], w_ple_gate[0], w_ple_proj[0], final_norm_g)
```

```python
import functools
import math

import jax
import jax.numpy as jnp
from jax import lax
from jax.experimental import pallas as pl
from jax.experimental.pallas import tpu as pltpu

F32 = jnp.float32
BF16 = jnp.bfloat16

EPS = 1e-6
A_HEADS = 4
A_DQK = 128
A_DV = 256
CONV_K = 4
B_HEADS = 16
B_DH = 64
LANES = 128
SUBLANES = 8
BF16_ROWS = 16
NEG_BIG = -1e30
LOG2E = math.log2(math.e)

IN_TM = 2048
IN_TN = 1024
IN_CHUNKS = 8
GATE_ROWS = 128
A_CHUNK = 256
FOX_TK = 256
FOX_TQ = 2 * FOX_TK
FOX_G = 4
MERGE_LAG = 1
V7X_VMEM_BYTES = 64 * 1024 * 1024
VMEM_LIMIT = V7X_VMEM_BYTES * 7 // 8

SECTION_W = A_HEADS * A_DV
SEC_QK, SEC_AV, SEC_AO, SEC_AZ, SEC_BQ, SEC_BK, SEC_BV, SEC_BZ, SEC_GA, SEC_GB = range(10)

N_GATE = 2 * A_HEADS + B_HEADS
B_LANE0 = 2 * A_HEADS
PIECE_OFFS = (B_LANE0, B_LANE0 + B_HEADS, B_LANE0 + 2 * B_HEADS)

def _sigmoid(x):
    return 1.0 / (1.0 + jnp.exp2(x * (-LOG2E)))


def _silu(x):
    return x * _sigmoid(x)


def _rms_norm(x, g):
    ms = jnp.mean(x * x, axis=-1, keepdims=True)
    return (x * lax.rsqrt(ms + EPS)) * g


def _split3(x):
    x1 = x.astype(BF16)
    r1 = x - x1.astype(F32)
    x2 = r1.astype(BF16)
    x3 = (r1 - x2.astype(F32)).astype(BF16)
    return x1, x2, x3


_NT = (((1,), (1,)), ((), ()))


def _gate_chunk(x, carry, col_ref, row_ref, pc_ref, r0):
    ch = x.shape[0]
    ls = jnp.minimum(x, 0.0) - jnp.log1p(jnp.exp(-jnp.abs(x)))
    lane = lax.broadcasted_iota(jnp.int32, x.shape, 1)
    n_grp = ch // SUBLANES
    y = ls.reshape(n_grp, SUBLANES, LANES)
    sub = lax.broadcasted_iota(jnp.int32, y.shape, 1)
    step = 1
    while step < SUBLANES:
        y = y + jnp.where(sub >= step, pltpu.roll(y, step, axis=1), 0.0)
        step *= 2
    tot = jnp.broadcast_to(y[:, SUBLANES - 1:SUBLANES, :], y.shape)
    offs = []
    for k in range(n_grp):
        offs.append(carry)
        carry = carry + tot[k]
    cum = (y + jnp.stack(offs, axis=0)).reshape(ch, LANES)
    res = jnp.where(lane < A_HEADS, x, cum) * LOG2E
    col_ref[r0:r0 + ch, :] = res
    row_ref[:, r0:r0 + ch] = res.T[0:SUBLANES, :]
    in_b = (lane >= B_LANE0) & (lane < B_LANE0 + B_HEADS)
    p1, p2, p3 = _split3(jnp.where(in_b, cum * (-LOG2E), 0.0))
    pieces = (p1.astype(F32)
              + pltpu.roll(p2.astype(F32), PIECE_OFFS[1] - B_LANE0, axis=1)
              + pltpu.roll(p3.astype(F32), PIECE_OFFS[2] - B_LANE0, axis=1))
    pc_ref[r0:r0 + ch, :] = pieces.astype(BF16)
    return carry


def _in_proj_kernel(x_ref, g_ref, wt_ref, wga_ref, wgb_ref, cw_ref, cb_ref, cs_ref,
                    bi_ref, bfa_ref, bfb_ref,
                    proj_ref, col_ref, row_ref, pc_ref, qk_ref,
                    h_scr, qkraw_scr, xpad_scr, xbuf_scr, gates_scr, x_sem):
    i = pl.program_id(0)
    j = pl.program_id(1)
    tm = xbuf_scr.shape[1]
    n_seq = pl.num_programs(0)
    last = pl.num_programs(1) - 1

    def x_copy(seq, slot):
        return pltpu.make_async_copy(x_ref.at[pl.ds(seq * tm, tm), :], xbuf_scr.at[slot],
                                     x_sem.at[slot])

    @pl.when((i == 0) & (j == 0))
    def _():
        qkraw_scr[...] = jnp.zeros_like(qkraw_scr)
        xpad_scr[0:SUBLANES, :] = jnp.zeros((SUBLANES, LANES), F32)
        x_copy(0, 0).start()

    @pl.when((j == 1) & (i + 1 < n_seq))
    def _():
        x_copy(i + 1, (i + 1) % 2).start()

    @pl.when(j == 0)
    def _():
        x_copy(i, i % 2).wait()
        h = _rms_norm(xbuf_scr[i % 2], g_ref[...]).astype(BF16)
        h_scr[0:tm, :] = h
        n_gate = wga_ref.shape[0] + wgb_ref.shape[0]
        wg = jnp.concatenate([wga_ref[...], wgb_ref[...],
                              jnp.zeros((LANES - n_gate, wga_ref.shape[1]), F32)], axis=0)
        gates_scr[...] = lax.dot_general(h, wg.astype(BF16), _NT, preferred_element_type=F32)

    ch = tm // IN_CHUNKS

    def matmul_chunk(w, r0):
        proj_ref[r0:r0 + ch, :] = lax.dot_general(
            h_scr[r0:r0 + ch, :], w, _NT, preferred_element_type=F32).astype(BF16)

    @pl.when(j < last)
    def _():
        n_strips = qkraw_scr.shape[1] // LANES
        lane0 = pl.multiple_of(jnp.clip(j - 1, 0, n_strips - 1) * LANES, LANES)
        xpad_scr[SUBLANES:SUBLANES + tm, :] = qkraw_scr[:, pl.ds(lane0, LANES)].astype(F32)
        w = wt_ref[...].astype(BF16)
        for c in range(IN_CHUNKS):
            r0 = c * ch
            matmul_chunk(w, r0)
            y = cb_ref[...]
            for d in range(CONV_K):
                y = y + (xpad_scr[SUBLANES + r0 - d:SUBLANES + r0 - d + ch, :]
                         * cw_ref[CONV_K - 1 - d:CONV_K - d, :])
            qk_ref[r0:r0 + ch, :] = (_silu(y) * cs_ref[...]).astype(BF16)

    @pl.when(j == last)
    def _():
        lane1 = lax.broadcasted_iota(jnp.int32, (1, LANES), 1)
        bias = jnp.zeros((1, LANES), F32)
        l0 = 0
        for ref in (bi_ref, bfa_ref, bfb_ref):
            for t in range(ref.shape[0]):
                bias = jnp.where(lane1 == l0 + t, ref[t], bias)
            l0 += ref.shape[0]
        carry = jnp.zeros((SUBLANES, LANES), F32)
        w = wt_ref[...].astype(BF16)
        for c in range(IN_CHUNKS):
            r0 = c * ch
            matmul_chunk(w, r0)
            for r in range(r0, r0 + ch, GATE_ROWS):
                carry = _gate_chunk(gates_scr[r:r + GATE_ROWS, :] + bias, carry,
                                    col_ref, row_ref, pc_ref, r)
            spare = pl.multiple_of(tm + (j - last) * BF16_ROWS, BF16_ROWS)
            h_scr[pl.ds(spare, BF16_ROWS), 0:LANES] = pc_ref[r0 + ch - BF16_ROWS:r0 + ch, :]

    @pl.when(j == 0)
    def _():
        qkraw_scr[...] = proj_ref[...]


def _in_proj(x2, g, w_t, gate_rows, seg_tiles, shifts, conv_w, conv_b, conv_scale,
             bias_i, bias_fa, bias_fb, seq):
    m, d = x2.shape
    n = sum(seg_tiles) * IN_TN
    n_tiles = n // IN_TN
    n_strips = SECTION_W // LANES
    assert IN_TM == seq and IN_TN == SECTION_W and SEC_QK == 0 and n_tiles > n_strips + 1
    assert (IN_TM // IN_CHUNKS) % GATE_ROWS == 0 and GATE_ROWS % LANES == 0
    assert bias_i.shape[0] + bias_fa.shape[0] + bias_fb.shape[0] == N_GATE
    strip = lambda i, j: (0, jnp.clip(j - 1, 0, n_strips - 1))
    smem = pl.BlockSpec(memory_space=pltpu.SMEM)

    def w_rows(i, j):
        shift = shifts[0] // SUBLANES
        lo = 0
        for n_tiles, s in zip(seg_tiles[:-1], shifts[1:]):
            lo += n_tiles
            shift = jnp.where(j >= lo, s // SUBLANES, shift)
        return ((j * (IN_TN // SUBLANES) + shift) * SUBLANES, 0)

    return pl.pallas_call(
        _in_proj_kernel,
        grid=(m // IN_TM, n // IN_TN),
        in_specs=[
            pl.BlockSpec(memory_space=pl.ANY),
            pl.BlockSpec((1, d), lambda i, j: (0, 0)),
            pl.BlockSpec((pl.Element(IN_TN), pl.Element(d)), w_rows),
            pl.BlockSpec((pl.Element(gate_rows[0][1]), pl.Element(d)),
                         lambda i, j: (gate_rows[0][0], 0)),
            pl.BlockSpec((pl.Element(gate_rows[1][1]), pl.Element(d)),
                         lambda i, j: (gate_rows[1][0], 0)),
            pl.BlockSpec((CONV_K, LANES), strip),
            pl.BlockSpec((1, LANES), strip),
            pl.BlockSpec((1, LANES), strip),
            smem, smem, smem,
        ],
        out_specs=[
            pl.BlockSpec((IN_TM, IN_TN), lambda i, j: (i, j)),
            pl.BlockSpec((IN_TM, LANES), lambda i, j: (i, 0)),
            pl.BlockSpec((None, SUBLANES, IN_TM), lambda i, j: (i, 0, 0)),
            pl.BlockSpec((IN_TM, LANES), lambda i, j: (i, 0)),
            pl.BlockSpec((IN_TM, LANES), lambda i, j: (i, strip(i, j)[1])),
        ],
        out_shape=[
            jax.ShapeDtypeStruct((m, n), BF16),
            jax.ShapeDtypeStruct((m, LANES), F32),
            jax.ShapeDtypeStruct((m // IN_TM, SUBLANES, IN_TM), F32),
            jax.ShapeDtypeStruct((m, LANES), BF16),
            jax.ShapeDtypeStruct((m, SECTION_W), BF16),
        ],
        scratch_shapes=[pltpu.VMEM((IN_TM + BF16_ROWS, d), BF16),
                        pltpu.VMEM((IN_TM, SECTION_W), BF16),
                        pltpu.VMEM((IN_TM + SUBLANES, LANES), F32),
                        pltpu.VMEM((2, IN_TM, d), F32),
                        pltpu.VMEM((IN_TM, LANES), F32),
                        pltpu.SemaphoreType.DMA((2,))],
        compiler_params=pltpu.CompilerParams(
            dimension_semantics=("arbitrary", "arbitrary"),
            vmem_limit_bytes=VMEM_LIMIT),
        name="in_proj",
    )(x2, g, w_t, w_t, w_t, conv_w, conv_b, conv_scale, bias_i, bias_fa, bias_fb)


def _mlstm_reset(c_scr, n_scr, m_scr, fprev_scr):
    c_scr[...] = jnp.zeros_like(c_scr)
    n_scr[...] = jnp.zeros_like(n_scr)
    m_scr[...] = jnp.zeros_like(m_scr)
    fprev_scr[...] = jnp.zeros_like(fprev_scr)


def _mlstm_chunk(qk_ref, v_ref, o_ref, z_ref, gcol_ref, grow_ref, hg_ref,
                 out_ref, c_scr, n_scr, m_scr, fprev_scr, after_head=lambda: None):
    L = A_CHUNK
    qkw = A_HEADS * A_DQK

    row = lax.broadcasted_iota(jnp.int32, (L, L), 0)
    col = lax.broadcasted_iota(jnp.int32, (L, L), 1)
    causal = row >= col

    for h in range(A_HEADS):
        qb = qk_ref[:, h * A_DQK:(h + 1) * A_DQK]
        kb = qk_ref[:, qkw + h * A_DQK:qkw + (h + 1) * A_DQK]
        q = qb.astype(F32)
        k = kb.astype(F32)
        v = v_ref[:, h * A_DV:(h + 1) * A_DV]

        li_c = gcol_ref[:, h:h + 1]
        f_c = gcol_ref[:, A_HEADS + h:A_HEADS + h + 1]
        li_r = grow_ref[h:h + 1, :]
        f_r = grow_ref[A_HEADS + h:A_HEADS + h + 1, :]
        f_prev = fprev_scr[0:1, A_HEADS + h:A_HEADS + h + 1]
        f_end = gcol_ref[L - 1:L, A_HEADS + h:A_HEADS + h + 1]
        m_st = m_scr[h, 0:1, 0:1]
        c_st = c_scr[h]
        n_st = n_scr[h]

        dmat = jnp.where(causal, (f_c - f_r) + li_r, NEG_BIG)
        inter = (f_c - f_prev) + m_st
        m_row = jnp.maximum(inter, jnp.max(dmat, axis=-1, keepdims=True))
        w_intra = jnp.exp2(dmat - m_row)
        w_inter = jnp.exp2(inter - m_row)
        s = lax.dot_general(qb, kb, (((1,), (1,)), ((), ())), preferred_element_type=F32)
        scores = s * w_intra
        num = (jnp.dot(scores.astype(BF16), v, preferred_element_type=F32)
               + w_inter * jnp.dot(qb, c_st.astype(BF16), preferred_element_type=F32))
        den = (jnp.sum(scores, axis=-1, keepdims=True)
               + w_inter * jnp.sum(q * n_st, axis=-1, keepdims=True))
        hh = num * (1.0 / jnp.maximum(jnp.abs(den), jnp.exp2(-m_row)))

        g_tot = f_end - f_prev
        to_end = (f_end - f_c) + li_c
        m_new = jnp.maximum(g_tot + m_st, jnp.max(to_end, axis=0, keepdims=True))
        w_k = jnp.exp2(to_end - m_new)
        decay = jnp.exp2(g_tot + m_st - m_new)
        kw = k * w_k
        c_scr[h] = decay * c_st + jnp.dot(kw.T.astype(BF16), v, preferred_element_type=F32)
        n_scr[h] = decay * n_st + jnp.sum(kw, axis=0, keepdims=True)
        m_scr[h] = jnp.broadcast_to(m_new, (SUBLANES, LANES))

        hn = hh * lax.rsqrt(jnp.mean(hh * hh, axis=-1, keepdims=True) + EPS)
        hn = hn * hg_ref[:, h * A_DV:(h + 1) * A_DV]
        og = _sigmoid(o_ref[:, h * A_DV:(h + 1) * A_DV].astype(F32))
        zz = _silu(z_ref[:, h * A_DV:(h + 1) * A_DV].astype(F32))
        out_ref[:, h * A_DV:(h + 1) * A_DV] = ((og * hn) * zz).astype(BF16)
        after_head()

    fprev_scr[...] = gcol_ref[L - 1:L, :]


FOX_ACC_ROWS = B_DH + BF16_ROWS


def _fox_kernel(q_ref, k_ref, v_ref, z_ref, pc_ref, out_ref, vt_scr, acc_scr, rhs_scr, m_scr,
                s2_scr, cm2_scr):
    s_scr = (s2_scr.at[0], s2_scr.at[1])
    cm_scr = (cm2_scr.at[0], cm2_scr.at[1])
    hg = pl.program_id(1)
    qi = pl.program_id(2)
    TQ, TK = FOX_TQ, FOX_TK
    seq = k_ref.shape[0]
    nh = 2 * FOX_G

    @pl.when(qi == 0)
    def _():
        for g in range(FOX_G):
            vt = v_ref[:, g * LANES:(g + 1) * LANES].astype(F32).T
            for hh in range(2):
                vt_scr[2 * g + hh, 0:B_DH, :] = vt[hh * B_DH:(hh + 1) * B_DH, :].astype(BF16)
                vt_scr[2 * g + hh, B_DH:FOX_ACC_ROWS, :] = jnp.ones((BF16_ROWS, seq), BF16)

    row = lax.broadcasted_iota(jnp.int32, (LANES, TQ), 0)
    for g in range(FOX_G):
        qt = (q_ref[:, g * LANES:(g + 1) * LANES].astype(F32) * (B_DH ** -0.5 * LOG2E)).T
        for hh in range(2):
            h = 2 * g + hh
            head = hg * nh + h
            qm = jnp.where((row >= hh * B_DH) & (row < (hh + 1) * B_DH), qt, 0.0)
            sel = jnp.where((row == PIECE_OFFS[0] + head) | (row == PIECE_OFFS[1] + head)
                            | (row == PIECE_OFFS[2] + head), 1.0, 0.0)
            rhs_scr[h, 0:LANES, :] = qm.astype(BF16)
            rhs_scr[h, LANES:2 * LANES, :] = sel.astype(BF16)

    acc_scr[...] = jnp.zeros_like(acc_scr)
    m_scr[...] = jnp.full(m_scr.shape, NEG_BIG, F32)

    def key_block(kj):
        k0 = pl.multiple_of(kj * TK, TK)
        pcs = pc_ref[pl.ds(k0, TK), :]
        return [jnp.concatenate([k_ref[pl.ds(k0, TK), g * LANES:(g + 1) * LANES], pcs], axis=1)
                for g in range(FOX_G)]

    def scores_head(h, lhs, slot):
        s = jnp.dot(lhs[h // 2], rhs_scr[h], preferred_element_type=F32)
        s_scr[slot][h] = s
        cm_scr[slot][h] = jnp.broadcast_to(jnp.max(s, axis=0, keepdims=True), (SUBLANES, TQ))

    def scores(kj, slot):
        lhs = key_block(kj)
        for h in range(nh):
            scores_head(h, lhs, slot)

    def softmax_pv(h, kj, s, cmax, lo):
        k0 = pl.multiple_of(kj * TK, TK)
        m_old = m_scr[h, 0:1, lo:TQ]
        m_new = jnp.maximum(m_old, cmax)
        alpha = jnp.exp2(m_old - m_new)
        p = jnp.exp2(s - m_new).astype(BF16)
        pv = jnp.dot(vt_scr[h, :, pl.ds(k0, TK)], p, preferred_element_type=F32)
        acc_scr[h, :, lo:TQ] = alpha * acc_scr[h, :, lo:TQ] + pv
        m_scr[h, :, lo:TQ] = jnp.broadcast_to(m_new, (SUBLANES, TQ - lo))

    def overlapped(kj_next, slot_next, kj, slot):
        lhs = key_block(kj_next)
        for h in range(nh):
            scores_head(h, lhs, slot_next)
            softmax_pv(h, kj, s_scr[slot][h], cm_scr[slot][h, 0:1, :], 0)

    def pair(i, carry):
        overlapped(2 * i + 1, 1, 2 * i, 0)
        overlapped(2 * i + 2, 0, 2 * i + 1, 1)
        return carry

    scores(0, 0)
    lax.fori_loop(0, qi, pair, 0)

    half = TQ - TK
    lhs_b = key_block(2 * qi + 1)
    r = lax.broadcasted_iota(jnp.int32, (TK, TQ), 0)
    c = lax.broadcasted_iota(jnp.int32, (TK, TQ), 1)
    rb = lax.broadcasted_iota(jnp.int32, (TK, TK), 0)
    cb = lax.broadcasted_iota(jnp.int32, (TK, TK), 1)
    for h in range(nh):
        sb = jnp.dot(lhs_b[h // 2], rhs_scr[h, :, half:TQ], preferred_element_type=F32)
        s_scr[1][h, :, half:TQ] = jnp.where(cb >= rb, sb, NEG_BIG)
        s = jnp.where(c >= r, s_scr[0][h], NEG_BIG)
        softmax_pv(h, 2 * qi, s, jnp.max(s, axis=0, keepdims=True), 0)

    for g in range(FOX_G):
        parts = []
        for hh in range(2):
            h = 2 * g + hh
            s = s_scr[1][h, :, half:TQ]
            softmax_pv(h, 2 * qi + 1, s, jnp.max(s, axis=0, keepdims=True), half)
            a = acc_scr[h]
            parts.append(a[0:B_DH, :] * (1.0 / a[B_DH:B_DH + 1, :]))
        o = jnp.concatenate(parts, axis=0).T
        zz = _silu(z_ref[:, g * LANES:(g + 1) * LANES].astype(F32))
        out_ref[:, g * LANES:(g + 1) * LANES] = (o * zz).astype(BF16)


def _fox(proj, pieces, bsz, seq):
    m = proj.shape[0]
    nq = seq // FOX_TQ
    nh = 2 * FOX_G
    w = FOX_G * LANES
    ngrp = (B_HEADS * B_DH) // w
    sec = SECTION_W // w
    col0 = SEC_BQ * sec
    assert (SEC_BK, SEC_BV, SEC_BZ) == (SEC_BQ + 1, SEC_BQ + 2, SEC_BQ + 3)
    return pl.pallas_call(
        _fox_kernel,
        grid=(bsz, ngrp, nq),
        in_specs=[
            pl.BlockSpec((FOX_TQ, w), lambda b, hg, qi: (b * nq + qi, col0 + hg)),
            pl.BlockSpec((seq, w), lambda b, hg, qi: (b, col0 + sec + hg)),
            pl.BlockSpec((seq, w), lambda b, hg, qi: (b, col0 + 2 * sec + hg)),
            pl.BlockSpec((FOX_TQ, w), lambda b, hg, qi: (b * nq + qi, col0 + 3 * sec + hg)),
            pl.BlockSpec((None, seq, LANES), lambda b, hg, qi: (b, 0, 0)),
        ],
        out_specs=pl.BlockSpec((FOX_TQ, w), lambda b, hg, qi: (b * nq + qi, hg)),
        out_shape=jax.ShapeDtypeStruct((m, B_HEADS * B_DH), BF16),
        scratch_shapes=[
            pltpu.VMEM((nh, FOX_ACC_ROWS, seq), BF16),
            pltpu.VMEM((nh, FOX_ACC_ROWS, FOX_TQ), F32),
            pltpu.VMEM((nh, 2 * LANES, FOX_TQ), BF16),
            pltpu.VMEM((nh, SUBLANES, FOX_TQ), F32),
            pltpu.VMEM((2, nh, FOX_TK, FOX_TQ), F32),
            pltpu.VMEM((2, nh, SUBLANES, FOX_TQ), F32),
        ],
        compiler_params=pltpu.CompilerParams(
            dimension_semantics=("arbitrary", "arbitrary", "arbitrary"),
            vmem_limit_bytes=VMEM_LIMIT),
        name="fox",
    )(proj, proj, proj, proj, pieces)


def _merge_stages(ha_ref, hb_ref, ga_ref, gb_ref, x_ref, p_ref, wa_ref, wb_ref, wo_ref, wg_ref,
                  wp_ref, png_ref, fng_ref, out_ref):
    ya = jnp.dot(ha_ref[...], wa_ref[...], preferred_element_type=F32)
    yb = jnp.dot(hb_ref[...], wb_ref[...], preferred_element_type=F32)
    yield
    merged = (_sigmoid(ga_ref[...].astype(F32)) * ya + _sigmoid(gb_ref[...].astype(F32)) * yb)
    x1 = x_ref[...] + jnp.dot(merged.astype(BF16), wo_ref[...], preferred_element_type=F32)
    yield
    r = _rms_norm(x1, png_ref[...]).astype(BF16)
    gate = _sigmoid(jnp.dot(r, wg_ref[...], preferred_element_type=F32))
    yield
    pp = jnp.dot(p_ref[...].astype(BF16), wp_ref[...], preferred_element_type=F32)
    x2 = x1 + gate * pp
    out_ref[...] = _rms_norm(x2, fng_ref[...])
    yield


def _mlstm_merge_kernel(nt, n_chunks,
                        qk_ref, v_ref, o_ref, z_ref, gcol_ref, grow_ref, hg_ref,
                        hb_ref, ga_ref, gb_ref, x_ref, p_ref, wa_ref, wb_ref, wo_ref, wg_ref,
                        wp_ref, png_ref, fng_ref, out_ref,
                        ha_scr, c_scr, n_scr, m_scr, fprev_scr,
                        wa_scr, wb_scr, wo_scr, wg_scr, wp_scr):
    s = pl.program_id(0)
    chunk = jnp.minimum(s, n_chunks - 1)
    state = (c_scr, n_scr, m_scr, fprev_scr)

    @pl.when(s == 0)
    def _():
        ha_scr[...] = jnp.zeros_like(ha_scr)
        for src, dst in ((wa_ref, wa_scr), (wb_ref, wb_scr), (wo_ref, wo_scr), (wg_ref, wg_scr),
                         (wp_ref, wp_scr)):
            dst[...] = src[...].astype(BF16)

    @pl.when(chunk % nt == 0)
    def _():
        _mlstm_reset(*state)

    stages = _merge_stages(ha_scr, hb_ref, ga_ref, gb_ref, x_ref, p_ref, wa_scr, wb_scr, wo_scr,
                           wg_scr, wp_scr, png_ref, fng_ref, out_ref)
    next(stages)
    heads_done = []

    def after_head():
        heads_done.append(None)
        if len(heads_done) > MERGE_LAG:
            next(stages, None)

    _mlstm_chunk(qk_ref, v_ref, o_ref, z_ref, gcol_ref, grow_ref, hg_ref,
                 ha_scr, *state, after_head=after_head)
    for _ in stages:
        pass


def _mlstm_merge(proj, qk_act, gcol, grow, head_g, hb, x2, p2, wa, wb, wo, wg, wp,
                 png, fng, bsz, seq):
    m, d = x2.shape
    pd = p2.shape[1]
    width = SECTION_W
    nt = seq // A_CHUNK
    n_chunks = bsz * nt
    cur = lambda s: jnp.minimum(s, n_chunks - 1)
    prev = lambda s: jnp.maximum(s - 1, 0)
    a_blk = lambda sec: pl.BlockSpec((A_CHUNK, width), lambda s: (cur(s), sec))
    m_blk = lambda w, sec: pl.BlockSpec((A_CHUNK, w), lambda s: (prev(s), sec))
    full = lambda r, c: pl.BlockSpec((r, c), lambda s: (0, 0))
    once = lambda r, c: pl.BlockSpec((r, c), lambda s: (0, 0), pipeline_mode=pl.Buffered(1))
    return pl.pallas_call(
        functools.partial(_mlstm_merge_kernel, nt, n_chunks),
        grid=(n_chunks + 1,),
        in_specs=[
            a_blk(0), a_blk(SEC_AV), a_blk(SEC_AO), a_blk(SEC_AZ),
            pl.BlockSpec((None, A_CHUNK, LANES), lambda s: (cur(s) // nt, cur(s) % nt, 0)),
            pl.BlockSpec((None, SUBLANES, A_CHUNK), lambda s: (cur(s) // nt, 0, cur(s) % nt)),
            full(1, width),
            m_blk(d, 0), m_blk(d, SEC_GA), m_blk(d, SEC_GB), m_blk(d, 0), m_blk(pd, 0),
            once(d, d), once(d, d), once(d, d), once(d, d), once(pd, d),
            full(1, d), full(1, d),
        ],
        out_specs=pl.BlockSpec((A_CHUNK, d), lambda s: (prev(s), 0)),
        out_shape=jax.ShapeDtypeStruct((m, d), F32),
        scratch_shapes=[
            pltpu.VMEM((A_CHUNK, width), BF16),
            pltpu.VMEM((A_HEADS, A_DQK, A_DV), F32),
            pltpu.VMEM((A_HEADS, 1, A_DQK), F32),
            pltpu.VMEM((A_HEADS, SUBLANES, LANES), F32),
            pltpu.VMEM((1, LANES), F32),
            pltpu.VMEM((d, d), BF16), pltpu.VMEM((d, d), BF16), pltpu.VMEM((d, d), BF16),
            pltpu.VMEM((d, d), BF16), pltpu.VMEM((pd, d), BF16),
        ],
        compiler_params=pltpu.CompilerParams(
            dimension_semantics=("arbitrary",), vmem_limit_bytes=VMEM_LIMIT),
        name="mlstm_merge",
    )(qk_act, proj, proj, proj, gcol, grow, head_g,
      hb, proj, proj, x2, p2, wa, wb, wo, wg, wp, png, fng)


def _split_w_in(w):
    qkw = A_HEADS * A_DQK
    aw = A_HEADS * A_DV
    bw = B_HEADS * B_DH
    d = w.shape[0]
    o_ai = 2 * qkw + aw
    o_ao = o_ai + 2 * A_HEADS
    o_bf = o_ao + 2 * aw + 3 * bw
    o_bz = o_bf + B_HEADS
    seg_cols = (o_ai, o_bf - o_ao, w.shape[1] - o_bz)
    shifts = (0, o_ao - o_ai, o_ao - o_ai + o_bz - o_bf)
    assert all(c % IN_TN == 0 for c in seg_cols) and all(s % SUBLANES == 0 for s in shifts)
    seg_tiles = tuple(c // IN_TN for c in seg_cols)
    gate_rows = ((o_ai, o_ao - o_ai), (o_bf, o_bz - o_bf))
    assert all(r % SUBLANES == 0 and n % SUBLANES == 0 for r, n in gate_rows)
    return w.T, gate_rows, seg_tiles, shifts


def _layer(x, p_i, attn_norm_g, w_in, conv_w, conv_b, a_bias_i, a_bias_f, a_head_norm_g, b_bias_f,
           w_branch_a, w_branch_b, w_out, ple_norm_g, w_ple_gate, w_ple_proj, out_norm_g):
    bsz, seq, d = x.shape
    m = bsz * seq
    x2 = x.reshape(m, d)
    w_t, gate_rows, seg_tiles, shifts = _split_w_in(w_in)
    qkw = A_HEADS * A_DQK
    conv_scale = jnp.concatenate([jnp.ones((1, qkw), F32), jnp.full((1, qkw), A_DQK ** -0.5, F32)],
                                 axis=1)
    proj, gcol, grow, pieces, qk_act = _in_proj(
        x2, attn_norm_g.reshape(1, d), w_t, gate_rows, seg_tiles, shifts,
        conv_w, conv_b.reshape(1, -1), conv_scale, a_bias_i, a_bias_f, b_bias_f, seq)
    gcol = gcol.reshape(bsz, seq, LANES)
    pieces = pieces.reshape(bsz, seq, LANES)

    hb = _fox(proj, pieces, bsz, seq)
    out = _mlstm_merge(proj, qk_act, gcol, grow,
                       a_head_norm_g.reshape(1, -1), hb, x2, p_i.reshape(m, -1),
                       w_branch_a, w_branch_b, w_out, w_ple_gate, w_ple_proj,
                       ple_norm_g.reshape(1, d), out_norm_g.reshape(1, d), bsz, seq)
    return out.reshape(bsz, seq, d)


def kernel(x, p, attn_norm_g, w_in, conv_w, conv_b, a_bias_i, a_bias_f, a_head_norm_g, b_bias_f,
           w_branch_a, w_branch_b, w_out, ple_norm_g, w_ple_gate, w_ple_proj, final_norm_g):
    depth = w_in.shape[0]
    assert depth == 1, "the final norm is fused into the single layer's merge kernel"
    return _layer(x, p[0], attn_norm_g[0], w_in[0], conv_w[0], conv_b[0], a_bias_i[0], a_bias_f[0],
                  a_head_norm_g[0], b_bias_f[0], w_branch_a[0], w_branch_b[0], w_out[0],
                  ple_norm_g[0], w_ple_gate[0], w_ple_proj[0], final_norm_g)
```

```python
import functools
import math

import jax
import jax.numpy as jnp
from jax import lax
from jax.experimental import pallas as pl
from jax.experimental.pallas import tpu as pltpu

F32 = jnp.float32
BF16 = jnp.bfloat16

EPS = 1e-6
A_HEADS = 4
A_DQK = 128
A_DV = 256
CONV_K = 4
B_HEADS = 16
B_DH = 64
LANES = 128
SUBLANES = 8
BF16_ROWS = 16
NEG_BIG = -1e30
LOG2E = math.log2(math.e)

IN_TM = 2048
IN_TN = 1024
IN_CHUNKS = 8
GATE_ROWS = 128
A_CHUNK = 256
FOX_TK = 256
FOX_TQ = 2 * FOX_TK
FOX_G = 4
MERGE_LAG = 1
V7X_VMEM_BYTES = 64 * 1024 * 1024
VMEM_LIMIT = V7X_VMEM_BYTES * 7 // 8

SECTION_W = A_HEADS * A_DV
SEC_QK, SEC_AV, SEC_AO, SEC_AZ, SEC_BQ, SEC_BK, SEC_BV, SEC_BZ, SEC_GA, SEC_GB = range(10)

N_GATE = 2 * A_HEADS + B_HEADS
B_LANE0 = 2 * A_HEADS
PIECE_OFFS = (B_LANE0, B_LANE0 + B_HEADS, B_LANE0 + 2 * B_HEADS)

def _sigmoid(x):
    return 1.0 / (1.0 + jnp.exp2(x * (-LOG2E)))


def _silu(x):
    return x * _sigmoid(x)


def _rms_norm(x, g):
    ms = jnp.mean(x * x, axis=-1, keepdims=True)
    return (x * lax.rsqrt(ms + EPS)) * g


def _split3(x):
    x1 = x.astype(BF16)
    r1 = x - x1.astype(F32)
    x2 = r1.astype(BF16)
    x3 = (r1 - x2.astype(F32)).astype(BF16)
    return x1, x2, x3


_NT = (((1,), (1,)), ((), ()))


def _gate_chunk(x, carry, col_ref, row_ref, pc_ref, r0):
    ch = x.shape[0]
    ls = jnp.minimum(x, 0.0) - jnp.log1p(jnp.exp(-jnp.abs(x)))
    lane = lax.broadcasted_iota(jnp.int32, x.shape, 1)
    n_grp = ch // SUBLANES
    y = ls.reshape(n_grp, SUBLANES, LANES)
    sub = lax.broadcasted_iota(jnp.int32, y.shape, 1)
    step = 1
    while step < SUBLANES:
        y = y + jnp.where(sub >= step, pltpu.roll(y, step, axis=1), 0.0)
        step *= 2
    tot = jnp.broadcast_to(y[:, SUBLANES - 1:SUBLANES, :], y.shape)
    offs = []
    for k in range(n_grp):
        offs.append(carry)
        carry = carry + tot[k]
    cum = (y + jnp.stack(offs, axis=0)).reshape(ch, LANES)
    res = jnp.where(lane < A_HEADS, x, cum) * LOG2E
    col_ref[r0:r0 + ch, :] = res
    row_ref[:, r0:r0 + ch] = res.T[0:SUBLANES, :]
    in_b = (lane >= B_LANE0) & (lane < B_LANE0 + B_HEADS)
    p1, p2, p3 = _split3(jnp.where(in_b, cum * (-LOG2E), 0.0))
    pieces = (p1.astype(F32)
              + pltpu.roll(p2.astype(F32), PIECE_OFFS[1] - B_LANE0, axis=1)
              + pltpu.roll(p3.astype(F32), PIECE_OFFS[2] - B_LANE0, axis=1))
    pc_ref[r0:r0 + ch, :] = pieces.astype(BF16)
    return carry


def _in_proj_kernel(x_ref, g_ref, wt_ref, wga_ref, wgb_ref, cw_ref, cb_ref, cs_ref,
                    bi_ref, bfa_ref, bfb_ref,
                    proj_ref, col_ref, row_ref, pc_ref, qk_ref,
                    h_scr, qkraw_scr, xpad_scr, xbuf_scr, gates_scr, x_sem):
    i = pl.program_id(0)
    j = pl.program_id(1)
    tm = xbuf_scr.shape[1]
    n_seq = pl.num_programs(0)
    last = pl.num_programs(1) - 1

    def x_copy(seq, slot):
        return pltpu.make_async_copy(x_ref.at[pl.ds(seq * tm, tm), :], xbuf_scr.at[slot],
                                     x_sem.at[slot])

    @pl.when((i == 0) & (j == 0))
    def _():
        xpad_scr[0:SUBLANES, :] = jnp.zeros((SUBLANES, LANES), F32)
        x_copy(0, 0).start()

    @pl.when((j == 1) & (i + 1 < n_seq))
    def _():
        x_copy(i + 1, (i + 1) % 2).start()

    @pl.when(j == 0)
    def _():
        x_copy(i, i % 2).wait()
        h = _rms_norm(xbuf_scr[i % 2], g_ref[...]).astype(BF16)
        h_scr[0:tm, :] = h
        n_gate = wga_ref.shape[0] + wgb_ref.shape[0]
        wg = jnp.concatenate([wga_ref[...], wgb_ref[...],
                              jnp.zeros((LANES - n_gate, wga_ref.shape[1]), F32)], axis=0)
        gates_scr[...] = lax.dot_general(h, wg.astype(BF16), _NT, preferred_element_type=F32)

    ch = tm // IN_CHUNKS

    def matmul_chunk(w, r0):
        proj_ref[r0:r0 + ch, :] = lax.dot_general(
            h_scr[r0:r0 + ch, :], w, _NT, preferred_element_type=F32).astype(BF16)

    @pl.when(j == 0)
    def _():
        w = wt_ref[...].astype(BF16)
        for c in range(IN_CHUNKS):
            r0 = c * ch
            res = lax.dot_general(h_scr[r0:r0 + ch, :], w, _NT,
                                  preferred_element_type=F32).astype(BF16)
            proj_ref[r0:r0 + ch, :] = res
            qkraw_scr[r0:r0 + ch, :] = res

    @pl.when((j > 0) & (j < last))
    def _():
        n_strips = qkraw_scr.shape[1] // LANES
        lane0 = pl.multiple_of(jnp.clip(j - 1, 0, n_strips - 1) * LANES, LANES)
        xpad_scr[SUBLANES:SUBLANES + tm, :] = qkraw_scr[:, pl.ds(lane0, LANES)].astype(F32)
        w = wt_ref[...].astype(BF16)
        for c in range(IN_CHUNKS):
            r0 = c * ch
            matmul_chunk(w, r0)
            y = cb_ref[...]
            for d in range(CONV_K):
                y = y + (xpad_scr[SUBLANES + r0 - d:SUBLANES + r0 - d + ch, :]
                         * cw_ref[CONV_K - 1 - d:CONV_K - d, :])
            qk_ref[r0:r0 + ch, :] = (_silu(y) * cs_ref[...]).astype(BF16)

    @pl.when(j == last)
    def _():
        lane1 = lax.broadcasted_iota(jnp.int32, (1, LANES), 1)
        bias = jnp.zeros((1, LANES), F32)
        l0 = 0
        for ref in (bi_ref, bfa_ref, bfb_ref):
            for t in range(ref.shape[0]):
                bias = jnp.where(lane1 == l0 + t, ref[t], bias)
            l0 += ref.shape[0]
        carry = jnp.zeros((SUBLANES, LANES), F32)
        w = wt_ref[...].astype(BF16)
        for c in range(IN_CHUNKS):
            r0 = c * ch
            matmul_chunk(w, r0)
            for r in range(r0, r0 + ch, GATE_ROWS):
                carry = _gate_chunk(gates_scr[r:r + GATE_ROWS, :] + bias, carry,
                                    col_ref, row_ref, pc_ref, r)
            spare = pl.multiple_of(tm + (j - last) * BF16_ROWS, BF16_ROWS)
            h_scr[pl.ds(spare, BF16_ROWS), 0:LANES] = pc_ref[r0 + ch - BF16_ROWS:r0 + ch, :]


def _in_proj(x2, g, w_t, gate_rows, seg_tiles, shifts, conv_w, conv_b, conv_scale,
             bias_i, bias_fa, bias_fb, seq):
    m, d = x2.shape
    n = sum(seg_tiles) * IN_TN
    n_tiles = n // IN_TN
    n_strips = SECTION_W // LANES
    assert IN_TM == seq and IN_TN == SECTION_W and SEC_QK == 0 and n_tiles > n_strips + 1
    assert (IN_TM // IN_CHUNKS) % GATE_ROWS == 0 and GATE_ROWS % LANES == 0
    assert bias_i.shape[0] + bias_fa.shape[0] + bias_fb.shape[0] == N_GATE
    strip = lambda i, j: (0, jnp.clip(j - 1, 0, n_strips - 1))
    smem = pl.BlockSpec(memory_space=pltpu.SMEM)

    def w_rows(i, j):
        shift = shifts[0] // SUBLANES
        lo = 0
        for n_tiles, s in zip(seg_tiles[:-1], shifts[1:]):
            lo += n_tiles
            shift = jnp.where(j >= lo, s // SUBLANES, shift)
        return ((j * (IN_TN // SUBLANES) + shift) * SUBLANES, 0)

    return pl.pallas_call(
        _in_proj_kernel,
        grid=(m // IN_TM, n // IN_TN),
        in_specs=[
            pl.BlockSpec(memory_space=pl.ANY),
            pl.BlockSpec((1, d), lambda i, j: (0, 0)),
            pl.BlockSpec((pl.Element(IN_TN), pl.Element(d)), w_rows),
            pl.BlockSpec((pl.Element(gate_rows[0][1]), pl.Element(d)),
                         lambda i, j: (gate_rows[0][0], 0)),
            pl.BlockSpec((pl.Element(gate_rows[1][1]), pl.Element(d)),
                         lambda i, j: (gate_rows[1][0], 0)),
            pl.BlockSpec((CONV_K, LANES), strip),
            pl.BlockSpec((1, LANES), strip),
            pl.BlockSpec((1, LANES), strip),
            smem, smem, smem,
        ],
        out_specs=[
            pl.BlockSpec((IN_TM, IN_TN), lambda i, j: (i, j)),
            pl.BlockSpec((IN_TM, LANES), lambda i, j: (i, 0)),
            pl.BlockSpec((None, SUBLANES, IN_TM), lambda i, j: (i, 0, 0)),
            pl.BlockSpec((IN_TM, LANES), lambda i, j: (i, 0)),
            pl.BlockSpec((IN_TM, LANES), lambda i, j: (i, strip(i, j)[1])),
        ],
        out_shape=[
            jax.ShapeDtypeStruct((m, n), BF16),
            jax.ShapeDtypeStruct((m, LANES), F32),
            jax.ShapeDtypeStruct((m // IN_TM, SUBLANES, IN_TM), F32),
            jax.ShapeDtypeStruct((m, LANES), BF16),
            jax.ShapeDtypeStruct((m, SECTION_W), BF16),
        ],
        scratch_shapes=[pltpu.VMEM((IN_TM + BF16_ROWS, d), BF16),
                        pltpu.VMEM((IN_TM, SECTION_W), BF16),
                        pltpu.VMEM((IN_TM + SUBLANES, LANES), F32),
                        pltpu.VMEM((2, IN_TM, d), F32),
                        pltpu.VMEM((IN_TM, LANES), F32),
                        pltpu.SemaphoreType.DMA((2,))],
        compiler_params=pltpu.CompilerParams(
            dimension_semantics=("arbitrary", "arbitrary"),
            vmem_limit_bytes=VMEM_LIMIT),
        name="in_proj",
    )(x2, g, w_t, w_t, w_t, conv_w, conv_b, conv_scale, bias_i, bias_fa, bias_fb)


def _mlstm_reset(c_scr, n_scr, m_scr, fprev_scr):
    c_scr[...] = jnp.zeros_like(c_scr)
    n_scr[...] = jnp.zeros_like(n_scr)
    m_scr[...] = jnp.zeros_like(m_scr)
    fprev_scr[...] = jnp.zeros_like(fprev_scr)


def _mlstm_chunk(qk_ref, v_ref, o_ref, z_ref, gcol_ref, grow_ref, hg_ref,
                 out_ref, c_scr, n_scr, m_scr, fprev_scr, after_head=lambda: None):
    L = A_CHUNK
    qkw = A_HEADS * A_DQK

    row = lax.broadcasted_iota(jnp.int32, (L, L), 0)
    col = lax.broadcasted_iota(jnp.int32, (L, L), 1)
    causal = row >= col

    for h in range(A_HEADS):
        qb = qk_ref[:, h * A_DQK:(h + 1) * A_DQK]
        kb = qk_ref[:, qkw + h * A_DQK:qkw + (h + 1) * A_DQK]
        q = qb.astype(F32)
        k = kb.astype(F32)
        v = v_ref[:, h * A_DV:(h + 1) * A_DV]

        li_c = gcol_ref[:, h:h + 1]
        f_c = gcol_ref[:, A_HEADS + h:A_HEADS + h + 1]
        li_r = grow_ref[h:h + 1, :]
        f_r = grow_ref[A_HEADS + h:A_HEADS + h + 1, :]
        f_prev = fprev_scr[0:1, A_HEADS + h:A_HEADS + h + 1]
        f_end = gcol_ref[L - 1:L, A_HEADS + h:A_HEADS + h + 1]
        m_st = m_scr[h, 0:1, 0:1]
        c_st = c_scr[h]
        n_st = n_scr[h]

        dmat = jnp.where(causal, (f_c - f_r) + li_r, NEG_BIG)
        inter = (f_c - f_prev) + m_st
        m_row = jnp.maximum(inter, jnp.max(dmat, axis=-1, keepdims=True))
        w_intra = jnp.exp2(dmat - m_row)
        w_inter = jnp.exp2(inter - m_row)
        s = lax.dot_general(qb, kb, (((1,), (1,)), ((), ())), preferred_element_type=F32)
        scores = s * w_intra
        num = (jnp.dot(scores.astype(BF16), v, preferred_element_type=F32)
               + w_inter * jnp.dot(qb, c_st.astype(BF16), preferred_element_type=F32))
        den = (jnp.sum(scores, axis=-1, keepdims=True)
               + w_inter * jnp.sum(q * n_st, axis=-1, keepdims=True))
        hh = num * (1.0 / jnp.maximum(jnp.abs(den), jnp.exp2(-m_row)))

        g_tot = f_end - f_prev
        to_end = (f_end - f_c) + li_c
        m_new = jnp.maximum(g_tot + m_st, jnp.max(to_end, axis=0, keepdims=True))
        w_k = jnp.exp2(to_end - m_new)
        decay = jnp.exp2(g_tot + m_st - m_new)
        kw = k * w_k
        c_scr[h] = decay * c_st + jnp.dot(kw.T.astype(BF16), v, preferred_element_type=F32)
        n_scr[h] = decay * n_st + jnp.sum(kw, axis=0, keepdims=True)
        m_scr[h] = jnp.broadcast_to(m_new, (SUBLANES, LANES))

        hn = hh * lax.rsqrt(jnp.mean(hh * hh, axis=-1, keepdims=True) + EPS)
        hn = hn * hg_ref[:, h * A_DV:(h + 1) * A_DV]
        og = _sigmoid(o_ref[:, h * A_DV:(h + 1) * A_DV].astype(F32))
        zz = _silu(z_ref[:, h * A_DV:(h + 1) * A_DV].astype(F32))
        out_ref[:, h * A_DV:(h + 1) * A_DV] = ((og * hn) * zz).astype(BF16)
        after_head()

    fprev_scr[...] = gcol_ref[L - 1:L, :]


FOX_ACC_ROWS = B_DH + BF16_ROWS


def _fox_kernel(q_ref, k_ref, v_ref, z_ref, pc_ref, out_ref, vt_scr, acc_scr, rhs_scr, m_scr,
                s2_scr, cm2_scr):
    s_scr = (s2_scr.at[0], s2_scr.at[1])
    cm_scr = (cm2_scr.at[0], cm2_scr.at[1])
    hg = pl.program_id(1)
    qi = pl.program_id(2)
    TQ, TK = FOX_TQ, FOX_TK
    seq = k_ref.shape[0]
    nh = 2 * FOX_G

    @pl.when(qi == 0)
    def _():
        for g in range(FOX_G):
            vt = v_ref[:, g * LANES:(g + 1) * LANES].astype(F32).T
            for hh in range(2):
                vt_scr[2 * g + hh, 0:B_DH, :] = vt[hh * B_DH:(hh + 1) * B_DH, :].astype(BF16)
                vt_scr[2 * g + hh, B_DH:FOX_ACC_ROWS, :] = jnp.ones((BF16_ROWS, seq), BF16)

    row = lax.broadcasted_iota(jnp.int32, (LANES, TQ), 0)
    for g in range(FOX_G):
        qt = (q_ref[:, g * LANES:(g + 1) * LANES].astype(F32) * (B_DH ** -0.5 * LOG2E)).T
        for hh in range(2):
            h = 2 * g + hh
            head = hg * nh + h
            qm = jnp.where((row >= hh * B_DH) & (row < (hh + 1) * B_DH), qt, 0.0)
            sel = jnp.where((row == PIECE_OFFS[0] + head) | (row == PIECE_OFFS[1] + head)
                            | (row == PIECE_OFFS[2] + head), 1.0, 0.0)
            rhs_scr[h, 0:LANES, :] = qm.astype(BF16)
            rhs_scr[h, LANES:2 * LANES, :] = sel.astype(BF16)

    acc_scr[...] = jnp.zeros_like(acc_scr)
    m_scr[...] = jnp.full(m_scr.shape, NEG_BIG, F32)

    def key_block(kj):
        k0 = pl.multiple_of(kj * TK, TK)
        pcs = pc_ref[pl.ds(k0, TK), :]
        return [jnp.concatenate([k_ref[pl.ds(k0, TK), g * LANES:(g + 1) * LANES], pcs], axis=1)
                for g in range(FOX_G)]

    def scores_head(h, lhs, slot):
        s = jnp.dot(lhs[h // 2], rhs_scr[h], preferred_element_type=F32)
        s_scr[slot][h] = s
        cm_scr[slot][h] = jnp.broadcast_to(jnp.max(s, axis=0, keepdims=True), (SUBLANES, TQ))

    def scores(kj, slot):
        lhs = key_block(kj)
        for h in range(nh):
            scores_head(h, lhs, slot)

    def softmax_pv(h, kj, s, cmax, lo):
        k0 = pl.multiple_of(kj * TK, TK)
        m_old = m_scr[h, 0:1, lo:TQ]
        m_new = jnp.maximum(m_old, cmax)
        alpha = jnp.exp2(m_old - m_new)
        p = jnp.exp2(s - m_new).astype(BF16)
        pv = jnp.dot(vt_scr[h, :, pl.ds(k0, TK)], p, preferred_element_type=F32)
        acc_scr[h, :, lo:TQ] = alpha * acc_scr[h, :, lo:TQ] + pv
        m_scr[h, :, lo:TQ] = jnp.broadcast_to(m_new, (SUBLANES, TQ - lo))

    def overlapped(kj_next, slot_next, kj, slot):
        lhs = key_block(kj_next)
        for h in range(nh):
            scores_head(h, lhs, slot_next)
            softmax_pv(h, kj, s_scr[slot][h], cm_scr[slot][h, 0:1, :], 0)

    def pair(i, carry):
        overlapped(2 * i + 1, 1, 2 * i, 0)
        overlapped(2 * i + 2, 0, 2 * i + 1, 1)
        return carry

    scores(0, 0)
    lax.fori_loop(0, qi, pair, 0)

    half = TQ - TK
    lhs_b = key_block(2 * qi + 1)
    r = lax.broadcasted_iota(jnp.int32, (TK, TQ), 0)
    c = lax.broadcasted_iota(jnp.int32, (TK, TQ), 1)
    rb = lax.broadcasted_iota(jnp.int32, (TK, TK), 0)
    cb = lax.broadcasted_iota(jnp.int32, (TK, TK), 1)
    for h in range(nh):
        sb = jnp.dot(lhs_b[h // 2], rhs_scr[h, :, half:TQ], preferred_element_type=F32)
        s_scr[1][h, :, half:TQ] = jnp.where(cb >= rb, sb, NEG_BIG)
        s = jnp.where(c >= r, s_scr[0][h], NEG_BIG)
        softmax_pv(h, 2 * qi, s, jnp.max(s, axis=0, keepdims=True), 0)

    for g in range(FOX_G):
        parts = []
        for hh in range(2):
            h = 2 * g + hh
            s = s_scr[1][h, :, half:TQ]
            softmax_pv(h, 2 * qi + 1, s, jnp.max(s, axis=0, keepdims=True), half)
            a = acc_scr[h]
            parts.append(a[0:B_DH, :] * (1.0 / a[B_DH:B_DH + 1, :]))
        o = jnp.concatenate(parts, axis=0).T
        zz = _silu(z_ref[:, g * LANES:(g + 1) * LANES].astype(F32))
        out_ref[:, g * LANES:(g + 1) * LANES] = (o * zz).astype(BF16)


def _fox(proj, pieces, bsz, seq):
    m = proj.shape[0]
    nq = seq // FOX_TQ
    nh = 2 * FOX_G
    w = FOX_G * LANES
    ngrp = (B_HEADS * B_DH) // w
    sec = SECTION_W // w
    col0 = SEC_BQ * sec
    assert (SEC_BK, SEC_BV, SEC_BZ) == (SEC_BQ + 1, SEC_BQ + 2, SEC_BQ + 3)
    return pl.pallas_call(
        _fox_kernel,
        grid=(bsz, ngrp, nq),
        in_specs=[
            pl.BlockSpec((FOX_TQ, w), lambda b, hg, qi: (b * nq + qi, col0 + hg)),
            pl.BlockSpec((seq, w), lambda b, hg, qi: (b, col0 + sec + hg)),
            pl.BlockSpec((seq, w), lambda b, hg, qi: (b, col0 + 2 * sec + hg)),
            pl.BlockSpec((FOX_TQ, w), lambda b, hg, qi: (b * nq + qi, col0 + 3 * sec + hg)),
            pl.BlockSpec((None, seq, LANES), lambda b, hg, qi: (b, 0, 0)),
        ],
        out_specs=pl.BlockSpec((FOX_TQ, w), lambda b, hg, qi: (b * nq + qi, hg)),
        out_shape=jax.ShapeDtypeStruct((m, B_HEADS * B_DH), BF16),
        scratch_shapes=[
            pltpu.VMEM((nh, FOX_ACC_ROWS, seq), BF16),
            pltpu.VMEM((nh, FOX_ACC_ROWS, FOX_TQ), F32),
            pltpu.VMEM((nh, 2 * LANES, FOX_TQ), BF16),
            pltpu.VMEM((nh, SUBLANES, FOX_TQ), F32),
            pltpu.VMEM((2, nh, FOX_TK, FOX_TQ), F32),
            pltpu.VMEM((2, nh, SUBLANES, FOX_TQ), F32),
        ],
        compiler_params=pltpu.CompilerParams(
            dimension_semantics=("arbitrary", "arbitrary", "arbitrary"),
            vmem_limit_bytes=VMEM_LIMIT),
        name="fox",
    )(proj, proj, proj, proj, pieces)


def _merge_stages(ha_ref, hb_ref, ga_ref, gb_ref, x_ref, p_ref, wa_ref, wb_ref, wo_ref, wg_ref,
                  wp_ref, png_ref, fng_ref, out_ref):
    ya = jnp.dot(ha_ref[...], wa_ref[...], preferred_element_type=F32)
    yb = jnp.dot(hb_ref[...], wb_ref[...], preferred_element_type=F32)
    yield
    merged = (_sigmoid(ga_ref[...].astype(F32)) * ya + _sigmoid(gb_ref[...].astype(F32)) * yb)
    x1 = x_ref[...] + jnp.dot(merged.astype(BF16), wo_ref[...], preferred_element_type=F32)
    yield
    r = _rms_norm(x1, png_ref[...]).astype(BF16)
    gate = _sigmoid(jnp.dot(r, wg_ref[...], preferred_element_type=F32))
    yield
    pp = jnp.dot(p_ref[...].astype(BF16), wp_ref[...], preferred_element_type=F32)
    x2 = x1 + gate * pp
    out_ref[...] = _rms_norm(x2, fng_ref[...])
    yield


def _mlstm_merge_kernel(nt, n_chunks,
                        qk_ref, v_ref, o_ref, z_ref, gcol_ref, grow_ref, hg_ref,
                        hb_ref, ga_ref, gb_ref, x_ref, p_ref, wa_ref, wb_ref, wo_ref, wg_ref,
                        wp_ref, png_ref, fng_ref, out_ref,
                        ha_scr, c_scr, n_scr, m_scr, fprev_scr,
                        wa_scr, wb_scr, wo_scr, wg_scr, wp_scr):
    s = pl.program_id(0)
    chunk = jnp.minimum(s, n_chunks - 1)
    state = (c_scr, n_scr, m_scr, fprev_scr)

    @pl.when(s == 0)
    def _():
        ha_scr[...] = jnp.zeros_like(ha_scr)
        for src, dst in ((wa_ref, wa_scr), (wb_ref, wb_scr), (wo_ref, wo_scr), (wg_ref, wg_scr),
                         (wp_ref, wp_scr)):
            dst[...] = src[...].astype(BF16)

    @pl.when(chunk % nt == 0)
    def _():
        _mlstm_reset(*state)

    stages = _merge_stages(ha_scr, hb_ref, ga_ref, gb_ref, x_ref, p_ref, wa_scr, wb_scr, wo_scr,
                           wg_scr, wp_scr, png_ref, fng_ref, out_ref)
    next(stages)
    heads_done = []

    def after_head():
        heads_done.append(None)
        if len(heads_done) > MERGE_LAG:
            next(stages, None)

    _mlstm_chunk(qk_ref, v_ref, o_ref, z_ref, gcol_ref, grow_ref, hg_ref,
                 ha_scr, *state, after_head=after_head)
    for _ in stages:
        pass


def _mlstm_merge(proj, qk_act, gcol, grow, head_g, hb, x2, p2, wa, wb, wo, wg, wp,
                 png, fng, bsz, seq):
    m, d = x2.shape
    pd = p2.shape[1]
    width = SECTION_W
    nt = seq // A_CHUNK
    n_chunks = bsz * nt
    cur = lambda s: jnp.minimum(s, n_chunks - 1)
    prev = lambda s: jnp.maximum(s - 1, 0)
    a_blk = lambda sec: pl.BlockSpec((A_CHUNK, width), lambda s: (cur(s), sec))
    m_blk = lambda w, sec: pl.BlockSpec((A_CHUNK, w), lambda s: (prev(s), sec))
    full = lambda r, c: pl.BlockSpec((r, c), lambda s: (0, 0))
    once = lambda r, c: pl.BlockSpec((r, c), lambda s: (0, 0), pipeline_mode=pl.Buffered(1))
    return pl.pallas_call(
        functools.partial(_mlstm_merge_kernel, nt, n_chunks),
        grid=(n_chunks + 1,),
        in_specs=[
            a_blk(0), a_blk(SEC_AV), a_blk(SEC_AO), a_blk(SEC_AZ),
            pl.BlockSpec((None, A_CHUNK, LANES), lambda s: (cur(s) // nt, cur(s) % nt, 0)),
            pl.BlockSpec((None, SUBLANES, A_CHUNK), lambda s: (cur(s) // nt, 0, cur(s) % nt)),
            full(1, width),
            m_blk(d, 0), m_blk(d, SEC_GA), m_blk(d, SEC_GB), m_blk(d, 0), m_blk(pd, 0),
            once(d, d), once(d, d), once(d, d), once(d, d), once(pd, d),
            full(1, d), full(1, d),
        ],
        out_specs=pl.BlockSpec((A_CHUNK, d), lambda s: (prev(s), 0)),
        out_shape=jax.ShapeDtypeStruct((m, d), F32),
        scratch_shapes=[
            pltpu.VMEM((A_CHUNK, width), BF16),
            pltpu.VMEM((A_HEADS, A_DQK, A_DV), F32),
            pltpu.VMEM((A_HEADS, 1, A_DQK), F32),
            pltpu.VMEM((A_HEADS, SUBLANES, LANES), F32),
            pltpu.VMEM((1, LANES), F32),
            pltpu.VMEM((d, d), BF16), pltpu.VMEM((d, d), BF16), pltpu.VMEM((d, d), BF16),
            pltpu.VMEM((d, d), BF16), pltpu.VMEM((pd, d), BF16),
        ],
        compiler_params=pltpu.CompilerParams(
            dimension_semantics=("arbitrary",), vmem_limit_bytes=VMEM_LIMIT),
        name="mlstm_merge",
    )(qk_act, proj, proj, proj, gcol, grow, head_g,
      hb, proj, proj, x2, p2, wa, wb, wo, wg, wp, png, fng)


def _split_w_in(w):
    qkw = A_HEADS * A_DQK
    aw = A_HEADS * A_DV
    bw = B_HEADS * B_DH
    d = w.shape[0]
    o_ai = 2 * qkw + aw
    o_ao = o_ai + 2 * A_HEADS
    o_bf = o_ao + 2 * aw + 3 * bw
    o_bz = o_bf + B_HEADS
    seg_cols = (o_ai, o_bf - o_ao, w.shape[1] - o_bz)
    shifts = (0, o_ao - o_ai, o_ao - o_ai + o_bz - o_bf)
    assert all(c % IN_TN == 0 for c in seg_cols) and all(s % SUBLANES == 0 for s in shifts)
    seg_tiles = tuple(c // IN_TN for c in seg_cols)
    gate_rows = ((o_ai, o_ao - o_ai), (o_bf, o_bz - o_bf))
    assert all(r % SUBLANES == 0 and n % SUBLANES == 0 for r, n in gate_rows)
    return w.T, gate_rows, seg_tiles, shifts


def _layer(x, p_i, attn_norm_g, w_in, conv_w, conv_b, a_bias_i, a_bias_f, a_head_norm_g, b_bias_f,
           w_branch_a, w_branch_b, w_out, ple_norm_g, w_ple_gate, w_ple_proj, out_norm_g):
    bsz, seq, d = x.shape
    m = bsz * seq
    x2 = x.reshape(m, d)
    w_t, gate_rows, seg_tiles, shifts = _split_w_in(w_in)
    qkw = A_HEADS * A_DQK
    conv_scale = jnp.concatenate([jnp.ones((1, qkw), F32), jnp.full((1, qkw), A_DQK ** -0.5, F32)],
                                 axis=1)
    proj, gcol, grow, pieces, qk_act = _in_proj(
        x2, attn_norm_g.reshape(1, d), w_t, gate_rows, seg_tiles, shifts,
        conv_w, conv_b.reshape(1, -1), conv_scale, a_bias_i, a_bias_f, b_bias_f, seq)
    gcol = gcol.reshape(bsz, seq, LANES)
    pieces = pieces.reshape(bsz, seq, LANES)

    hb = _fox(proj, pieces, bsz, seq)
    out = _mlstm_merge(proj, qk_act, gcol, grow,
                       a_head_norm_g.reshape(1, -1), hb, x2, p_i.reshape(m, -1),
                       w_branch_a, w_branch_b, w_out, w_ple_gate, w_ple_proj,
                       ple_norm_g.reshape(1, d), out_norm_g.reshape(1, d), bsz, seq)
    return out.reshape(bsz, seq, d)


def kernel(x, p, attn_norm_g, w_in, conv_w, conv_b, a_bias_i, a_bias_f, a_head_norm_g, b_bias_f,
           w_branch_a, w_branch_b, w_out, ple_norm_g, w_ple_gate, w_ple_proj, final_norm_g):
    depth = w_in.shape[0]
    assert depth == 1, "the final norm is fused into the single layer's merge kernel"
    return _layer(x, p[0], attn_norm_g[0], w_in[0], conv_w[0], conv_b[0], a_bias_i[0], a_bias_f[0],
                  a_head_norm_g[0], b_bias_f[0], w_branch_a[0], w_branch_b[0], w_out[0],
                  ple_norm_g[0], w_ple_gate[0], w_ple_proj[0], final_norm_g)
```
